```python
import jax, jax.numpy as jnp
from jax import lax
import numpy as np

D_MODEL = 2048
BATCH = 8
SEQ = 8192
DEPTH = 2

N_META = 16
BLOCK = 128
PAD_FRONT = BLOCK - N_META
REAL_START = BLOCK
RMS_EPS = 1e-6
NEG_INF = -1e30
LOG_FLOOR = 1e-30

HA = 4
DK_A = 128
DV_A = 128
A_WIDTH = HA * DV_A
CHUNK_A = 16

HB = 16
KVH_B = 2
DH_B = 64
B_WIDTH = HB * DH_B
WINDOW = 128
ROPE_THETA = 10000.0

POOL_WINDOWS = (2, 4, 8, 16)
NG_C = 4
CG_C = 128
C_WIDTH = NG_C * CG_C

MIX_WIDTH = A_WIDTH + B_WIDTH + C_WIDTH
IN_SIZES = (HA * DK_A, HA * DK_A, A_WIDTH, A_WIDTH,
            B_WIDTH, KVH_B * DH_B, KVH_B * DH_B, B_WIDTH,
            C_WIDTH, C_WIDTH)
IN_COLS = sum(IN_SIZES)

kernel_name = "hymba_hgrn2_swa_pool_hybrid"


def rms_norm(x, g):
    x32 = x.astype(jnp.float32)
    y = x32 * lax.rsqrt(jnp.mean(x32 * x32, axis=-1, keepdims=True) + RMS_EPS)
    return (y * g.astype(jnp.float32)).astype(x.dtype)


def rope(x, pos):
    d = x.shape[-1]
    half = d // 2
    inv = jnp.power(ROPE_THETA, -jnp.arange(half, dtype=jnp.float32) * 2.0 / d)
    ang = pos.astype(jnp.float32)[:, None] * inv[None, :]
    cos = jnp.cos(ang)[None, :, None, :]
    sin = jnp.sin(ang)[None, :, None, :]
    x32 = x.astype(jnp.float32)
    x1, x2 = x32[..., :half], x32[..., half:]
    out = jnp.concatenate([x1 * cos - x2 * sin, x2 * cos + x1 * sin], axis=-1)
    return out.astype(x.dtype)


def hgrn2_mixer(q, z_f, i_in, lb, valid):
    b, p = q.shape[0], q.shape[1]
    n = p // CHUNK_A
    lb32 = lb.astype(jnp.float32)
    z = z_f.astype(jnp.float32)
    m = valid[None, :, None]
    sig = jax.nn.sigmoid(z)
    f = lb32 + (1.0 - lb32) * sig
    log_f = jnp.log(jnp.maximum(f, LOG_FLOOR)) * m
    k = (1.0 - lb32) * (1.0 - sig) * m
    qf = jax.nn.silu(q.astype(jnp.float32))
    v = i_in.astype(jnp.float32)

    def heads(t, d):
        return t.reshape(b, n, CHUNK_A, HA, d).transpose(0, 3, 1, 2, 4)

    qf, k, log_f, v = heads(qf, DK_A), heads(k, DK_A), heads(log_f, DK_A), heads(v, DV_A)
    G = jnp.cumsum(log_f, axis=3)
    G_last = G[:, :, :, -1:, :]

    causal = jnp.tril(jnp.ones((CHUNK_A, CHUNK_A), dtype=bool))
    diff = G[:, :, :, :, None, :] - G[:, :, :, None, :, :]
    decay = jnp.exp(jnp.where(causal[:, :, None], diff, NEG_INF))
    attn = jnp.einsum('bhnik,bhnijk->bhnij', qf, decay * k[:, :, :, None, :, :])
    o_intra = jnp.einsum('bhnij,bhnjv->bhniv', attn, v)

    q_dec = qf * jnp.exp(G)
    k_dec = k * jnp.exp(G_last - G)
    a = jnp.exp(G_last[:, :, :, 0, :])

    def step(S, xs):
        qn, kn, vn, an = xs
        o = jnp.einsum('bhck,bhkv->bhcv', qn, S)
        S = an[..., None] * S + jnp.einsum('bhck,bhcv->bhkv', kn, vn)
        return S, o

    xs = (jnp.moveaxis(q_dec, 2, 0), jnp.moveaxis(k_dec, 2, 0),
          jnp.moveaxis(v, 2, 0), jnp.moveaxis(a, 2, 0))
    S0 = jnp.zeros((b, HA, DK_A, DV_A), jnp.float32)
    _, o_inter = lax.scan(step, S0, xs)
    return o_intra + jnp.moveaxis(o_inter, 0, 2)


def swa_sink_attention(q, k, v, sinks):
    b, p = q.shape[0], q.shape[1]
    nb = p // BLOCK
    grp = HB // KVH_B
    qb = q.reshape(b, nb, BLOCK, KVH_B, grp, DH_B)

    def band(t):
        tb = t.reshape(b, nb, BLOCK, KVH_B, DH_B)
        prev = jnp.pad(tb, ((0, 0), (1, 0), (0, 0), (0, 0), (0, 0)))[:, :nb]
        meta = jnp.broadcast_to(t[:, None, PAD_FRONT:REAL_START], (b, nb, N_META, KVH_B, DH_B))
        return jnp.concatenate([meta, prev, tb], axis=2)

    k_all, v_all = band(k), band(v)
    s = jnp.einsum('bnqkgd,bnskd->bnkgqs', qb, k_all).astype(jnp.float32) * (DH_B ** -0.5)

    q_idx = jnp.arange(nb)[:, None] * BLOCK + jnp.arange(BLOCK)[None, :]
    band_idx = jnp.arange(nb)[:, None] * BLOCK - BLOCK + jnp.arange(2 * BLOCK)[None, :]
    meta_idx = jnp.broadcast_to(PAD_FRONT + jnp.arange(N_META)[None, :], (nb, N_META))
    key_idx = jnp.concatenate([meta_idx, band_idx], axis=1)
    is_meta = jnp.concatenate([jnp.ones((N_META,), bool), jnp.zeros((2 * BLOCK,), bool)])
    qi = q_idx[:, :, None]
    kj = key_idx[:, None, :]
    in_band = (kj > qi - WINDOW) & (kj >= REAL_START)
    allowed = (kj <= qi) & (is_meta[None, None, :] | in_band)
    s = jnp.where(allowed[None, :, None, None], s, NEG_INF)

    sink = sinks.astype(jnp.float32).reshape(KVH_B, grp)[None, None, :, :, None, None]
    mx = jnp.maximum(jnp.max(s, axis=-1, keepdims=True), sink)
    pr = jnp.exp(s - mx)
    denom = jnp.sum(pr, axis=-1, keepdims=True) + jnp.exp(sink - mx)
    w = (pr / denom).astype(v.dtype)
    o = jnp.einsum('bnkgqs,bnskd->bnqkgd', w, v_all)
    return o.reshape(b, p, HB * DH_B)


def multiscale_pool(u, valid, pool_w, pool_scale):
    b, p = u.shape[0], u.shape[1]
    ug = (u.astype(jnp.float32) * valid[None, :, None]).reshape(b, p, NG_C, CG_C)
    cs = jnp.cumsum(ug, axis=1)
    cnt = jnp.cumsum(valid)
    outs = []
    for g, w in enumerate(POOL_WINDOWS):
        cs_g = cs[:, :, g]
        cs_prev = jnp.pad(cs_g, ((0, 0), (w, 0), (0, 0)))[:, :p]
        cnt_prev = jnp.pad(cnt, (w, 0))[:p]
        mean = (cs_g - cs_prev) / jnp.maximum(cnt - cnt_prev, 1.0)[None, :, None]
        outs.append((mean - ug[:, :, g]) * valid[None, :, None])
    pooled = jnp.stack(outs, axis=2)
    y = jnp.einsum('bpgc,gcd->bpgd', pooled, pool_w.astype(jnp.float32))
    return (y.reshape(b, p, C_WIDTH) * pool_scale.astype(jnp.float32)).astype(u.dtype)


def hybrid_layer(h, norm_g, w_in, lb, q_norm_g, k_norm_g, sinks, hgrn_norm_g,
                 pool_w, pool_scale, w_out, pos, valid):
    b, p = h.shape[0], h.shape[1]
    xn = rms_norm(h, norm_g)
    proj = xn @ w_in
    splits = np.cumsum(IN_SIZES)[:-1].tolist()
    qa, fa, ia, ga, qb, kb, vb, gb, uc, gc = jnp.split(proj, splits, axis=-1)

    oa = hgrn2_mixer(qa, fa, ia, lb, valid)
    oa = oa.transpose(0, 2, 3, 1, 4).reshape(b, p, HA, DV_A)
    ya = rms_norm(oa, hgrn_norm_g).reshape(b, p, A_WIDTH).astype(h.dtype)

    qh = rope(rms_norm(qb.reshape(b, p, HB, DH_B), q_norm_g), pos)
    kh = rope(rms_norm(kb.reshape(b, p, KVH_B, DH_B), k_norm_g), pos)
    vh = vb.reshape(b, p, KVH_B, DH_B)
    yb = swa_sink_attention(qh, kh, vh, sinks).astype(h.dtype)

    yc = multiscale_pool(uc, valid, pool_w, pool_scale)

    mixed = jnp.concatenate([ya * jax.nn.silu(ga), yb * jax.nn.silu(gb),
                             yc * jax.nn.silu(gc)], axis=-1)
    return h + mixed @ w_out


def _fwd_setup_inputs(seed: int = 0) -> dict:
    key = jax.random.key(seed)
    ks = jax.random.split(key, 12)
    f32 = jnp.float32
    return {
        "x": jax.random.normal(ks[0], (BATCH, SEQ, D_MODEL), f32),
        "meta_tokens": jax.random.normal(ks[1], (N_META, D_MODEL), f32),
        "lb_logits": 0.5 * jax.random.normal(ks[2], (DEPTH, HA * DK_A), f32),
        "norm_g": 1.0 + 0.02 * jax.random.normal(ks[3], (DEPTH, D_MODEL), f32),
        "w_in": jax.random.normal(ks[4], (DEPTH, D_MODEL, IN_COLS), f32) * D_MODEL ** -0.5,
        "q_norm_g": 1.0 + 0.02 * jax.random.normal(ks[5], (DEPTH, DH_B), f32),
        "k_norm_g": 1.0 + 0.02 * jax.random.normal(ks[6], (DEPTH, DH_B), f32),
        "attn_sinks": 0.5 * jax.random.normal(ks[7], (DEPTH, HB), f32),
        "hgrn_norm_g": 1.0 + 0.02 * jax.random.normal(ks[8], (DEPTH, DV_A), f32),
        "pool_w": jax.random.normal(ks[9], (DEPTH, NG_C, CG_C, CG_C), f32) * CG_C ** -0.5,
        "pool_scale": 1.0 + 0.02 * jax.random.normal(ks[10], (DEPTH, C_WIDTH), f32),
        "w_out": jax.random.normal(ks[11], (DEPTH, MIX_WIDTH, D_MODEL), f32) * MIX_WIDTH ** -0.5,
    }


def _fwd_reference(x, meta_tokens, lb_logits, norm_g, w_in, q_norm_g, k_norm_g, attn_sinks,
              hgrn_norm_g, pool_w, pool_scale, w_out):
    b = x.shape[0]
    h = jnp.concatenate([
        jnp.zeros((b, PAD_FRONT, D_MODEL), x.dtype),
        jnp.broadcast_to(meta_tokens.astype(x.dtype)[None], (b, N_META, D_MODEL)),
        x], axis=1)
    p = h.shape[1]
    idx = jnp.arange(p)
    valid = (idx >= PAD_FRONT).astype(jnp.float32)
    pos = idx - PAD_FRONT

    sm = jax.nn.softmax(lb_logits.astype(jnp.float32), axis=0)
    lb_all = jnp.cumsum(sm, axis=0) - sm[0:1]

    for l in range(DEPTH):
        h = hybrid_layer(h, norm_g[l], w_in[l], lb_all[l], q_norm_g[l], k_norm_g[l],
                         attn_sinks[l], hgrn_norm_g[l], pool_w[l], pool_scale[l],
                         w_out[l], pos, valid)
    return h[:, REAL_START:]


import jax as _jax
import jax.numpy as _jnp

TWIN_FORMAT = 'train_step'
FWD_PARAMS = ['x', 'meta_tokens', 'lb_logits', 'norm_g', 'w_in', 'q_norm_g', 'k_norm_g', 'attn_sinks', 'hgrn_norm_g', 'pool_w', 'pool_scale', 'w_out']
TWIN_WEIGHTS = ['meta_tokens', 'lb_logits', 'norm_g', 'w_in', 'q_norm_g', 'k_norm_g', 'attn_sinks', 'hgrn_norm_g', 'pool_w', 'pool_scale', 'w_out']
TWIN_DIFF_INPUT = 'x'
TWIN_INPUTS = ['x', 'meta_tokens', 'lb_logits', 'norm_g', 'w_in', 'q_norm_g', 'k_norm_g', 'attn_sinks', 'hgrn_norm_g', 'pool_w', 'pool_scale', 'w_out', 'loss_target', 'm_meta_tokens', 'm_lb_logits', 'm_norm_g', 'm_w_in', 'm_q_norm_g', 'm_k_norm_g', 'm_attn_sinks', 'm_hgrn_norm_g', 'm_pool_w', 'm_pool_scale', 'm_w_out', 'v_meta_tokens', 'v_lb_logits', 'v_norm_g', 'v_w_in', 'v_q_norm_g', 'v_k_norm_g', 'v_attn_sinks', 'v_hgrn_norm_g', 'v_pool_w', 'v_pool_scale', 'v_w_out']
TWIN_OUTPUTS = ['loss', 'grad_x', 'grad_meta_tokens', 'grad_lb_logits', 'grad_norm_g', 'grad_w_in', 'grad_q_norm_g', 'grad_k_norm_g', 'grad_attn_sinks', 'grad_hgrn_norm_g', 'grad_pool_w', 'grad_pool_scale', 'grad_w_out', 'delta_meta_tokens', 'delta_lb_logits', 'delta_norm_g', 'delta_w_in', 'delta_q_norm_g', 'delta_k_norm_g', 'delta_attn_sinks', 'delta_hgrn_norm_g', 'delta_pool_w', 'delta_pool_scale', 'delta_w_out', 'new_m_meta_tokens', 'new_m_lb_logits', 'new_m_norm_g', 'new_m_w_in', 'new_m_q_norm_g', 'new_m_k_norm_g', 'new_m_attn_sinks', 'new_m_hgrn_norm_g', 'new_m_pool_w', 'new_m_pool_scale', 'new_m_w_out', 'new_v_meta_tokens', 'new_v_lb_logits', 'new_v_norm_g', 'new_v_w_in', 'new_v_q_norm_g', 'new_v_k_norm_g', 'new_v_attn_sinks', 'new_v_hgrn_norm_g', 'new_v_pool_w', 'new_v_pool_scale', 'new_v_w_out']
TWIN_LEAF_KINDS = {'loss': 'loss', 'grad_x': 'grad_x', 'grad_meta_tokens': 'grad_w', 'grad_lb_logits': 'grad_w', 'grad_norm_g': 'grad_w', 'grad_w_in': 'grad_w', 'grad_q_norm_g': 'grad_w', 'grad_k_norm_g': 'grad_w', 'grad_attn_sinks': 'grad_w', 'grad_hgrn_norm_g': 'grad_w', 'grad_pool_w': 'grad_w', 'grad_pool_scale': 'grad_w', 'grad_w_out': 'grad_w', 'delta_meta_tokens': 'delta_w', 'delta_lb_logits': 'delta_w', 'delta_norm_g': 'delta_w', 'delta_w_in': 'delta_w', 'delta_q_norm_g': 'delta_w', 'delta_k_norm_g': 'delta_w', 'delta_attn_sinks': 'delta_w', 'delta_hgrn_norm_g': 'delta_w', 'delta_pool_w': 'delta_w', 'delta_pool_scale': 'delta_w', 'delta_w_out': 'delta_w', 'new_m_meta_tokens': 'new_m', 'new_m_lb_logits': 'new_m', 'new_m_norm_g': 'new_m', 'new_m_w_in': 'new_m', 'new_m_q_norm_g': 'new_m', 'new_m_k_norm_g': 'new_m', 'new_m_attn_sinks': 'new_m', 'new_m_hgrn_norm_g': 'new_m', 'new_m_pool_w': 'new_m', 'new_m_pool_scale': 'new_m', 'new_m_w_out': 'new_m', 'new_v_meta_tokens': 'new_v', 'new_v_lb_logits': 'new_v', 'new_v_norm_g': 'new_v', 'new_v_w_in': 'new_v', 'new_v_q_norm_g': 'new_v', 'new_v_k_norm_g': 'new_v', 'new_v_attn_sinks': 'new_v', 'new_v_hgrn_norm_g': 'new_v', 'new_v_pool_w': 'new_v', 'new_v_pool_scale': 'new_v', 'new_v_w_out': 'new_v'}


def _forward(args):
    return _fwd_reference(*[args[k] for k in FWD_PARAMS])


def _output_shape():
    def fwd():
        inp = _fwd_setup_inputs(0)
        return _fwd_reference(*[inp[k] for k in FWD_PARAMS])
    out = _jax.eval_shape(fwd)
    return out.shape, out.dtype

N_MICROBATCH = 1
ADAM_LR = 0.001
ADAM_B1 = 0.9
ADAM_B2 = 0.999
ADAM_EPS = 1e-08
ADAM_WD = 0.01
ADAM_STEP = 10
PER_EXAMPLE_BATCH_AXIS = {'x': 0, 'loss_target': 0}
SHARED_INPUTS = []
_WEIGHT_DTYPES = {'meta_tokens': _jnp.float32, 'lb_logits': _jnp.float32, 'norm_g': _jnp.float32, 'w_in': _jnp.float32, 'q_norm_g': _jnp.float32, 'k_norm_g': _jnp.float32, 'attn_sinks': _jnp.float32, 'hgrn_norm_g': _jnp.float32, 'pool_w': _jnp.float32, 'pool_scale': _jnp.float32, 'w_out': _jnp.float32}
MOMENT_SCALE = {'meta_tokens': 9.506590e-03, 'lb_logits': 1.770768e-02, 'norm_g': 8.098935e+00, 'w_in': 1.352146e-01, 'q_norm_g': 1.921814e+00, 'k_norm_g': 1.922857e+00, 'attn_sinks': 5.051847e-02, 'hgrn_norm_g': 4.548469e+01, 'pool_w': 6.423943e-01, 'pool_scale': 8.874554e+00, 'w_out': 1.391415e-01}


def _to_microbatches(a, axis):
    t = _jnp.moveaxis(a, axis, 0)
    t = t.reshape((N_MICROBATCH, t.shape[0] // N_MICROBATCH) + t.shape[1:])
    return _jnp.moveaxis(t, 1, axis + 1)


def setup_inputs(seed: int = 0) -> dict:
    inp = _fwd_setup_inputs(seed)
    key = _jax.random.fold_in(_jax.random.key(seed), 7919)
    shape, _ = _output_shape()
    out = dict(inp)
    out["loss_target"] = _jax.random.normal(_jax.random.fold_in(key, 0), shape, _jnp.float32)
    for i, name in enumerate(TWIN_WEIGHTS):
        w = inp[name].astype(_jnp.float32)
        if MOMENT_SCALE is None:
            s = _jnp.sqrt(_jnp.mean(_jnp.square(w)) + 1e-30)
        else:
            s = MOMENT_SCALE[name]
        km, kv = _jax.random.split(_jax.random.fold_in(key, i + 1))
        out[name] = w
        out["m_" + name] = s * _jax.random.normal(km, w.shape, _jnp.float32)
        out["v_" + name] = (s * s) * _jax.random.uniform(kv, w.shape, _jnp.float32, 0.5, 1.5)
    if N_MICROBATCH > 1:
        for name, axis in PER_EXAMPLE_BATCH_AXIS.items():
            out[name] = _to_microbatches(out[name], axis)
    return {'x': out['x'], 'meta_tokens': out['meta_tokens'], 'lb_logits': out['lb_logits'], 'norm_g': out['norm_g'], 'w_in': out['w_in'], 'q_norm_g': out['q_norm_g'], 'k_norm_g': out['k_norm_g'], 'attn_sinks': out['attn_sinks'], 'hgrn_norm_g': out['hgrn_norm_g'], 'pool_w': out['pool_w'], 'pool_scale': out['pool_scale'], 'w_out': out['w_out'], 'loss_target': out['loss_target'], 'm_meta_tokens': out['m_meta_tokens'], 'm_lb_logits': out['m_lb_logits'], 'm_norm_g': out['m_norm_g'], 'm_w_in': out['m_w_in'], 'm_q_norm_g': out['m_q_norm_g'], 'm_k_norm_g': out['m_k_norm_g'], 'm_attn_sinks': out['m_attn_sinks'], 'm_hgrn_norm_g': out['m_hgrn_norm_g'], 'm_pool_w': out['m_pool_w'], 'm_pool_scale': out['m_pool_scale'], 'm_w_out': out['m_w_out'], 'v_meta_tokens': out['v_meta_tokens'], 'v_lb_logits': out['v_lb_logits'], 'v_norm_g': out['v_norm_g'], 'v_w_in': out['v_w_in'], 'v_q_norm_g': out['v_q_norm_g'], 'v_k_norm_g': out['v_k_norm_g'], 'v_attn_sinks': out['v_attn_sinks'], 'v_hgrn_norm_g': out['v_hgrn_norm_g'], 'v_pool_w': out['v_pool_w'], 'v_pool_scale': out['v_pool_scale'], 'v_w_out': out['v_w_out']}


def _loss(weights, diff, rest, loss_target):
    with _jax.named_scope("forward"):
        args = {**rest, TWIN_DIFF_INPUT: diff, **{k: w.astype(_WEIGHT_DTYPES[k]) for k, w in weights.items()}}
        y = _forward(args)
    with _jax.named_scope("loss_head"):
        err = _jnp.square(y.astype(_jnp.float32) - loss_target)
        return 0.5 * _jnp.sum(_jnp.mean(err, axis=-1)) if err.ndim else 0.5 * err


def _adamw(w, g, m, v):
    m = ADAM_B1 * m + (1.0 - ADAM_B1) * g
    v = ADAM_B2 * v + (1.0 - ADAM_B2) * _jnp.square(g)
    m_hat = m / (1.0 - ADAM_B1 ** ADAM_STEP)
    v_hat = v / (1.0 - ADAM_B2 ** ADAM_STEP)
    delta = -ADAM_LR * (m_hat / (_jnp.sqrt(v_hat) + ADAM_EPS) + ADAM_WD * w)
    return delta, m, v


def reference(x, meta_tokens, lb_logits, norm_g, w_in, q_norm_g, k_norm_g, attn_sinks, hgrn_norm_g, pool_w, pool_scale, w_out, loss_target, m_meta_tokens, m_lb_logits, m_norm_g, m_w_in, m_q_norm_g, m_k_norm_g, m_attn_sinks, m_hgrn_norm_g, m_pool_w, m_pool_scale, m_w_out, v_meta_tokens, v_lb_logits, v_norm_g, v_w_in, v_q_norm_g, v_k_norm_g, v_attn_sinks, v_hgrn_norm_g, v_pool_w, v_pool_scale, v_w_out):
    given = dict(x=x, meta_tokens=meta_tokens, lb_logits=lb_logits, norm_g=norm_g, w_in=w_in, q_norm_g=q_norm_g, k_norm_g=k_norm_g, attn_sinks=attn_sinks, hgrn_norm_g=hgrn_norm_g, pool_w=pool_w, pool_scale=pool_scale, w_out=w_out, loss_target=loss_target, m_meta_tokens=m_meta_tokens, m_lb_logits=m_lb_logits, m_norm_g=m_norm_g, m_w_in=m_w_in, m_q_norm_g=m_q_norm_g, m_k_norm_g=m_k_norm_g, m_attn_sinks=m_attn_sinks, m_hgrn_norm_g=m_hgrn_norm_g, m_pool_w=m_pool_w, m_pool_scale=m_pool_scale, m_w_out=m_w_out, v_meta_tokens=v_meta_tokens, v_lb_logits=v_lb_logits, v_norm_g=v_norm_g, v_w_in=v_w_in, v_q_norm_g=v_q_norm_g, v_k_norm_g=v_k_norm_g, v_attn_sinks=v_attn_sinks, v_hgrn_norm_g=v_hgrn_norm_g, v_pool_w=v_pool_w, v_pool_scale=v_pool_scale, v_w_out=v_w_out)
    weights = {n: given[n] for n in TWIN_WEIGHTS}
    shared = {n: given[n] for n in SHARED_INPUTS}
    per_example = {n: given[n] for n in ['x']}
    grad_fn = _jax.value_and_grad(_loss, argnums=(0, 1))

    def one_microbatch(ex, loss_target):
        ex = dict(ex)
        diff = ex.pop(TWIN_DIFF_INPUT)
        return grad_fn(weights, diff, {**shared, **ex}, loss_target)

    if N_MICROBATCH == 1:
        loss, (grad_w, grad_x) = one_microbatch(per_example, given["loss_target"])
    else:
        def body(carry, xs):
            loss_sum, grad_sum = carry
            l_k, (gw_k, gx_k) = one_microbatch(xs[0], xs[1])
            with _jax.named_scope("update"):
                return (loss_sum + l_k, _jax.tree.map(_jnp.add, grad_sum, gw_k)), gx_k

        init = (_jnp.zeros((), _jnp.float32), _jax.tree.map(_jnp.zeros_like, weights))
        (loss, grad_w), grad_x = _jax.lax.scan(body, init, (per_example, given["loss_target"]))
    with _jax.named_scope("update"):
        delta_w, new_m, new_v = {}, {}, {}
        for n in TWIN_WEIGHTS:
            delta_w[n], new_m[n], new_v[n] = _adamw(weights[n], grad_w[n], given["m_" + n], given["v_" + n])
    return (loss, grad_x, *[grad_w[n] for n in TWIN_WEIGHTS], *[delta_w[n] for n in TWIN_WEIGHTS],
            *[new_m[n] for n in TWIN_WEIGHTS], *[new_v[n] for n in TWIN_WEIGHTS])
```

```python
import functools

import numpy as np
import jax
import jax.numpy as jnp
from jax import lax
from jax.experimental import pallas as pl
from jax.experimental.pallas import tpu as pltpu

F32, BF16 = jnp.float32, jnp.bfloat16

D_MODEL = 2048
DEPTH = 2
N_META = 16
TB = 128
PAD_FRONT = TB - N_META
RMS_EPS = 1e-6
NEG_INF = -1e30
LOG_FLOOR = 1e-30
HA, DK_A = 4, 128
CH = 16
NCH = TB // CH
HB, KVH, DH = 16, 2, 64
GRP = HB // KVH
ROPE_THETA = 10000.0
POOL_WINDOWS = (2, 4, 8, 16)
PROJ_COLS = 5376
MIX = 2048
N_DEV = 8
C_QA, C_FA, C_IA, C_GA = 0, 1, 2, 3
C_QB, C_GB = 2, 3
C_UC, C_GC = 8, 9
C_KB, C_VB = 40, 41

ADAM_LR, ADAM_B1, ADAM_B2, ADAM_EPS, ADAM_WD, ADAM_STEP = 0.001, 0.9, 0.999, 1e-08, 0.01, 10

VMEM_LIMIT = 48 * 1024 * 1024

NN = ((1,), (0,))
NT = ((1,), (1,))
TN = ((0,), (0,))


def _dot(a, b, dims):
    return lax.dot_general(a, b, (dims, ((), ())), preferred_element_type=F32)


def _split3(x):
    hi = x.astype(BF16)
    r = x - hi.astype(F32)
    mid = r.astype(BF16)
    lo = (r - mid.astype(F32)).astype(BF16)
    return hi, mid, lo


def _xdot(m01, x):
    hi, mid, lo = _split3(x)
    return _dot(m01, hi, NN) + _dot(m01, mid, NN) + _dot(m01, lo, NN)


def _xdot_r(x, m01):
    hi, mid, lo = _split3(x)
    return _dot(hi, m01, NN) + _dot(mid, m01, NN) + _dot(lo, m01, NN)


def _iota(shape, dim):
    return lax.broadcasted_iota(jnp.int32, shape, dim)


def _params(*sem):
    return pltpu.CompilerParams(dimension_semantics=sem, vmem_limit_bytes=VMEM_LIMIT)


def _row_tile(p, target):
    best = TB
    t = TB
    while t <= target:
        if p % t == 0:
            best = t
        t += TB
    return best


def _col_tile(n, target):
    best = 128
    t = 128
    while t <= target:
        if n % t == 0:
            best = t
        t += 128
    return best


def _sigmoid(x):
    return 1.0 / (1.0 + jnp.exp(-x))


def _mm_nn(a, b, res=None, *, name, tm=640, tn=768):
    m, k = a.shape
    n = b.shape[1]
    tm, tn = _row_tile(m, tm), _col_tile(n, tn)

    def body(*refs):
        if res is None:
            a_ref, b_ref, o_ref = refs
            o_ref[...] = _dot(a_ref[...], b_ref[...], NN)
        else:
            a_ref, b_ref, r_ref, o_ref = refs
            o_ref[...] = r_ref[...] + _dot(a_ref[...], b_ref[...], NN)

    in_specs = [pl.BlockSpec((tm, k), lambda j, i: (i, 0)), pl.BlockSpec((k, tn), lambda j, i: (0, j))]
    args = [a, b]
    if res is not None:
        in_specs.append(pl.BlockSpec((tm, tn), lambda j, i: (i, j)))
        args.append(res)
    return pl.pallas_call(
        body, name=name, grid=(n // tn, m // tm), in_specs=in_specs,
        out_specs=pl.BlockSpec((tm, tn), lambda j, i: (i, j)),
        out_shape=jax.ShapeDtypeStruct((m, n), F32),
        compiler_params=_params("parallel", "parallel"),
    )(*args)


def _mm_nt(a, b, *, name, tm=640, tn=512, tk=2048):
    m, k = a.shape
    n = b.shape[0]
    tm, tn, tk = _row_tile(m, tm), _col_tile(n, tn), _col_tile(k, tk)

    def body(a_ref, b_ref, o_ref):
        @pl.when(pl.program_id(2) == 0)
        def _():
            o_ref[...] = jnp.zeros_like(o_ref)

        o_ref[...] += _dot(a_ref[...], b_ref[...], NT)

    return pl.pallas_call(
        body, name=name, grid=(n // tn, m // tm, k // tk),
        in_specs=[pl.BlockSpec((tm, tk), lambda j, i, kk: (i, kk)), pl.BlockSpec((tn, tk), lambda j, i, kk: (j, kk))],
        out_specs=pl.BlockSpec((tm, tn), lambda j, i, kk: (i, j)),
        out_shape=jax.ShapeDtypeStruct((m, n), F32),
        compiler_params=_params("parallel", "parallel", "arbitrary"),
    )(a, b)


def _mm_tn(a, b, *, name, tm=1024, tn=1344, tk=640):
    k, m = a.shape
    n = b.shape[1]
    tm, tn, tk = _col_tile(m, tm), _col_tile(n, tn), _row_tile(k, tk)

    def body(a_ref, b_ref, o_ref):
        @pl.when(pl.program_id(2) == 0)
        def _():
            o_ref[...] = jnp.zeros_like(o_ref)

        o_ref[...] += _dot(a_ref[...], b_ref[...], TN)

    return pl.pallas_call(
        body, name=name, grid=(m // tm, n // tn, k // tk),
        in_specs=[pl.BlockSpec((tk, tm), lambda i, j, kk: (kk, i)), pl.BlockSpec((tk, tn), lambda i, j, kk: (kk, j))],
        out_specs=pl.BlockSpec((tm, tn), lambda i, j, kk: (i, j)),
        out_shape=jax.ShapeDtypeStruct((m, n), F32),
        compiler_params=_params("parallel", "parallel", "arbitrary"),
    )(a, b)


def _rmsnorm_fwd(h, g, *, name):
    p, dm = h.shape
    tm = _row_tile(p, 640)

    def body(h_ref, g_ref, xn_ref):
        hv = h_ref[...]
        r = lax.rsqrt(jnp.mean(hv * hv, axis=-1, keepdims=True) + RMS_EPS)
        xn_ref[...] = (hv * r * g_ref[...]).astype(BF16)

    return pl.pallas_call(
        body, name=name, grid=(p // tm,),
        in_specs=[pl.BlockSpec((tm, dm), lambda i: (i, 0)), pl.BlockSpec((1, dm), lambda i: (0, 0))],
        out_specs=pl.BlockSpec((tm, dm), lambda i: (i, 0)),
        out_shape=jax.ShapeDtypeStruct((p, dm), BF16),
        compiler_params=_params("parallel"),
    )(h, g)


def _rmsnorm_bwd(dxn, h, g, dh_out, *, name):
    p, dm = h.shape
    tm = _row_tile(p, 384)

    def body(dxn_ref, h_ref, g_ref, dho_ref, dh_ref, dg_ref):
        hv = h_ref[...]
        r = lax.rsqrt(jnp.mean(hv * hv, axis=-1, keepdims=True) + RMS_EPS)
        xh = hv * r
        dy = dxn_ref[...]
        dyn = dy * g_ref[...]
        dh_ref[...] = dho_ref[...] + r * (dyn - xh * jnp.mean(dyn * xh, axis=-1, keepdims=True))

        @pl.when(pl.program_id(0) == 0)
        def _():
            dg_ref[...] = jnp.zeros_like(dg_ref)

        dg_ref[...] += jnp.sum(dy * xh, axis=0, keepdims=True)

    row = pl.BlockSpec((tm, dm), lambda i: (i, 0))
    vec = pl.BlockSpec((1, dm), lambda i: (0, 0))
    return pl.pallas_call(
        body, name=name, grid=(p // tm,),
        in_specs=[row, row, vec, row], out_specs=[row, vec],
        out_shape=[jax.ShapeDtypeStruct((p, dm), F32), jax.ShapeDtypeStruct((1, dm), F32)],
        compiler_params=_params("arbitrary"),
    )(dxn, h, g, dh_out)


def _loss_grad(h, target, *, name):
    p, dm = h.shape

    def body(h_ref, t_ref, dh_ref, sq_ref):
        n = pl.program_id(0)

        @pl.when(n == 0)
        def _():
            dh_ref[...] = jnp.zeros_like(dh_ref)
            sq_ref[...] = jnp.zeros_like(sq_ref)

        @pl.when(n > 0)
        def _():
            err = h_ref[...] - t_ref[...]
            dh_ref[...] = err * (1.0 / dm)
            sq_ref[...] += jnp.sum(err * err, axis=0, keepdims=True)

    return pl.pallas_call(
        body, name=name, grid=(p // TB,),
        in_specs=[pl.BlockSpec((TB, dm), lambda n: (n, 0)), pl.BlockSpec((TB, dm), lambda n: (jnp.maximum(n - 1, 0), 0))],
        out_specs=[pl.BlockSpec((TB, dm), lambda n: (n, 0)), pl.BlockSpec((1, dm), lambda n: (0, 0))],
        out_shape=[jax.ShapeDtypeStruct((p, dm), F32), jax.ShapeDtypeStruct((1, dm), F32)],
        compiler_params=_params("arbitrary"),
    )(h, target)


def _chunk_masks():
    ri, ci = _iota((TB, TB), 0), _iota((TB, TB), 1)
    same = (ri >> 4) == (ci >> 4)
    causal = same & (ci <= ri)
    lower = jnp.where(causal, 1.0, 0.0).astype(BF16)
    upper = jnp.where(same & (ci >= ri), 1.0, 0.0).astype(BF16)
    ones = jnp.where(same, 1.0, 0.0).astype(BF16)
    return causal, lower, upper, ones


def _hgrn_gates(q, z, lbh, m):
    sig = _sigmoid(z)
    f = lbh + (1.0 - lbh) * sig
    lf = jnp.log(jnp.maximum(f, LOG_FLOOR)) * m
    kk = (1.0 - lbh) * (1.0 - sig) * m
    sq = _sigmoid(q)
    return sig, f, lf, kk, sq, q * sq


def _hgrn_fwd(proj, lb, *, name):
    p = proj.shape[0]
    nb = p // TB

    def body(qa_ref, fa_ref, ia_ref, lb_ref, oa_ref, sck_ref, st_ref):
        n = pl.program_id(0)

        @pl.when(n == 0)
        def _():
            st_ref[...] = jnp.zeros_like(st_ref)

        causal, lower, _, ones = _chunk_masks()
        m = ((n * TB + _iota((TB, 1), 0)) >= PAD_FRONT).astype(F32)
        for hd in range(HA):
            sl = slice(hd * DK_A, (hd + 1) * DK_A)
            q, z, v, lbh = qa_ref[:, sl], fa_ref[:, sl], ia_ref[:, sl], lb_ref[:, sl]
            _, _, lf, kk, _, qf = _hgrn_gates(q, z, lbh, m)
            g = _xdot(lower, lf)
            gl = _xdot(ones, lf)
            qd = (qf * jnp.exp(g)).astype(BF16)
            kt = (kk * jnp.exp(-g)).astype(BF16)
            kd = (kk * jnp.exp(gl - g)).astype(BF16)
            vb = v.astype(BF16)
            a_all = jnp.exp(gl)
            att = jnp.where(causal, _dot(qd, kt, NT), 0.0).astype(BF16)
            o = _dot(att, vb, NN)
            st = st_ref[hd]
            sck_ref[0, hd] = st
            outs = []
            for c in range(NCH):
                r = slice(c * CH, (c + 1) * CH)
                outs.append(_dot(qd[r], st.astype(BF16), NT))
                st = st * a_all[c * CH:c * CH + 1, :] + _dot(vb[r], kd[r], TN)
            st_ref[hd] = st
            oa_ref[:, sl] = o + jnp.concatenate(outs, axis=0)

    blk = lambda c: pl.BlockSpec((TB, 512), lambda n, c=c: (n, c))
    return pl.pallas_call(
        body, name=name, grid=(nb,),
        in_specs=[blk(C_QA), blk(C_FA), blk(C_IA), pl.BlockSpec((1, 512), lambda n: (0, 0))],
        out_specs=[pl.BlockSpec((TB, 512), lambda n: (n, 0)), pl.BlockSpec((1, HA, TB, TB), lambda n: (n, 0, 0, 0))],
        out_shape=[jax.ShapeDtypeStruct((p, 512), F32), jax.ShapeDtypeStruct((nb, HA, TB, TB), F32)],
        scratch_shapes=[pltpu.VMEM((HA, TB, TB), F32)],
        compiler_params=_params("arbitrary"),
    )(proj, proj, proj, lb)


def _hgrn_bwd(proj, lb, sck, d_oa, *, name):
    p = proj.shape[0]
    nb = p // TB

    def body(qa_ref, fa_ref, ia_ref, lb_ref, sck_ref, do_ref, dq_ref, dz_ref, dv_ref, dlb_ref,
             dst_ref, stc_ref, dqd_ref, dkd_ref, dvs_ref, dgl_ref):
        i = pl.program_id(0)
        n = nb - 1 - i

        @pl.when(i == 0)
        def _():
            dst_ref[...] = jnp.zeros_like(dst_ref)
            dlb_ref[...] = jnp.zeros_like(dlb_ref)

        causal, lower, upper, ones = _chunk_masks()
        m = ((n * TB + _iota((TB, 1), 0)) >= PAD_FRONT).astype(F32)
        for hd in range(HA):
            sl = slice(hd * DK_A, (hd + 1) * DK_A)
            q, z, v, lbh = qa_ref[:, sl], fa_ref[:, sl], ia_ref[:, sl], lb_ref[:, sl]
            sig, f, lf, kk, sq, qf = _hgrn_gates(q, z, lbh, m)
            g = _xdot(lower, lf)
            gl = _xdot(ones, lf)
            e_g, e_ng, e_d = jnp.exp(g), jnp.exp(-g), jnp.exp(gl - g)
            qd_f, kt_f, kd_f = qf * e_g, kk * e_ng, kk * e_d
            qd, kt, kd = qd_f.astype(BF16), kt_f.astype(BF16), kd_f.astype(BF16)
            vb = v.astype(BF16)
            a_all = jnp.exp(gl)
            att = jnp.where(causal, _dot(qd, kt, NT), 0.0).astype(BF16)
            do = do_ref[:, sl]
            dob = do.astype(BF16)

            st = sck_ref[0, hd]
            for c in range(NCH):
                r = slice(c * CH, (c + 1) * CH)
                stc_ref[c] = st
                if c + 1 < NCH:
                    st = st * a_all[c * CH:c * CH + 1, :] + _dot(vb[r], kd[r], TN)

            d_att = jnp.where(causal, _dot(dob, vb, NT), 0.0).astype(BF16)
            d_v = _dot(att, dob, TN)
            d_qd = _dot(d_att, kt, NN)
            d_kt = _dot(d_att, qd, TN)

            dst = dst_ref[hd]
            for c in range(NCH - 1, -1, -1):
                r = slice(c * CH, (c + 1) * CH)
                stc = stc_ref[c]
                a_c = a_all[c * CH:c * CH + 1, :]
                dstb = dst.astype(BF16)
                dvs_ref[r, :] = _dot(kd[r], dstb, NT)
                dkd_ref[r, :] = _dot(vb[r], dstb, NN)
                d_a = jnp.sum(dst * stc, axis=0, keepdims=True)
                dgl_ref[r, :] = jnp.broadcast_to(d_a * a_c, (CH, TB))
                dqd_ref[r, :] = _dot(dob[r], stc.astype(BF16), NN)
                dst = dst * a_c + _dot(dob[r], qd[r], TN)
            dst_ref[hd] = dst

            d_qd = d_qd + dqd_ref[...]
            d_kd = dkd_ref[...]
            d_v = d_v + dvs_ref[...]
            kd_term = d_kd * kd_f
            d_g = d_qd * qd_f - d_kt * kt_f - kd_term
            d_lf = _xdot(upper, d_g) + _xdot(ones, kd_term) + dgl_ref[...]
            d_kk = (d_kt * e_ng + d_kd * e_d) * m
            t1 = d_lf * m * jnp.where(f > LOG_FLOOR, 1.0 / f, 0.0)
            dq_ref[:, sl] = d_qd * e_g * (sq * (1.0 + q * (1.0 - sq)))
            dz_ref[:, sl] = (t1 - d_kk) * (1.0 - lbh) * sig * (1.0 - sig)
            dv_ref[:, sl] = d_v
            dlb_ref[:, sl] += jnp.sum((t1 - d_kk) * (1.0 - sig), axis=0, keepdims=True)

    blk = lambda c: pl.BlockSpec((TB, 512), lambda i, c=c: (nb - 1 - i, c))
    out_blk = pl.BlockSpec((TB, 512), lambda i: (nb - 1 - i, 0))
    vec = pl.BlockSpec((1, 512), lambda i: (0, 0))
    return pl.pallas_call(
        body, name=name, grid=(nb,),
        in_specs=[blk(C_QA), blk(C_FA), blk(C_IA), vec,
                  pl.BlockSpec((1, HA, TB, TB), lambda i: (nb - 1 - i, 0, 0, 0)), out_blk],
        out_specs=[out_blk, out_blk, out_blk, vec],
        out_shape=[jax.ShapeDtypeStruct((p, 512), F32)] * 3 + [jax.ShapeDtypeStruct((1, 512), F32)],
        scratch_shapes=[pltpu.VMEM((HA, TB, TB), F32), pltpu.VMEM((NCH, TB, TB), F32)]
        + [pltpu.VMEM((TB, TB), F32)] * 4,
        compiler_params=_params("arbitrary"),
    )(proj, proj, proj, lb, sck, d_oa)


def _lane():
    return _iota((1, TB), 1)


def _swap_halves(y):
    first = (_lane() & 63) < 32
    return jnp.where(first, pltpu.roll(y, 96, 1), pltpu.roll(y, 32, 1))


def _head_ones():
    ri, ci = _iota((TB, TB), 0), _iota((TB, TB), 1)
    return jnp.where((ri >> 6) == (ci >> 6), 1.0, 0.0).astype(BF16)


def _norm_rope(x, g, cos, sin, bd):
    r = lax.rsqrt(_xdot_r(x * x, bd) * (1.0 / DH) + RMS_EPS)
    y = x * r * g
    return y * cos + _swap_halves(y) * sin


def _norm_rope_bwd(d_out, x, g, cos, sin, bd):
    d = d_out * cos - _swap_halves(d_out) * sin
    r = lax.rsqrt(_xdot_r(x * x, bd) * (1.0 / DH) + RMS_EPS)
    xh = x * r
    dyn = d * g
    dx = r * (dyn - xh * (_xdot_r(dyn * xh, bd) * (1.0 / DH)))
    return dx, jnp.sum(d * xh, axis=0, keepdims=True)


def _dup_heads(k):
    first = _lane() < DH
    r = pltpu.roll(k, DH, 1)
    return jnp.where(first, k, r), jnp.where(first, r, k)


def _qk_prep(proj, gq, gk, cos, sin, *, name):
    p = proj.shape[0]

    def body(qb_ref, kb_ref, vb_ref, gq_ref, gk_ref, cos_ref, sin_ref, qh_ref, k2_ref, v2_ref):
        bd = _head_ones()
        cos_v, sin_v = cos_ref[...], sin_ref[...]
        for j in range(HB // 2):
            sl = slice(j * TB, (j + 1) * TB)
            qh_ref[:, sl] = _norm_rope(qb_ref[:, sl], gq_ref[...], cos_v, sin_v, bd).astype(BF16)
        k0, k1 = _dup_heads(_norm_rope(kb_ref[...], gk_ref[...], cos_v, sin_v, bd))
        k2_ref[:, 0:TB] = k0.astype(BF16)
        k2_ref[:, TB:2 * TB] = k1.astype(BF16)
        v0, v1 = _dup_heads(vb_ref[...])
        v2_ref[:, 0:TB] = v0.astype(BF16)
        v2_ref[:, TB:2 * TB] = v1.astype(BF16)

    vec = pl.BlockSpec((1, TB), lambda n: (0, 0))
    tab = pl.BlockSpec((TB, TB), lambda n: (n, 0))
    return pl.pallas_call(
        body, name=name, grid=(p // TB,),
        in_specs=[pl.BlockSpec((TB, 1024), lambda n: (n, C_QB)), pl.BlockSpec((TB, TB), lambda n: (n, C_KB)),
                  pl.BlockSpec((TB, TB), lambda n: (n, C_VB)), vec, vec, tab, tab],
        out_specs=[pl.BlockSpec((TB, 1024), lambda n: (n, 0)), pl.BlockSpec((TB, 256), lambda n: (n, 0)),
                   pl.BlockSpec((TB, 256), lambda n: (n, 0))],
        out_shape=[jax.ShapeDtypeStruct((p, 1024), BF16), jax.ShapeDtypeStruct((p, 256), BF16),
                   jax.ShapeDtypeStruct((p, 256), BF16)],
        compiler_params=_params("parallel"),
    )(proj, proj, proj, gq, gk, cos, sin)


NKEY = N_META + 2 * TB


def _attn_mask(n):
    r = _iota((TB, NKEY), 0)
    j = _iota((TB, NKEY), 1)
    meta = (j < N_META) & ((n >= 1) | (j + PAD_FRONT <= r))
    prev = (j >= N_META) & (j < N_META + TB) & (n >= 2) & (j - N_META > r)
    cur = (j >= N_META + TB) & (n >= 1) & (j - (N_META + TB) <= r)
    return meta | prev | cur


def _attn_specs():
    cur = lambda w: pl.BlockSpec((TB, w), lambda n: (n, 0))
    prev = pl.BlockSpec((TB, 256), lambda n: (jnp.maximum(n - 1, 0), 0))
    meta = pl.BlockSpec((N_META, 256), lambda n: (PAD_FRONT // N_META, 0))
    sink = pl.BlockSpec(memory_space=pltpu.SMEM)
    return cur, prev, meta, sink


def _softmax_sink(qm, kall, mask, sink):
    s = _dot(qm, kall, NT) * (DH ** -0.5)
    s = jnp.where(mask, s, NEG_INF)
    mx = jnp.maximum(jnp.max(s, axis=1, keepdims=True), sink)
    pr = jnp.exp(s - mx)
    inv = 1.0 / (jnp.sum(pr, axis=1, keepdims=True) + jnp.exp(sink - mx))
    return pr * inv, jnp.exp(sink - mx) * inv


def _attn_fwd(qh, k2, v2, sinks, *, name):
    p = qh.shape[0]

    def body(sink_ref, q_ref, kc_ref, kp_ref, km_ref, vc_ref, vp_ref, vm_ref, o_ref):
        n = pl.program_id(0)
        mask = _attn_mask(n)
        first = _lane() < DH
        for kv in range(KVH):
            ks = slice(kv * TB, (kv + 1) * TB)
            kall = jnp.concatenate([km_ref[:, ks], kp_ref[:, ks], kc_ref[:, ks]], axis=0)
            vall = jnp.concatenate([vm_ref[:, ks], vp_ref[:, ks], vc_ref[:, ks]], axis=0)
            for jj in range(GRP // 2):
                j = kv * (GRP // 2) + jj
                sl = slice(j * TB, (j + 1) * TB)
                qj = q_ref[:, sl]
                halves = []
                for half in range(2):
                    qm = jnp.where(first if half == 0 else ~first, qj, jnp.zeros_like(qj))
                    w, _ = _softmax_sink(qm, kall, mask, sink_ref[2 * j + half])
                    halves.append(_dot(w.astype(BF16), vall, NN))
                o_ref[:, sl] = jnp.where(first, halves[0], halves[1])

    cur, prev, meta, sink = _attn_specs()
    return pl.pallas_call(
        body, name=name, grid=(p // TB,),
        in_specs=[sink, cur(1024), cur(256), prev, meta, cur(256), prev, meta],
        out_specs=cur(1024),
        out_shape=jax.ShapeDtypeStruct((p, 1024), F32),
        compiler_params=_params("parallel"),
    )(sinks, qh, k2, k2, k2, v2, v2, v2)


def _attn_bwd(qh, k2, v2, sinks, o, d_o, *, name):
    p = qh.shape[0]

    def body(sink_ref, q_ref, kc_ref, kp_ref, km_ref, vc_ref, vp_ref, vm_ref, o_ref, do_ref,
             dq_ref, dkc_ref, dkp_ref, dvc_ref, dvp_ref, dkm_ref, dvm_ref, dsink_ref):
        n = pl.program_id(0)

        @pl.when(n == 0)
        def _():
            dkm_ref[...] = jnp.zeros_like(dkm_ref)
            dvm_ref[...] = jnp.zeros_like(dvm_ref)
            dsink_ref[...] = jnp.zeros_like(dsink_ref)

        mask = _attn_mask(n)
        lane = _lane()
        first = lane < DH
        sink_acc = jnp.zeros((TB, TB), F32)
        for kv in range(KVH):
            ks = slice(kv * TB, (kv + 1) * TB)
            kall = jnp.concatenate([km_ref[:, ks], kp_ref[:, ks], kc_ref[:, ks]], axis=0)
            vall = jnp.concatenate([vm_ref[:, ks], vp_ref[:, ks], vc_ref[:, ks]], axis=0)
            d_kall = jnp.zeros((NKEY, TB), F32)
            d_vall = jnp.zeros((NKEY, TB), F32)
            for jj in range(GRP // 2):
                j = kv * (GRP // 2) + jj
                sl = slice(j * TB, (j + 1) * TB)
                qj, oj, doj = q_ref[:, sl], o_ref[:, sl], do_ref[:, sl]
                halves = []
                for half in range(2):
                    sel = first if half == 0 else ~first
                    qm = jnp.where(sel, qj, jnp.zeros_like(qj))
                    dom = jnp.where(sel, doj, 0.0)
                    w, w_sink = _softmax_sink(qm, kall, mask, sink_ref[2 * j + half])
                    delta = jnp.sum(dom * oj, axis=1, keepdims=True)
                    domb = dom.astype(BF16)
                    d_w = _dot(domb, vall, NT)
                    dsb = (w * (d_w - delta) * (DH ** -0.5)).astype(BF16)
                    sink_acc = sink_acc + jnp.where(lane == 2 * j + half, -(w_sink * delta), 0.0)
                    halves.append(_dot(dsb, kall, NN))
                    d_kall = d_kall + _dot(dsb, qm, TN)
                    d_vall = d_vall + _dot(w.astype(BF16), domb, TN)
                dq_ref[:, sl] = jnp.where(first, halves[0], halves[1])
            dkm_ref[:, ks] += d_kall[0:N_META]
            dkp_ref[:, ks] = d_kall[N_META:N_META + TB]
            dkc_ref[:, ks] = d_kall[N_META + TB:NKEY]
            dvm_ref[:, ks] += d_vall[0:N_META]
            dvp_ref[:, ks] = d_vall[N_META:N_META + TB]
            dvc_ref[:, ks] = d_vall[N_META + TB:NKEY]
        dsink_ref[...] += jnp.sum(sink_acc, axis=0, keepdims=True)

    cur, prev, meta, sink = _attn_specs()
    acc = lambda r: pl.BlockSpec((r, 256), lambda n: (0, 0))
    return pl.pallas_call(
        body, name=name, grid=(p // TB,),
        in_specs=[sink, cur(1024), cur(256), prev, meta, cur(256), prev, meta, cur(1024), cur(1024)],
        out_specs=[cur(1024), cur(256), cur(256), cur(256), cur(256), acc(N_META), acc(N_META),
                   pl.BlockSpec((1, TB), lambda n: (0, 0))],
        out_shape=[jax.ShapeDtypeStruct((p, 1024), F32)] + [jax.ShapeDtypeStruct((p, 256), F32)] * 4
        + [jax.ShapeDtypeStruct((N_META, 256), F32)] * 2 + [jax.ShapeDtypeStruct((1, TB), F32)],
        compiler_params=_params("arbitrary"),
    )(sinks, qh, k2, k2, k2, v2, v2, v2, o, d_o)


def _qk_post(proj, gq, gk, cos, sin, dqh, dkc, dkp, dkm, dvc, dvp, dvm, *, name):
    p = proj.shape[0]
    nb = p // TB

    def body(qb_ref, kb_ref, gq_ref, gk_ref, cos_ref, sin_ref, dqh_ref, dkc_ref, dkp_ref, dkm_ref,
             dvc_ref, dvp_ref, dvm_ref, dqb_ref, dkb_ref, dvb_ref, dgq_ref, dgk_ref, tk_ref, tv_ref):
        n = pl.program_id(0)

        @pl.when(n == 0)
        def _():
            dgq_ref[...] = jnp.zeros_like(dgq_ref)
            dgk_ref[...] = jnp.zeros_like(dgk_ref)

        keep = jnp.where(n == nb - 1, 0.0, 1.0)
        tk_ref[...] = dkc_ref[...] + keep * dkp_ref[...]
        tv_ref[...] = dvc_ref[...] + keep * dvp_ref[...]

        @pl.when(n == 0)
        def _():
            tk_ref[PAD_FRONT:TB, :] += dkm_ref[...]
            tv_ref[PAD_FRONT:TB, :] += dvm_ref[...]

        first = _lane() < DH

        def fold(t_ref):
            t0, t1 = t_ref[:, 0:TB], t_ref[:, TB:2 * TB]
            return jnp.where(first, t0 + pltpu.roll(t0, DH, 1), t1 + pltpu.roll(t1, DH, 1))

        bd = _head_ones()
        cos_v, sin_v = cos_ref[...], sin_ref[...]
        dvb_ref[...] = fold(tv_ref)
        dkb, dgk = _norm_rope_bwd(fold(tk_ref), kb_ref[...], gk_ref[...], cos_v, sin_v, bd)
        dkb_ref[...] = dkb
        dgk_ref[...] += dgk
        dgq = jnp.zeros((1, TB), F32)
        for j in range(HB // 2):
            sl = slice(j * TB, (j + 1) * TB)
            dqb, dg = _norm_rope_bwd(dqh_ref[:, sl], qb_ref[:, sl], gq_ref[...], cos_v, sin_v, bd)
            dqb_ref[:, sl] = dqb
            dgq = dgq + dg
        dgq_ref[...] += dgq

    vec = pl.BlockSpec((1, TB), lambda n: (0, 0))
    tab = pl.BlockSpec((TB, TB), lambda n: (n, 0))
    cur = lambda w: pl.BlockSpec((TB, w), lambda n: (n, 0))
    nxt = pl.BlockSpec((TB, 256), lambda n: (jnp.minimum(n + 1, nb - 1), 0))
    meta = pl.BlockSpec((N_META, 256), lambda n: (0, 0))
    return pl.pallas_call(
        body, name=name, grid=(nb,),
        in_specs=[pl.BlockSpec((TB, 1024), lambda n: (n, C_QB)), pl.BlockSpec((TB, TB), lambda n: (n, C_KB)),
                  vec, vec, tab, tab, cur(1024), cur(256), nxt, meta, cur(256), nxt, meta],
        out_specs=[cur(1024), cur(TB), cur(TB), vec, vec],
        out_shape=[jax.ShapeDtypeStruct((p, 1024), F32), jax.ShapeDtypeStruct((p, TB), F32),
                   jax.ShapeDtypeStruct((p, TB), F32), jax.ShapeDtypeStruct((1, TB), F32),
                   jax.ShapeDtypeStruct((1, TB), F32)],
        scratch_shapes=[pltpu.VMEM((TB, 256), F32), pltpu.VMEM((TB, 256), F32)],
        compiler_params=_params("arbitrary"),
    )(proj, proj, gq, gk, cos, sin, dqh, dkc, dkp, dkm, dvc, dvp, dvm)


EXT = TB + N_META


def _pool_count_inv(n, w):
    t = n * TB + _iota((TB, 1), 0)
    cnt = jnp.clip(t - (PAD_FRONT - 1), 1, w)
    return 1.0 / cnt.astype(F32)


def _silu_parts(gate):
    s = _sigmoid(gate)
    return gate * s, s * (1.0 + gate * (1.0 - s))


def _mix_fwd(proj, oa, yb, hg, pool_w, pool_scale, *, name):
    p = proj.shape[0]

    def body(ga_ref, gb_ref, uc_ref, up_ref, gc_ref, oa_ref, yb_ref, hg_ref, pw_ref, ps_ref, mx_ref, pooled_ref):
        n = pl.program_id(0)
        valid = ((n * TB + _iota((TB, 1), 0)) >= PAD_FRONT).astype(F32)
        for hd in range(HA):
            sl = slice(hd * TB, (hd + 1) * TB)
            o = oa_ref[:, sl]
            r = lax.rsqrt(jnp.mean(o * o, axis=-1, keepdims=True) + RMS_EPS)
            act, _ = _silu_parts(ga_ref[:, sl])
            mx_ref[:, sl] = (o * r * hg_ref[...] * act).astype(BF16)
        for j in range(HB // 2):
            sl = slice(j * TB, (j + 1) * TB)
            act, _ = _silu_parts(gb_ref[:, sl])
            mx_ref[:, 512 + j * TB:512 + (j + 1) * TB] = (yb_ref[:, sl] * act).astype(BF16)
        ri, ci = _iota((TB, EXT), 0), _iota((TB, EXT), 1)
        has_prev = jnp.where(n == 0, 0.0, 1.0)
        for gi, w in enumerate(POOL_WINDOWS):
            sl = slice(gi * TB, (gi + 1) * TB)
            ug = uc_ref[:, sl] * valid
            ext = jnp.concatenate([up_ref[:, sl] * has_prev, ug], axis=0)
            band = jnp.where((ci <= ri + N_META) & (ci > ri + N_META - w), 1.0, 0.0).astype(BF16)
            pooled = (_xdot(band, ext) * _pool_count_inv(n, w) - ug) * valid
            pooled_ref[:, sl] = pooled
            yc = _dot(pooled.astype(BF16), pw_ref[gi], NN) * ps_ref[:, sl]
            act, _ = _silu_parts(gc_ref[:, sl])
            mx_ref[:, 1536 + gi * TB:1536 + (gi + 1) * TB] = (yc * act).astype(BF16)

    cur = lambda w, c=0: pl.BlockSpec((TB, w), lambda n, c=c: (n, c))
    prev16 = pl.BlockSpec((N_META, 512), lambda n: (jnp.maximum(n * (TB // N_META) - 1, 0), C_UC))
    return pl.pallas_call(
        body, name=name, grid=(p // TB,),
        in_specs=[cur(512, C_GA), cur(1024, C_GB), cur(512, C_UC), prev16, cur(512, C_GC), cur(512), cur(1024),
                  pl.BlockSpec((1, TB), lambda n: (0, 0)), pl.BlockSpec((4, TB, TB), lambda n: (0, 0, 0)),
                  pl.BlockSpec((1, 512), lambda n: (0, 0))],
        out_specs=[cur(MIX), cur(512)],
        out_shape=[jax.ShapeDtypeStruct((p, MIX), BF16), jax.ShapeDtypeStruct((p, 512), F32)],
        compiler_params=_params("parallel"),
    )(proj, proj, proj, proj, proj, oa, yb, hg, pool_w, pool_scale)


def _mix_bwd(proj, oa, yb, pooled, hg, pool_w, pool_scale, d_mixed, *, name):
    p = proj.shape[0]

    def body(ga_ref, gb_ref, gc_ref, oa_ref, yb_ref, pooled_ref, hg_ref, pw_ref, ps_ref, dm_ref,
             doa_ref, dyb_ref, dga_ref, dgb_ref, dgc_ref, dp_ref, dhg_ref, dpw_ref, dps_ref):
        n = pl.program_id(0)

        @pl.when(n == 0)
        def _():
            dhg_ref[...] = jnp.zeros_like(dhg_ref)
            dpw_ref[...] = jnp.zeros_like(dpw_ref)
            dps_ref[...] = jnp.zeros_like(dps_ref)

        valid = ((n * TB + _iota((TB, 1), 0)) >= PAD_FRONT).astype(F32)
        dhg = jnp.zeros((1, TB), F32)
        for hd in range(HA):
            sl = slice(hd * TB, (hd + 1) * TB)
            o = oa_ref[:, sl]
            r = lax.rsqrt(jnp.mean(o * o, axis=-1, keepdims=True) + RMS_EPS)
            on = o * r
            gate = ga_ref[:, sl]
            act, dact = _silu_parts(gate)
            dmx = dm_ref[:, sl]
            d_ya = dmx * act
            dga_ref[:, sl] = dmx * on * hg_ref[...] * dact
            dyn = d_ya * hg_ref[...]
            doa_ref[:, sl] = r * (dyn - on * jnp.mean(dyn * on, axis=-1, keepdims=True))
            dhg = dhg + jnp.sum(d_ya * on, axis=0, keepdims=True)
        dhg_ref[...] += dhg
        for j in range(HB // 2):
            sl = slice(j * TB, (j + 1) * TB)
            act, dact = _silu_parts(gb_ref[:, sl])
            dmx = dm_ref[:, 512 + j * TB:512 + (j + 1) * TB]
            dyb_ref[:, sl] = dmx * act
            dgb_ref[:, sl] = dmx * yb_ref[:, sl] * dact
        for gi in range(len(POOL_WINDOWS)):
            sl = slice(gi * TB, (gi + 1) * TB)
            pooled = pooled_ref[:, sl]
            pooled_b = pooled.astype(BF16)
            t = _dot(pooled_b, pw_ref[gi], NN)
            act, dact = _silu_parts(gc_ref[:, sl])
            dmx = dm_ref[:, 1536 + gi * TB:1536 + (gi + 1) * TB]
            d_yc = dmx * act
            dgc_ref[:, sl] = dmx * t * ps_ref[:, sl] * dact
            dps_ref[:, sl] += jnp.sum(d_yc * t, axis=0, keepdims=True)
            d_t = (d_yc * ps_ref[:, sl]).astype(BF16)
            dp_ref[:, sl] = _dot(d_t, pw_ref[gi], NT) * valid
            dpw_ref[gi] += _dot(pooled_b, d_t, TN)

    cur = lambda w, c=0: pl.BlockSpec((TB, w), lambda n, c=c: (n, c))
    return pl.pallas_call(
        body, name=name, grid=(p // TB,),
        in_specs=[cur(512, C_GA), cur(1024, C_GB), cur(512, C_GC), cur(512), cur(1024), cur(512),
                  pl.BlockSpec((1, TB), lambda n: (0, 0)), pl.BlockSpec((4, TB, TB), lambda n: (0, 0, 0)),
                  pl.BlockSpec((1, 512), lambda n: (0, 0)), cur(MIX)],
        out_specs=[cur(512), cur(1024), cur(512), cur(1024), cur(512), cur(512),
                   pl.BlockSpec((1, TB), lambda n: (0, 0)), pl.BlockSpec((4, TB, TB), lambda n: (0, 0, 0)),
                   pl.BlockSpec((1, 512), lambda n: (0, 0))],
        out_shape=[jax.ShapeDtypeStruct((p, 512), F32), jax.ShapeDtypeStruct((p, 1024), F32),
                   jax.ShapeDtypeStruct((p, 512), F32), jax.ShapeDtypeStruct((p, 1024), F32),
                   jax.ShapeDtypeStruct((p, 512), F32), jax.ShapeDtypeStruct((p, 512), F32),
                   jax.ShapeDtypeStruct((1, TB), F32), jax.ShapeDtypeStruct((4, TB, TB), F32),
                   jax.ShapeDtypeStruct((1, 512), F32)],
        compiler_params=_params("arbitrary"),
    )(proj, proj, proj, oa, yb, pooled, hg, pool_w, pool_scale, d_mixed)


def _pool_bwd(dp, *, name):
    p = dp.shape[0]
    nb = p // TB

    def body(dp_ref, dn_ref, duc_ref):
        n = pl.program_id(0)
        valid = ((n * TB + _iota((TB, 1), 0)) >= PAD_FRONT).astype(F32)
        has_next = jnp.where(n == nb - 1, 0.0, 1.0)
        ri, ci = _iota((TB, EXT), 0), _iota((TB, EXT), 1)
        for gi, w in enumerate(POOL_WINDOWS):
            sl = slice(gi * TB, (gi + 1) * TB)
            d_p = dp_ref[:, sl]
            ext = jnp.concatenate([d_p * _pool_count_inv(n, w), dn_ref[:, sl] * (has_next / w)], axis=0)
            band = jnp.where((ci >= ri) & (ci < ri + w), 1.0, 0.0).astype(BF16)
            duc_ref[:, sl] = (_xdot(band, ext) - d_p) * valid

    return pl.pallas_call(
        body, name=name, grid=(nb,),
        in_specs=[pl.BlockSpec((TB, 512), lambda n: (n, 0)),
                  pl.BlockSpec((N_META, 512), lambda n: (jnp.minimum(n + 1, nb - 1) * (TB // N_META), 0))],
        out_specs=pl.BlockSpec((TB, 512), lambda n: (n, 0)),
        out_shape=jax.ShapeDtypeStruct((p, 512), F32),
        compiler_params=_params("parallel"),
    )(dp, dp)


def _layer_fwd(h, w, tag):
    xn = _rmsnorm_fwd(h, w["norm_g"], name=f"rmsnorm_fwd{tag}")
    proj = _mm_nn(xn, w["w_in"], name=f"in_proj{tag}")
    oa, sck = _hgrn_fwd(proj, w["lb"], name=f"hgrn_fwd{tag}")
    qh, k2, v2 = _qk_prep(proj, w["gq"], w["gk"], w["cos"], w["sin"], name=f"qk_prep{tag}")
    yb = _attn_fwd(qh, k2, v2, w["sinks"], name=f"attn_fwd{tag}")
    mixed, pooled = _mix_fwd(proj, oa, yb, w["hg"], w["pool_w"], w["pool_scale"], name=f"mix_fwd{tag}")
    h_next = _mm_nn(mixed, w["w_out"], h, name=f"out_proj{tag}", tn=1024)
    saved = dict(h=h, xn=xn, proj=proj, oa=oa, sck=sck, qh=qh, k2=k2, v2=v2, yb=yb, mixed=mixed, pooled=pooled)
    return h_next, saved


def _layer_bwd(dh_out, s, w, tag):
    dhb = dh_out.astype(BF16)
    d_mixed = _mm_nt(dhb, w["w_out"], name=f"d_mixed{tag}", tn=1024)
    dw_out = _mm_tn(s["mixed"], dhb, name=f"dw_out{tag}", tn=1024)
    d_oa, d_yb, d_ga, d_gb, d_gc, d_p, d_hg, d_pw, d_ps = _mix_bwd(
        s["proj"], s["oa"], s["yb"], s["pooled"], w["hg"], w["pool_w"], w["pool_scale"], d_mixed, name=f"mix_bwd{tag}")
    d_uc = _pool_bwd(d_p, name=f"pool_bwd{tag}")
    d_qh, dkc, dkp, dvc, dvp, dkm, dvm, d_sink = _attn_bwd(s["qh"], s["k2"], s["v2"], w["sinks"], s["yb"], d_yb,
                                                          name=f"attn_bwd{tag}")
    d_qb, d_kb, d_vb, d_gq, d_gk = _qk_post(s["proj"], w["gq"], w["gk"], w["cos"], w["sin"], d_qh, dkc, dkp, dkm,
                                            dvc, dvp, dvm, name=f"qk_post{tag}")
    d_qa, d_fa, d_ia, d_lb = _hgrn_bwd(s["proj"], w["lb"], s["sck"], d_oa, name=f"hgrn_bwd{tag}")
    d_proj = jnp.concatenate([d_qa, d_fa, d_ia, d_ga, d_qb, d_gb, d_uc, d_gc, d_kb, d_vb], axis=1).astype(BF16)
    d_xn = _mm_nt(d_proj, w["w_in"], name=f"d_xn{tag}", tk=_col_tile(PROJ_COLS, 2688))
    dw_in = _mm_tn(s["xn"], d_proj, name=f"dw_in{tag}")
    dh, d_ng = _rmsnorm_bwd(d_xn, s["h"], w["norm_g"], dh_out, name=f"rmsnorm_bwd{tag}")
    grads = dict(w_in=dw_in, w_out=dw_out, norm_g=d_ng, lb=d_lb, gq=d_gq, gk=d_gk, sinks=d_sink, hg=d_hg,
                 pool_w=d_pw, pool_scale=d_ps)
    return dh, grads


def _peers():
    x, y, c = lax.axis_index("x"), lax.axis_index("y"), lax.axis_index("c")
    out = []
    for k in range(1, N_DEV):
        kx, ky, kc = (k >> 2) & 1, (k >> 1) & 1, k & 1
        px, py, pc = x ^ kx, y ^ ky, c ^ kc
        out.append(((px, py, pc), 4 * px + 2 * py + pc))
    return 4 * x + 2 * y + c, out


def _exchange(src, *, gather, name):
    rows, cols = src.shape[-2:]

    def body(src_ref, out_ref, send_sems, recv_sems, local_sem):
        me, peers = _peers()
        mine = pltpu.make_async_copy(src_ref if gather else src_ref.at[me], out_ref.at[me], local_sem)
        mine.start()
        copies = []
        for k, (dev, idx) in enumerate(peers):
            copies.append(pltpu.make_async_remote_copy(
                src_ref=src_ref if gather else src_ref.at[idx], dst_ref=out_ref.at[me],
                send_sem=send_sems.at[k], recv_sem=recv_sems.at[k],
                device_id=dev, device_id_type=pl.DeviceIdType.MESH))
        for cp in copies:
            cp.start()
        for cp in copies:
            cp.wait_recv()
        for cp in copies:
            cp.wait_send()
        mine.wait()

    return pl.pallas_call(
        body, name=name,
        in_specs=[pl.BlockSpec(memory_space=pl.ANY)], out_specs=pl.BlockSpec(memory_space=pl.ANY),
        out_shape=jax.ShapeDtypeStruct((N_DEV, rows, cols), src.dtype),
        scratch_shapes=[pltpu.SemaphoreType.DMA((N_DEV - 1,)), pltpu.SemaphoreType.DMA((N_DEV - 1,)),
                        pltpu.SemaphoreType.DMA],
    )(src)


def _adamw(stack, w, m, v, *, name):
    rows, cols = w.shape
    tr = rows
    for cand in (256, 128, 64, 32, 16):
        if rows % cand == 0:
            tr = cand
            break

    def body(s_ref, w_ref, m_ref, v_ref, g_ref, d_ref, nm_ref, nv_ref):
        g = s_ref[0].astype(F32)
        for d in range(1, N_DEV):
            g = g + s_ref[d].astype(F32)
        nm = ADAM_B1 * m_ref[...] + (1.0 - ADAM_B1) * g
        nv = ADAM_B2 * v_ref[...] + (1.0 - ADAM_B2) * (g * g)
        m_hat = nm / (1.0 - ADAM_B1 ** ADAM_STEP)
        v_hat = nv / (1.0 - ADAM_B2 ** ADAM_STEP)
        g_ref[...] = g
        d_ref[...] = -ADAM_LR * (m_hat / (jnp.sqrt(v_hat) + ADAM_EPS) + ADAM_WD * w_ref[...])
        nm_ref[...] = nm
        nv_ref[...] = nv

    blk = pl.BlockSpec((tr, cols), lambda i: (i, 0))
    return pl.pallas_call(
        body, name=name, grid=(rows // tr,),
        in_specs=[pl.BlockSpec((N_DEV, tr, cols), lambda i: (0, i, 0)), blk, blk, blk],
        out_specs=[blk] * 4, out_shape=[jax.ShapeDtypeStruct((rows, cols), F32)] * 4,
        compiler_params=_params("parallel"),
    )(stack, w, m, v)


def _lb_all(lb_logits):
    sm = jax.nn.softmax(lb_logits.astype(F32), axis=0)
    return jnp.cumsum(sm, axis=0) - sm[0:1]


def _rope_tables(p):
    half = DH // 2
    inv = jnp.power(ROPE_THETA, -jnp.arange(half, dtype=F32) * 2.0 / DH)
    pos = (jnp.arange(p) - PAD_FRONT).astype(F32)
    ang = pos[:, None] * inv[None, :]
    cos, sin = jnp.cos(ang), jnp.sin(ang)
    return jnp.tile(cos, (1, 4)), jnp.tile(jnp.concatenate([-sin, sin], axis=1), (1, 2))


def _permute_cols(w):
    return jnp.concatenate([w[:, :3072], w[:, 3328:], w[:, 3072:3328]], axis=1)


def _unpermute_cols(w):
    return jnp.concatenate([w[:, :3072], w[:, 5120:], w[:, 3072:5120]], axis=1)


SMALL = (("lb_logits", (DEPTH, 512)), ("norm_g", (DEPTH, D_MODEL)), ("q_norm_g", (DEPTH, DH)),
         ("k_norm_g", (DEPTH, DH)), ("attn_sinks", (DEPTH, HB)), ("hgrn_norm_g", (DEPTH, 128)),
         ("pool_w", (DEPTH, 4, 128, 128)), ("pool_scale", (DEPTH, 512)))


def _pack_small(d):
    flat = jnp.concatenate([d[k].astype(F32).reshape(-1) for k, _ in SMALL])
    pad = (-flat.shape[0]) % (8 * 128)
    return jnp.pad(flat, (0, pad)).reshape(-1, 128)


def _unpack_small(a):
    flat = a.reshape(-1)
    out, off = {}, 0
    for k, shp in SMALL:
        n = int(np.prod(shp))
        out[k] = flat[off:off + n].reshape(shp)
        off += n
    return out


def kernel(x, meta_tokens, lb_logits, norm_g, w_in, q_norm_g, k_norm_g, attn_sinks, hgrn_norm_g, pool_w, pool_scale, w_out, loss_target, m_meta_tokens, m_lb_logits, m_norm_g, m_w_in, m_q_norm_g, m_k_norm_g, m_attn_sinks, m_hgrn_norm_g, m_pool_w, m_pool_scale, m_w_out, v_meta_tokens, v_lb_logits, v_norm_g, v_w_in, v_q_norm_g, v_k_norm_g, v_attn_sinks, v_hgrn_norm_g, v_pool_w, v_pool_scale, v_w_out):
    seq = x.shape[1]
    p = seq + TB
    cs = PROJ_COLS // N_DEV
    rs = MIX // N_DEV
    ms = D_MODEL // N_DEV

    win_all = _exchange(w_in.reshape(DEPTH * D_MODEL, cs).astype(BF16), gather=True, name="gather_w_in")
    wout_all = _exchange(w_out.reshape(DEPTH * rs, D_MODEL).astype(BF16), gather=True, name="gather_w_out")
    meta_all = _exchange(meta_tokens, gather=True, name="gather_meta")
    win_all = win_all.reshape(N_DEV, DEPTH, D_MODEL, cs)
    wout_all = wout_all.reshape(N_DEV, DEPTH, rs, D_MODEL)
    meta_full = meta_all.transpose(1, 0, 2).reshape(N_META, D_MODEL)

    lb_all, lb_vjp = jax.vjp(_lb_all, lb_logits)
    cos, sin = _rope_tables(p)
    layers = []
    for l in range(DEPTH):
        layers.append(dict(
            norm_g=norm_g[l][None], lb=lb_all[l][None],
            w_in=_permute_cols(win_all[:, l].transpose(1, 0, 2).reshape(D_MODEL, PROJ_COLS)),
            w_out=wout_all[:, l].reshape(MIX, D_MODEL),
            gq=jnp.tile(q_norm_g[l], 2)[None], gk=jnp.tile(k_norm_g[l], 2)[None], sinks=attn_sinks[l],
            hg=hgrn_norm_g[l][None], pool_w=pool_w[l].astype(BF16), pool_scale=pool_scale[l][None], cos=cos, sin=sin))

    h = jnp.concatenate([jnp.zeros((PAD_FRONT, D_MODEL), F32), meta_full, x[0]], axis=0)
    saved = []
    for l in range(DEPTH):
        h, s = _layer_fwd(h, layers[l], f"_l{l}")
        saved.append(s)
    dh, sq = _loss_grad(h, loss_target[0], name="loss_grad")
    loss = lax.psum(0.5 * jnp.sum(sq) / D_MODEL, ("x", "y", "c"))

    grads = [None] * DEPTH
    for l in reversed(range(DEPTH)):
        dh, grads[l] = _layer_bwd(dh, saved[l], layers[l], f"_l{l}")
    grad_x = dh[TB:][None]

    dwin = jnp.stack([_unpermute_cols(grads[l]["w_in"]) for l in range(DEPTH)])
    dwin = dwin.reshape(DEPTH * D_MODEL, N_DEV, cs).transpose(1, 0, 2).astype(BF16)
    dwout = jnp.stack([grads[l]["w_out"] for l in range(DEPTH)])
    dwout = dwout.reshape(DEPTH, N_DEV, rs, D_MODEL).transpose(1, 0, 2, 3).reshape(N_DEV, DEPTH * rs, D_MODEL).astype(BF16)
    dmeta = dh[PAD_FRONT:TB].reshape(N_META, N_DEV, ms).transpose(1, 0, 2)
    win_stack = _exchange(dwin, gather=False, name="scatter_dw_in")
    wout_stack = _exchange(dwout, gather=False, name="scatter_dw_out")
    meta_stack = _exchange(dmeta, gather=False, name="scatter_dmeta")

    stk = lambda k: jnp.stack([grads[l][k][0] for l in range(DEPTH)])
    fold = lambda a: a[:, :DH] + a[:, DH:]
    small_local = dict(
        lb_logits=lb_vjp(stk("lb"))[0], norm_g=stk("norm_g"), q_norm_g=fold(stk("gq")), k_norm_g=fold(stk("gk")),
        attn_sinks=stk("sinks")[:, :HB], hgrn_norm_g=stk("hg"),
        pool_w=jnp.stack([grads[l]["pool_w"] for l in range(DEPTH)]), pool_scale=stk("pool_scale"))
    small_stack = _exchange(_pack_small(small_local), gather=True, name="gather_small_grads")

    g_win, d_win, nm_win, nv_win = _adamw(win_stack, w_in.reshape(DEPTH * D_MODEL, cs), m_w_in.reshape(DEPTH * D_MODEL, cs),
                                          v_w_in.reshape(DEPTH * D_MODEL, cs), name="adamw_w_in")
    g_wout, d_wout, nm_wout, nv_wout = _adamw(wout_stack, w_out.reshape(DEPTH * rs, D_MODEL), m_w_out.reshape(DEPTH * rs, D_MODEL),
                                              v_w_out.reshape(DEPTH * rs, D_MODEL), name="adamw_w_out")
    g_meta, d_meta, nm_meta, nv_meta = _adamw(meta_stack, meta_tokens, m_meta_tokens, v_meta_tokens, name="adamw_meta")
    small_w = dict(lb_logits=lb_logits, norm_g=norm_g, q_norm_g=q_norm_g, k_norm_g=k_norm_g, attn_sinks=attn_sinks,
                   hgrn_norm_g=hgrn_norm_g, pool_w=pool_w, pool_scale=pool_scale)
    small_m = dict(lb_logits=m_lb_logits, norm_g=m_norm_g, q_norm_g=m_q_norm_g, k_norm_g=m_k_norm_g, attn_sinks=m_attn_sinks,
                   hgrn_norm_g=m_hgrn_norm_g, pool_w=m_pool_w, pool_scale=m_pool_scale)
    small_v = dict(lb_logits=v_lb_logits, norm_g=v_norm_g, q_norm_g=v_q_norm_g, k_norm_g=v_k_norm_g, attn_sinks=v_attn_sinks,
                   hgrn_norm_g=v_hgrn_norm_g, pool_w=v_pool_w, pool_scale=v_pool_scale)
    small_out = [_unpack_small(a) for a in _adamw(small_stack, _pack_small(small_w), _pack_small(small_m),
                                                  _pack_small(small_v), name="adamw_small")]

    big = dict(
        meta_tokens=(g_meta, d_meta, nm_meta, nv_meta),
        w_in=tuple(a.reshape(DEPTH, D_MODEL, cs) for a in (g_win, d_win, nm_win, nv_win)),
        w_out=tuple(a.reshape(DEPTH, rs, D_MODEL) for a in (g_wout, d_wout, nm_wout, nv_wout)))
    order = ("meta_tokens", "lb_logits", "norm_g", "w_in", "q_norm_g", "k_norm_g", "attn_sinks", "hgrn_norm_g",
             "pool_w", "pool_scale", "w_out")
    outs = [loss, grad_x]
    for kind in range(4):
        for k in order:
            outs.append(big[k][kind] if k in big else small_out[kind][k])
    return tuple(outs)
```

```python
import functools

import numpy as np
import jax
import jax.numpy as jnp
from jax import lax
from jax.experimental import pallas as pl
from jax.experimental.pallas import tpu as pltpu

F32, BF16 = jnp.float32, jnp.bfloat16

D_MODEL = 2048
DEPTH = 2
N_META = 16
TB = 128
PAD_FRONT = TB - N_META
RMS_EPS = 1e-6
NEG_INF = -1e30
LOG_FLOOR = 1e-30
HA, DK_A = 4, 128
CH = 16
NCH = TB // CH
HB, KVH, DH = 16, 2, 64
GRP = HB // KVH
ROPE_THETA = 10000.0
POOL_WINDOWS = (2, 4, 8, 16)
PROJ_COLS = 5376
MIX = 2048
N_DEV = 8
C_QA, C_FA, C_IA, C_GA = 0, 1, 2, 3
C_QB, C_GB = 2, 3
C_UC, C_GC = 8, 9
C_KB, C_VB = 40, 41

ADAM_LR, ADAM_B1, ADAM_B2, ADAM_EPS, ADAM_WD, ADAM_STEP = 0.001, 0.9, 0.999, 1e-08, 0.01, 10

VMEM_LIMIT = 48 * 1024 * 1024

NN = ((1,), (0,))
NT = ((1,), (1,))
TN = ((0,), (0,))


def _dot(a, b, dims):
    return lax.dot_general(a, b, (dims, ((), ())), preferred_element_type=F32)


def _split3(x):
    hi = x.astype(BF16)
    r = x - hi.astype(F32)
    mid = r.astype(BF16)
    lo = (r - mid.astype(F32)).astype(BF16)
    return hi, mid, lo


def _xdot(m01, x):
    hi, mid, lo = _split3(x)
    return _dot(m01, hi, NN) + _dot(m01, mid, NN) + _dot(m01, lo, NN)


def _xdot_r(x, m01):
    hi, mid, lo = _split3(x)
    return _dot(hi, m01, NN) + _dot(mid, m01, NN) + _dot(lo, m01, NN)


def _iota(shape, dim):
    return lax.broadcasted_iota(jnp.int32, shape, dim)


def _params(*sem):
    return pltpu.CompilerParams(dimension_semantics=sem, vmem_limit_bytes=VMEM_LIMIT)


def _row_tile(p, target):
    best = TB
    t = TB
    while t <= target:
        if p % t == 0:
            best = t
        t += TB
    return best


def _col_tile(n, target):
    best = 128
    t = 128
    while t <= target:
        if n % t == 0:
            best = t
        t += 128
    return best


def _sigmoid(x):
    return 1.0 / (1.0 + jnp.exp(-x))


def _mm_nn(a, b, res=None, *, name, tm=1664, tn=768, carry=None):
    m, k = a.shape
    n = b.shape[1]
    tm, tn = _row_tile(m, tm), _col_tile(n, tn)

    def body(*refs):
        if res is None:
            a_ref, b_ref, o_ref = refs
            o_ref[...] = _dot(a_ref[...], b_ref[...], NN)
        else:
            a_ref, b_ref, r_ref, o_ref = refs
            o_ref[...] = r_ref[...] + _dot(a_ref[...], b_ref[...], NN)

    in_specs = [pl.BlockSpec((tm, k), lambda j, i: (i, 0)), pl.BlockSpec((k, tn), lambda j, i: (0, j))]
    args = [a, b]
    if res is not None:
        in_specs.append(pl.BlockSpec((tm, tn), lambda j, i: (i, j)))
        args.append(res)
    return _call(
        body, name=name, grid=(n // tn, m // tm), in_specs=in_specs,
        out_specs=[pl.BlockSpec((tm, tn), lambda j, i: (i, j))],
        out_shape=[jax.ShapeDtypeStruct((m, n), F32)],
        args=args, sem=("parallel", "parallel"), carry=carry)


def _mm_nt(a, b, *, name, tm=640, tn=512, tk=2048):
    m, k = a.shape
    n = b.shape[0]
    tm, tn, tk = _row_tile(m, tm), _col_tile(n, tn), _col_tile(k, tk)

    def body(a_ref, b_ref, o_ref):
        @pl.when(pl.program_id(2) == 0)
        def _():
            o_ref[...] = jnp.zeros_like(o_ref)

        o_ref[...] += _dot(a_ref[...], b_ref[...], NT)

    return pl.pallas_call(
        body, name=name, grid=(n // tn, m // tm, k // tk),
        in_specs=[pl.BlockSpec((tm, tk), lambda j, i, kk: (i, kk)), pl.BlockSpec((tn, tk), lambda j, i, kk: (j, kk))],
        out_specs=pl.BlockSpec((tm, tn), lambda j, i, kk: (i, j)),
        out_shape=jax.ShapeDtypeStruct((m, n), F32),
        compiler_params=_params("parallel", "parallel", "arbitrary"),
    )(a, b)


def _mm_tn(a, b, *, name, tm=1024, tn=1344, tk=1664):
    k, m = a.shape
    n = b.shape[1]
    tm, tn, tk = _col_tile(m, tm), _col_tile(n, tn), _row_tile(k, tk)

    def body(a_ref, b_ref, o_ref):
        @pl.when(pl.program_id(2) == 0)
        def _():
            o_ref[...] = jnp.zeros_like(o_ref)

        o_ref[...] += _dot(a_ref[...], b_ref[...], TN)

    return pl.pallas_call(
        body, name=name, grid=(m // tm, n // tn, k // tk),
        in_specs=[pl.BlockSpec((tk, tm), lambda i, j, kk: (kk, i)), pl.BlockSpec((tk, tn), lambda i, j, kk: (kk, j))],
        out_specs=pl.BlockSpec((tm, tn), lambda i, j, kk: (i, j)),
        out_shape=jax.ShapeDtypeStruct((m, n), F32),
        compiler_params=_params("parallel", "parallel", "arbitrary"),
    )(a, b)


def _rmsnorm_fwd(h, g, *, name):
    p, dm = h.shape
    tm = _row_tile(p, 640)

    def body(h_ref, g_ref, xn_ref):
        hv = h_ref[...]
        r = lax.rsqrt(jnp.mean(hv * hv, axis=-1, keepdims=True) + RMS_EPS)
        xn_ref[...] = (hv * r * g_ref[...]).astype(BF16)

    return pl.pallas_call(
        body, name=name, grid=(p // tm,),
        in_specs=[pl.BlockSpec((tm, dm), lambda i: (i, 0)), pl.BlockSpec((1, dm), lambda i: (0, 0))],
        out_specs=pl.BlockSpec((tm, dm), lambda i: (i, 0)),
        out_shape=jax.ShapeDtypeStruct((p, dm), BF16),
        compiler_params=_params("parallel"),
    )(h, g)


def _rmsnorm_bwd(dxn, h, g, dh_out, *, name, carry=None):
    p, dm = h.shape
    tm = _row_tile(p, 384)

    def body(dxn_ref, h_ref, g_ref, dho_ref, dh_ref, dg_ref):
        hv = h_ref[...]
        r = lax.rsqrt(jnp.mean(hv * hv, axis=-1, keepdims=True) + RMS_EPS)
        xh = hv * r
        dy = dxn_ref[...]
        dyn = dy * g_ref[...]
        dh_ref[...] = dho_ref[...] + r * (dyn - xh * jnp.mean(dyn * xh, axis=-1, keepdims=True))

        @pl.when(pl.program_id(0) == 0)
        def _():
            dg_ref[...] = jnp.zeros_like(dg_ref)

        dg_ref[...] += jnp.sum(dy * xh, axis=0, keepdims=True)

    row = pl.BlockSpec((tm, dm), lambda i: (i, 0))
    vec = pl.BlockSpec((1, dm), lambda i: (0, 0))
    return _call(
        body, name=name, grid=(p // tm,),
        in_specs=[row, row, vec, row], out_specs=[row, vec],
        out_shape=[jax.ShapeDtypeStruct((p, dm), F32), jax.ShapeDtypeStruct((1, dm), F32)],
        args=(dxn, h, g, dh_out), sem=("arbitrary",), carry=carry)


def _loss_grad(h, target, *, name):
    p, dm = h.shape

    def body(h_ref, t_ref, dh_ref, sq_ref):
        n = pl.program_id(0)

        @pl.when(n == 0)
        def _():
            dh_ref[...] = jnp.zeros_like(dh_ref)
            sq_ref[...] = jnp.zeros_like(sq_ref)

        @pl.when(n > 0)
        def _():
            err = h_ref[...] - t_ref[...]
            dh_ref[...] = err * (1.0 / dm)
            sq_ref[...] += jnp.sum(err * err, axis=0, keepdims=True)

    return pl.pallas_call(
        body, name=name, grid=(p // TB,),
        in_specs=[pl.BlockSpec((TB, dm), lambda n: (n, 0)), pl.BlockSpec((TB, dm), lambda n: (jnp.maximum(n - 1, 0), 0))],
        out_specs=[pl.BlockSpec((TB, dm), lambda n: (n, 0)), pl.BlockSpec((1, dm), lambda n: (0, 0))],
        out_shape=[jax.ShapeDtypeStruct((p, dm), F32), jax.ShapeDtypeStruct((1, dm), F32)],
        compiler_params=_params("arbitrary"),
    )(h, target)


def _chunk_masks():
    ri, ci = _iota((TB, TB), 0), _iota((TB, TB), 1)
    same = (ri >> 4) == (ci >> 4)
    causal = same & (ci <= ri)
    lower = jnp.where(causal, 1.0, 0.0).astype(BF16)
    upper = jnp.where(same & (ci >= ri), 1.0, 0.0).astype(BF16)
    ones = jnp.where(same, 1.0, 0.0).astype(BF16)
    return causal, lower, upper, ones


def _hgrn_gates(q, z, lbh, m):
    sig = _sigmoid(z)
    f = lbh + (1.0 - lbh) * sig
    lf = jnp.log(jnp.maximum(f, LOG_FLOOR)) * m
    kk = (1.0 - lbh) * (1.0 - sig) * m
    sq = _sigmoid(q)
    return sig, f, lf, kk, sq, q * sq


def _hgrn_fwd(proj, lb, *, name):
    p = proj.shape[0]
    nb = p // TB

    def body(qa_ref, fa_ref, ia_ref, lb_ref, oa_ref, sck_ref, st_ref):
        n = pl.program_id(0)

        @pl.when(n == 0)
        def _():
            st_ref[...] = jnp.zeros_like(st_ref)

        causal, lower, _, ones = _chunk_masks()
        m = ((n * TB + _iota((TB, 1), 0)) >= PAD_FRONT).astype(F32)
        for hd in range(HA):
            sl = slice(hd * DK_A, (hd + 1) * DK_A)
            q, z, v, lbh = qa_ref[:, sl], fa_ref[:, sl], ia_ref[:, sl], lb_ref[:, sl]
            _, _, lf, kk, _, qf = _hgrn_gates(q, z, lbh, m)
            g = _xdot(lower, lf)
            gl = _xdot(ones, lf)
            qd = (qf * jnp.exp(g)).astype(BF16)
            kt = (kk * jnp.exp(-g)).astype(BF16)
            kd = (kk * jnp.exp(gl - g)).astype(BF16)
            vb = v.astype(BF16)
            a_all = jnp.exp(gl)
            att = jnp.where(causal, _dot(qd, kt, NT), 0.0).astype(BF16)
            o = _dot(att, vb, NN)
            st = st_ref[hd]
            sck_ref[0, hd] = st
            outs = []
            for c in range(NCH):
                r = slice(c * CH, (c + 1) * CH)
                outs.append(_dot(qd[r], st.astype(BF16), NT))
                st = st * a_all[c * CH:c * CH + 1, :] + _dot(vb[r], kd[r], TN)
            st_ref[hd] = st
            oa_ref[:, sl] = o + jnp.concatenate(outs, axis=0)

    blk = lambda c: pl.BlockSpec((TB, 512), lambda n, c=c: (n, c))
    return pl.pallas_call(
        body, name=name, grid=(nb,),
        in_specs=[blk(C_QA), blk(C_FA), blk(C_IA), pl.BlockSpec((1, 512), lambda n: (0, 0))],
        out_specs=[pl.BlockSpec((TB, 512), lambda n: (n, 0)), pl.BlockSpec((1, HA, TB, TB), lambda n: (n, 0, 0, 0))],
        out_shape=[jax.ShapeDtypeStruct((p, 512), F32), jax.ShapeDtypeStruct((nb, HA, TB, TB), F32)],
        scratch_shapes=[pltpu.VMEM((HA, TB, TB), F32)],
        compiler_params=_params("arbitrary"),
    )(proj, proj, proj, lb)


def _hgrn_bwd(proj, lb, sck, d_oa, *, name, carry=None):
    p = proj.shape[0]
    nb = p // TB

    def body(qa_ref, fa_ref, ia_ref, lb_ref, sck_ref, do_ref, dq_ref, dz_ref, dv_ref, dlb_ref,
             dst_ref, stc_ref, dqd_ref, dkd_ref, dvs_ref, dgl_ref):
        i = pl.program_id(0)
        n = nb - 1 - i

        @pl.when(i == 0)
        def _():
            dst_ref[...] = jnp.zeros_like(dst_ref)
            dlb_ref[...] = jnp.zeros_like(dlb_ref)

        causal, lower, upper, ones = _chunk_masks()
        m = ((n * TB + _iota((TB, 1), 0)) >= PAD_FRONT).astype(F32)
        for hd in range(HA):
            sl = slice(hd * DK_A, (hd + 1) * DK_A)
            q, z, v, lbh = qa_ref[:, sl], fa_ref[:, sl], ia_ref[:, sl], lb_ref[:, sl]
            sig, f, lf, kk, sq, qf = _hgrn_gates(q, z, lbh, m)
            g = _xdot(lower, lf)
            gl = _xdot(ones, lf)
            e_g, e_ng, e_d = jnp.exp(g), jnp.exp(-g), jnp.exp(gl - g)
            qd_f, kt_f, kd_f = qf * e_g, kk * e_ng, kk * e_d
            qd, kt, kd = qd_f.astype(BF16), kt_f.astype(BF16), kd_f.astype(BF16)
            vb = v.astype(BF16)
            a_all = jnp.exp(gl)
            att = jnp.where(causal, _dot(qd, kt, NT), 0.0).astype(BF16)
            do = do_ref[:, sl]
            dob = do.astype(BF16)

            st = sck_ref[0, hd]
            for c in range(NCH):
                r = slice(c * CH, (c + 1) * CH)
                stc_ref[c] = st
                if c + 1 < NCH:
                    st = st * a_all[c * CH:c * CH + 1, :] + _dot(vb[r], kd[r], TN)

            d_att = jnp.where(causal, _dot(dob, vb, NT), 0.0).astype(BF16)
            d_v = _dot(att, dob, TN)
            d_qd = _dot(d_att, kt, NN)
            d_kt = _dot(d_att, qd, TN)

            dst = dst_ref[hd]
            for c in range(NCH - 1, -1, -1):
                r = slice(c * CH, (c + 1) * CH)
                stc = stc_ref[c]
                a_c = a_all[c * CH:c * CH + 1, :]
                dstb = dst.astype(BF16)
                dvs_ref[r, :] = _dot(kd[r], dstb, NT)
                dkd_ref[r, :] = _dot(vb[r], dstb, NN)
                d_a = jnp.sum(dst * stc, axis=0, keepdims=True)
                dgl_ref[r, :] = jnp.broadcast_to(d_a * a_c, (CH, TB))
                dqd_ref[r, :] = _dot(dob[r], stc.astype(BF16), NN)
                dst = dst * a_c + _dot(dob[r], qd[r], TN)
            dst_ref[hd] = dst

            d_qd = d_qd + dqd_ref[...]
            d_kd = dkd_ref[...]
            d_v = d_v + dvs_ref[...]
            kd_term = d_kd * kd_f
            d_g = d_qd * qd_f - d_kt * kt_f - kd_term
            d_lf = _xdot(upper, d_g) + _xdot(ones, kd_term) + dgl_ref[...]
            d_kk = (d_kt * e_ng + d_kd * e_d) * m
            t1 = d_lf * m * jnp.where(f > LOG_FLOOR, 1.0 / f, 0.0)
            dq_ref[:, sl] = d_qd * e_g * (sq * (1.0 + q * (1.0 - sq)))
            dz_ref[:, sl] = (t1 - d_kk) * (1.0 - lbh) * sig * (1.0 - sig)
            dv_ref[:, sl] = d_v
            dlb_ref[:, sl] += jnp.sum((t1 - d_kk) * (1.0 - sig), axis=0, keepdims=True)

    blk = lambda c: pl.BlockSpec((TB, 512), lambda i, c=c: (nb - 1 - i, c))
    out_blk = pl.BlockSpec((TB, 512), lambda i: (nb - 1 - i, 0))
    vec = pl.BlockSpec((1, 512), lambda i: (0, 0))
    return _call(
        body, name=name, grid=(nb,),
        in_specs=[blk(C_QA), blk(C_FA), blk(C_IA), vec,
                  pl.BlockSpec((1, HA, TB, TB), lambda i: (nb - 1 - i, 0, 0, 0)), out_blk],
        out_specs=[out_blk, out_blk, out_blk, vec],
        out_shape=[jax.ShapeDtypeStruct((p, 512), F32)] * 3 + [jax.ShapeDtypeStruct((1, 512), F32)],
        scratch_shapes=[pltpu.VMEM((HA, TB, TB), F32), pltpu.VMEM((NCH, TB, TB), F32)]
        + [pltpu.VMEM((TB, TB), F32)] * 4,
        args=(proj, proj, proj, lb, sck, d_oa), sem=("arbitrary",), carry=carry)


def _lane():
    return _iota((1, TB), 1)


def _swap_halves(y):
    first = (_lane() & 63) < 32
    return jnp.where(first, pltpu.roll(y, 96, 1), pltpu.roll(y, 32, 1))


def _head_ones():
    ri, ci = _iota((TB, TB), 0), _iota((TB, TB), 1)
    return jnp.where((ri >> 6) == (ci >> 6), 1.0, 0.0).astype(BF16)


def _norm_rope(x, g, cos, sin, bd):
    r = lax.rsqrt(_xdot_r(x * x, bd) * (1.0 / DH) + RMS_EPS)
    y = x * r * g
    return y * cos + _swap_halves(y) * sin


def _norm_rope_bwd(d_out, x, g, cos, sin, bd):
    d = d_out * cos - _swap_halves(d_out) * sin
    r = lax.rsqrt(_xdot_r(x * x, bd) * (1.0 / DH) + RMS_EPS)
    xh = x * r
    dyn = d * g
    dx = r * (dyn - xh * (_xdot_r(dyn * xh, bd) * (1.0 / DH)))
    return dx, jnp.sum(d * xh, axis=0, keepdims=True)


def _dup_heads(k):
    first = _lane() < DH
    r = pltpu.roll(k, DH, 1)
    return jnp.where(first, k, r), jnp.where(first, r, k)


def _qk_prep(proj, gq, gk, cos, sin, *, name):
    p = proj.shape[0]

    def body(qb_ref, kb_ref, vb_ref, gq_ref, gk_ref, cos_ref, sin_ref, qh_ref, k2_ref, v2_ref):
        bd = _head_ones()
        cos_v, sin_v = cos_ref[...], sin_ref[...]
        for j in range(HB // 2):
            sl = slice(j * TB, (j + 1) * TB)
            qh_ref[:, sl] = _norm_rope(qb_ref[:, sl], gq_ref[...], cos_v, sin_v, bd).astype(BF16)
        k0, k1 = _dup_heads(_norm_rope(kb_ref[...], gk_ref[...], cos_v, sin_v, bd))
        k2_ref[:, 0:TB] = k0.astype(BF16)
        k2_ref[:, TB:2 * TB] = k1.astype(BF16)
        v0, v1 = _dup_heads(vb_ref[...])
        v2_ref[:, 0:TB] = v0.astype(BF16)
        v2_ref[:, TB:2 * TB] = v1.astype(BF16)

    vec = pl.BlockSpec((1, TB), lambda n: (0, 0))
    tab = pl.BlockSpec((TB, TB), lambda n: (n, 0))
    return pl.pallas_call(
        body, name=name, grid=(p // TB,),
        in_specs=[pl.BlockSpec((TB, 1024), lambda n: (n, C_QB)), pl.BlockSpec((TB, TB), lambda n: (n, C_KB)),
                  pl.BlockSpec((TB, TB), lambda n: (n, C_VB)), vec, vec, tab, tab],
        out_specs=[pl.BlockSpec((TB, 1024), lambda n: (n, 0)), pl.BlockSpec((TB, 256), lambda n: (n, 0)),
                   pl.BlockSpec((TB, 256), lambda n: (n, 0))],
        out_shape=[jax.ShapeDtypeStruct((p, 1024), BF16), jax.ShapeDtypeStruct((p, 256), BF16),
                   jax.ShapeDtypeStruct((p, 256), BF16)],
        compiler_params=_params("parallel"),
    )(proj, proj, proj, gq, gk, cos, sin)


NKEY = N_META + 2 * TB


def _attn_mask(n):
    r = _iota((TB, NKEY), 0)
    j = _iota((TB, NKEY), 1)
    meta = (j < N_META) & ((n >= 1) | (j + PAD_FRONT <= r))
    prev = (j >= N_META) & (j < N_META + TB) & (n >= 2) & (j - N_META > r)
    cur = (j >= N_META + TB) & (n >= 1) & (j - (N_META + TB) <= r)
    return meta | prev | cur


def _attn_specs():
    cur = lambda w: pl.BlockSpec((TB, w), lambda n: (n, 0))
    prev = pl.BlockSpec((TB, 256), lambda n: (jnp.maximum(n - 1, 0), 0))
    meta = pl.BlockSpec((N_META, 256), lambda n: (PAD_FRONT // N_META, 0))
    sink = pl.BlockSpec(memory_space=pltpu.SMEM)
    return cur, prev, meta, sink


def _softmax_sink(qm, kall, mask, sink):
    s = _dot(qm, kall, NT) * (DH ** -0.5)
    s = jnp.where(mask, s, NEG_INF)
    mx = jnp.maximum(jnp.max(s, axis=1, keepdims=True), sink)
    pr = jnp.exp(s - mx)
    inv = 1.0 / (jnp.sum(pr, axis=1, keepdims=True) + jnp.exp(sink - mx))
    return pr * inv, jnp.exp(sink - mx) * inv


def _attn_fwd(qh, k2, v2, sinks, *, name, carry=None):
    p = qh.shape[0]

    def body(sink_ref, q_ref, kc_ref, kp_ref, km_ref, vc_ref, vp_ref, vm_ref, o_ref):
        n = pl.program_id(0)
        mask = _attn_mask(n)
        first = _lane() < DH
        for kv in range(KVH):
            ks = slice(kv * TB, (kv + 1) * TB)
            kall = jnp.concatenate([km_ref[:, ks], kp_ref[:, ks], kc_ref[:, ks]], axis=0)
            vall = jnp.concatenate([vm_ref[:, ks], vp_ref[:, ks], vc_ref[:, ks]], axis=0)
            for jj in range(GRP // 2):
                j = kv * (GRP // 2) + jj
                sl = slice(j * TB, (j + 1) * TB)
                qj = q_ref[:, sl]
                halves = []
                for half in range(2):
                    qm = jnp.where(first if half == 0 else ~first, qj, jnp.zeros_like(qj))
                    w, _ = _softmax_sink(qm, kall, mask, sink_ref[2 * j + half])
                    halves.append(_dot(w.astype(BF16), vall, NN))
                o_ref[:, sl] = jnp.where(first, halves[0], halves[1])

    cur, prev, meta, sink = _attn_specs()
    return _call(
        body, name=name, grid=(p // TB,),
        in_specs=[sink, cur(1024), cur(256), prev, meta, cur(256), prev, meta],
        out_specs=[cur(1024)],
        out_shape=[jax.ShapeDtypeStruct((p, 1024), F32)],
        args=(sinks, qh, k2, k2, k2, v2, v2, v2), sem=("parallel",), carry=carry)


def _attn_bwd(qh, k2, v2, sinks, o, d_o, *, name, carry=None):
    p = qh.shape[0]

    def body(sink_ref, q_ref, kc_ref, kp_ref, km_ref, vc_ref, vp_ref, vm_ref, o_ref, do_ref,
             dq_ref, dkc_ref, dkp_ref, dvc_ref, dvp_ref, dkm_ref, dvm_ref, dsink_ref):
        n = pl.program_id(0)

        @pl.when(n == 0)
        def _():
            dkm_ref[...] = jnp.zeros_like(dkm_ref)
            dvm_ref[...] = jnp.zeros_like(dvm_ref)
            dsink_ref[...] = jnp.zeros_like(dsink_ref)

        mask = _attn_mask(n)
        lane = _lane()
        first = lane < DH
        sink_acc = jnp.zeros((TB, TB), F32)
        for kv in range(KVH):
            ks = slice(kv * TB, (kv + 1) * TB)
            kall = jnp.concatenate([km_ref[:, ks], kp_ref[:, ks], kc_ref[:, ks]], axis=0)
            vall = jnp.concatenate([vm_ref[:, ks], vp_ref[:, ks], vc_ref[:, ks]], axis=0)
            d_kall = jnp.zeros((NKEY, TB), F32)
            d_vall = jnp.zeros((NKEY, TB), F32)
            for jj in range(GRP // 2):
                j = kv * (GRP // 2) + jj
                sl = slice(j * TB, (j + 1) * TB)
                qj, oj, doj = q_ref[:, sl], o_ref[:, sl], do_ref[:, sl]
                halves = []
                for half in range(2):
                    sel = first if half == 0 else ~first
                    qm = jnp.where(sel, qj, jnp.zeros_like(qj))
                    dom = jnp.where(sel, doj, 0.0)
                    w, w_sink = _softmax_sink(qm, kall, mask, sink_ref[2 * j + half])
                    delta = jnp.sum(dom * oj, axis=1, keepdims=True)
                    domb = dom.astype(BF16)
                    d_w = _dot(domb, vall, NT)
                    dsb = (w * (d_w - delta) * (DH ** -0.5)).astype(BF16)
                    sink_acc = sink_acc + jnp.where(lane == 2 * j + half, -(w_sink * delta), 0.0)
                    halves.append(_dot(dsb, kall, NN))
                    d_kall = d_kall + _dot(dsb, qm, TN)
                    d_vall = d_vall + _dot(w.astype(BF16), domb, TN)
                dq_ref[:, sl] = jnp.where(first, halves[0], halves[1])
            dkm_ref[:, ks] += d_kall[0:N_META]
            dkp_ref[:, ks] = d_kall[N_META:N_META + TB]
            dkc_ref[:, ks] = d_kall[N_META + TB:NKEY]
            dvm_ref[:, ks] += d_vall[0:N_META]
            dvp_ref[:, ks] = d_vall[N_META:N_META + TB]
            dvc_ref[:, ks] = d_vall[N_META + TB:NKEY]
        dsink_ref[...] += jnp.sum(sink_acc, axis=0, keepdims=True)

    cur, prev, meta, sink = _attn_specs()
    acc = lambda r: pl.BlockSpec((r, 256), lambda n: (0, 0))
    return _call(
        body, name=name, grid=(p // TB,),
        in_specs=[sink, cur(1024), cur(256), prev, meta, cur(256), prev, meta, cur(1024), cur(1024)],
        out_specs=[cur(1024), cur(256), cur(256), cur(256), cur(256), acc(N_META), acc(N_META),
                   pl.BlockSpec((1, TB), lambda n: (0, 0))],
        out_shape=[jax.ShapeDtypeStruct((p, 1024), F32)] + [jax.ShapeDtypeStruct((p, 256), F32)] * 4
        + [jax.ShapeDtypeStruct((N_META, 256), F32)] * 2 + [jax.ShapeDtypeStruct((1, TB), F32)],
        args=(sinks, qh, k2, k2, k2, v2, v2, v2, o, d_o), sem=("arbitrary",), carry=carry)


def _qk_post(proj, gq, gk, cos, sin, dqh, dkc, dkp, dkm, dvc, dvp, dvm, *, name):
    p = proj.shape[0]
    nb = p // TB

    def body(qb_ref, kb_ref, gq_ref, gk_ref, cos_ref, sin_ref, dqh_ref, dkc_ref, dkp_ref, dkm_ref,
             dvc_ref, dvp_ref, dvm_ref, dqb_ref, dkb_ref, dvb_ref, dgq_ref, dgk_ref, tk_ref, tv_ref):
        n = pl.program_id(0)

        @pl.when(n == 0)
        def _():
            dgq_ref[...] = jnp.zeros_like(dgq_ref)
            dgk_ref[...] = jnp.zeros_like(dgk_ref)

        keep = jnp.where(n == nb - 1, 0.0, 1.0)
        tk_ref[...] = dkc_ref[...] + keep * dkp_ref[...]
        tv_ref[...] = dvc_ref[...] + keep * dvp_ref[...]

        @pl.when(n == 0)
        def _():
            tk_ref[PAD_FRONT:TB, :] += dkm_ref[...]
            tv_ref[PAD_FRONT:TB, :] += dvm_ref[...]

        first = _lane() < DH

        def fold(t_ref):
            t0, t1 = t_ref[:, 0:TB], t_ref[:, TB:2 * TB]
            return jnp.where(first, t0 + pltpu.roll(t0, DH, 1), t1 + pltpu.roll(t1, DH, 1))

        bd = _head_ones()
        cos_v, sin_v = cos_ref[...], sin_ref[...]
        dvb_ref[...] = fold(tv_ref)
        dkb, dgk = _norm_rope_bwd(fold(tk_ref), kb_ref[...], gk_ref[...], cos_v, sin_v, bd)
        dkb_ref[...] = dkb
        dgk_ref[...] += dgk
        dgq = jnp.zeros((1, TB), F32)
        for j in range(HB // 2):
            sl = slice(j * TB, (j + 1) * TB)
            dqb, dg = _norm_rope_bwd(dqh_ref[:, sl], qb_ref[:, sl], gq_ref[...], cos_v, sin_v, bd)
            dqb_ref[:, sl] = dqb
            dgq = dgq + dg
        dgq_ref[...] += dgq

    vec = pl.BlockSpec((1, TB), lambda n: (0, 0))
    tab = pl.BlockSpec((TB, TB), lambda n: (n, 0))
    cur = lambda w: pl.BlockSpec((TB, w), lambda n: (n, 0))
    nxt = pl.BlockSpec((TB, 256), lambda n: (jnp.minimum(n + 1, nb - 1), 0))
    meta = pl.BlockSpec((N_META, 256), lambda n: (0, 0))
    return pl.pallas_call(
        body, name=name, grid=(nb,),
        in_specs=[pl.BlockSpec((TB, 1024), lambda n: (n, C_QB)), pl.BlockSpec((TB, TB), lambda n: (n, C_KB)),
                  vec, vec, tab, tab, cur(1024), cur(256), nxt, meta, cur(256), nxt, meta],
        out_specs=[cur(1024), cur(TB), cur(TB), vec, vec],
        out_shape=[jax.ShapeDtypeStruct((p, 1024), F32), jax.ShapeDtypeStruct((p, TB), F32),
                   jax.ShapeDtypeStruct((p, TB), F32), jax.ShapeDtypeStruct((1, TB), F32),
                   jax.ShapeDtypeStruct((1, TB), F32)],
        scratch_shapes=[pltpu.VMEM((TB, 256), F32), pltpu.VMEM((TB, 256), F32)],
        compiler_params=_params("arbitrary"),
    )(proj, proj, gq, gk, cos, sin, dqh, dkc, dkp, dkm, dvc, dvp, dvm)


EXT = TB + N_META


def _pool_count_inv(n, w):
    t = n * TB + _iota((TB, 1), 0)
    cnt = jnp.clip(t - (PAD_FRONT - 1), 1, w)
    return 1.0 / cnt.astype(F32)


def _silu_parts(gate):
    s = _sigmoid(gate)
    return gate * s, s * (1.0 + gate * (1.0 - s))


def _mix_fwd(proj, oa, yb, hg, pool_w, pool_scale, *, name):
    p = proj.shape[0]

    def body(ga_ref, gb_ref, uc_ref, up_ref, gc_ref, oa_ref, yb_ref, hg_ref, pw_ref, ps_ref, mx_ref, pooled_ref):
        n = pl.program_id(0)
        valid = ((n * TB + _iota((TB, 1), 0)) >= PAD_FRONT).astype(F32)
        for hd in range(HA):
            sl = slice(hd * TB, (hd + 1) * TB)
            o = oa_ref[:, sl]
            r = lax.rsqrt(jnp.mean(o * o, axis=-1, keepdims=True) + RMS_EPS)
            act, _ = _silu_parts(ga_ref[:, sl])
            mx_ref[:, sl] = (o * r * hg_ref[...] * act).astype(BF16)
        for j in range(HB // 2):
            sl = slice(j * TB, (j + 1) * TB)
            act, _ = _silu_parts(gb_ref[:, sl])
            mx_ref[:, 512 + j * TB:512 + (j + 1) * TB] = (yb_ref[:, sl] * act).astype(BF16)
        ri, ci = _iota((TB, EXT), 0), _iota((TB, EXT), 1)
        has_prev = jnp.where(n == 0, 0.0, 1.0)
        for gi, w in enumerate(POOL_WINDOWS):
            sl = slice(gi * TB, (gi + 1) * TB)
            ug = uc_ref[:, sl] * valid
            ext = jnp.concatenate([up_ref[:, sl] * has_prev, ug], axis=0)
            band = jnp.where((ci <= ri + N_META) & (ci > ri + N_META - w), 1.0, 0.0).astype(BF16)
            pooled = (_xdot(band, ext) * _pool_count_inv(n, w) - ug) * valid
            pooled_ref[:, sl] = pooled
            yc = _dot(pooled.astype(BF16), pw_ref[gi], NN) * ps_ref[:, sl]
            act, _ = _silu_parts(gc_ref[:, sl])
            mx_ref[:, 1536 + gi * TB:1536 + (gi + 1) * TB] = (yc * act).astype(BF16)

    cur = lambda w, c=0: pl.BlockSpec((TB, w), lambda n, c=c: (n, c))
    prev16 = pl.BlockSpec((N_META, 512), lambda n: (jnp.maximum(n * (TB // N_META) - 1, 0), C_UC))
    return pl.pallas_call(
        body, name=name, grid=(p // TB,),
        in_specs=[cur(512, C_GA), cur(1024, C_GB), cur(512, C_UC), prev16, cur(512, C_GC), cur(512), cur(1024),
                  pl.BlockSpec((1, TB), lambda n: (0, 0)), pl.BlockSpec((4, TB, TB), lambda n: (0, 0, 0)),
                  pl.BlockSpec((1, 512), lambda n: (0, 0))],
        out_specs=[cur(MIX), cur(512)],
        out_shape=[jax.ShapeDtypeStruct((p, MIX), BF16), jax.ShapeDtypeStruct((p, 512), F32)],
        compiler_params=_params("parallel"),
    )(proj, proj, proj, proj, proj, oa, yb, hg, pool_w, pool_scale)


def _mix_bwd(proj, oa, yb, pooled, hg, pool_w, pool_scale, d_mixed, *, name):
    p = proj.shape[0]

    def body(ga_ref, gb_ref, gc_ref, oa_ref, yb_ref, pooled_ref, hg_ref, pw_ref, ps_ref, dm_ref,
             doa_ref, dyb_ref, dga_ref, dgb_ref, dgc_ref, dp_ref, dhg_ref, dpw_ref, dps_ref):
        n = pl.program_id(0)

        @pl.when(n == 0)
        def _():
            dhg_ref[...] = jnp.zeros_like(dhg_ref)
            dpw_ref[...] = jnp.zeros_like(dpw_ref)
            dps_ref[...] = jnp.zeros_like(dps_ref)

        valid = ((n * TB + _iota((TB, 1), 0)) >= PAD_FRONT).astype(F32)
        dhg = jnp.zeros((1, TB), F32)
        for hd in range(HA):
            sl = slice(hd * TB, (hd + 1) * TB)
            o = oa_ref[:, sl]
            r = lax.rsqrt(jnp.mean(o * o, axis=-1, keepdims=True) + RMS_EPS)
            on = o * r
            gate = ga_ref[:, sl]
            act, dact = _silu_parts(gate)
            dmx = dm_ref[:, sl]
            d_ya = dmx * act
            dga_ref[:, sl] = dmx * on * hg_ref[...] * dact
            dyn = d_ya * hg_ref[...]
            doa_ref[:, sl] = r * (dyn - on * jnp.mean(dyn * on, axis=-1, keepdims=True))
            dhg = dhg + jnp.sum(d_ya * on, axis=0, keepdims=True)
        dhg_ref[...] += dhg
        for j in range(HB // 2):
            sl = slice(j * TB, (j + 1) * TB)
            act, dact = _silu_parts(gb_ref[:, sl])
            dmx = dm_ref[:, 512 + j * TB:512 + (j + 1) * TB]
            dyb_ref[:, sl] = dmx * act
            dgb_ref[:, sl] = dmx * yb_ref[:, sl] * dact
        for gi in range(len(POOL_WINDOWS)):
            sl = slice(gi * TB, (gi + 1) * TB)
            pooled = pooled_ref[:, sl]
            pooled_b = pooled.astype(BF16)
            t = _dot(pooled_b, pw_ref[gi], NN)
            act, dact = _silu_parts(gc_ref[:, sl])
            dmx = dm_ref[:, 1536 + gi * TB:1536 + (gi + 1) * TB]
            d_yc = dmx * act
            dgc_ref[:, sl] = dmx * t * ps_ref[:, sl] * dact
            dps_ref[:, sl] += jnp.sum(d_yc * t, axis=0, keepdims=True)
            d_t = (d_yc * ps_ref[:, sl]).astype(BF16)
            dp_ref[:, sl] = _dot(d_t, pw_ref[gi], NT) * valid
            dpw_ref[gi] += _dot(pooled_b, d_t, TN)

    cur = lambda w, c=0: pl.BlockSpec((TB, w), lambda n, c=c: (n, c))
    return pl.pallas_call(
        body, name=name, grid=(p // TB,),
        in_specs=[cur(512, C_GA), cur(1024, C_GB), cur(512, C_GC), cur(512), cur(1024), cur(512),
                  pl.BlockSpec((1, TB), lambda n: (0, 0)), pl.BlockSpec((4, TB, TB), lambda n: (0, 0, 0)),
                  pl.BlockSpec((1, 512), lambda n: (0, 0)), cur(MIX)],
        out_specs=[cur(512), cur(1024), cur(512), cur(1024), cur(512), cur(512),
                   pl.BlockSpec((1, TB), lambda n: (0, 0)), pl.BlockSpec((4, TB, TB), lambda n: (0, 0, 0)),
                   pl.BlockSpec((1, 512), lambda n: (0, 0))],
        out_shape=[jax.ShapeDtypeStruct((p, 512), F32), jax.ShapeDtypeStruct((p, 1024), F32),
                   jax.ShapeDtypeStruct((p, 512), F32), jax.ShapeDtypeStruct((p, 1024), F32),
                   jax.ShapeDtypeStruct((p, 512), F32), jax.ShapeDtypeStruct((p, 512), F32),
                   jax.ShapeDtypeStruct((1, TB), F32), jax.ShapeDtypeStruct((4, TB, TB), F32),
                   jax.ShapeDtypeStruct((1, 512), F32)],
        compiler_params=_params("arbitrary"),
    )(proj, proj, proj, oa, yb, pooled, hg, pool_w, pool_scale, d_mixed)


def _pool_bwd(dp, *, name):
    p = dp.shape[0]
    nb = p // TB

    def body(dp_ref, dn_ref, duc_ref):
        n = pl.program_id(0)
        valid = ((n * TB + _iota((TB, 1), 0)) >= PAD_FRONT).astype(F32)
        has_next = jnp.where(n == nb - 1, 0.0, 1.0)
        ri, ci = _iota((TB, EXT), 0), _iota((TB, EXT), 1)
        for gi, w in enumerate(POOL_WINDOWS):
            sl = slice(gi * TB, (gi + 1) * TB)
            d_p = dp_ref[:, sl]
            ext = jnp.concatenate([d_p * _pool_count_inv(n, w), dn_ref[:, sl] * (has_next / w)], axis=0)
            band = jnp.where((ci >= ri) & (ci < ri + w), 1.0, 0.0).astype(BF16)
            duc_ref[:, sl] = (_xdot(band, ext) - d_p) * valid

    return pl.pallas_call(
        body, name=name, grid=(nb,),
        in_specs=[pl.BlockSpec((TB, 512), lambda n: (n, 0)),
                  pl.BlockSpec((N_META, 512), lambda n: (jnp.minimum(n + 1, nb - 1) * (TB // N_META), 0))],
        out_specs=pl.BlockSpec((TB, 512), lambda n: (n, 0)),
        out_shape=jax.ShapeDtypeStruct((p, 512), F32),
        compiler_params=_params("parallel"),
    )(dp, dp)


def _carried(carries, key, local, fn, *args, **kw):
    if key not in carries:
        return fn(*args, **kw)
    make_src, gather, done = carries[key]
    *outs, stack = fn(*args, carry=(make_src(local), gather), **kw)
    done(stack)
    return outs


def _layer_fwd(h, w, tag, carries):
    xn = _rmsnorm_fwd(h, w["norm_g"], name=f"rmsnorm_fwd{tag}")
    proj, = _carried(carries, "in_proj", None, _mm_nn, xn, w["w_in"], name=f"in_proj{tag}")
    oa, sck = _hgrn_fwd(proj, w["lb"], name=f"hgrn_fwd{tag}")
    qh, k2, v2 = _qk_prep(proj, w["gq"], w["gk"], w["cos"], w["sin"], name=f"qk_prep{tag}")
    yb, = _carried(carries, "attn_fwd", None, _attn_fwd, qh, k2, v2, w["sinks"], name=f"attn_fwd{tag}")
    mixed, pooled = _mix_fwd(proj, oa, yb, w["hg"], w["pool_w"], w["pool_scale"], name=f"mix_fwd{tag}")
    h_next, = _mm_nn(mixed, w["w_out"], h, name=f"out_proj{tag}", tn=512)
    saved = dict(h=h, xn=xn, proj=proj, oa=oa, sck=sck, qh=qh, k2=k2, v2=v2, yb=yb, mixed=mixed, pooled=pooled)
    return h_next, saved


def _layer_bwd(dh_out, s, w, tag, carries):
    g = {}
    dhb = dh_out.astype(BF16)
    d_mixed = _mm_nt(dhb, w["w_out"], name=f"d_mixed{tag}", tm=1664, tn=512)
    g["w_out"] = _mm_tn(s["mixed"], dhb, name=f"dw_out{tag}", tn=1024)
    d_oa, d_yb, d_ga, d_gb, d_gc, d_p, g["hg"], g["pool_w"], g["pool_scale"] = _mix_bwd(
        s["proj"], s["oa"], s["yb"], s["pooled"], w["hg"], w["pool_w"], w["pool_scale"], d_mixed, name=f"mix_bwd{tag}")
    d_uc = _pool_bwd(d_p, name=f"pool_bwd{tag}")
    d_qh, dkc, dkp, dvc, dvp, dkm, dvm, g["sinks"] = _carried(
        carries, "attn_bwd", g, _attn_bwd, s["qh"], s["k2"], s["v2"], w["sinks"], s["yb"], d_yb, name=f"attn_bwd{tag}")
    d_qb, d_kb, d_vb, g["gq"], g["gk"] = _qk_post(s["proj"], w["gq"], w["gk"], w["cos"], w["sin"], d_qh, dkc, dkp, dkm,
                                                  dvc, dvp, dvm, name=f"qk_post{tag}")
    d_qa, d_fa, d_ia, g["lb"] = _carried(carries, "hgrn_bwd", g, _hgrn_bwd, s["proj"], w["lb"], s["sck"], d_oa,
                                         name=f"hgrn_bwd{tag}")
    d_proj = jnp.concatenate([d_qa, d_fa, d_ia, d_ga, d_qb, d_gb, d_uc, d_gc, d_kb, d_vb], axis=1).astype(BF16)
    d_xn = _mm_nt(d_proj, w["w_in"], name=f"d_xn{tag}", tk=PROJ_COLS)
    g["w_in"] = _mm_tn(s["xn"], d_proj, name=f"dw_in{tag}")
    dh, g["norm_g"] = _carried(carries, "rmsnorm_bwd", g, _rmsnorm_bwd, d_xn, s["h"], w["norm_g"], dh_out,
                               name=f"rmsnorm_bwd{tag}")
    return dh, g


def _peers():
    x, y, c = lax.axis_index("x"), lax.axis_index("y"), lax.axis_index("c")
    out = []
    for k in range(1, N_DEV):
        kx, ky, kc = (k >> 2) & 1, (k >> 1) & 1, k & 1
        px, py, pc = x ^ kx, y ^ ky, c ^ kc
        out.append(((px, py, pc), 4 * px + 2 * py + pc))
    return 4 * x + 2 * y + c, out


def _exchange_copies(src_ref, out_ref, send_sems, recv_sems, local_sem, gather):
    me, peers = _peers()
    mine = pltpu.make_async_copy(src_ref if gather else src_ref.at[me], out_ref.at[me], local_sem)
    copies = []
    for k, (dev, idx) in enumerate(peers):
        copies.append(pltpu.make_async_remote_copy(
            src_ref=src_ref if gather else src_ref.at[idx], dst_ref=out_ref.at[me],
            send_sem=send_sems.at[k], recv_sem=recv_sems.at[k],
            device_id=dev, device_id_type=pl.DeviceIdType.MESH))
    return mine, copies


def _exchange_start(*refs, gather):
    mine, copies = _exchange_copies(*refs, gather)
    mine.start()
    for cp in copies:
        cp.start()


def _exchange_wait(*refs, gather):
    mine, copies = _exchange_copies(*refs, gather)
    for cp in copies:
        cp.wait_recv()
    for cp in copies:
        cp.wait_send()
    mine.wait()


def _exchange_scratch():
    return [pltpu.SemaphoreType.DMA((N_DEV - 1,)), pltpu.SemaphoreType.DMA((N_DEV - 1,)), pltpu.SemaphoreType.DMA]


def _exchange(src, *, gather, name):
    rows, cols = src.shape[-2:]

    def body(src_ref, out_ref, send_sems, recv_sems, local_sem):
        _exchange_start(src_ref, out_ref, send_sems, recv_sems, local_sem, gather=gather)
        _exchange_wait(src_ref, out_ref, send_sems, recv_sems, local_sem, gather=gather)

    return pl.pallas_call(
        body, name=name,
        in_specs=[pl.BlockSpec(memory_space=pl.ANY)], out_specs=pl.BlockSpec(memory_space=pl.ANY),
        out_shape=jax.ShapeDtypeStruct((N_DEV, rows, cols), src.dtype),
        scratch_shapes=_exchange_scratch(),
    )(src)


def _call(body, *, name, grid, in_specs, out_specs, out_shape, args, sem, scratch_shapes=(), carry=None):
    if carry is None:
        return pl.pallas_call(
            body, name=name, grid=grid, in_specs=list(in_specs), out_specs=list(out_specs), out_shape=list(out_shape),
            scratch_shapes=list(scratch_shapes), compiler_params=_params(*sem))(*args)
    src, gather = carry
    n_in, n_out, n_scr = len(in_specs), len(out_specs), len(scratch_shapes)
    rows, cols = src.shape[-2:]

    def carrying(*refs):
        ins, src_ref = refs[:n_in], refs[n_in]
        outs, dst_ref = refs[n_in + 1:n_in + 1 + n_out], refs[n_in + 1 + n_out]
        scr = refs[n_in + 2 + n_out:]
        exch = (src_ref, dst_ref) + tuple(scr[n_scr:])
        first, last = None, None
        for a, size in enumerate(grid):
            f, l = pl.program_id(a) == 0, pl.program_id(a) == size - 1
            first = f if first is None else first & f
            last = l if last is None else last & l

        @pl.when(first)
        def _():
            _exchange_start(*exch, gather=gather)

        body(*ins, *outs, *scr[:n_scr])

        @pl.when(last)
        def _():
            _exchange_wait(*exch, gather=gather)

    hbm = pl.BlockSpec(memory_space=pl.ANY)
    return pl.pallas_call(
        carrying, name=name, grid=grid, in_specs=list(in_specs) + [hbm], out_specs=list(out_specs) + [hbm],
        out_shape=list(out_shape) + [jax.ShapeDtypeStruct((N_DEV, rows, cols), src.dtype)],
        scratch_shapes=list(scratch_shapes) + _exchange_scratch(),
        compiler_params=_params(*(("arbitrary",) * len(grid))))(*args, src)


def _adamw(stacks, w, m, v, *, name):
    nl = len(stacks)
    rows, cols = stacks[0].shape[1:]
    tr = rows
    for cand in (256, 128, 64, 32, 16):
        if rows % cand == 0:
            tr = cand
            break
    nt = rows // tr

    def body(*refs):
        s_refs = refs[:nl]
        w_ref, m_ref, v_ref, g_ref, d_ref, nm_ref, nv_ref = refs[nl:]
        for l, s_ref in enumerate(s_refs):
            @pl.when(pl.program_id(0) == l)
            def _(s_ref=s_ref):
                acc = s_ref[0].astype(F32)
                for d in range(1, N_DEV):
                    acc = acc + s_ref[d].astype(F32)
                g_ref[...] = acc

        g = g_ref[...]
        nm = ADAM_B1 * m_ref[...] + (1.0 - ADAM_B1) * g
        nv = ADAM_B2 * v_ref[...] + (1.0 - ADAM_B2) * (g * g)
        m_hat = nm / (1.0 - ADAM_B1 ** ADAM_STEP)
        v_hat = nv / (1.0 - ADAM_B2 ** ADAM_STEP)
        d_ref[...] = -ADAM_LR * (m_hat / (jnp.sqrt(v_hat) + ADAM_EPS) + ADAM_WD * w_ref[...])
        nm_ref[...] = nm
        nv_ref[...] = nv

    blk = pl.BlockSpec((tr, cols), lambda l, i: (l * nt + i, 0))
    return pl.pallas_call(
        body, name=name, grid=(nl, nt),
        in_specs=[pl.BlockSpec((N_DEV, tr, cols), lambda l, i: (0, i, 0))] * nl + [blk, blk, blk],
        out_specs=[blk] * 4, out_shape=[jax.ShapeDtypeStruct((nl * rows, cols), F32)] * 4,
        compiler_params=_params("arbitrary", "arbitrary"),
    )(*stacks, w, m, v)


def _lb_all(lb_logits):
    sm = jax.nn.softmax(lb_logits.astype(F32), axis=0)
    return jnp.cumsum(sm, axis=0) - sm[0:1]


def _rope_tables(p):
    half = DH // 2
    inv = jnp.power(ROPE_THETA, -jnp.arange(half, dtype=F32) * 2.0 / DH)
    pos = (jnp.arange(p) - PAD_FRONT).astype(F32)
    ang = pos[:, None] * inv[None, :]
    cos, sin = jnp.cos(ang), jnp.sin(ang)
    return jnp.tile(cos, (1, 4)), jnp.tile(jnp.concatenate([-sin, sin], axis=1), (1, 2))


def _permute_cols(w):
    return jnp.concatenate([w[:, :3072], w[:, 3328:], w[:, 3072:3328]], axis=1)


def _unpermute_cols(w):
    return jnp.concatenate([w[:, :3072], w[:, 5120:], w[:, 3072:5120]], axis=1)


SMALL = (("lb_logits", (DEPTH, 512)), ("norm_g", (DEPTH, D_MODEL)), ("q_norm_g", (DEPTH, DH)),
         ("k_norm_g", (DEPTH, DH)), ("attn_sinks", (DEPTH, HB)), ("hgrn_norm_g", (DEPTH, 128)),
         ("pool_w", (DEPTH, 4, 128, 128)), ("pool_scale", (DEPTH, 512)))


def _pack_small(d):
    flat = jnp.concatenate([d[k].astype(F32).reshape(-1) for k, _ in SMALL])
    pad = (-flat.shape[0]) % (8 * 128)
    return jnp.pad(flat, (0, pad)).reshape(-1, 128)


def _unpack_small(a):
    flat = a.reshape(-1)
    out, off = {}, 0
    for k, shp in SMALL:
        n = int(np.prod(shp))
        out[k] = flat[off:off + n].reshape(shp)
        off += n
    return out


def kernel(x, meta_tokens, lb_logits, norm_g, w_in, q_norm_g, k_norm_g, attn_sinks, hgrn_norm_g, pool_w, pool_scale, w_out, loss_target, m_meta_tokens, m_lb_logits, m_norm_g, m_w_in, m_q_norm_g, m_k_norm_g, m_attn_sinks, m_hgrn_norm_g, m_pool_w, m_pool_scale, m_w_out, v_meta_tokens, v_lb_logits, v_norm_g, v_w_in, v_q_norm_g, v_k_norm_g, v_attn_sinks, v_hgrn_norm_g, v_pool_w, v_pool_scale, v_w_out):
    seq = x.shape[1]
    p = seq + TB
    cs = PROJ_COLS // N_DEV
    rs = MIX // N_DEV
    ms = D_MODEL // N_DEV

    full_w_in = lambda st: _permute_cols(st.transpose(1, 0, 2).reshape(D_MODEL, PROJ_COLS))
    dw_in_blocks = lambda g: _unpermute_cols(g["w_in"]).reshape(D_MODEL, N_DEV, cs).transpose(1, 0, 2).astype(BF16)
    dw_out_blocks = lambda g: g["w_out"].reshape(N_DEV, rs, D_MODEL).astype(BF16)
    w_in_bf = w_in.astype(BF16)

    lb_all, lb_vjp = jax.vjp(_lb_all, lb_logits)
    cos, sin = _rope_tables(p)
    layers = []
    for l in range(DEPTH):
        layers.append(dict(
            norm_g=norm_g[l][None], lb=lb_all[l][None],
            gq=jnp.tile(q_norm_g[l], 2)[None], gk=jnp.tile(k_norm_g[l], 2)[None], sinks=attn_sinks[l],
            hg=hgrn_norm_g[l][None], pool_w=pool_w[l].astype(BF16), pool_scale=pool_scale[l][None], cos=cos, sin=sin))
    layers[0]["w_in"] = full_w_in(_exchange(w_in_bf[0], gather=True, name="gather_w_in_l0"))
    meta_all = _exchange(meta_tokens, gather=True, name="gather_meta")
    meta_full = meta_all.transpose(1, 0, 2).reshape(N_META, D_MODEL)

    def got_w_out(st):
        st = st.reshape(N_DEV, DEPTH, rs, D_MODEL)
        for l in range(DEPTH):
            layers[l]["w_out"] = st[:, l].reshape(MIX, D_MODEL)

    def got_w_in_l1(st):
        layers[1]["w_in"] = full_w_in(st)

    fwd_carries = [
        dict(in_proj=(lambda _: w_out.reshape(DEPTH * rs, D_MODEL).astype(BF16), True, got_w_out),
             attn_fwd=(lambda _: w_in_bf[1], True, got_w_in_l1)),
        {}]
    h = jnp.concatenate([jnp.zeros((PAD_FRONT, D_MODEL), F32), meta_full, x[0]], axis=0)
    saved = []
    for l in range(DEPTH):
        h, s = _layer_fwd(h, layers[l], f"_l{l}", fwd_carries[l])
        saved.append(s)
    dh, sq = _loss_grad(h, loss_target[0], name="loss_grad")
    loss = lax.psum(0.5 * jnp.sum(sq) / D_MODEL, ("x", "y", "c"))

    grads = [None] * DEPTH
    win_stacks, wout_stacks = [None] * DEPTH, [None] * DEPTH

    def into(stacks, l):
        def done(st):
            stacks[l] = st
        return done

    bwd_carries = [
        dict(attn_bwd=(lambda _: dw_in_blocks(grads[1]), False, into(win_stacks, 1)),
             hgrn_bwd=(lambda g: dw_out_blocks(g), False, into(wout_stacks, 0)),
             rmsnorm_bwd=(lambda g: dw_in_blocks(g), False, into(win_stacks, 0))),
        dict(attn_bwd=(lambda g: dw_out_blocks(g), False, into(wout_stacks, 1)))]
    for l in reversed(range(DEPTH)):
        dh, grads[l] = _layer_bwd(dh, saved[l], layers[l], f"_l{l}", bwd_carries[l])
    grad_x = dh[TB:][None]
    dmeta = dh[PAD_FRONT:TB].reshape(N_META, N_DEV, ms).transpose(1, 0, 2)
    meta_stack = _exchange(dmeta, gather=False, name="scatter_dmeta")

    stk = lambda k: jnp.stack([grads[l][k][0] for l in range(DEPTH)])
    fold = lambda a: a[:, :DH] + a[:, DH:]
    small_local = dict(
        lb_logits=lb_vjp(stk("lb"))[0], norm_g=stk("norm_g"), q_norm_g=fold(stk("gq")), k_norm_g=fold(stk("gk")),
        attn_sinks=stk("sinks")[:, :HB], hgrn_norm_g=stk("hg"),
        pool_w=jnp.stack([grads[l]["pool_w"] for l in range(DEPTH)]), pool_scale=stk("pool_scale"))
    small_stack = _exchange(_pack_small(small_local), gather=True, name="gather_small_grads")

    g_win, d_win, nm_win, nv_win = _adamw(win_stacks, w_in.reshape(DEPTH * D_MODEL, cs), m_w_in.reshape(DEPTH * D_MODEL, cs),
                                          v_w_in.reshape(DEPTH * D_MODEL, cs), name="adamw_w_in")
    g_wout, d_wout, nm_wout, nv_wout = _adamw(wout_stacks, w_out.reshape(DEPTH * rs, D_MODEL), m_w_out.reshape(DEPTH * rs, D_MODEL),
                                              v_w_out.reshape(DEPTH * rs, D_MODEL), name="adamw_w_out")
    g_meta, d_meta, nm_meta, nv_meta = _adamw([meta_stack], meta_tokens, m_meta_tokens, v_meta_tokens, name="adamw_meta")
    small_w = dict(lb_logits=lb_logits, norm_g=norm_g, q_norm_g=q_norm_g, k_norm_g=k_norm_g, attn_sinks=attn_sinks,
                   hgrn_norm_g=hgrn_norm_g, pool_w=pool_w, pool_scale=pool_scale)
    small_m = dict(lb_logits=m_lb_logits, norm_g=m_norm_g, q_norm_g=m_q_norm_g, k_norm_g=m_k_norm_g, attn_sinks=m_attn_sinks,
                   hgrn_norm_g=m_hgrn_norm_g, pool_w=m_pool_w, pool_scale=m_pool_scale)
    small_v = dict(lb_logits=v_lb_logits, norm_g=v_norm_g, q_norm_g=v_q_norm_g, k_norm_g=v_k_norm_g, attn_sinks=v_attn_sinks,
                   hgrn_norm_g=v_hgrn_norm_g, pool_w=v_pool_w, pool_scale=v_pool_scale)
    small_out = [_unpack_small(a) for a in _adamw([small_stack], _pack_small(small_w), _pack_small(small_m),
                                                  _pack_small(small_v), name="adamw_small")]

    big = dict(
        meta_tokens=(g_meta, d_meta, nm_meta, nv_meta),
        w_in=tuple(a.reshape(DEPTH, D_MODEL, cs) for a in (g_win, d_win, nm_win, nv_win)),
        w_out=tuple(a.reshape(DEPTH, rs, D_MODEL) for a in (g_wout, d_wout, nm_wout, nv_wout)))
    order = ("meta_tokens", "lb_logits", "norm_g", "w_in", "q_norm_g", "k_norm_g", "attn_sinks", "hgrn_norm_g",
             "pool_w", "pool_scale", "w_out")
    outs = [loss, grad_x]
    for kind in range(4):
        for k in order:
            outs.append(big[k][kind] if k in big else small_out[kind][k])
    return tuple(outs)
```

```python
import functools

import numpy as np
import jax
import jax.numpy as jnp
from jax import lax
from jax.experimental import pallas as pl
from jax.experimental.pallas import tpu as pltpu

F32, BF16 = jnp.float32, jnp.bfloat16

D_MODEL = 2048
DEPTH = 2
N_META = 16
TB = 128
PAD_FRONT = TB - N_META
RMS_EPS = 1e-6
NEG_INF = -1e30
LOG_FLOOR = 1e-30
HA, DK_A = 4, 128
CH = 16
NCH = TB // CH
HB, KVH, DH = 16, 2, 64
GRP = HB // KVH
ROPE_THETA = 10000.0
POOL_WINDOWS = (2, 4, 8, 16)
PROJ_COLS = 5376
MIX = 2048
N_DEV = 8
C_QA, C_FA, C_IA, C_GA = 0, 1, 2, 3
C_QB, C_GB = 2, 3
C_UC, C_GC = 8, 9
C_KB, C_VB = 40, 41

ADAM_LR, ADAM_B1, ADAM_B2, ADAM_EPS, ADAM_WD, ADAM_STEP = 0.001, 0.9, 0.999, 1e-08, 0.01, 10

VMEM_LIMIT = 48 * 1024 * 1024

NN = ((1,), (0,))
NT = ((1,), (1,))
TN = ((0,), (0,))


def _dot(a, b, dims):
    return lax.dot_general(a, b, (dims, ((), ())), preferred_element_type=F32)


def _split3(x):
    hi = x.astype(BF16)
    r = x - hi.astype(F32)
    mid = r.astype(BF16)
    lo = (r - mid.astype(F32)).astype(BF16)
    return hi, mid, lo


def _xdot(m01, x):
    hi, mid, lo = _split3(x)
    return _dot(m01, hi, NN) + _dot(m01, mid, NN) + _dot(m01, lo, NN)


def _xdot_r(x, m01):
    hi, mid, lo = _split3(x)
    return _dot(hi, m01, NN) + _dot(mid, m01, NN) + _dot(lo, m01, NN)


def _iota(shape, dim):
    return lax.broadcasted_iota(jnp.int32, shape, dim)


def _params(*sem):
    return pltpu.CompilerParams(dimension_semantics=sem, vmem_limit_bytes=VMEM_LIMIT)


def _row_tile(p, target):
    best = TB
    t = TB
    while t <= target:
        if p % t == 0:
            best = t
        t += TB
    return best


def _col_tile(n, target):
    best = 128
    t = 128
    while t <= target:
        if n % t == 0:
            best = t
        t += 128
    return best


def _sigmoid(x):
    return 1.0 / (1.0 + jnp.exp(-x))


def _mm_nn(a, b, res=None, *, name, tm=1664, tn=768, carry=None):
    m, k = a.shape
    n = b.shape[1]
    tm, tn = _row_tile(m, tm), _col_tile(n, tn)

    def body(*refs):
        if res is None:
            a_ref, b_ref, o_ref = refs
            o_ref[...] = _dot(a_ref[...], b_ref[...], NN)
        else:
            a_ref, b_ref, r_ref, o_ref = refs
            o_ref[...] = r_ref[...] + _dot(a_ref[...], b_ref[...], NN)

    in_specs = [pl.BlockSpec((tm, k), lambda j, i: (i, 0)), pl.BlockSpec((k, tn), lambda j, i: (0, j))]
    args = [a, b]
    if res is not None:
        in_specs.append(pl.BlockSpec((tm, tn), lambda j, i: (i, j)))
        args.append(res)
    return _call(
        body, name=name, grid=(n // tn, m // tm), in_specs=in_specs,
        out_specs=[pl.BlockSpec((tm, tn), lambda j, i: (i, j))],
        out_shape=[jax.ShapeDtypeStruct((m, n), F32)],
        args=args, sem=("parallel", "parallel"), carry=carry)


def _mm_nt(a, b, *, name, tm=640, tn=512, tk=2048):
    m, k = a.shape
    n = b.shape[0]
    tm, tn, tk = _row_tile(m, tm), _col_tile(n, tn), _col_tile(k, tk)

    def body(a_ref, b_ref, o_ref):
        @pl.when(pl.program_id(2) == 0)
        def _():
            o_ref[...] = jnp.zeros_like(o_ref)

        o_ref[...] += _dot(a_ref[...], b_ref[...], NT)

    return pl.pallas_call(
        body, name=name, grid=(n // tn, m // tm, k // tk),
        in_specs=[pl.BlockSpec((tm, tk), lambda j, i, kk: (i, kk)), pl.BlockSpec((tn, tk), lambda j, i, kk: (j, kk))],
        out_specs=pl.BlockSpec((tm, tn), lambda j, i, kk: (i, j)),
        out_shape=jax.ShapeDtypeStruct((m, n), F32),
        compiler_params=_params("parallel", "parallel", "arbitrary"),
    )(a, b)


def _mm_tn(a, b, *, name, tm=1024, tn=1344, tk=1664):
    k, m = a.shape
    n = b.shape[1]
    tm, tn, tk = _col_tile(m, tm), _col_tile(n, tn), _row_tile(k, tk)

    def body(a_ref, b_ref, o_ref):
        @pl.when(pl.program_id(2) == 0)
        def _():
            o_ref[...] = jnp.zeros_like(o_ref)

        o_ref[...] += _dot(a_ref[...], b_ref[...], TN)

    return pl.pallas_call(
        body, name=name, grid=(m // tm, n // tn, k // tk),
        in_specs=[pl.BlockSpec((tk, tm), lambda i, j, kk: (kk, i)), pl.BlockSpec((tk, tn), lambda i, j, kk: (kk, j))],
        out_specs=pl.BlockSpec((tm, tn), lambda i, j, kk: (i, j)),
        out_shape=jax.ShapeDtypeStruct((m, n), F32),
        compiler_params=_params("parallel", "parallel", "arbitrary"),
    )(a, b)


def _rmsnorm_fwd(h, g, *, name):
    p, dm = h.shape
    tm = _row_tile(p, 640)

    def body(h_ref, g_ref, xn_ref):
        hv = h_ref[...]
        r = lax.rsqrt(jnp.mean(hv * hv, axis=-1, keepdims=True) + RMS_EPS)
        xn_ref[...] = (hv * r * g_ref[...]).astype(BF16)

    return pl.pallas_call(
        body, name=name, grid=(p // tm,),
        in_specs=[pl.BlockSpec((tm, dm), lambda i: (i, 0)), pl.BlockSpec((1, dm), lambda i: (0, 0))],
        out_specs=pl.BlockSpec((tm, dm), lambda i: (i, 0)),
        out_shape=jax.ShapeDtypeStruct((p, dm), BF16),
        compiler_params=_params("parallel"),
    )(h, g)


def _rmsnorm_bwd(dxn, h, g, dh_out, *, name, carry=None):
    p, dm = h.shape
    tm = _row_tile(p, 384)

    def body(dxn_ref, h_ref, g_ref, dho_ref, dh_ref, dg_ref):
        hv = h_ref[...]
        r = lax.rsqrt(jnp.mean(hv * hv, axis=-1, keepdims=True) + RMS_EPS)
        xh = hv * r
        dy = dxn_ref[...]
        dyn = dy * g_ref[...]
        dh_ref[...] = dho_ref[...] + r * (dyn - xh * jnp.mean(dyn * xh, axis=-1, keepdims=True))

        @pl.when(pl.program_id(0) == 0)
        def _():
            dg_ref[...] = jnp.zeros_like(dg_ref)

        dg_ref[...] += jnp.sum(dy * xh, axis=0, keepdims=True)

    row = pl.BlockSpec((tm, dm), lambda i: (i, 0))
    vec = pl.BlockSpec((1, dm), lambda i: (0, 0))
    return _call(
        body, name=name, grid=(p // tm,),
        in_specs=[row, row, vec, row], out_specs=[row, vec],
        out_shape=[jax.ShapeDtypeStruct((p, dm), F32), jax.ShapeDtypeStruct((1, dm), F32)],
        args=(dxn, h, g, dh_out), sem=("arbitrary",), carry=carry)


def _loss_grad(h, target, *, name):
    p, dm = h.shape

    def body(h_ref, t_ref, dh_ref, sq_ref):
        n = pl.program_id(0)

        @pl.when(n == 0)
        def _():
            dh_ref[...] = jnp.zeros_like(dh_ref)
            sq_ref[...] = jnp.zeros_like(sq_ref)

        @pl.when(n > 0)
        def _():
            err = h_ref[...] - t_ref[...]
            dh_ref[...] = err * (1.0 / dm)
            sq_ref[...] += jnp.sum(err * err, axis=0, keepdims=True)

    return pl.pallas_call(
        body, name=name, grid=(p // TB,),
        in_specs=[pl.BlockSpec((TB, dm), lambda n: (n, 0)), pl.BlockSpec((TB, dm), lambda n: (jnp.maximum(n - 1, 0), 0))],
        out_specs=[pl.BlockSpec((TB, dm), lambda n: (n, 0)), pl.BlockSpec((1, dm), lambda n: (0, 0))],
        out_shape=[jax.ShapeDtypeStruct((p, dm), F32), jax.ShapeDtypeStruct((1, dm), F32)],
        compiler_params=_params("arbitrary"),
    )(h, target)


def _chunk_masks():
    ri, ci = _iota((TB, TB), 0), _iota((TB, TB), 1)
    same = (ri >> 4) == (ci >> 4)
    causal = same & (ci <= ri)
    lower = jnp.where(causal, 1.0, 0.0).astype(BF16)
    upper = jnp.where(same & (ci >= ri), 1.0, 0.0).astype(BF16)
    ones = jnp.where(same, 1.0, 0.0).astype(BF16)
    return causal, lower, upper, ones


def _hgrn_gates(q, z, lbh, m):
    sig = _sigmoid(z)
    f = lbh + (1.0 - lbh) * sig
    lf = jnp.log(jnp.maximum(f, LOG_FLOOR)) * m
    kk = (1.0 - lbh) * (1.0 - sig) * m
    sq = _sigmoid(q)
    return sig, f, lf, kk, sq, q * sq


def _hgrn_fwd(proj, lb, *, name):
    p = proj.shape[0]
    nb = p // TB

    def body(qa_ref, fa_ref, ia_ref, lb_ref, oa_ref, sck_ref, st_ref):
        n = pl.program_id(0)

        @pl.when(n == 0)
        def _():
            st_ref[...] = jnp.zeros_like(st_ref)

        causal, lower, _, ones = _chunk_masks()
        m = ((n * TB + _iota((TB, 1), 0)) >= PAD_FRONT).astype(F32)
        heads = range(HA)
        sls = [slice(hd * DK_A, (hd + 1) * DK_A) for hd in heads]
        rows = [slice(c * CH, (c + 1) * CH) for c in range(NCH)]
        gates = [_hgrn_gates(qa_ref[:, sl], fa_ref[:, sl], lb_ref[:, sl], m) for sl in sls]
        lf = [t[2] for t in gates]
        g = [_xdot(lower, x) for x in lf]
        gl = [_xdot(ones, x) for x in lf]
        qd = [(gates[hd][5] * jnp.exp(g[hd])).astype(BF16) for hd in heads]
        kt = [(gates[hd][3] * jnp.exp(-g[hd])).astype(BF16) for hd in heads]
        kd = [(gates[hd][3] * jnp.exp(gl[hd] - g[hd])).astype(BF16) for hd in heads]
        vb = [ia_ref[:, sl].astype(BF16) for sl in sls]
        a_all = [jnp.exp(x) for x in gl]
        att = [jnp.where(causal, _dot(qd[hd], kt[hd], NT), 0.0).astype(BF16) for hd in heads]
        kv = [[_dot(vb[hd][r], kd[hd][r], TN) for r in rows] for hd in heads]
        o = [_dot(att[hd], vb[hd], NN) for hd in heads]
        before = []
        for hd in heads:
            st = st_ref[hd]
            sck_ref[0, hd] = st
            per_chunk = []
            for c in range(NCH):
                per_chunk.append(st.astype(BF16))
                st = st * a_all[hd][c * CH:c * CH + 1, :] + kv[hd][c]
            st_ref[hd] = st
            before.append(per_chunk)
        inter = [[_dot(qd[hd][rows[c]], before[hd][c], NT) for c in range(NCH)] for hd in heads]
        for hd in heads:
            oa_ref[:, sls[hd]] = o[hd] + jnp.concatenate(inter[hd], axis=0)

    blk = lambda c: pl.BlockSpec((TB, 512), lambda n, c=c: (n, c))
    return pl.pallas_call(
        body, name=name, grid=(nb,),
        in_specs=[blk(C_QA), blk(C_FA), blk(C_IA), pl.BlockSpec((1, 512), lambda n: (0, 0))],
        out_specs=[pl.BlockSpec((TB, 512), lambda n: (n, 0)), pl.BlockSpec((1, HA, TB, TB), lambda n: (n, 0, 0, 0))],
        out_shape=[jax.ShapeDtypeStruct((p, 512), F32), jax.ShapeDtypeStruct((nb, HA, TB, TB), F32)],
        scratch_shapes=[pltpu.VMEM((HA, TB, TB), F32)],
        compiler_params=_params("arbitrary"),
    )(proj, proj, proj, lb)


def _hgrn_bwd(proj, lb, sck, d_oa, *, name, carry=None):
    p = proj.shape[0]
    nb = p // TB

    def body(qa_ref, fa_ref, ia_ref, lb_ref, sck_ref, do_ref, dq_ref, dz_ref, dv_ref, dlb_ref, dst_ref):
        i = pl.program_id(0)
        n = nb - 1 - i

        @pl.when(i == 0)
        def _():
            dst_ref[...] = jnp.zeros_like(dst_ref)
            dlb_ref[...] = jnp.zeros_like(dlb_ref)

        causal, lower, upper, ones = _chunk_masks()
        m = ((n * TB + _iota((TB, 1), 0)) >= PAD_FRONT).astype(F32)
        heads = range(HA)
        sls = [slice(hd * DK_A, (hd + 1) * DK_A) for hd in heads]
        rows = [slice(c * CH, (c + 1) * CH) for c in range(NCH)]
        a_row = lambda a, c: a[c * CH:c * CH + 1, :]
        gates = [_hgrn_gates(qa_ref[:, sl], fa_ref[:, sl], lb_ref[:, sl], m) for sl in sls]
        g = [_xdot(lower, t[2]) for t in gates]
        gl = [_xdot(ones, t[2]) for t in gates]
        e_g = [jnp.exp(x) for x in g]
        e_ng = [jnp.exp(-x) for x in g]
        e_d = [jnp.exp(gl[hd] - g[hd]) for hd in heads]
        a_all = [jnp.exp(x) for x in gl]
        qd_f = [gates[hd][5] * e_g[hd] for hd in heads]
        kt_f = [gates[hd][3] * e_ng[hd] for hd in heads]
        kd_f = [gates[hd][3] * e_d[hd] for hd in heads]
        qd, kt, kd = ([x.astype(BF16) for x in xs] for xs in (qd_f, kt_f, kd_f))
        vb = [ia_ref[:, sl].astype(BF16) for sl in sls]
        dob = [do_ref[:, sl].astype(BF16) for sl in sls]
        att = [jnp.where(causal, _dot(qd[hd], kt[hd], NT), 0.0).astype(BF16) for hd in heads]
        d_att = [jnp.where(causal, _dot(dob[hd], vb[hd], NT), 0.0).astype(BF16) for hd in heads]
        kv = [[_dot(vb[hd][r], kd[hd][r], TN) for r in rows] for hd in heads]
        dqk = [[_dot(dob[hd][r], qd[hd][r], TN) for r in rows] for hd in heads]
        d_v = [_dot(att[hd], dob[hd], TN) for hd in heads]
        d_qd = [_dot(d_att[hd], kt[hd], NN) for hd in heads]
        d_kt = [_dot(d_att[hd], qd[hd], TN) for hd in heads]
        stc, dsc = [], []
        for hd in heads:
            st, before = sck_ref[0, hd], []
            for c in range(NCH):
                before.append(st)
                if c + 1 < NCH:
                    st = st * a_row(a_all[hd], c) + kv[hd][c]
            dst, after = dst_ref[hd], [None] * NCH
            for c in range(NCH - 1, -1, -1):
                after[c] = dst
                dst = dst * a_row(a_all[hd], c) + dqk[hd][c]
            dst_ref[hd] = dst
            stc.append(before)
            dsc.append(after)
        dscb = [[x.astype(BF16) for x in dsc[hd]] for hd in heads]
        dvs = [[_dot(kd[hd][rows[c]], dscb[hd][c], NT) for c in range(NCH)] for hd in heads]
        dkd = [[_dot(vb[hd][rows[c]], dscb[hd][c], NN) for c in range(NCH)] for hd in heads]
        dqd = [[_dot(dob[hd][rows[c]], stc[hd][c].astype(BF16), NN) for c in range(NCH)] for hd in heads]
        dgl = [[jnp.broadcast_to(jnp.sum(dsc[hd][c] * stc[hd][c], axis=0, keepdims=True) * a_row(a_all[hd], c), (CH, TB))
                for c in range(NCH)] for hd in heads]
        d_qd = [d_qd[hd] + jnp.concatenate(dqd[hd], axis=0) for hd in heads]
        d_kd = [jnp.concatenate(dkd[hd], axis=0) for hd in heads]
        kd_term = [d_kd[hd] * kd_f[hd] for hd in heads]
        d_g = [d_qd[hd] * qd_f[hd] - d_kt[hd] * kt_f[hd] - kd_term[hd] for hd in heads]
        d_lf = [_xdot(upper, d_g[hd]) + _xdot(ones, kd_term[hd]) + jnp.concatenate(dgl[hd], axis=0) for hd in heads]
        for hd in heads:
            sl = sls[hd]
            sig, f, _, _, sq, _ = gates[hd]
            q, lbh = qa_ref[:, sl], lb_ref[:, sl]
            d_kk = (d_kt[hd] * e_ng[hd] + d_kd[hd] * e_d[hd]) * m
            t1 = d_lf[hd] * m * jnp.where(f > LOG_FLOOR, 1.0 / f, 0.0)
            dq_ref[:, sl] = d_qd[hd] * e_g[hd] * (sq * (1.0 + q * (1.0 - sq)))
            dz_ref[:, sl] = (t1 - d_kk) * (1.0 - lbh) * sig * (1.0 - sig)
            dv_ref[:, sl] = d_v[hd] + jnp.concatenate(dvs[hd], axis=0)
            dlb_ref[:, sl] += jnp.sum((t1 - d_kk) * (1.0 - sig), axis=0, keepdims=True)

    blk = lambda c: pl.BlockSpec((TB, 512), lambda i, c=c: (nb - 1 - i, c))
    out_blk = pl.BlockSpec((TB, 512), lambda i: (nb - 1 - i, 0))
    vec = pl.BlockSpec((1, 512), lambda i: (0, 0))
    return _call(
        body, name=name, grid=(nb,),
        in_specs=[blk(C_QA), blk(C_FA), blk(C_IA), vec,
                  pl.BlockSpec((1, HA, TB, TB), lambda i: (nb - 1 - i, 0, 0, 0)), out_blk],
        out_specs=[out_blk, out_blk, out_blk, vec],
        out_shape=[jax.ShapeDtypeStruct((p, 512), F32)] * 3 + [jax.ShapeDtypeStruct((1, 512), F32)],
        scratch_shapes=[pltpu.VMEM((HA, TB, TB), F32)],
        args=(proj, proj, proj, lb, sck, d_oa), sem=("arbitrary",), carry=carry)


def _lane():
    return _iota((1, TB), 1)


def _swap_halves(y):
    first = (_lane() & 63) < 32
    return jnp.where(first, pltpu.roll(y, 96, 1), pltpu.roll(y, 32, 1))


def _head_ones():
    ri, ci = _iota((TB, TB), 0), _iota((TB, TB), 1)
    return jnp.where((ri >> 6) == (ci >> 6), 1.0, 0.0).astype(BF16)


def _norm_rope(x, g, cos, sin, bd):
    r = lax.rsqrt(_xdot_r(x * x, bd) * (1.0 / DH) + RMS_EPS)
    y = x * r * g
    return y * cos + _swap_halves(y) * sin


def _norm_rope_bwd(d_out, x, g, cos, sin, bd):
    d = d_out * cos - _swap_halves(d_out) * sin
    r = lax.rsqrt(_xdot_r(x * x, bd) * (1.0 / DH) + RMS_EPS)
    xh = x * r
    dyn = d * g
    dx = r * (dyn - xh * (_xdot_r(dyn * xh, bd) * (1.0 / DH)))
    return dx, jnp.sum(d * xh, axis=0, keepdims=True)


def _dup_heads(k):
    first = _lane() < DH
    r = pltpu.roll(k, DH, 1)
    return jnp.where(first, k, r), jnp.where(first, r, k)


def _qk_prep(proj, gq, gk, cos, sin, *, name):
    p = proj.shape[0]

    def body(qb_ref, kb_ref, vb_ref, gq_ref, gk_ref, cos_ref, sin_ref, qh_ref, k2_ref, v2_ref):
        bd = _head_ones()
        cos_v, sin_v = cos_ref[...], sin_ref[...]
        for j in range(HB // 2):
            sl = slice(j * TB, (j + 1) * TB)
            qh_ref[:, sl] = _norm_rope(qb_ref[:, sl], gq_ref[...], cos_v, sin_v, bd).astype(BF16)
        k0, k1 = _dup_heads(_norm_rope(kb_ref[...], gk_ref[...], cos_v, sin_v, bd))
        k2_ref[:, 0:TB] = k0.astype(BF16)
        k2_ref[:, TB:2 * TB] = k1.astype(BF16)
        v0, v1 = _dup_heads(vb_ref[...])
        v2_ref[:, 0:TB] = v0.astype(BF16)
        v2_ref[:, TB:2 * TB] = v1.astype(BF16)

    vec = pl.BlockSpec((1, TB), lambda n: (0, 0))
    tab = pl.BlockSpec((TB, TB), lambda n: (n, 0))
    return pl.pallas_call(
        body, name=name, grid=(p // TB,),
        in_specs=[pl.BlockSpec((TB, 1024), lambda n: (n, C_QB)), pl.BlockSpec((TB, TB), lambda n: (n, C_KB)),
                  pl.BlockSpec((TB, TB), lambda n: (n, C_VB)), vec, vec, tab, tab],
        out_specs=[pl.BlockSpec((TB, 1024), lambda n: (n, 0)), pl.BlockSpec((TB, 256), lambda n: (n, 0)),
                   pl.BlockSpec((TB, 256), lambda n: (n, 0))],
        out_shape=[jax.ShapeDtypeStruct((p, 1024), BF16), jax.ShapeDtypeStruct((p, 256), BF16),
                   jax.ShapeDtypeStruct((p, 256), BF16)],
        compiler_params=_params("parallel"),
    )(proj, proj, proj, gq, gk, cos, sin)


NKEY = N_META + 2 * TB


def _attn_mask(n):
    r = _iota((TB, NKEY), 0)
    j = _iota((TB, NKEY), 1)
    meta = (j < N_META) & ((n >= 1) | (j + PAD_FRONT <= r))
    prev = (j >= N_META) & (j < N_META + TB) & (n >= 2) & (j - N_META > r)
    cur = (j >= N_META + TB) & (n >= 1) & (j - (N_META + TB) <= r)
    return meta | prev | cur


def _attn_specs():
    cur = lambda w: pl.BlockSpec((TB, w), lambda n: (n, 0))
    prev = pl.BlockSpec((TB, 256), lambda n: (jnp.maximum(n - 1, 0), 0))
    meta = pl.BlockSpec((N_META, 256), lambda n: (PAD_FRONT // N_META, 0))
    sink = pl.BlockSpec(memory_space=pltpu.SMEM)
    return cur, prev, meta, sink


ATTN_GROUP = 4


def _head_queries(q_ref, kv, first):
    out = []
    for jj in range(GRP // 2):
        j = kv * (GRP // 2) + jj
        qj = q_ref[:, j * TB:(j + 1) * TB] * (DH ** -0.5)
        for half in range(2):
            out.append((j, half, jnp.where(first if half == 0 else ~first, qj, jnp.zeros_like(qj))))
    return out


def _softmax_sink(s, mask, sink):
    s = jnp.where(mask, s, NEG_INF)
    mx = jnp.maximum(jnp.max(s, axis=1, keepdims=True), sink)
    pr = jnp.exp(s - mx)
    e_sink = jnp.exp(sink - mx)
    inv = 1.0 / (jnp.sum(pr, axis=1, keepdims=True) + e_sink)
    return pr, inv, e_sink * inv


def _attn_fwd(qh, k2, v2, sinks, *, name, carry=None):
    p = qh.shape[0]

    def body(sink_ref, q_ref, kc_ref, kp_ref, km_ref, vc_ref, vp_ref, vm_ref, o_ref):
        n = pl.program_id(0)
        mask = _attn_mask(n)
        first = _lane() < DH
        for kv in range(KVH):
            ks = slice(kv * TB, (kv + 1) * TB)
            kall = jnp.concatenate([km_ref[:, ks], kp_ref[:, ks], kc_ref[:, ks]], axis=0)
            vall = jnp.concatenate([vm_ref[:, ks], vp_ref[:, ks], vc_ref[:, ks]], axis=0)
            hq = _head_queries(q_ref, kv, first)
            for g0 in range(0, GRP, ATTN_GROUP):
                grp = hq[g0:g0 + ATTN_GROUP]
                s = [_dot(qm, kall, NT) for _, _, qm in grp]
                soft = [_softmax_sink(s[i], mask, sink_ref[2 * j + half]) for i, (j, half, _) in enumerate(grp)]
                o = [_dot(pr.astype(BF16), vall, NN) * inv for pr, inv, _ in soft]
                for i in range(0, len(grp), 2):
                    j = grp[i][0]
                    o_ref[:, j * TB:(j + 1) * TB] = jnp.where(first, o[i], o[i + 1])

    cur, prev, meta, sink = _attn_specs()
    return _call(
        body, name=name, grid=(p // TB,),
        in_specs=[sink, cur(1024), cur(256), prev, meta, cur(256), prev, meta],
        out_specs=[cur(1024)],
        out_shape=[jax.ShapeDtypeStruct((p, 1024), F32)],
        args=(sinks, qh, k2, k2, k2, v2, v2, v2), sem=("parallel",), carry=carry)


def _attn_bwd(qh, k2, v2, sinks, o, d_o, *, name, carry=None):
    p = qh.shape[0]

    def body(sink_ref, q_ref, kc_ref, kp_ref, km_ref, vc_ref, vp_ref, vm_ref, o_ref, do_ref,
             dq_ref, dkc_ref, dkp_ref, dvc_ref, dvp_ref, dkm_ref, dvm_ref, dsink_ref):
        n = pl.program_id(0)

        @pl.when(n == 0)
        def _():
            dkm_ref[...] = jnp.zeros_like(dkm_ref)
            dvm_ref[...] = jnp.zeros_like(dvm_ref)
            dsink_ref[...] = jnp.zeros_like(dsink_ref)

        mask = _attn_mask(n)
        lane = _lane()
        first = lane < DH
        sink_acc = jnp.zeros((TB, TB), F32)
        for kv in range(KVH):
            ks = slice(kv * TB, (kv + 1) * TB)
            kall = jnp.concatenate([km_ref[:, ks], kp_ref[:, ks], kc_ref[:, ks]], axis=0)
            vall = jnp.concatenate([vm_ref[:, ks], vp_ref[:, ks], vc_ref[:, ks]], axis=0)
            d_kall = jnp.zeros((NKEY, TB), F32)
            d_vall = jnp.zeros((NKEY, TB), F32)
            hq = _head_queries(q_ref, kv, first)
            for g0 in range(0, GRP, ATTN_GROUP):
                grp = hq[g0:g0 + ATTN_GROUP]
                idx = range(len(grp))
                s = [_dot(qm, kall, NT) for _, _, qm in grp]
                dom = [jnp.where(first if half == 0 else ~first, do_ref[:, j * TB:(j + 1) * TB], 0.0) for j, half, _ in grp]
                domb = [x.astype(BF16) for x in dom]
                d_w = [_dot(x, vall, NT) for x in domb]
                delta = [jnp.sum(dom[i] * o_ref[:, grp[i][0] * TB:(grp[i][0] + 1) * TB], axis=1, keepdims=True) for i in idx]
                soft = [_softmax_sink(s[i], mask, sink_ref[2 * j + half]) for i, (j, half, _) in enumerate(grp)]
                w = [pr * inv for pr, inv, _ in soft]
                dsb = [(w[i] * (d_w[i] - delta[i])).astype(BF16) for i in idx]
                for i, (j, half, _) in enumerate(grp):
                    sink_acc = sink_acc + jnp.where(lane == 2 * j + half, -(soft[i][2] * delta[i]), 0.0)
                d_q = [_dot(x, kall, NN) * (DH ** -0.5) for x in dsb]
                d_k = [_dot(dsb[i], grp[i][2], TN) for i in idx]
                d_v = [_dot(w[i].astype(BF16), domb[i], TN) for i in idx]
                for i in idx:
                    d_kall = d_kall + d_k[i]
                    d_vall = d_vall + d_v[i]
                for i in range(0, len(grp), 2):
                    j = grp[i][0]
                    dq_ref[:, j * TB:(j + 1) * TB] = jnp.where(first, d_q[i], d_q[i + 1])
            dkm_ref[:, ks] += d_kall[0:N_META]
            dkp_ref[:, ks] = d_kall[N_META:N_META + TB]
            dkc_ref[:, ks] = d_kall[N_META + TB:NKEY]
            dvm_ref[:, ks] += d_vall[0:N_META]
            dvp_ref[:, ks] = d_vall[N_META:N_META + TB]
            dvc_ref[:, ks] = d_vall[N_META + TB:NKEY]
        dsink_ref[...] += jnp.sum(sink_acc, axis=0, keepdims=True)

    cur, prev, meta, sink = _attn_specs()
    acc = lambda r: pl.BlockSpec((r, 256), lambda n: (0, 0))
    return _call(
        body, name=name, grid=(p // TB,),
        in_specs=[sink, cur(1024), cur(256), prev, meta, cur(256), prev, meta, cur(1024), cur(1024)],
        out_specs=[cur(1024), cur(256), cur(256), cur(256), cur(256), acc(N_META), acc(N_META),
                   pl.BlockSpec((1, TB), lambda n: (0, 0))],
        out_shape=[jax.ShapeDtypeStruct((p, 1024), F32)] + [jax.ShapeDtypeStruct((p, 256), F32)] * 4
        + [jax.ShapeDtypeStruct((N_META, 256), F32)] * 2 + [jax.ShapeDtypeStruct((1, TB), F32)],
        args=(sinks, qh, k2, k2, k2, v2, v2, v2, o, d_o), sem=("arbitrary",), carry=carry)


def _qk_post(proj, gq, gk, cos, sin, dqh, dkc, dkp, dkm, dvc, dvp, dvm, *, name):
    p = proj.shape[0]
    nb = p // TB

    def body(qb_ref, kb_ref, gq_ref, gk_ref, cos_ref, sin_ref, dqh_ref, dkc_ref, dkp_ref, dkm_ref,
             dvc_ref, dvp_ref, dvm_ref, dqb_ref, dkb_ref, dvb_ref, dgq_ref, dgk_ref, tk_ref, tv_ref):
        n = pl.program_id(0)

        @pl.when(n == 0)
        def _():
            dgq_ref[...] = jnp.zeros_like(dgq_ref)
            dgk_ref[...] = jnp.zeros_like(dgk_ref)

        keep = jnp.where(n == nb - 1, 0.0, 1.0)
        tk_ref[...] = dkc_ref[...] + keep * dkp_ref[...]
        tv_ref[...] = dvc_ref[...] + keep * dvp_ref[...]

        @pl.when(n == 0)
        def _():
            tk_ref[PAD_FRONT:TB, :] += dkm_ref[...]
            tv_ref[PAD_FRONT:TB, :] += dvm_ref[...]

        first = _lane() < DH

        def fold(t_ref):
            t0, t1 = t_ref[:, 0:TB], t_ref[:, TB:2 * TB]
            return jnp.where(first, t0 + pltpu.roll(t0, DH, 1), t1 + pltpu.roll(t1, DH, 1))

        bd = _head_ones()
        cos_v, sin_v = cos_ref[...], sin_ref[...]
        dvb_ref[...] = fold(tv_ref)
        dkb, dgk = _norm_rope_bwd(fold(tk_ref), kb_ref[...], gk_ref[...], cos_v, sin_v, bd)
        dkb_ref[...] = dkb
        dgk_ref[...] += dgk
        dgq = jnp.zeros((1, TB), F32)
        for j in range(HB // 2):
            sl = slice(j * TB, (j + 1) * TB)
            dqb, dg = _norm_rope_bwd(dqh_ref[:, sl], qb_ref[:, sl], gq_ref[...], cos_v, sin_v, bd)
            dqb_ref[:, sl] = dqb
            dgq = dgq + dg
        dgq_ref[...] += dgq

    vec = pl.BlockSpec((1, TB), lambda n: (0, 0))
    tab = pl.BlockSpec((TB, TB), lambda n: (n, 0))
    cur = lambda w: pl.BlockSpec((TB, w), lambda n: (n, 0))
    nxt = pl.BlockSpec((TB, 256), lambda n: (jnp.minimum(n + 1, nb - 1), 0))
    meta = pl.BlockSpec((N_META, 256), lambda n: (0, 0))
    return pl.pallas_call(
        body, name=name, grid=(nb,),
        in_specs=[pl.BlockSpec((TB, 1024), lambda n: (n, C_QB)), pl.BlockSpec((TB, TB), lambda n: (n, C_KB)),
                  vec, vec, tab, tab, cur(1024), cur(256), nxt, meta, cur(256), nxt, meta],
        out_specs=[cur(1024), cur(TB), cur(TB), vec, vec],
        out_shape=[jax.ShapeDtypeStruct((p, 1024), F32), jax.ShapeDtypeStruct((p, TB), F32),
                   jax.ShapeDtypeStruct((p, TB), F32), jax.ShapeDtypeStruct((1, TB), F32),
                   jax.ShapeDtypeStruct((1, TB), F32)],
        scratch_shapes=[pltpu.VMEM((TB, 256), F32), pltpu.VMEM((TB, 256), F32)],
        compiler_params=_params("arbitrary"),
    )(proj, proj, gq, gk, cos, sin, dqh, dkc, dkp, dkm, dvc, dvp, dvm)


EXT = TB + N_META


def _pool_count_inv(n, w):
    t = n * TB + _iota((TB, 1), 0)
    cnt = jnp.clip(t - (PAD_FRONT - 1), 1, w)
    return 1.0 / cnt.astype(F32)


def _silu_parts(gate):
    s = _sigmoid(gate)
    return gate * s, s * (1.0 + gate * (1.0 - s))


def _mix_fwd(proj, oa, yb, hg, pool_w, pool_scale, *, name):
    p = proj.shape[0]

    def body(ga_ref, gb_ref, uc_ref, up_ref, gc_ref, oa_ref, yb_ref, hg_ref, pw_ref, ps_ref, mx_ref, pooled_ref):
        n = pl.program_id(0)
        valid = ((n * TB + _iota((TB, 1), 0)) >= PAD_FRONT).astype(F32)
        for hd in range(HA):
            sl = slice(hd * TB, (hd + 1) * TB)
            o = oa_ref[:, sl]
            r = lax.rsqrt(jnp.mean(o * o, axis=-1, keepdims=True) + RMS_EPS)
            act, _ = _silu_parts(ga_ref[:, sl])
            mx_ref[:, sl] = (o * r * hg_ref[...] * act).astype(BF16)
        for j in range(HB // 2):
            sl = slice(j * TB, (j + 1) * TB)
            act, _ = _silu_parts(gb_ref[:, sl])
            mx_ref[:, 512 + j * TB:512 + (j + 1) * TB] = (yb_ref[:, sl] * act).astype(BF16)
        ri, ci = _iota((TB, EXT), 0), _iota((TB, EXT), 1)
        has_prev = jnp.where(n == 0, 0.0, 1.0)
        for gi, w in enumerate(POOL_WINDOWS):
            sl = slice(gi * TB, (gi + 1) * TB)
            ug = uc_ref[:, sl] * valid
            ext = jnp.concatenate([up_ref[:, sl] * has_prev, ug], axis=0)
            band = jnp.where((ci <= ri + N_META) & (ci > ri + N_META - w), 1.0, 0.0).astype(BF16)
            pooled = (_xdot(band, ext) * _pool_count_inv(n, w) - ug) * valid
            pooled_ref[:, sl] = pooled
            yc = _dot(pooled.astype(BF16), pw_ref[gi], NN) * ps_ref[:, sl]
            act, _ = _silu_parts(gc_ref[:, sl])
            mx_ref[:, 1536 + gi * TB:1536 + (gi + 1) * TB] = (yc * act).astype(BF16)

    cur = lambda w, c=0: pl.BlockSpec((TB, w), lambda n, c=c: (n, c))
    prev16 = pl.BlockSpec((N_META, 512), lambda n: (jnp.maximum(n * (TB // N_META) - 1, 0), C_UC))
    return pl.pallas_call(
        body, name=name, grid=(p // TB,),
        in_specs=[cur(512, C_GA), cur(1024, C_GB), cur(512, C_UC), prev16, cur(512, C_GC), cur(512), cur(1024),
                  pl.BlockSpec((1, TB), lambda n: (0, 0)), pl.BlockSpec((4, TB, TB), lambda n: (0, 0, 0)),
                  pl.BlockSpec((1, 512), lambda n: (0, 0))],
        out_specs=[cur(MIX), cur(512)],
        out_shape=[jax.ShapeDtypeStruct((p, MIX), BF16), jax.ShapeDtypeStruct((p, 512), F32)],
        compiler_params=_params("parallel"),
    )(proj, proj, proj, proj, proj, oa, yb, hg, pool_w, pool_scale)


def _mix_bwd(proj, oa, yb, pooled, hg, pool_w, pool_scale, d_mixed, *, name):
    p = proj.shape[0]

    def body(ga_ref, gb_ref, gc_ref, oa_ref, yb_ref, pooled_ref, hg_ref, pw_ref, ps_ref, dm_ref,
             doa_ref, dyb_ref, dga_ref, dgb_ref, dgc_ref, dp_ref, dhg_ref, dpw_ref, dps_ref):
        n = pl.program_id(0)

        @pl.when(n == 0)
        def _():
            dhg_ref[...] = jnp.zeros_like(dhg_ref)
            dpw_ref[...] = jnp.zeros_like(dpw_ref)
            dps_ref[...] = jnp.zeros_like(dps_ref)

        valid = ((n * TB + _iota((TB, 1), 0)) >= PAD_FRONT).astype(F32)
        dhg = jnp.zeros((1, TB), F32)
        for hd in range(HA):
            sl = slice(hd * TB, (hd + 1) * TB)
            o = oa_ref[:, sl]
            r = lax.rsqrt(jnp.mean(o * o, axis=-1, keepdims=True) + RMS_EPS)
            on = o * r
            gate = ga_ref[:, sl]
            act, dact = _silu_parts(gate)
            dmx = dm_ref[:, sl]
            d_ya = dmx * act
            dga_ref[:, sl] = dmx * on * hg_ref[...] * dact
            dyn = d_ya * hg_ref[...]
            doa_ref[:, sl] = r * (dyn - on * jnp.mean(dyn * on, axis=-1, keepdims=True))
            dhg = dhg + jnp.sum(d_ya * on, axis=0, keepdims=True)
        dhg_ref[...] += dhg
        for j in range(HB // 2):
            sl = slice(j * TB, (j + 1) * TB)
            act, dact = _silu_parts(gb_ref[:, sl])
            dmx = dm_ref[:, 512 + j * TB:512 + (j + 1) * TB]
            dyb_ref[:, sl] = dmx * act
            dgb_ref[:, sl] = dmx * yb_ref[:, sl] * dact
        for gi in range(len(POOL_WINDOWS)):
            sl = slice(gi * TB, (gi + 1) * TB)
            pooled = pooled_ref[:, sl]
            pooled_b = pooled.astype(BF16)
            t = _dot(pooled_b, pw_ref[gi], NN)
            act, dact = _silu_parts(gc_ref[:, sl])
            dmx = dm_ref[:, 1536 + gi * TB:1536 + (gi + 1) * TB]
            d_yc = dmx * act
            dgc_ref[:, sl] = dmx * t * ps_ref[:, sl] * dact
            dps_ref[:, sl] += jnp.sum(d_yc * t, axis=0, keepdims=True)
            d_t = (d_yc * ps_ref[:, sl]).astype(BF16)
            dp_ref[:, sl] = _dot(d_t, pw_ref[gi], NT) * valid
            dpw_ref[gi] += _dot(pooled_b, d_t, TN)

    cur = lambda w, c=0: pl.BlockSpec((TB, w), lambda n, c=c: (n, c))
    return pl.pallas_call(
        body, name=name, grid=(p // TB,),
        in_specs=[cur(512, C_GA), cur(1024, C_GB), cur(512, C_GC), cur(512), cur(1024), cur(512),
                  pl.BlockSpec((1, TB), lambda n: (0, 0)), pl.BlockSpec((4, TB, TB), lambda n: (0, 0, 0)),
                  pl.BlockSpec((1, 512), lambda n: (0, 0)), cur(MIX)],
        out_specs=[cur(512), cur(1024), cur(512), cur(1024), cur(512), cur(512),
                   pl.BlockSpec((1, TB), lambda n: (0, 0)), pl.BlockSpec((4, TB, TB), lambda n: (0, 0, 0)),
                   pl.BlockSpec((1, 512), lambda n: (0, 0))],
        out_shape=[jax.ShapeDtypeStruct((p, 512), F32), jax.ShapeDtypeStruct((p, 1024), F32),
                   jax.ShapeDtypeStruct((p, 512), F32), jax.ShapeDtypeStruct((p, 1024), F32),
                   jax.ShapeDtypeStruct((p, 512), F32), jax.ShapeDtypeStruct((p, 512), F32),
                   jax.ShapeDtypeStruct((1, TB), F32), jax.ShapeDtypeStruct((4, TB, TB), F32),
                   jax.ShapeDtypeStruct((1, 512), F32)],
        compiler_params=_params("arbitrary"),
    )(proj, proj, proj, oa, yb, pooled, hg, pool_w, pool_scale, d_mixed)


def _pool_bwd(dp, *, name):
    p = dp.shape[0]
    nb = p // TB

    def body(dp_ref, dn_ref, duc_ref):
        n = pl.program_id(0)
        valid = ((n * TB + _iota((TB, 1), 0)) >= PAD_FRONT).astype(F32)
        has_next = jnp.where(n == nb - 1, 0.0, 1.0)
        ri, ci = _iota((TB, EXT), 0), _iota((TB, EXT), 1)
        for gi, w in enumerate(POOL_WINDOWS):
            sl = slice(gi * TB, (gi + 1) * TB)
            d_p = dp_ref[:, sl]
            ext = jnp.concatenate([d_p * _pool_count_inv(n, w), dn_ref[:, sl] * (has_next / w)], axis=0)
            band = jnp.where((ci >= ri) & (ci < ri + w), 1.0, 0.0).astype(BF16)
            duc_ref[:, sl] = (_xdot(band, ext) - d_p) * valid

    return pl.pallas_call(
        body, name=name, grid=(nb,),
        in_specs=[pl.BlockSpec((TB, 512), lambda n: (n, 0)),
                  pl.BlockSpec((N_META, 512), lambda n: (jnp.minimum(n + 1, nb - 1) * (TB // N_META), 0))],
        out_specs=pl.BlockSpec((TB, 512), lambda n: (n, 0)),
        out_shape=jax.ShapeDtypeStruct((p, 512), F32),
        compiler_params=_params("parallel"),
    )(dp, dp)


def _carried(carries, key, local, fn, *args, **kw):
    if key not in carries:
        return fn(*args, **kw)
    make_src, gather, done = carries[key]
    *outs, stack = fn(*args, carry=(make_src(local), gather), **kw)
    done(stack)
    return outs


def _layer_fwd(h, w, tag, carries):
    xn = _rmsnorm_fwd(h, w["norm_g"], name=f"rmsnorm_fwd{tag}")
    proj, = _carried(carries, "in_proj", None, _mm_nn, xn, w["w_in"], name=f"in_proj{tag}")
    oa, sck = _hgrn_fwd(proj, w["lb"], name=f"hgrn_fwd{tag}")
    qh, k2, v2 = _qk_prep(proj, w["gq"], w["gk"], w["cos"], w["sin"], name=f"qk_prep{tag}")
    yb, = _carried(carries, "attn_fwd", None, _attn_fwd, qh, k2, v2, w["sinks"], name=f"attn_fwd{tag}")
    mixed, pooled = _mix_fwd(proj, oa, yb, w["hg"], w["pool_w"], w["pool_scale"], name=f"mix_fwd{tag}")
    h_next, = _mm_nn(mixed, w["w_out"], h, name=f"out_proj{tag}", tn=512)
    saved = dict(h=h, xn=xn, proj=proj, oa=oa, sck=sck, qh=qh, k2=k2, v2=v2, yb=yb, mixed=mixed, pooled=pooled)
    return h_next, saved


def _layer_bwd(dh_out, s, w, tag, carries):
    g = {}
    dhb = dh_out.astype(BF16)
    d_mixed = _mm_nt(dhb, w["w_out"], name=f"d_mixed{tag}", tm=1664, tn=512)
    g["w_out"] = _mm_tn(s["mixed"], dhb, name=f"dw_out{tag}", tn=1024)
    d_oa, d_yb, d_ga, d_gb, d_gc, d_p, g["hg"], g["pool_w"], g["pool_scale"] = _mix_bwd(
        s["proj"], s["oa"], s["yb"], s["pooled"], w["hg"], w["pool_w"], w["pool_scale"], d_mixed, name=f"mix_bwd{tag}")
    d_uc = _pool_bwd(d_p, name=f"pool_bwd{tag}")
    d_qh, dkc, dkp, dvc, dvp, dkm, dvm, g["sinks"] = _carried(
        carries, "attn_bwd", g, _attn_bwd, s["qh"], s["k2"], s["v2"], w["sinks"], s["yb"], d_yb, name=f"attn_bwd{tag}")
    d_qb, d_kb, d_vb, g["gq"], g["gk"] = _qk_post(s["proj"], w["gq"], w["gk"], w["cos"], w["sin"], d_qh, dkc, dkp, dkm,
                                                  dvc, dvp, dvm, name=f"qk_post{tag}")
    d_qa, d_fa, d_ia, g["lb"] = _carried(carries, "hgrn_bwd", g, _hgrn_bwd, s["proj"], w["lb"], s["sck"], d_oa,
                                         name=f"hgrn_bwd{tag}")
    d_proj = jnp.concatenate([d_qa, d_fa, d_ia, d_ga, d_qb, d_gb, d_uc, d_gc, d_kb, d_vb], axis=1).astype(BF16)
    d_xn = _mm_nt(d_proj, w["w_in"], name=f"d_xn{tag}", tk=PROJ_COLS)
    g["w_in"] = _mm_tn(s["xn"], d_proj, name=f"dw_in{tag}")
    dh, g["norm_g"] = _carried(carries, "rmsnorm_bwd", g, _rmsnorm_bwd, d_xn, s["h"], w["norm_g"], dh_out,
                               name=f"rmsnorm_bwd{tag}")
    return dh, g


def _peers():
    x, y, c = lax.axis_index("x"), lax.axis_index("y"), lax.axis_index("c")
    out = []
    for k in range(1, N_DEV):
        kx, ky, kc = (k >> 2) & 1, (k >> 1) & 1, k & 1
        px, py, pc = x ^ kx, y ^ ky, c ^ kc
        out.append(((px, py, pc), 4 * px + 2 * py + pc))
    return 4 * x + 2 * y + c, out


def _exchange_copies(src_ref, out_ref, send_sems, recv_sems, local_sem, gather):
    me, peers = _peers()
    mine = pltpu.make_async_copy(src_ref if gather else src_ref.at[me], out_ref.at[me], local_sem)
    copies = []
    for k, (dev, idx) in enumerate(peers):
        copies.append(pltpu.make_async_remote_copy(
            src_ref=src_ref if gather else src_ref.at[idx], dst_ref=out_ref.at[me],
            send_sem=send_sems.at[k], recv_sem=recv_sems.at[k],
            device_id=dev, device_id_type=pl.DeviceIdType.MESH))
    return mine, copies


def _exchange_start(*refs, gather):
    mine, copies = _exchange_copies(*refs, gather)
    mine.start()
    for cp in copies:
        cp.start()


def _exchange_wait(*refs, gather):
    mine, copies = _exchange_copies(*refs, gather)
    for cp in copies:
        cp.wait_recv()
    for cp in copies:
        cp.wait_send()
    mine.wait()


def _exchange_scratch():
    return [pltpu.SemaphoreType.DMA((N_DEV - 1,)), pltpu.SemaphoreType.DMA((N_DEV - 1,)), pltpu.SemaphoreType.DMA]


def _exchange(src, *, gather, name):
    rows, cols = src.shape[-2:]

    def body(src_ref, out_ref, send_sems, recv_sems, local_sem):
        _exchange_start(src_ref, out_ref, send_sems, recv_sems, local_sem, gather=gather)
        _exchange_wait(src_ref, out_ref, send_sems, recv_sems, local_sem, gather=gather)

    return pl.pallas_call(
        body, name=name,
        in_specs=[pl.BlockSpec(memory_space=pl.ANY)], out_specs=pl.BlockSpec(memory_space=pl.ANY),
        out_shape=jax.ShapeDtypeStruct((N_DEV, rows, cols), src.dtype),
        scratch_shapes=_exchange_scratch(),
    )(src)


def _call(body, *, name, grid, in_specs, out_specs, out_shape, args, sem, scratch_shapes=(), carry=None):
    if carry is None:
        return pl.pallas_call(
            body, name=name, grid=grid, in_specs=list(in_specs), out_specs=list(out_specs), out_shape=list(out_shape),
            scratch_shapes=list(scratch_shapes), compiler_params=_params(*sem))(*args)
    src, gather = carry
    n_in, n_out, n_scr = len(in_specs), len(out_specs), len(scratch_shapes)
    rows, cols = src.shape[-2:]

    def carrying(*refs):
        ins, src_ref = refs[:n_in], refs[n_in]
        outs, dst_ref = refs[n_in + 1:n_in + 1 + n_out], refs[n_in + 1 + n_out]
        scr = refs[n_in + 2 + n_out:]
        exch = (src_ref, dst_ref) + tuple(scr[n_scr:])
        first, last = None, None
        for a, size in enumerate(grid):
            f, l = pl.program_id(a) == 0, pl.program_id(a) == size - 1
            first = f if first is None else first & f
            last = l if last is None else last & l

        @pl.when(first)
        def _():
            _exchange_start(*exch, gather=gather)

        body(*ins, *outs, *scr[:n_scr])

        @pl.when(last)
        def _():
            _exchange_wait(*exch, gather=gather)

    hbm = pl.BlockSpec(memory_space=pl.ANY)
    return pl.pallas_call(
        carrying, name=name, grid=grid, in_specs=list(in_specs) + [hbm], out_specs=list(out_specs) + [hbm],
        out_shape=list(out_shape) + [jax.ShapeDtypeStruct((N_DEV, rows, cols), src.dtype)],
        scratch_shapes=list(scratch_shapes) + _exchange_scratch(),
        compiler_params=_params(*(("arbitrary",) * len(grid))))(*args, src)


def _adamw(stacks, w, m, v, *, name):
    nl = len(stacks)
    rows, cols = stacks[0].shape[1:]
    tr = rows
    for cand in (256, 128, 64, 32, 16):
        if rows % cand == 0:
            tr = cand
            break
    nt = rows // tr

    def body(*refs):
        s_refs = refs[:nl]
        w_ref, m_ref, v_ref, g_ref, d_ref, nm_ref, nv_ref = refs[nl:]
        for l, s_ref in enumerate(s_refs):
            @pl.when(pl.program_id(0) == l)
            def _(s_ref=s_ref):
                acc = s_ref[0].astype(F32)
                for d in range(1, N_DEV):
                    acc = acc + s_ref[d].astype(F32)
                g_ref[...] = acc

        g = g_ref[...]
        nm = ADAM_B1 * m_ref[...] + (1.0 - ADAM_B1) * g
        nv = ADAM_B2 * v_ref[...] + (1.0 - ADAM_B2) * (g * g)
        m_hat = nm / (1.0 - ADAM_B1 ** ADAM_STEP)
        v_hat = nv / (1.0 - ADAM_B2 ** ADAM_STEP)
        d_ref[...] = -ADAM_LR * (m_hat / (jnp.sqrt(v_hat) + ADAM_EPS) + ADAM_WD * w_ref[...])
        nm_ref[...] = nm
        nv_ref[...] = nv

    blk = pl.BlockSpec((tr, cols), lambda l, i: (l * nt + i, 0))
    return pl.pallas_call(
        body, name=name, grid=(nl, nt),
        in_specs=[pl.BlockSpec((N_DEV, tr, cols), lambda l, i: (0, i, 0))] * nl + [blk, blk, blk],
        out_specs=[blk] * 4, out_shape=[jax.ShapeDtypeStruct((nl * rows, cols), F32)] * 4,
        compiler_params=_params("arbitrary", "arbitrary"),
    )(*stacks, w, m, v)


def _lb_all(lb_logits):
    sm = jax.nn.softmax(lb_logits.astype(F32), axis=0)
    return jnp.cumsum(sm, axis=0) - sm[0:1]


def _rope_tables(p):
    half = DH // 2
    inv = jnp.power(ROPE_THETA, -jnp.arange(half, dtype=F32) * 2.0 / DH)
    pos = (jnp.arange(p) - PAD_FRONT).astype(F32)
    ang = pos[:, None] * inv[None, :]
    cos, sin = jnp.cos(ang), jnp.sin(ang)
    return jnp.tile(cos, (1, 4)), jnp.tile(jnp.concatenate([-sin, sin], axis=1), (1, 2))


def _permute_cols(w):
    return jnp.concatenate([w[:, :3072], w[:, 3328:], w[:, 3072:3328]], axis=1)


def _unpermute_cols(w):
    return jnp.concatenate([w[:, :3072], w[:, 5120:], w[:, 3072:5120]], axis=1)


SMALL = (("lb_logits", (DEPTH, 512)), ("norm_g", (DEPTH, D_MODEL)), ("q_norm_g", (DEPTH, DH)),
         ("k_norm_g", (DEPTH, DH)), ("attn_sinks", (DEPTH, HB)), ("hgrn_norm_g", (DEPTH, 128)),
         ("pool_w", (DEPTH, 4, 128, 128)), ("pool_scale", (DEPTH, 512)))


def _pack_small(d):
    flat = jnp.concatenate([d[k].astype(F32).reshape(-1) for k, _ in SMALL])
    pad = (-flat.shape[0]) % (8 * 128)
    return jnp.pad(flat, (0, pad)).reshape(-1, 128)


def _unpack_small(a):
    flat = a.reshape(-1)
    out, off = {}, 0
    for k, shp in SMALL:
        n = int(np.prod(shp))
        out[k] = flat[off:off + n].reshape(shp)
        off += n
    return out


def kernel(x, meta_tokens, lb_logits, norm_g, w_in, q_norm_g, k_norm_g, attn_sinks, hgrn_norm_g, pool_w, pool_scale, w_out, loss_target, m_meta_tokens, m_lb_logits, m_norm_g, m_w_in, m_q_norm_g, m_k_norm_g, m_attn_sinks, m_hgrn_norm_g, m_pool_w, m_pool_scale, m_w_out, v_meta_tokens, v_lb_logits, v_norm_g, v_w_in, v_q_norm_g, v_k_norm_g, v_attn_sinks, v_hgrn_norm_g, v_pool_w, v_pool_scale, v_w_out):
    seq = x.shape[1]
    p = seq + TB
    cs = PROJ_COLS // N_DEV
    rs = MIX // N_DEV
    ms = D_MODEL // N_DEV

    full_w_in = lambda st: _permute_cols(st.transpose(1, 0, 2).reshape(D_MODEL, PROJ_COLS))
    dw_in_blocks = lambda g: _unpermute_cols(g["w_in"]).reshape(D_MODEL, N_DEV, cs).transpose(1, 0, 2).astype(BF16)
    dw_out_blocks = lambda g: g["w_out"].reshape(N_DEV, rs, D_MODEL).astype(BF16)
    w_in_bf = w_in.astype(BF16)

    lb_all, lb_vjp = jax.vjp(_lb_all, lb_logits)
    cos, sin = _rope_tables(p)
    layers = []
    for l in range(DEPTH):
        layers.append(dict(
            norm_g=norm_g[l][None], lb=lb_all[l][None],
            gq=jnp.tile(q_norm_g[l], 2)[None], gk=jnp.tile(k_norm_g[l], 2)[None], sinks=attn_sinks[l],
            hg=hgrn_norm_g[l][None], pool_w=pool_w[l].astype(BF16), pool_scale=pool_scale[l][None], cos=cos, sin=sin))
    layers[0]["w_in"] = full_w_in(_exchange(w_in_bf[0], gather=True, name="gather_w_in_l0"))
    meta_all = _exchange(meta_tokens, gather=True, name="gather_meta")
    meta_full = meta_all.transpose(1, 0, 2).reshape(N_META, D_MODEL)

    def got_w_out(st):
        st = st.reshape(N_DEV, DEPTH, rs, D_MODEL)
        for l in range(DEPTH):
            layers[l]["w_out"] = st[:, l].reshape(MIX, D_MODEL)

    def got_w_in_l1(st):
        layers[1]["w_in"] = full_w_in(st)

    fwd_carries = [
        dict(in_proj=(lambda _: w_out.reshape(DEPTH * rs, D_MODEL).astype(BF16), True, got_w_out),
             attn_fwd=(lambda _: w_in_bf[1], True, got_w_in_l1)),
        {}]
    h = jnp.concatenate([jnp.zeros((PAD_FRONT, D_MODEL), F32), meta_full, x[0]], axis=0)
    saved = []
    for l in range(DEPTH):
        h, s = _layer_fwd(h, layers[l], f"_l{l}", fwd_carries[l])
        saved.append(s)
    dh, sq = _loss_grad(h, loss_target[0], name="loss_grad")
    loss = lax.psum(0.5 * jnp.sum(sq) / D_MODEL, ("x", "y", "c"))

    grads = [None] * DEPTH
    win_stacks, wout_stacks = [None] * DEPTH, [None] * DEPTH

    def into(stacks, l):
        def done(st):
            stacks[l] = st
        return done

    bwd_carries = [
        dict(attn_bwd=(lambda _: dw_in_blocks(grads[1]), False, into(win_stacks, 1)),
             hgrn_bwd=(lambda g: dw_out_blocks(g), False, into(wout_stacks, 0)),
             rmsnorm_bwd=(lambda g: dw_in_blocks(g), False, into(win_stacks, 0))),
        dict(attn_bwd=(lambda g: dw_out_blocks(g), False, into(wout_stacks, 1)))]
    for l in reversed(range(DEPTH)):
        dh, grads[l] = _layer_bwd(dh, saved[l], layers[l], f"_l{l}", bwd_carries[l])
    grad_x = dh[TB:][None]
    dmeta = dh[PAD_FRONT:TB].reshape(N_META, N_DEV, ms).transpose(1, 0, 2)
    meta_stack = _exchange(dmeta, gather=False, name="scatter_dmeta")

    stk = lambda k: jnp.stack([grads[l][k][0] for l in range(DEPTH)])
    fold = lambda a: a[:, :DH] + a[:, DH:]
    small_local = dict(
        lb_logits=lb_vjp(stk("lb"))[0], norm_g=stk("norm_g"), q_norm_g=fold(stk("gq")), k_norm_g=fold(stk("gk")),
        attn_sinks=stk("sinks")[:, :HB], hgrn_norm_g=stk("hg"),
        pool_w=jnp.stack([grads[l]["pool_w"] for l in range(DEPTH)]), pool_scale=stk("pool_scale"))
    small_stack = _exchange(_pack_small(small_local), gather=True, name="gather_small_grads")

    g_win, d_win, nm_win, nv_win = _adamw(win_stacks, w_in.reshape(DEPTH * D_MODEL, cs), m_w_in.reshape(DEPTH * D_MODEL, cs),
                                          v_w_in.reshape(DEPTH * D_MODEL, cs), name="adamw_w_in")
    g_wout, d_wout, nm_wout, nv_wout = _adamw(wout_stacks, w_out.reshape(DEPTH * rs, D_MODEL), m_w_out.reshape(DEPTH * rs, D_MODEL),
                                              v_w_out.reshape(DEPTH * rs, D_MODEL), name="adamw_w_out")
    g_meta, d_meta, nm_meta, nv_meta = _adamw([meta_stack], meta_tokens, m_meta_tokens, v_meta_tokens, name="adamw_meta")
    small_w = dict(lb_logits=lb_logits, norm_g=norm_g, q_norm_g=q_norm_g, k_norm_g=k_norm_g, attn_sinks=attn_sinks,
                   hgrn_norm_g=hgrn_norm_g, pool_w=pool_w, pool_scale=pool_scale)
    small_m = dict(lb_logits=m_lb_logits, norm_g=m_norm_g, q_norm_g=m_q_norm_g, k_norm_g=m_k_norm_g, attn_sinks=m_attn_sinks,
                   hgrn_norm_g=m_hgrn_norm_g, pool_w=m_pool_w, pool_scale=m_pool_scale)
    small_v = dict(lb_logits=v_lb_logits, norm_g=v_norm_g, q_norm_g=v_q_norm_g, k_norm_g=v_k_norm_g, attn_sinks=v_attn_sinks,
                   hgrn_norm_g=v_hgrn_norm_g, pool_w=v_pool_w, pool_scale=v_pool_scale)
    small_out = [_unpack_small(a) for a in _adamw([small_stack], _pack_small(small_w), _pack_small(small_m),
                                                  _pack_small(small_v), name="adamw_small")]

    big = dict(
        meta_tokens=(g_meta, d_meta, nm_meta, nv_meta),
        w_in=tuple(a.reshape(DEPTH, D_MODEL, cs) for a in (g_win, d_win, nm_win, nv_win)),
        w_out=tuple(a.reshape(DEPTH, rs, D_MODEL) for a in (g_wout, d_wout, nm_wout, nv_wout)))
    order = ("meta_tokens", "lb_logits", "norm_g", "w_in", "q_norm_g", "k_norm_g", "attn_sinks", "hgrn_norm_g",
             "pool_w", "pool_scale", "w_out")
    outs = [loss, grad_x]
    for kind in range(4):
        for k in order:
            outs.append(big[k][kind] if k in big else small_out[kind][k])
    return tuple(outs)
```

```python
import functools

import numpy as np
import jax
import jax.numpy as jnp
from jax import lax
from jax.experimental import pallas as pl
from jax.experimental.pallas import tpu as pltpu

F32, BF16 = jnp.float32, jnp.bfloat16

D_MODEL = 2048
DEPTH = 2
N_META = 16
TB = 128
PAD_FRONT = TB - N_META
RMS_EPS = 1e-6
NEG_INF = -1e30
LOG_FLOOR = 1e-30
HA, DK_A = 4, 128
CH = 16
NCH = TB // CH
HB, KVH, DH = 16, 2, 64
GRP = HB // KVH
ROPE_THETA = 10000.0
POOL_WINDOWS = (2, 4, 8, 16)
PROJ_COLS = 5376
MIX = 2048
N_DEV = 8
C_QA, C_FA, C_IA, C_GA = 0, 1, 2, 3
C_QB, C_GB = 2, 3
C_UC, C_GC = 8, 9
C_KB, C_VB = 40, 41

ADAM_LR, ADAM_B1, ADAM_B2, ADAM_EPS, ADAM_WD, ADAM_STEP = 0.001, 0.9, 0.999, 1e-08, 0.01, 10

VMEM_LIMIT = 48 * 1024 * 1024

NN = ((1,), (0,))
NT = ((1,), (1,))
TN = ((0,), (0,))


def _dot(a, b, dims):
    return lax.dot_general(a, b, (dims, ((), ())), preferred_element_type=F32)


def _split3(x):
    hi = x.astype(BF16)
    r = x - hi.astype(F32)
    mid = r.astype(BF16)
    lo = (r - mid.astype(F32)).astype(BF16)
    return hi, mid, lo


def _xdot(m01, x):
    hi, mid, lo = _split3(x)
    return _dot(m01, hi, NN) + _dot(m01, mid, NN) + _dot(m01, lo, NN)


def _xdot_r(x, m01):
    hi, mid, lo = _split3(x)
    return _dot(hi, m01, NN) + _dot(mid, m01, NN) + _dot(lo, m01, NN)


def _iota(shape, dim):
    return lax.broadcasted_iota(jnp.int32, shape, dim)


def _params(*sem):
    return pltpu.CompilerParams(dimension_semantics=sem, vmem_limit_bytes=VMEM_LIMIT)


def _row_tile(p, target):
    best = TB
    t = TB
    while t <= target:
        if p % t == 0:
            best = t
        t += TB
    return best


def _col_tile(n, target):
    best = 128
    t = 128
    while t <= target:
        if n % t == 0:
            best = t
        t += 128
    return best


def _sigmoid(x):
    return 1.0 / (1.0 + jnp.exp(-x))


def _mm_nn(a, b, res=None, *, name, tm=1664, tn=768, carry=None):
    m, k = a.shape
    n = b.shape[1]
    tm, tn = _row_tile(m, tm), _col_tile(n, tn)

    def body(*refs):
        if res is None:
            a_ref, b_ref, o_ref = refs
            o_ref[...] = _dot(a_ref[...], b_ref[...], NN)
        else:
            a_ref, b_ref, r_ref, o_ref = refs
            o_ref[...] = r_ref[...] + _dot(a_ref[...], b_ref[...], NN)

    in_specs = [pl.BlockSpec((tm, k), lambda j, i: (i, 0)), pl.BlockSpec((k, tn), lambda j, i: (0, j))]
    args = [a, b]
    if res is not None:
        in_specs.append(pl.BlockSpec((tm, tn), lambda j, i: (i, j)))
        args.append(res)
    return _call(
        body, name=name, grid=(n // tn, m // tm), in_specs=in_specs,
        out_specs=[pl.BlockSpec((tm, tn), lambda j, i: (i, j))],
        out_shape=[jax.ShapeDtypeStruct((m, n), F32)],
        args=args, sem=("parallel", "parallel"), carry=carry)


def _mm_nt(a, b, *, name, tm=640, tn=512, tk=2048):
    m, k = a.shape
    n = b.shape[0]
    tm, tn, tk = _row_tile(m, tm), _col_tile(n, tn), _col_tile(k, tk)

    def body(a_ref, b_ref, o_ref):
        @pl.when(pl.program_id(2) == 0)
        def _():
            o_ref[...] = jnp.zeros_like(o_ref)

        o_ref[...] += _dot(a_ref[...], b_ref[...], NT)

    return pl.pallas_call(
        body, name=name, grid=(n // tn, m // tm, k // tk),
        in_specs=[pl.BlockSpec((tm, tk), lambda j, i, kk: (i, kk)), pl.BlockSpec((tn, tk), lambda j, i, kk: (j, kk))],
        out_specs=pl.BlockSpec((tm, tn), lambda j, i, kk: (i, j)),
        out_shape=jax.ShapeDtypeStruct((m, n), F32),
        compiler_params=_params("parallel", "parallel", "arbitrary"),
    )(a, b)


def _mm_tn(a, b, *, name, tm=1024, tn=1344, tk=1664, m_part=None, carry=None):
    k, m = a.shape
    n = b.shape[1]
    first, m = (0, m) if m_part is None else (m_part[0], m // m_part[1])
    tm, tn, tk = _col_tile(m, tm), _col_tile(n, tn), _row_tile(k, tk)
    first *= m // tm

    def body(a_ref, b_ref, o_ref):
        @pl.when(pl.program_id(2) == 0)
        def _():
            o_ref[...] = jnp.zeros_like(o_ref)

        o_ref[...] += _dot(a_ref[...], b_ref[...], TN)

    return _call(
        body, name=name, grid=(m // tm, n // tn, k // tk),
        in_specs=[pl.BlockSpec((tk, tm), lambda i, j, kk: (kk, first + i)), pl.BlockSpec((tk, tn), lambda i, j, kk: (kk, j))],
        out_specs=[pl.BlockSpec((tm, tn), lambda i, j, kk: (i, j))],
        out_shape=[jax.ShapeDtypeStruct((m, n), F32)],
        args=(a, b), sem=("parallel", "parallel", "arbitrary"), carry=carry)


def _rmsnorm_fwd(h, g, *, name):
    p, dm = h.shape
    tm = _row_tile(p, 640)

    def body(h_ref, g_ref, xn_ref):
        hv = h_ref[...]
        r = lax.rsqrt(jnp.mean(hv * hv, axis=-1, keepdims=True) + RMS_EPS)
        xn_ref[...] = (hv * r * g_ref[...]).astype(BF16)

    return pl.pallas_call(
        body, name=name, grid=(p // tm,),
        in_specs=[pl.BlockSpec((tm, dm), lambda i: (i, 0)), pl.BlockSpec((1, dm), lambda i: (0, 0))],
        out_specs=pl.BlockSpec((tm, dm), lambda i: (i, 0)),
        out_shape=jax.ShapeDtypeStruct((p, dm), BF16),
        compiler_params=_params("parallel"),
    )(h, g)


def _rmsnorm_bwd(dxn, h, g, dh_out, *, name, first_layer=False, carry=None):
    p, dm = h.shape
    tm = TB if first_layer else _row_tile(p, 384)

    def body(dxn_ref, h_ref, g_ref, dho_ref, out_ref, aux_ref, dg_ref):
        hv = h_ref[...]
        r = lax.rsqrt(jnp.mean(hv * hv, axis=-1, keepdims=True) + RMS_EPS)
        xh = hv * r
        dy = dxn_ref[...]
        dyn = dy * g_ref[...]
        dh = dho_ref[...] + r * (dyn - xh * jnp.mean(dyn * xh, axis=-1, keepdims=True))
        out_ref[...] = dh

        @pl.when(pl.program_id(0) == 0)
        def _():
            dg_ref[...] = jnp.zeros_like(dg_ref)
            if first_layer:
                aux_ref[...] = dh[PAD_FRONT:TB]

        if not first_layer:
            aux_ref[...] = dh.astype(BF16)
        dg_ref[...] += jnp.sum(dy * xh, axis=0, keepdims=True)

    row = pl.BlockSpec((tm, dm), lambda i: (i, 0))
    vec = pl.BlockSpec((1, dm), lambda i: (0, 0))
    if first_layer:
        out_specs = [pl.BlockSpec((TB, dm), lambda i: (jnp.maximum(i - 1, 0), 0)), pl.BlockSpec((N_META, dm), lambda i: (0, 0)), vec]
        out_shape = [jax.ShapeDtypeStruct((p - TB, dm), F32), jax.ShapeDtypeStruct((N_META, dm), F32)]
    else:
        out_specs = [row, row, vec]
        out_shape = [jax.ShapeDtypeStruct((p, dm), F32), jax.ShapeDtypeStruct((p, dm), BF16)]
    return _call(
        body, name=name, grid=(p // tm,),
        in_specs=[row, row, vec, row], out_specs=out_specs,
        out_shape=out_shape + [jax.ShapeDtypeStruct((1, dm), F32)],
        args=(dxn, h, g, dh_out), sem=("arbitrary",), carry=carry)


def _loss_grad(h, target, *, name):
    p, dm = h.shape

    def body(h_ref, t_ref, dh_ref, dhb_ref, sq_ref):
        n = pl.program_id(0)

        @pl.when(n == 0)
        def _():
            dh_ref[...] = jnp.zeros_like(dh_ref)
            dhb_ref[...] = jnp.zeros_like(dhb_ref)
            sq_ref[...] = jnp.zeros_like(sq_ref)

        @pl.when(n > 0)
        def _():
            err = h_ref[...] - t_ref[...]
            dh = err * (1.0 / dm)
            dh_ref[...] = dh
            dhb_ref[...] = dh.astype(BF16)
            sq_ref[...] += jnp.sum(err * err, axis=0, keepdims=True)

    row = pl.BlockSpec((TB, dm), lambda n: (n, 0))
    return pl.pallas_call(
        body, name=name, grid=(p // TB,),
        in_specs=[row, pl.BlockSpec((TB, dm), lambda n: (jnp.maximum(n - 1, 0), 0))],
        out_specs=[row, row, pl.BlockSpec((1, dm), lambda n: (0, 0))],
        out_shape=[jax.ShapeDtypeStruct((p, dm), F32), jax.ShapeDtypeStruct((p, dm), BF16), jax.ShapeDtypeStruct((1, dm), F32)],
        compiler_params=_params("arbitrary"),
    )(h, target)


def _chunk_masks():
    ri, ci = _iota((TB, TB), 0), _iota((TB, TB), 1)
    same = (ri >> 4) == (ci >> 4)
    causal = same & (ci <= ri)
    lower = jnp.where(causal, 1.0, 0.0).astype(BF16)
    upper = jnp.where(same & (ci >= ri), 1.0, 0.0).astype(BF16)
    ones = jnp.where(same, 1.0, 0.0).astype(BF16)
    return causal, lower, upper, ones


def _hgrn_gates(q, z, lbh, m):
    sig = _sigmoid(z)
    f = lbh + (1.0 - lbh) * sig
    lf = jnp.log(jnp.maximum(f, LOG_FLOOR)) * m
    kk = (1.0 - lbh) * (1.0 - sig) * m
    sq = _sigmoid(q)
    return sig, f, lf, kk, sq, q * sq


def _hgrn_fwd(proj, lb, *, name, carry=None):
    p = proj.shape[0]
    nb = p // TB

    def body(qa_ref, fa_ref, ia_ref, lb_ref, oa_ref, sck_ref, st_ref):
        n = pl.program_id(0)

        @pl.when(n == 0)
        def _():
            st_ref[...] = jnp.zeros_like(st_ref)

        causal, lower, _, ones = _chunk_masks()
        m = ((n * TB + _iota((TB, 1), 0)) >= PAD_FRONT).astype(F32)
        heads = range(HA)
        sls = [slice(hd * DK_A, (hd + 1) * DK_A) for hd in heads]
        rows = [slice(c * CH, (c + 1) * CH) for c in range(NCH)]
        gates = [_hgrn_gates(qa_ref[:, sl], fa_ref[:, sl], lb_ref[:, sl], m) for sl in sls]
        lf = [t[2] for t in gates]
        g = [_xdot(lower, x) for x in lf]
        gl = [_xdot(ones, x) for x in lf]
        qd = [(gates[hd][5] * jnp.exp(g[hd])).astype(BF16) for hd in heads]
        kt = [(gates[hd][3] * jnp.exp(-g[hd])).astype(BF16) for hd in heads]
        kd = [(gates[hd][3] * jnp.exp(gl[hd] - g[hd])).astype(BF16) for hd in heads]
        vb = [ia_ref[:, sl].astype(BF16) for sl in sls]
        a_all = [jnp.exp(x) for x in gl]
        att = [jnp.where(causal, _dot(qd[hd], kt[hd], NT), 0.0).astype(BF16) for hd in heads]
        kv = [[_dot(vb[hd][r], kd[hd][r], TN) for r in rows] for hd in heads]
        o = [_dot(att[hd], vb[hd], NN) for hd in heads]
        before = []
        for hd in heads:
            st = st_ref[hd]
            sck_ref[0, hd] = st
            per_chunk = []
            for c in range(NCH):
                per_chunk.append(st.astype(BF16))
                st = st * a_all[hd][c * CH:c * CH + 1, :] + kv[hd][c]
            st_ref[hd] = st
            before.append(per_chunk)
        inter = [[_dot(qd[hd][rows[c]], before[hd][c], NT) for c in range(NCH)] for hd in heads]
        for hd in heads:
            oa_ref[:, sls[hd]] = o[hd] + jnp.concatenate(inter[hd], axis=0)

    blk = lambda c: pl.BlockSpec((TB, 512), lambda n, c=c: (n, c))
    return _call(
        body, name=name, grid=(nb,),
        in_specs=[blk(C_QA), blk(C_FA), blk(C_IA), pl.BlockSpec((1, 512), lambda n: (0, 0))],
        out_specs=[pl.BlockSpec((TB, 512), lambda n: (n, 0)), pl.BlockSpec((1, HA, TB, TB), lambda n: (n, 0, 0, 0))],
        out_shape=[jax.ShapeDtypeStruct((p, 512), F32), jax.ShapeDtypeStruct((nb, HA, TB, TB), F32)],
        scratch_shapes=[pltpu.VMEM((HA, TB, TB), F32)],
        args=(proj, proj, proj, lb), sem=("arbitrary",), carry=carry)


def _hgrn_bwd(proj, lb, sck, d_oa, *, name, carry=None):
    p = proj.shape[0]
    nb = p // TB

    def body(qa_ref, fa_ref, ia_ref, lb_ref, sck_ref, do_ref, dq_ref, dz_ref, dv_ref, dlb_ref, dst_ref):
        i = pl.program_id(0)
        n = nb - 1 - i

        @pl.when(i == 0)
        def _():
            dst_ref[...] = jnp.zeros_like(dst_ref)
            dlb_ref[...] = jnp.zeros_like(dlb_ref)

        causal, lower, upper, ones = _chunk_masks()
        m = ((n * TB + _iota((TB, 1), 0)) >= PAD_FRONT).astype(F32)
        heads = range(HA)
        sls = [slice(hd * DK_A, (hd + 1) * DK_A) for hd in heads]
        rows = [slice(c * CH, (c + 1) * CH) for c in range(NCH)]
        a_row = lambda a, c: a[c * CH:c * CH + 1, :]
        gates = [_hgrn_gates(qa_ref[:, sl], fa_ref[:, sl], lb_ref[:, sl], m) for sl in sls]
        g = [_xdot(lower, t[2]) for t in gates]
        gl = [_xdot(ones, t[2]) for t in gates]
        e_g = [jnp.exp(x) for x in g]
        e_ng = [jnp.exp(-x) for x in g]
        e_d = [jnp.exp(gl[hd] - g[hd]) for hd in heads]
        a_all = [jnp.exp(x) for x in gl]
        qd_f = [gates[hd][5] * e_g[hd] for hd in heads]
        kt_f = [gates[hd][3] * e_ng[hd] for hd in heads]
        kd_f = [gates[hd][3] * e_d[hd] for hd in heads]
        qd, kt, kd = ([x.astype(BF16) for x in xs] for xs in (qd_f, kt_f, kd_f))
        vb = [ia_ref[:, sl].astype(BF16) for sl in sls]
        dob = [do_ref[:, sl].astype(BF16) for sl in sls]
        att = [jnp.where(causal, _dot(qd[hd], kt[hd], NT), 0.0).astype(BF16) for hd in heads]
        d_att = [jnp.where(causal, _dot(dob[hd], vb[hd], NT), 0.0).astype(BF16) for hd in heads]
        kv = [[_dot(vb[hd][r], kd[hd][r], TN) for r in rows] for hd in heads]
        dqk = [[_dot(dob[hd][r], qd[hd][r], TN) for r in rows] for hd in heads]
        d_v = [_dot(att[hd], dob[hd], TN) for hd in heads]
        d_qd = [_dot(d_att[hd], kt[hd], NN) for hd in heads]
        d_kt = [_dot(d_att[hd], qd[hd], TN) for hd in heads]
        stc, dsc = [], []
        for hd in heads:
            st, before = sck_ref[0, hd], []
            for c in range(NCH):
                before.append(st)
                if c + 1 < NCH:
                    st = st * a_row(a_all[hd], c) + kv[hd][c]
            dst, after = dst_ref[hd], [None] * NCH
            for c in range(NCH - 1, -1, -1):
                after[c] = dst
                dst = dst * a_row(a_all[hd], c) + dqk[hd][c]
            dst_ref[hd] = dst
            stc.append(before)
            dsc.append(after)
        dscb = [[x.astype(BF16) for x in dsc[hd]] for hd in heads]
        dvs = [[_dot(kd[hd][rows[c]], dscb[hd][c], NT) for c in range(NCH)] for hd in heads]
        dkd = [[_dot(vb[hd][rows[c]], dscb[hd][c], NN) for c in range(NCH)] for hd in heads]
        dqd = [[_dot(dob[hd][rows[c]], stc[hd][c].astype(BF16), NN) for c in range(NCH)] for hd in heads]
        dgl = [[jnp.broadcast_to(jnp.sum(dsc[hd][c] * stc[hd][c], axis=0, keepdims=True) * a_row(a_all[hd], c), (CH, TB))
                for c in range(NCH)] for hd in heads]
        d_qd = [d_qd[hd] + jnp.concatenate(dqd[hd], axis=0) for hd in heads]
        d_kd = [jnp.concatenate(dkd[hd], axis=0) for hd in heads]
        kd_term = [d_kd[hd] * kd_f[hd] for hd in heads]
        d_g = [d_qd[hd] * qd_f[hd] - d_kt[hd] * kt_f[hd] - kd_term[hd] for hd in heads]
        d_lf = [_xdot(upper, d_g[hd]) + _xdot(ones, kd_term[hd]) + jnp.concatenate(dgl[hd], axis=0) for hd in heads]
        for hd in heads:
            sl = sls[hd]
            sig, f, _, _, sq, _ = gates[hd]
            q, lbh = qa_ref[:, sl], lb_ref[:, sl]
            d_kk = (d_kt[hd] * e_ng[hd] + d_kd[hd] * e_d[hd]) * m
            t1 = d_lf[hd] * m * jnp.where(f > LOG_FLOOR, 1.0 / f, 0.0)
            dq_ref[:, sl] = (d_qd[hd] * e_g[hd] * (sq * (1.0 + q * (1.0 - sq)))).astype(BF16)
            dz_ref[:, sl] = ((t1 - d_kk) * (1.0 - lbh) * sig * (1.0 - sig)).astype(BF16)
            dv_ref[:, sl] = (d_v[hd] + jnp.concatenate(dvs[hd], axis=0)).astype(BF16)
            dlb_ref[:, sl] += jnp.sum((t1 - d_kk) * (1.0 - sig), axis=0, keepdims=True)

    blk = lambda c: pl.BlockSpec((TB, 512), lambda i, c=c: (nb - 1 - i, c))
    out_blk = pl.BlockSpec((TB, 512), lambda i: (nb - 1 - i, 0))
    vec = pl.BlockSpec((1, 512), lambda i: (0, 0))
    return _call(
        body, name=name, grid=(nb,),
        in_specs=[blk(C_QA), blk(C_FA), blk(C_IA), vec,
                  pl.BlockSpec((1, HA, TB, TB), lambda i: (nb - 1 - i, 0, 0, 0)), out_blk],
        out_specs=[out_blk, out_blk, out_blk, vec],
        out_shape=[jax.ShapeDtypeStruct((p, 512), BF16)] * 3 + [jax.ShapeDtypeStruct((1, 512), F32)],
        scratch_shapes=[pltpu.VMEM((HA, TB, TB), F32)],
        args=(proj, proj, proj, lb, sck, d_oa), sem=("arbitrary",), carry=carry)


def _lane():
    return _iota((1, TB), 1)


def _swap_halves(y):
    first = (_lane() & 63) < 32
    return jnp.where(first, pltpu.roll(y, 96, 1), pltpu.roll(y, 32, 1))


def _head_ones():
    ri, ci = _iota((TB, TB), 0), _iota((TB, TB), 1)
    return jnp.where((ri >> 6) == (ci >> 6), 1.0, 0.0).astype(BF16)


def _norm_rope(x, g, cos, sin, bd):
    r = lax.rsqrt(_xdot_r(x * x, bd) * (1.0 / DH) + RMS_EPS)
    y = x * r * g
    return y * cos + _swap_halves(y) * sin


def _norm_rope_bwd(d_out, x, g, cos, sin, bd):
    d = d_out * cos - _swap_halves(d_out) * sin
    r = lax.rsqrt(_xdot_r(x * x, bd) * (1.0 / DH) + RMS_EPS)
    xh = x * r
    dyn = d * g
    dx = r * (dyn - xh * (_xdot_r(dyn * xh, bd) * (1.0 / DH)))
    return dx, jnp.sum(d * xh, axis=0, keepdims=True)


def _dup_heads(k):
    first = _lane() < DH
    r = pltpu.roll(k, DH, 1)
    return jnp.where(first, k, r), jnp.where(first, r, k)


def _qk_prep(proj, gq, gk, cos, sin, *, name):
    p = proj.shape[0]

    def body(qb_ref, kb_ref, vb_ref, gq_ref, gk_ref, cos_ref, sin_ref, qh_ref, k2_ref, v2_ref):
        bd = _head_ones()
        cos_v, sin_v = cos_ref[...], sin_ref[...]
        for j in range(HB // 2):
            sl = slice(j * TB, (j + 1) * TB)
            qh_ref[:, sl] = _norm_rope(qb_ref[:, sl], gq_ref[...], cos_v, sin_v, bd).astype(BF16)
        k0, k1 = _dup_heads(_norm_rope(kb_ref[...], gk_ref[...], cos_v, sin_v, bd))
        k2_ref[:, 0:TB] = k0.astype(BF16)
        k2_ref[:, TB:2 * TB] = k1.astype(BF16)
        v0, v1 = _dup_heads(vb_ref[...])
        v2_ref[:, 0:TB] = v0.astype(BF16)
        v2_ref[:, TB:2 * TB] = v1.astype(BF16)

    vec = pl.BlockSpec((1, TB), lambda n: (0, 0))
    tab = pl.BlockSpec((TB, TB), lambda n: (n, 0))
    return pl.pallas_call(
        body, name=name, grid=(p // TB,),
        in_specs=[pl.BlockSpec((TB, 1024), lambda n: (n, C_QB)), pl.BlockSpec((TB, TB), lambda n: (n, C_KB)),
                  pl.BlockSpec((TB, TB), lambda n: (n, C_VB)), vec, vec, tab, tab],
        out_specs=[pl.BlockSpec((TB, 1024), lambda n: (n, 0)), pl.BlockSpec((TB, 256), lambda n: (n, 0)),
                   pl.BlockSpec((TB, 256), lambda n: (n, 0))],
        out_shape=[jax.ShapeDtypeStruct((p, 1024), BF16), jax.ShapeDtypeStruct((p, 256), BF16),
                   jax.ShapeDtypeStruct((p, 256), BF16)],
        compiler_params=_params("parallel"),
    )(proj, proj, proj, gq, gk, cos, sin)


NKEY = N_META + 2 * TB


def _attn_mask(n):
    r = _iota((TB, NKEY), 0)
    j = _iota((TB, NKEY), 1)
    meta = (j < N_META) & ((n >= 1) | (j + PAD_FRONT <= r))
    prev = (j >= N_META) & (j < N_META + TB) & (n >= 2) & (j - N_META > r)
    cur = (j >= N_META + TB) & (n >= 1) & (j - (N_META + TB) <= r)
    return meta | prev | cur


def _attn_specs():
    cur = lambda w: pl.BlockSpec((TB, w), lambda n: (n, 0))
    prev = pl.BlockSpec((TB, 256), lambda n: (jnp.maximum(n - 1, 0), 0))
    meta = pl.BlockSpec((N_META, 256), lambda n: (PAD_FRONT // N_META, 0))
    sink = pl.BlockSpec(memory_space=pltpu.SMEM)
    return cur, prev, meta, sink


ATTN_GROUP = 4


def _head_queries(q_ref, kv, first):
    out = []
    for jj in range(GRP // 2):
        j = kv * (GRP // 2) + jj
        qj = q_ref[:, j * TB:(j + 1) * TB] * (DH ** -0.5)
        for half in range(2):
            out.append((j, half, jnp.where(first if half == 0 else ~first, qj, jnp.zeros_like(qj))))
    return out


def _softmax_sink(s, mask, sink):
    s = jnp.where(mask, s, NEG_INF)
    mx = jnp.maximum(jnp.max(s, axis=1, keepdims=True), sink)
    pr = jnp.exp(s - mx)
    e_sink = jnp.exp(sink - mx)
    inv = 1.0 / (jnp.sum(pr, axis=1, keepdims=True) + e_sink)
    return pr, inv, e_sink * inv


def _attn_fwd(qh, k2, v2, sinks, *, name, carry=None):
    p = qh.shape[0]

    def body(sink_ref, q_ref, kc_ref, kp_ref, km_ref, vc_ref, vp_ref, vm_ref, o_ref):
        n = pl.program_id(0)
        mask = _attn_mask(n)
        first = _lane() < DH
        for kv in range(KVH):
            ks = slice(kv * TB, (kv + 1) * TB)
            kall = jnp.concatenate([km_ref[:, ks], kp_ref[:, ks], kc_ref[:, ks]], axis=0)
            vall = jnp.concatenate([vm_ref[:, ks], vp_ref[:, ks], vc_ref[:, ks]], axis=0)
            hq = _head_queries(q_ref, kv, first)
            for g0 in range(0, GRP, ATTN_GROUP):
                grp = hq[g0:g0 + ATTN_GROUP]
                s = [_dot(qm, kall, NT) for _, _, qm in grp]
                soft = [_softmax_sink(s[i], mask, sink_ref[2 * j + half]) for i, (j, half, _) in enumerate(grp)]
                o = [_dot(pr.astype(BF16), vall, NN) * inv for pr, inv, _ in soft]
                for i in range(0, len(grp), 2):
                    j = grp[i][0]
                    o_ref[:, j * TB:(j + 1) * TB] = jnp.where(first, o[i], o[i + 1])

    cur, prev, meta, sink = _attn_specs()
    return _call(
        body, name=name, grid=(p // TB,),
        in_specs=[sink, cur(1024), cur(256), prev, meta, cur(256), prev, meta],
        out_specs=[cur(1024)],
        out_shape=[jax.ShapeDtypeStruct((p, 1024), F32)],
        args=(sinks, qh, k2, k2, k2, v2, v2, v2), sem=("parallel",), carry=carry)


def _attn_bwd(qh, k2, v2, sinks, o, d_o, *, name, carry=None):
    p = qh.shape[0]

    def body(sink_ref, q_ref, kc_ref, kp_ref, km_ref, vc_ref, vp_ref, vm_ref, o_ref, do_ref,
             dq_ref, dkc_ref, dkp_ref, dvc_ref, dvp_ref, dkm_ref, dvm_ref, dsink_ref):
        n = pl.program_id(0)

        @pl.when(n == 0)
        def _():
            dkm_ref[...] = jnp.zeros_like(dkm_ref)
            dvm_ref[...] = jnp.zeros_like(dvm_ref)
            dsink_ref[...] = jnp.zeros_like(dsink_ref)

        mask = _attn_mask(n)
        lane = _lane()
        first = lane < DH
        sink_acc = jnp.zeros((TB, TB), F32)
        for kv in range(KVH):
            ks = slice(kv * TB, (kv + 1) * TB)
            kall = jnp.concatenate([km_ref[:, ks], kp_ref[:, ks], kc_ref[:, ks]], axis=0)
            vall = jnp.concatenate([vm_ref[:, ks], vp_ref[:, ks], vc_ref[:, ks]], axis=0)
            d_kall = jnp.zeros((NKEY, TB), F32)
            d_vall = jnp.zeros((NKEY, TB), F32)
            hq = _head_queries(q_ref, kv, first)
            for g0 in range(0, GRP, ATTN_GROUP):
                grp = hq[g0:g0 + ATTN_GROUP]
                idx = range(len(grp))
                s = [_dot(qm, kall, NT) for _, _, qm in grp]
                dom = [jnp.where(first if half == 0 else ~first, do_ref[:, j * TB:(j + 1) * TB], 0.0) for j, half, _ in grp]
                domb = [x.astype(BF16) for x in dom]
                d_w = [_dot(x, vall, NT) for x in domb]
                delta = [jnp.sum(dom[i] * o_ref[:, grp[i][0] * TB:(grp[i][0] + 1) * TB], axis=1, keepdims=True) for i in idx]
                soft = [_softmax_sink(s[i], mask, sink_ref[2 * j + half]) for i, (j, half, _) in enumerate(grp)]
                w = [pr * inv for pr, inv, _ in soft]
                dsb = [(w[i] * (d_w[i] - delta[i])).astype(BF16) for i in idx]
                for i, (j, half, _) in enumerate(grp):
                    sink_acc = sink_acc + jnp.where(lane == 2 * j + half, -(soft[i][2] * delta[i]), 0.0)
                d_q = [_dot(x, kall, NN) * (DH ** -0.5) for x in dsb]
                d_k = [_dot(dsb[i], grp[i][2], TN) for i in idx]
                d_v = [_dot(w[i].astype(BF16), domb[i], TN) for i in idx]
                for i in idx:
                    d_kall = d_kall + d_k[i]
                    d_vall = d_vall + d_v[i]
                for i in range(0, len(grp), 2):
                    j = grp[i][0]
                    dq_ref[:, j * TB:(j + 1) * TB] = jnp.where(first, d_q[i], d_q[i + 1])
            dkm_ref[:, ks] += d_kall[0:N_META]
            dkp_ref[:, ks] = d_kall[N_META:N_META + TB]
            dkc_ref[:, ks] = d_kall[N_META + TB:NKEY]
            dvm_ref[:, ks] += d_vall[0:N_META]
            dvp_ref[:, ks] = d_vall[N_META:N_META + TB]
            dvc_ref[:, ks] = d_vall[N_META + TB:NKEY]
        dsink_ref[...] += jnp.sum(sink_acc, axis=0, keepdims=True)

    cur, prev, meta, sink = _attn_specs()
    acc = lambda r: pl.BlockSpec((r, 256), lambda n: (0, 0))
    return _call(
        body, name=name, grid=(p // TB,),
        in_specs=[sink, cur(1024), cur(256), prev, meta, cur(256), prev, meta, cur(1024), cur(1024)],
        out_specs=[cur(1024), cur(256), cur(256), cur(256), cur(256), acc(N_META), acc(N_META),
                   pl.BlockSpec((1, TB), lambda n: (0, 0))],
        out_shape=[jax.ShapeDtypeStruct((p, 1024), F32)] + [jax.ShapeDtypeStruct((p, 256), F32)] * 4
        + [jax.ShapeDtypeStruct((N_META, 256), F32)] * 2 + [jax.ShapeDtypeStruct((1, TB), F32)],
        args=(sinks, qh, k2, k2, k2, v2, v2, v2, o, d_o), sem=("arbitrary",), carry=carry)


def _qk_post(proj, gq, gk, cos, sin, dqh, dkc, dkp, dkm, dvc, dvp, dvm, *, name):
    p = proj.shape[0]
    nb = p // TB

    def body(qb_ref, kb_ref, gq_ref, gk_ref, cos_ref, sin_ref, dqh_ref, dkc_ref, dkp_ref, dkm_ref,
             dvc_ref, dvp_ref, dvm_ref, dqb_ref, dkb_ref, dvb_ref, dgq_ref, dgk_ref, tk_ref, tv_ref):
        n = pl.program_id(0)

        @pl.when(n == 0)
        def _():
            dgq_ref[...] = jnp.zeros_like(dgq_ref)
            dgk_ref[...] = jnp.zeros_like(dgk_ref)

        keep = jnp.where(n == nb - 1, 0.0, 1.0)
        tk_ref[...] = dkc_ref[...] + keep * dkp_ref[...]
        tv_ref[...] = dvc_ref[...] + keep * dvp_ref[...]

        @pl.when(n == 0)
        def _():
            tk_ref[PAD_FRONT:TB, :] += dkm_ref[...]
            tv_ref[PAD_FRONT:TB, :] += dvm_ref[...]

        first = _lane() < DH

        def fold(t_ref):
            t0, t1 = t_ref[:, 0:TB], t_ref[:, TB:2 * TB]
            return jnp.where(first, t0 + pltpu.roll(t0, DH, 1), t1 + pltpu.roll(t1, DH, 1))

        bd = _head_ones()
        cos_v, sin_v = cos_ref[...], sin_ref[...]
        dvb_ref[...] = fold(tv_ref).astype(BF16)
        dkb, dgk = _norm_rope_bwd(fold(tk_ref), kb_ref[...], gk_ref[...], cos_v, sin_v, bd)
        dkb_ref[...] = dkb.astype(BF16)
        dgk_ref[...] += dgk
        dgq = jnp.zeros((1, TB), F32)
        for j in range(HB // 2):
            sl = slice(j * TB, (j + 1) * TB)
            dqb, dg = _norm_rope_bwd(dqh_ref[:, sl], qb_ref[:, sl], gq_ref[...], cos_v, sin_v, bd)
            dqb_ref[:, sl] = dqb.astype(BF16)
            dgq = dgq + dg
        dgq_ref[...] += dgq

    vec = pl.BlockSpec((1, TB), lambda n: (0, 0))
    tab = pl.BlockSpec((TB, TB), lambda n: (n, 0))
    cur = lambda w: pl.BlockSpec((TB, w), lambda n: (n, 0))
    nxt = pl.BlockSpec((TB, 256), lambda n: (jnp.minimum(n + 1, nb - 1), 0))
    meta = pl.BlockSpec((N_META, 256), lambda n: (0, 0))
    return pl.pallas_call(
        body, name=name, grid=(nb,),
        in_specs=[pl.BlockSpec((TB, 1024), lambda n: (n, C_QB)), pl.BlockSpec((TB, TB), lambda n: (n, C_KB)),
                  vec, vec, tab, tab, cur(1024), cur(256), nxt, meta, cur(256), nxt, meta],
        out_specs=[cur(1024), cur(TB), cur(TB), vec, vec],
        out_shape=[jax.ShapeDtypeStruct((p, 1024), BF16), jax.ShapeDtypeStruct((p, TB), BF16),
                   jax.ShapeDtypeStruct((p, TB), BF16), jax.ShapeDtypeStruct((1, TB), F32),
                   jax.ShapeDtypeStruct((1, TB), F32)],
        scratch_shapes=[pltpu.VMEM((TB, 256), F32), pltpu.VMEM((TB, 256), F32)],
        compiler_params=_params("arbitrary"),
    )(proj, proj, gq, gk, cos, sin, dqh, dkc, dkp, dkm, dvc, dvp, dvm)


EXT = TB + N_META


def _pool_count_inv(n, w):
    t = n * TB + _iota((TB, 1), 0)
    cnt = jnp.clip(t - (PAD_FRONT - 1), 1, w)
    return 1.0 / cnt.astype(F32)


def _silu_parts(gate):
    s = _sigmoid(gate)
    return gate * s, s * (1.0 + gate * (1.0 - s))


def _mix_fwd(proj, oa, yb, hg, pool_w, pool_scale, *, name):
    p = proj.shape[0]

    def body(ga_ref, gb_ref, uc_ref, up_ref, gc_ref, oa_ref, yb_ref, hg_ref, pw_ref, ps_ref, mx_ref, pooled_ref):
        n = pl.program_id(0)
        valid = ((n * TB + _iota((TB, 1), 0)) >= PAD_FRONT).astype(F32)
        for hd in range(HA):
            sl = slice(hd * TB, (hd + 1) * TB)
            o = oa_ref[:, sl]
            r = lax.rsqrt(jnp.mean(o * o, axis=-1, keepdims=True) + RMS_EPS)
            act, _ = _silu_parts(ga_ref[:, sl])
            mx_ref[:, sl] = (o * r * hg_ref[...] * act).astype(BF16)
        for j in range(HB // 2):
            sl = slice(j * TB, (j + 1) * TB)
            act, _ = _silu_parts(gb_ref[:, sl])
            mx_ref[:, 512 + j * TB:512 + (j + 1) * TB] = (yb_ref[:, sl] * act).astype(BF16)
        ri, ci = _iota((TB, EXT), 0), _iota((TB, EXT), 1)
        has_prev = jnp.where(n == 0, 0.0, 1.0)
        for gi, w in enumerate(POOL_WINDOWS):
            sl = slice(gi * TB, (gi + 1) * TB)
            ug = uc_ref[:, sl] * valid
            ext = jnp.concatenate([up_ref[:, sl] * has_prev, ug], axis=0)
            band = jnp.where((ci <= ri + N_META) & (ci > ri + N_META - w), 1.0, 0.0).astype(BF16)
            pooled = (_xdot(band, ext) * _pool_count_inv(n, w) - ug) * valid
            pooled_ref[:, sl] = pooled
            yc = _dot(pooled.astype(BF16), pw_ref[gi], NN) * ps_ref[:, sl]
            act, _ = _silu_parts(gc_ref[:, sl])
            mx_ref[:, 1536 + gi * TB:1536 + (gi + 1) * TB] = (yc * act).astype(BF16)

    cur = lambda w, c=0: pl.BlockSpec((TB, w), lambda n, c=c: (n, c))
    prev16 = pl.BlockSpec((N_META, 512), lambda n: (jnp.maximum(n * (TB // N_META) - 1, 0), C_UC))
    return pl.pallas_call(
        body, name=name, grid=(p // TB,),
        in_specs=[cur(512, C_GA), cur(1024, C_GB), cur(512, C_UC), prev16, cur(512, C_GC), cur(512), cur(1024),
                  pl.BlockSpec((1, TB), lambda n: (0, 0)), pl.BlockSpec((4, TB, TB), lambda n: (0, 0, 0)),
                  pl.BlockSpec((1, 512), lambda n: (0, 0))],
        out_specs=[cur(MIX), cur(512)],
        out_shape=[jax.ShapeDtypeStruct((p, MIX), BF16), jax.ShapeDtypeStruct((p, 512), F32)],
        compiler_params=_params("parallel"),
    )(proj, proj, proj, proj, proj, oa, yb, hg, pool_w, pool_scale)


def _mix_bwd(proj, oa, yb, pooled, hg, pool_w, pool_scale, d_mixed, *, name, carry=None):
    p = proj.shape[0]

    def body(ga_ref, gb_ref, gc_ref, oa_ref, yb_ref, pooled_ref, hg_ref, pw_ref, ps_ref, dm_ref,
             doa_ref, dyb_ref, dga_ref, dgb_ref, dgc_ref, dp_ref, dhg_ref, dpw_ref, dps_ref):
        n = pl.program_id(0)

        @pl.when(n == 0)
        def _():
            dhg_ref[...] = jnp.zeros_like(dhg_ref)
            dpw_ref[...] = jnp.zeros_like(dpw_ref)
            dps_ref[...] = jnp.zeros_like(dps_ref)

        valid = ((n * TB + _iota((TB, 1), 0)) >= PAD_FRONT).astype(F32)
        dhg = jnp.zeros((1, TB), F32)
        for hd in range(HA):
            sl = slice(hd * TB, (hd + 1) * TB)
            o = oa_ref[:, sl]
            r = lax.rsqrt(jnp.mean(o * o, axis=-1, keepdims=True) + RMS_EPS)
            on = o * r
            gate = ga_ref[:, sl]
            act, dact = _silu_parts(gate)
            dmx = dm_ref[:, sl]
            d_ya = dmx * act
            dga_ref[:, sl] = (dmx * on * hg_ref[...] * dact).astype(BF16)
            dyn = d_ya * hg_ref[...]
            doa_ref[:, sl] = r * (dyn - on * jnp.mean(dyn * on, axis=-1, keepdims=True))
            dhg = dhg + jnp.sum(d_ya * on, axis=0, keepdims=True)
        dhg_ref[...] += dhg
        for j in range(HB // 2):
            sl = slice(j * TB, (j + 1) * TB)
            act, dact = _silu_parts(gb_ref[:, sl])
            dmx = dm_ref[:, 512 + j * TB:512 + (j + 1) * TB]
            dyb_ref[:, sl] = dmx * act
            dgb_ref[:, sl] = (dmx * yb_ref[:, sl] * dact).astype(BF16)
        for gi in range(len(POOL_WINDOWS)):
            sl = slice(gi * TB, (gi + 1) * TB)
            pooled = pooled_ref[:, sl]
            pooled_b = pooled.astype(BF16)
            t = _dot(pooled_b, pw_ref[gi], NN)
            act, dact = _silu_parts(gc_ref[:, sl])
            dmx = dm_ref[:, 1536 + gi * TB:1536 + (gi + 1) * TB]
            d_yc = dmx * act
            dgc_ref[:, sl] = (dmx * t * ps_ref[:, sl] * dact).astype(BF16)
            dps_ref[:, sl] += jnp.sum(d_yc * t, axis=0, keepdims=True)
            d_t = (d_yc * ps_ref[:, sl]).astype(BF16)
            dp_ref[:, sl] = _dot(d_t, pw_ref[gi], NT) * valid
            dpw_ref[gi] += _dot(pooled_b, d_t, TN)

    cur = lambda w, c=0: pl.BlockSpec((TB, w), lambda n, c=c: (n, c))
    return _call(
        body, name=name, grid=(p // TB,),
        in_specs=[cur(512, C_GA), cur(1024, C_GB), cur(512, C_GC), cur(512), cur(1024), cur(512),
                  pl.BlockSpec((1, TB), lambda n: (0, 0)), pl.BlockSpec((4, TB, TB), lambda n: (0, 0, 0)),
                  pl.BlockSpec((1, 512), lambda n: (0, 0)), cur(MIX)],
        out_specs=[cur(512), cur(1024), cur(512), cur(1024), cur(512), cur(512),
                   pl.BlockSpec((1, TB), lambda n: (0, 0)), pl.BlockSpec((4, TB, TB), lambda n: (0, 0, 0)),
                   pl.BlockSpec((1, 512), lambda n: (0, 0))],
        out_shape=[jax.ShapeDtypeStruct((p, 512), F32), jax.ShapeDtypeStruct((p, 1024), F32),
                   jax.ShapeDtypeStruct((p, 512), BF16), jax.ShapeDtypeStruct((p, 1024), BF16),
                   jax.ShapeDtypeStruct((p, 512), BF16), jax.ShapeDtypeStruct((p, 512), F32),
                   jax.ShapeDtypeStruct((1, TB), F32), jax.ShapeDtypeStruct((4, TB, TB), F32),
                   jax.ShapeDtypeStruct((1, 512), F32)],
        args=(proj, proj, proj, oa, yb, pooled, hg, pool_w, pool_scale, d_mixed), sem=("arbitrary",), carry=carry)


def _pool_bwd(dp, *, name):
    p = dp.shape[0]
    nb = p // TB

    def body(dp_ref, dn_ref, duc_ref):
        n = pl.program_id(0)
        valid = ((n * TB + _iota((TB, 1), 0)) >= PAD_FRONT).astype(F32)
        has_next = jnp.where(n == nb - 1, 0.0, 1.0)
        ri, ci = _iota((TB, EXT), 0), _iota((TB, EXT), 1)
        for gi, w in enumerate(POOL_WINDOWS):
            sl = slice(gi * TB, (gi + 1) * TB)
            d_p = dp_ref[:, sl]
            ext = jnp.concatenate([d_p * _pool_count_inv(n, w), dn_ref[:, sl] * (has_next / w)], axis=0)
            band = jnp.where((ci >= ri) & (ci < ri + w), 1.0, 0.0).astype(BF16)
            duc_ref[:, sl] = ((_xdot(band, ext) - d_p) * valid).astype(BF16)

    return pl.pallas_call(
        body, name=name, grid=(nb,),
        in_specs=[pl.BlockSpec((TB, 512), lambda n: (n, 0)),
                  pl.BlockSpec((N_META, 512), lambda n: (jnp.minimum(n + 1, nb - 1) * (TB // N_META), 0))],
        out_specs=pl.BlockSpec((TB, 512), lambda n: (n, 0)),
        out_shape=jax.ShapeDtypeStruct((p, 512), BF16),
        compiler_params=_params("parallel"),
    )(dp, dp)


def _carried(carries, key, local, fn, *args, **kw):
    if key not in carries:
        return fn(*args, **kw)
    make_src, gather, done = carries[key]
    *outs, stack = fn(*args, carry=(make_src(local), gather), **kw)
    done(stack)
    return outs


def _layer_fwd(h, w, tag, carries):
    xn = _rmsnorm_fwd(h, w["norm_g"], name=f"rmsnorm_fwd{tag}")
    proj, = _carried(carries, "in_proj", None, _mm_nn, xn, w["w_in"], name=f"in_proj{tag}")
    oa, sck = _carried(carries, "hgrn_fwd", None, _hgrn_fwd, proj, w["lb"], name=f"hgrn_fwd{tag}")
    qh, k2, v2 = _qk_prep(proj, w["gq"], w["gk"], w["cos"], w["sin"], name=f"qk_prep{tag}")
    yb, = _carried(carries, "attn_fwd", None, _attn_fwd, qh, k2, v2, w["sinks"], name=f"attn_fwd{tag}")
    mixed, pooled = _mix_fwd(proj, oa, yb, w["hg"], w["pool_w"], w["pool_scale"], name=f"mix_fwd{tag}")
    h_next, = _mm_nn(mixed, w["w_out"], h, name=f"out_proj{tag}", tn=512)
    saved = dict(h=h, xn=xn, proj=proj, oa=oa, sck=sck, qh=qh, k2=k2, v2=v2, yb=yb, mixed=mixed, pooled=pooled)
    return h_next, saved


def _layer_bwd(dh_out, dhb, s, w, tag, carries, first_layer=False):
    g = {}
    d_mixed = _mm_nt(dhb, w["w_out"], name=f"d_mixed{tag}", tm=1664, tn=512)
    g["w_out"], = _mm_tn(s["mixed"], dhb, name=f"dw_out{tag}", tn=1024)
    d_oa, d_yb, d_ga, d_gb, d_gc, d_p, g["hg"], g["pool_w"], g["pool_scale"] = _carried(
        carries, "mix_bwd", g, _mix_bwd,
        s["proj"], s["oa"], s["yb"], s["pooled"], w["hg"], w["pool_w"], w["pool_scale"], d_mixed, name=f"mix_bwd{tag}")
    d_uc = _pool_bwd(d_p, name=f"pool_bwd{tag}")
    d_qh, dkc, dkp, dvc, dvp, dkm, dvm, g["sinks"] = _carried(
        carries, "attn_bwd", g, _attn_bwd, s["qh"], s["k2"], s["v2"], w["sinks"], s["yb"], d_yb, name=f"attn_bwd{tag}")
    d_qb, d_kb, d_vb, g["gq"], g["gk"] = _qk_post(s["proj"], w["gq"], w["gk"], w["cos"], w["sin"], d_qh, dkc, dkp, dkm,
                                                  dvc, dvp, dvm, name=f"qk_post{tag}")
    d_qa, d_fa, d_ia, g["lb"] = _carried(carries, "hgrn_bwd", g, _hgrn_bwd, s["proj"], w["lb"], s["sck"], d_oa,
                                         name=f"hgrn_bwd{tag}")
    d_proj = jnp.concatenate([d_qa, d_fa, d_ia, d_ga, d_qb, d_gb, d_uc, d_gc, d_kb, d_vb], axis=1)
    d_xn = _mm_nt(d_proj, w["w_in"], name=f"d_xn{tag}", tk=PROJ_COLS)
    g["w_in_top"], = _carried(carries, "dw_in_top", g, _mm_tn, s["xn"], d_proj, name=f"dw_in_top{tag}", m_part=(0, 2))
    g["w_in_bot"], = _carried(carries, "dw_in_bot", g, _mm_tn, s["xn"], d_proj, name=f"dw_in_bot{tag}", m_part=(1, 2))
    out, aux, g["norm_g"] = _carried(carries, "rmsnorm_bwd", g, _rmsnorm_bwd, d_xn, s["h"], w["norm_g"], dh_out,
                                     name=f"rmsnorm_bwd{tag}", first_layer=first_layer)
    return out, aux, g


def _peers():
    x, y, c = lax.axis_index("x"), lax.axis_index("y"), lax.axis_index("c")
    out = []
    for k in range(1, N_DEV):
        kx, ky, kc = (k >> 2) & 1, (k >> 1) & 1, k & 1
        px, py, pc = x ^ kx, y ^ ky, c ^ kc
        out.append(((px, py, pc), 4 * px + 2 * py + pc))
    return 4 * x + 2 * y + c, out


def _exchange_copies(src_ref, out_ref, send_sems, recv_sems, local_sem, gather):
    me, peers = _peers()
    mine = pltpu.make_async_copy(src_ref if gather else src_ref.at[me], out_ref.at[me], local_sem)
    copies = []
    for k, (dev, idx) in enumerate(peers):
        copies.append(pltpu.make_async_remote_copy(
            src_ref=src_ref if gather else src_ref.at[idx], dst_ref=out_ref.at[me],
            send_sem=send_sems.at[k], recv_sem=recv_sems.at[k],
            device_id=dev, device_id_type=pl.DeviceIdType.MESH))
    return mine, copies


def _gather_copies(src_ref, out_ref, send_sems, recv_sems, local_sem):
    x, y, c = lax.axis_index("x"), lax.axis_index("y"), lax.axis_index("c")
    slot = lambda px, py, pc: out_ref.at[4 * px + 2 * py + pc]
    sibling = (x, y, 1 - c)
    chips = [(1 - x, y), (x, 1 - y), (1 - x, 1 - y)]

    def copy(k, src, block, to):
        return pltpu.make_async_remote_copy(src_ref=src, dst_ref=slot(*block), send_sem=send_sems.at[k],
                                            recv_sem=recv_sems.at[k], device_id=to, device_id_type=pl.DeviceIdType.MESH)

    mine = lambda: pltpu.make_async_copy(src_ref, slot(x, y, c), local_sem)
    own = lambda: ([copy(0, src_ref, (x, y, c), sibling)]
                   + [copy(1 + j, src_ref, (x, y, c), (*chip, c)) for j, chip in enumerate(chips)])
    passing = lambda: [copy(4 + j, slot(*chip, c), (*chip, c), sibling) for j, chip in enumerate(chips)]
    arrivals = lambda: ([copy(0, src_ref, sibling, sibling)]
                        + [copy(1 + j, src_ref, (*chip, c), sibling) for j, chip in enumerate(chips)]
                        + [copy(4 + j, src_ref, (*chip, 1 - c), sibling) for j, chip in enumerate(chips)])
    return mine, own, passing, arrivals


def _exchange_start(*refs, gather):
    if gather:
        mine, own, _, _ = _gather_copies(*refs)
        mine().start()
        for cp in own():
            cp.start()
        return
    mine, copies = _exchange_copies(*refs, gather)
    mine.start()
    for cp in copies:
        cp.start()


def _exchange_wait(*refs, gather):
    if gather:
        mine, own, passing, arrivals = _gather_copies(*refs)
        passing, arrivals = passing(), arrivals()
        for j, cp in enumerate(passing):
            arrivals[1 + j].wait_recv()
            cp.start()
        arrivals[0].wait_recv()
        for cp in arrivals[4:]:
            cp.wait_recv()
        for cp in own() + passing:
            cp.wait_send()
        mine().wait()
        return
    mine, copies = _exchange_copies(*refs, gather)
    for cp in copies:
        cp.wait_recv()
    for cp in copies:
        cp.wait_send()
    mine.wait()


def _exchange_scratch():
    return [pltpu.SemaphoreType.DMA((N_DEV - 1,)), pltpu.SemaphoreType.DMA((N_DEV - 1,)), pltpu.SemaphoreType.DMA]


def _exchange(src, *, gather, name):
    rows, cols = src.shape[-2:]

    def body(src_ref, out_ref, send_sems, recv_sems, local_sem):
        _exchange_start(src_ref, out_ref, send_sems, recv_sems, local_sem, gather=gather)
        _exchange_wait(src_ref, out_ref, send_sems, recv_sems, local_sem, gather=gather)

    return pl.pallas_call(
        body, name=name,
        in_specs=[pl.BlockSpec(memory_space=pl.ANY)], out_specs=pl.BlockSpec(memory_space=pl.ANY),
        out_shape=jax.ShapeDtypeStruct((N_DEV, rows, cols), src.dtype),
        scratch_shapes=_exchange_scratch(),
    )(src)


def _call(body, *, name, grid, in_specs, out_specs, out_shape, args, sem, scratch_shapes=(), carry=None):
    if carry is None:
        return pl.pallas_call(
            body, name=name, grid=grid, in_specs=list(in_specs), out_specs=list(out_specs), out_shape=list(out_shape),
            scratch_shapes=list(scratch_shapes), compiler_params=_params(*sem))(*args)
    src, gather = carry
    n_in, n_out, n_scr = len(in_specs), len(out_specs), len(scratch_shapes)
    rows, cols = src.shape[-2:]

    def carrying(*refs):
        ins, src_ref = refs[:n_in], refs[n_in]
        outs, dst_ref = refs[n_in + 1:n_in + 1 + n_out], refs[n_in + 1 + n_out]
        scr = refs[n_in + 2 + n_out:]
        exch = (src_ref, dst_ref) + tuple(scr[n_scr:])
        first, last = None, None
        for a, size in enumerate(grid):
            f, l = pl.program_id(a) == 0, pl.program_id(a) == size - 1
            first = f if first is None else first & f
            last = l if last is None else last & l

        @pl.when(first)
        def _():
            _exchange_start(*exch, gather=gather)

        body(*ins, *outs, *scr[:n_scr])

        @pl.when(last)
        def _():
            _exchange_wait(*exch, gather=gather)

    hbm = pl.BlockSpec(memory_space=pl.ANY)
    return pl.pallas_call(
        carrying, name=name, grid=grid, in_specs=list(in_specs) + [hbm], out_specs=list(out_specs) + [hbm],
        out_shape=list(out_shape) + [jax.ShapeDtypeStruct((N_DEV, rows, cols), src.dtype)],
        scratch_shapes=list(scratch_shapes) + _exchange_scratch(),
        compiler_params=_params(*(("arbitrary",) * len(grid))))(*args, src)


def _adamw(stacks, w, m, v, *, name):
    nl = len(stacks)
    rows, cols = stacks[0].shape[1:]
    tr = rows
    stack_block_bytes = 4 * 1024 * 1024
    for cand in (256, 128, 64, 32, 16):
        if rows % cand == 0 and N_DEV * cand * cols * stacks[0].dtype.itemsize <= stack_block_bytes:
            tr = cand
            break
    nt = rows // tr

    def body(*refs):
        s_refs = refs[:nl]
        w_ref, m_ref, v_ref, g_ref, d_ref, nm_ref, nv_ref = refs[nl:]
        for l, s_ref in enumerate(s_refs):
            @pl.when(pl.program_id(0) == l)
            def _(s_ref=s_ref):
                acc = s_ref[0].astype(F32)
                for d in range(1, N_DEV):
                    acc = acc + s_ref[d].astype(F32)
                g_ref[...] = acc

        g = g_ref[...]
        nm = ADAM_B1 * m_ref[...] + (1.0 - ADAM_B1) * g
        nv = ADAM_B2 * v_ref[...] + (1.0 - ADAM_B2) * (g * g)
        m_hat = nm / (1.0 - ADAM_B1 ** ADAM_STEP)
        v_hat = nv / (1.0 - ADAM_B2 ** ADAM_STEP)
        d_ref[...] = -ADAM_LR * (m_hat / (jnp.sqrt(v_hat) + ADAM_EPS) + ADAM_WD * w_ref[...])
        nm_ref[...] = nm
        nv_ref[...] = nv

    blk = pl.BlockSpec((tr, cols), lambda l, i: (l * nt + i, 0))
    return pl.pallas_call(
        body, name=name, grid=(nl, nt),
        in_specs=[pl.BlockSpec((N_DEV, tr, cols), lambda l, i, k=k: (0, jnp.where(l == k, i, 0), 0)) for k in range(nl)]
        + [blk, blk, blk],
        out_specs=[blk] * 4, out_shape=[jax.ShapeDtypeStruct((nl * rows, cols), F32)] * 4,
        compiler_params=_params("arbitrary", "arbitrary"),
    )(*stacks, w, m, v)


def _lb_all(lb_logits):
    sm = jax.nn.softmax(lb_logits.astype(F32), axis=0)
    return jnp.cumsum(sm, axis=0) - sm[0:1]


def _rope_tables(p):
    half = DH // 2
    inv = jnp.power(ROPE_THETA, -jnp.arange(half, dtype=F32) * 2.0 / DH)
    pos = (jnp.arange(p) - PAD_FRONT).astype(F32)
    ang = pos[:, None] * inv[None, :]
    cos, sin = jnp.cos(ang), jnp.sin(ang)
    return jnp.tile(cos, (1, 4)), jnp.tile(jnp.concatenate([-sin, sin], axis=1), (1, 2))


def _permute_cols(w):
    return jnp.concatenate([w[:, :3072], w[:, 3328:], w[:, 3072:3328]], axis=1)


def _unpermute_cols(w):
    return jnp.concatenate([w[:, :3072], w[:, 5120:], w[:, 3072:5120]], axis=1)


SMALL = (("lb_logits", (DEPTH, 512)), ("norm_g", (DEPTH, D_MODEL)), ("q_norm_g", (DEPTH, DH)),
         ("k_norm_g", (DEPTH, DH)), ("attn_sinks", (DEPTH, HB)), ("hgrn_norm_g", (DEPTH, 128)),
         ("pool_w", (DEPTH, 4, 128, 128)), ("pool_scale", (DEPTH, 512)))


def _pack_small(d):
    flat = jnp.concatenate([d[k].astype(F32).reshape(-1) for k, _ in SMALL])
    pad = (-flat.shape[0]) % (8 * 128)
    return jnp.pad(flat, (0, pad)).reshape(-1, 128)


def _unpack_small(a):
    flat = a.reshape(-1)
    out, off = {}, 0
    for k, shp in SMALL:
        n = int(np.prod(shp))
        out[k] = flat[off:off + n].reshape(shp)
        off += n
    return out


def kernel(x, meta_tokens, lb_logits, norm_g, w_in, q_norm_g, k_norm_g, attn_sinks, hgrn_norm_g, pool_w, pool_scale, w_out, loss_target, m_meta_tokens, m_lb_logits, m_norm_g, m_w_in, m_q_norm_g, m_k_norm_g, m_attn_sinks, m_hgrn_norm_g, m_pool_w, m_pool_scale, m_w_out, v_meta_tokens, v_lb_logits, v_norm_g, v_w_in, v_q_norm_g, v_k_norm_g, v_attn_sinks, v_hgrn_norm_g, v_pool_w, v_pool_scale, v_w_out):
    seq = x.shape[1]
    p = seq + TB
    cs = PROJ_COLS // N_DEV
    rs = MIX // N_DEV
    ms = D_MODEL // N_DEV

    full_w_in = lambda st: _permute_cols(st.transpose(1, 0, 2).reshape(D_MODEL, PROJ_COLS))
    dw_in_blocks = lambda rows: _unpermute_cols(rows).reshape(rows.shape[0], N_DEV, cs).transpose(1, 0, 2).astype(BF16)
    dw_out_blocks = lambda g: g["w_out"].reshape(N_DEV, rs, D_MODEL).astype(BF16)
    w_in_bf = w_in.astype(BF16)

    lb_all, lb_vjp = jax.vjp(_lb_all, lb_logits)
    cos, sin = _rope_tables(p)
    layers = []
    for l in range(DEPTH):
        layers.append(dict(
            norm_g=norm_g[l][None], lb=lb_all[l][None],
            gq=jnp.tile(q_norm_g[l], 2)[None], gk=jnp.tile(k_norm_g[l], 2)[None], sinks=attn_sinks[l],
            hg=hgrn_norm_g[l][None], pool_w=pool_w[l].astype(BF16), pool_scale=pool_scale[l][None], cos=cos, sin=sin))
    layers[0]["w_in"] = full_w_in(_exchange(w_in_bf[0], gather=True, name="gather_w_in_l0"))
    meta_all = _exchange(meta_tokens, gather=True, name="gather_meta")
    meta_full = meta_all.transpose(1, 0, 2).reshape(N_META, D_MODEL)

    def got_w_out(st):
        st = st.reshape(N_DEV, DEPTH, rs, D_MODEL)
        for l in range(DEPTH):
            layers[l]["w_out"] = st[:, l].reshape(MIX, D_MODEL)

    def got_w_in_l1(st):
        layers[1]["w_in"] = full_w_in(st)

    fwd_carries = [
        dict(in_proj=(lambda _: w_in_bf[1], True, got_w_in_l1),
             hgrn_fwd=(lambda _: w_out.reshape(DEPTH * rs, D_MODEL).astype(BF16), True, got_w_out)),
        {}]
    h = jnp.concatenate([jnp.zeros((PAD_FRONT, D_MODEL), F32), meta_full, x[0]], axis=0)
    saved = []
    for l in range(DEPTH):
        h, s = _layer_fwd(h, layers[l], f"_l{l}", fwd_carries[l])
        saved.append(s)
    dh, dhb, sq = _loss_grad(h, loss_target[0], name="loss_grad")
    loss = lax.psum(0.5 * jnp.sum(sq) / D_MODEL, ("x", "y", "c"))

    grads = [None] * DEPTH
    win_stacks, wout_stacks = [None] * (2 * DEPTH), [None] * DEPTH

    def into(stacks, i):
        def done(st):
            stacks[i] = st
        return done

    bwd_carries = [
        dict(mix_bwd=(lambda _: dw_in_blocks(grads[1]["w_in_top"]), False, into(win_stacks, 2)),
             attn_bwd=(lambda _: dw_in_blocks(grads[1]["w_in_bot"]), False, into(win_stacks, 3)),
             hgrn_bwd=(lambda g: dw_out_blocks(g), False, into(wout_stacks, 0)),
             dw_in_bot=(lambda g: dw_in_blocks(g["w_in_top"]), False, into(win_stacks, 0)),
             rmsnorm_bwd=(lambda g: dw_in_blocks(g["w_in_bot"]), False, into(win_stacks, 1))),
        dict(attn_bwd=(lambda g: dw_out_blocks(g), False, into(wout_stacks, 1)))]
    dh, dhb, grads[1] = _layer_bwd(dh, dhb, saved[1], layers[1], "_l1", bwd_carries[1])
    grad_x, d_front, grads[0] = _layer_bwd(dh, dhb, saved[0], layers[0], "_l0", bwd_carries[0], first_layer=True)
    grad_x = grad_x[None]
    dmeta = d_front.reshape(N_META, N_DEV, ms).transpose(1, 0, 2)
    meta_stack = _exchange(dmeta, gather=False, name="scatter_dmeta")

    stk = lambda k: jnp.stack([grads[l][k][0] for l in range(DEPTH)])
    fold = lambda a: a[:, :DH] + a[:, DH:]
    small_local = dict(
        lb_logits=lb_vjp(stk("lb"))[0], norm_g=stk("norm_g"), q_norm_g=fold(stk("gq")), k_norm_g=fold(stk("gk")),
        attn_sinks=stk("sinks")[:, :HB], hgrn_norm_g=stk("hg"),
        pool_w=jnp.stack([grads[l]["pool_w"] for l in range(DEPTH)]), pool_scale=stk("pool_scale"))
    small_stack = _exchange(_pack_small(small_local), gather=True, name="gather_small_grads")

    g_win, d_win, nm_win, nv_win = _adamw(win_stacks, w_in.reshape(DEPTH * D_MODEL, cs), m_w_in.reshape(DEPTH * D_MODEL, cs),
                                          v_w_in.reshape(DEPTH * D_MODEL, cs), name="adamw_w_in")
    g_wout, d_wout, nm_wout, nv_wout = _adamw(wout_stacks, w_out.reshape(DEPTH * rs, D_MODEL), m_w_out.reshape(DEPTH * rs, D_MODEL),
                                              v_w_out.reshape(DEPTH * rs, D_MODEL), name="adamw_w_out")
    g_meta, d_meta, nm_meta, nv_meta = _adamw([meta_stack], meta_tokens, m_meta_tokens, v_meta_tokens, name="adamw_meta")
    small_w = dict(lb_logits=lb_logits, norm_g=norm_g, q_norm_g=q_norm_g, k_norm_g=k_norm_g, attn_sinks=attn_sinks,
                   hgrn_norm_g=hgrn_norm_g, pool_w=pool_w, pool_scale=pool_scale)
    small_m = dict(lb_logits=m_lb_logits, norm_g=m_norm_g, q_norm_g=m_q_norm_g, k_norm_g=m_k_norm_g, attn_sinks=m_attn_sinks,
                   hgrn_norm_g=m_hgrn_norm_g, pool_w=m_pool_w, pool_scale=m_pool_scale)
    small_v = dict(lb_logits=v_lb_logits, norm_g=v_norm_g, q_norm_g=v_q_norm_g, k_norm_g=v_k_norm_g, attn_sinks=v_attn_sinks,
                   hgrn_norm_g=v_hgrn_norm_g, pool_w=v_pool_w, pool_scale=v_pool_scale)
    small_out = [_unpack_small(a) for a in _adamw([small_stack], _pack_small(small_w), _pack_small(small_m),
                                                  _pack_small(small_v), name="adamw_small")]

    big = dict(
        meta_tokens=(g_meta, d_meta, nm_meta, nv_meta),
        w_in=tuple(a.reshape(DEPTH, D_MODEL, cs) for a in (g_win, d_win, nm_win, nv_win)),
        w_out=tuple(a.reshape(DEPTH, rs, D_MODEL) for a in (g_wout, d_wout, nm_wout, nv_wout)))
    order = ("meta_tokens", "lb_logits", "norm_g", "w_in", "q_norm_g", "k_norm_g", "attn_sinks", "hgrn_norm_g",
             "pool_w", "pool_scale", "w_out")
    outs = [loss, grad_x]
    for kind in range(4):
        for k in order:
            outs.append(big[k][kind] if k in big else small_out[kind][k])
    return tuple(outs)
```

```python
import functools

import numpy as np
import jax
import jax.numpy as jnp
from jax import lax
from jax.experimental import pallas as pl
from jax.experimental.pallas import tpu as pltpu

F32, BF16 = jnp.float32, jnp.bfloat16

D_MODEL = 2048
DEPTH = 2
N_META = 16
TB = 128
PAD_FRONT = TB - N_META
RMS_EPS = 1e-6
NEG_INF = -1e30
LOG_FLOOR = 1e-30
HA, DK_A = 4, 128
CH = 16
NCH = TB // CH
HB, KVH, DH = 16, 2, 64
GRP = HB // KVH
ROPE_THETA = 10000.0
POOL_WINDOWS = (2, 4, 8, 16)
PROJ_COLS = 5376
MIX = 2048
N_DEV = 8
W_IN_PARTS = 4
C_QA, C_FA, C_IA, C_GA = 0, 1, 2, 3
C_QB, C_GB = 2, 3
C_UC, C_GC = 8, 9
C_KB, C_VB = 40, 41

ADAM_LR, ADAM_B1, ADAM_B2, ADAM_EPS, ADAM_WD, ADAM_STEP = 0.001, 0.9, 0.999, 1e-08, 0.01, 10

VMEM_LIMIT = 48 * 1024 * 1024

NN = ((1,), (0,))
NT = ((1,), (1,))
TN = ((0,), (0,))


def _dot(a, b, dims):
    return lax.dot_general(a, b, (dims, ((), ())), preferred_element_type=F32)


def _split3(x):
    hi = x.astype(BF16)
    r = x - hi.astype(F32)
    mid = r.astype(BF16)
    lo = (r - mid.astype(F32)).astype(BF16)
    return hi, mid, lo


def _xdot(m01, x):
    hi, mid, lo = _split3(x)
    return _dot(m01, hi, NN) + _dot(m01, mid, NN) + _dot(m01, lo, NN)


def _xdot_r(x, m01):
    hi, mid, lo = _split3(x)
    return _dot(hi, m01, NN) + _dot(mid, m01, NN) + _dot(lo, m01, NN)


def _iota(shape, dim):
    return lax.broadcasted_iota(jnp.int32, shape, dim)


def _params(*sem):
    return pltpu.CompilerParams(dimension_semantics=sem, vmem_limit_bytes=VMEM_LIMIT)


def _row_tile(p, target):
    best = TB
    t = TB
    while t <= target:
        if p % t == 0:
            best = t
        t += TB
    return best


def _col_tile(n, target):
    best = 128
    t = 128
    while t <= target:
        if n % t == 0:
            best = t
        t += 128
    return best


def _sigmoid(x):
    return 1.0 / (1.0 + jnp.exp(-x))


def _mm_nn(a, b, res=None, *, name, tm=1664, tn=768, carry=None):
    m, k = a.shape
    n = b.shape[1]
    tm, tn = _row_tile(m, tm), _col_tile(n, tn)

    def body(*refs):
        if res is None:
            a_ref, b_ref, o_ref = refs
            o_ref[...] = _dot(a_ref[...], b_ref[...], NN)
        else:
            a_ref, b_ref, r_ref, o_ref = refs
            o_ref[...] = r_ref[...] + _dot(a_ref[...], b_ref[...], NN)

    in_specs = [pl.BlockSpec((tm, k), lambda j, i: (i, 0)), pl.BlockSpec((k, tn), lambda j, i: (0, j))]
    args = [a, b]
    if res is not None:
        in_specs.append(pl.BlockSpec((tm, tn), lambda j, i: (i, j)))
        args.append(res)
    return _call(
        body, name=name, grid=(n // tn, m // tm), in_specs=in_specs,
        out_specs=[pl.BlockSpec((tm, tn), lambda j, i: (i, j))],
        out_shape=[jax.ShapeDtypeStruct((m, n), F32)],
        args=args, sem=("parallel", "parallel"), carry=carry)


def _mm_nt(a, b, *, name, tm=640, tn=512, tk=2048, carry=None):
    m, k = a.shape
    n = b.shape[0]
    tm, tn, tk = _row_tile(m, tm), _col_tile(n, tn), _col_tile(k, tk)

    def body(a_ref, b_ref, o_ref):
        @pl.when(pl.program_id(2) == 0)
        def _():
            o_ref[...] = jnp.zeros_like(o_ref)

        o_ref[...] += _dot(a_ref[...], b_ref[...], NT)

    return _call(
        body, name=name, grid=(n // tn, m // tm, k // tk),
        in_specs=[pl.BlockSpec((tm, tk), lambda j, i, kk: (i, kk)), pl.BlockSpec((tn, tk), lambda j, i, kk: (j, kk))],
        out_specs=[pl.BlockSpec((tm, tn), lambda j, i, kk: (i, j))],
        out_shape=[jax.ShapeDtypeStruct((m, n), F32)],
        args=(a, b), sem=("parallel", "parallel", "arbitrary"), carry=carry)


def _mm_tn(a, b, *, name, tm=1024, tn=1344, tk=1664, m_part=None, carry=None):
    k, m = a.shape
    n = b.shape[1]
    first, m = (0, m) if m_part is None else (m_part[0], m // m_part[1])
    tm, tn, tk = _col_tile(m, tm), _col_tile(n, tn), _row_tile(k, tk)
    first *= m // tm

    def body(a_ref, b_ref, o_ref):
        @pl.when(pl.program_id(2) == 0)
        def _():
            o_ref[...] = jnp.zeros_like(o_ref)

        o_ref[...] += _dot(a_ref[...], b_ref[...], TN)

    return _call(
        body, name=name, grid=(m // tm, n // tn, k // tk),
        in_specs=[pl.BlockSpec((tk, tm), lambda i, j, kk: (kk, first + i)), pl.BlockSpec((tk, tn), lambda i, j, kk: (kk, j))],
        out_specs=[pl.BlockSpec((tm, tn), lambda i, j, kk: (i, j))],
        out_shape=[jax.ShapeDtypeStruct((m, n), F32)],
        args=(a, b), sem=("parallel", "parallel", "arbitrary"), carry=carry)


def _rmsnorm_fwd(h, g, *, name, carry=None):
    p, dm = h.shape
    tm = _row_tile(p, 640)

    def body(h_ref, g_ref, xn_ref):
        hv = h_ref[...]
        r = lax.rsqrt(jnp.mean(hv * hv, axis=-1, keepdims=True) + RMS_EPS)
        xn_ref[...] = (hv * r * g_ref[...]).astype(BF16)

    return _call(
        body, name=name, grid=(p // tm,),
        in_specs=[pl.BlockSpec((tm, dm), lambda i: (i, 0)), pl.BlockSpec((1, dm), lambda i: (0, 0))],
        out_specs=[pl.BlockSpec((tm, dm), lambda i: (i, 0))],
        out_shape=[jax.ShapeDtypeStruct((p, dm), BF16)],
        args=(h, g), sem=("parallel",), carry=carry)


def _rmsnorm_bwd(dxn, h, g, dh_out, *, name, first_layer=False, carry=None):
    p, dm = h.shape
    tm = TB if first_layer else _row_tile(p, 384)

    def body(dxn_ref, h_ref, g_ref, dho_ref, out_ref, aux_ref, dg_ref):
        hv = h_ref[...]
        r = lax.rsqrt(jnp.mean(hv * hv, axis=-1, keepdims=True) + RMS_EPS)
        xh = hv * r
        dy = dxn_ref[...]
        dyn = dy * g_ref[...]
        dh = dho_ref[...] + r * (dyn - xh * jnp.mean(dyn * xh, axis=-1, keepdims=True))
        out_ref[...] = dh

        @pl.when(pl.program_id(0) == 0)
        def _():
            dg_ref[...] = jnp.zeros_like(dg_ref)
            if first_layer:
                aux_ref[...] = dh[PAD_FRONT:TB]

        if not first_layer:
            aux_ref[...] = dh.astype(BF16)
        dg_ref[...] += jnp.sum(dy * xh, axis=0, keepdims=True)

    row = pl.BlockSpec((tm, dm), lambda i: (i, 0))
    vec = pl.BlockSpec((1, dm), lambda i: (0, 0))
    if first_layer:
        out_specs = [pl.BlockSpec((TB, dm), lambda i: (jnp.maximum(i - 1, 0), 0)), pl.BlockSpec((N_META, dm), lambda i: (0, 0)), vec]
        out_shape = [jax.ShapeDtypeStruct((p - TB, dm), F32), jax.ShapeDtypeStruct((N_META, dm), F32)]
    else:
        out_specs = [row, row, vec]
        out_shape = [jax.ShapeDtypeStruct((p, dm), F32), jax.ShapeDtypeStruct((p, dm), BF16)]
    return _call(
        body, name=name, grid=(p // tm,),
        in_specs=[row, row, vec, row], out_specs=out_specs,
        out_shape=out_shape + [jax.ShapeDtypeStruct((1, dm), F32)],
        args=(dxn, h, g, dh_out), sem=("arbitrary",), carry=carry)


def _loss_grad(h, target, *, name):
    p, dm = h.shape

    def body(h_ref, t_ref, dh_ref, dhb_ref, sq_ref):
        n = pl.program_id(0)

        @pl.when(n == 0)
        def _():
            dh_ref[...] = jnp.zeros_like(dh_ref)
            dhb_ref[...] = jnp.zeros_like(dhb_ref)
            sq_ref[...] = jnp.zeros_like(sq_ref)

        @pl.when(n > 0)
        def _():
            err = h_ref[...] - t_ref[...]
            dh = err * (1.0 / dm)
            dh_ref[...] = dh
            dhb_ref[...] = dh.astype(BF16)
            sq_ref[...] += jnp.sum(err * err, axis=0, keepdims=True)

    row = pl.BlockSpec((TB, dm), lambda n: (n, 0))
    return pl.pallas_call(
        body, name=name, grid=(p // TB,),
        in_specs=[row, pl.BlockSpec((TB, dm), lambda n: (jnp.maximum(n - 1, 0), 0))],
        out_specs=[row, row, pl.BlockSpec((1, dm), lambda n: (0, 0))],
        out_shape=[jax.ShapeDtypeStruct((p, dm), F32), jax.ShapeDtypeStruct((p, dm), BF16), jax.ShapeDtypeStruct((1, dm), F32)],
        compiler_params=_params("arbitrary"),
    )(h, target)


def _chunk_masks():
    ri, ci = _iota((TB, TB), 0), _iota((TB, TB), 1)
    same = (ri >> 4) == (ci >> 4)
    causal = same & (ci <= ri)
    lower = jnp.where(causal, 1.0, 0.0).astype(BF16)
    upper = jnp.where(same & (ci >= ri), 1.0, 0.0).astype(BF16)
    ones = jnp.where(same, 1.0, 0.0).astype(BF16)
    return causal, lower, upper, ones


def _hgrn_gates(q, z, lbh, m):
    sig = _sigmoid(z)
    f = lbh + (1.0 - lbh) * sig
    lf = jnp.log(jnp.maximum(f, LOG_FLOOR)) * m
    kk = (1.0 - lbh) * (1.0 - sig) * m
    sq = _sigmoid(q)
    return sig, f, lf, kk, sq, q * sq


def _hgrn_fwd(proj, lb, *, name, carry=None):
    p = proj.shape[0]
    nb = p // TB

    def body(qa_ref, fa_ref, ia_ref, lb_ref, oa_ref, sck_ref, st_ref):
        n = pl.program_id(0)

        @pl.when(n == 0)
        def _():
            st_ref[...] = jnp.zeros_like(st_ref)

        causal, lower, _, ones = _chunk_masks()
        m = ((n * TB + _iota((TB, 1), 0)) >= PAD_FRONT).astype(F32)
        heads = range(HA)
        sls = [slice(hd * DK_A, (hd + 1) * DK_A) for hd in heads]
        rows = [slice(c * CH, (c + 1) * CH) for c in range(NCH)]
        gates = [_hgrn_gates(qa_ref[:, sl], fa_ref[:, sl], lb_ref[:, sl], m) for sl in sls]
        lf = [t[2] for t in gates]
        g = [_xdot(lower, x) for x in lf]
        gl = [_xdot(ones, x) for x in lf]
        qd = [(gates[hd][5] * jnp.exp(g[hd])).astype(BF16) for hd in heads]
        kt = [(gates[hd][3] * jnp.exp(-g[hd])).astype(BF16) for hd in heads]
        kd = [(gates[hd][3] * jnp.exp(gl[hd] - g[hd])).astype(BF16) for hd in heads]
        vb = [ia_ref[:, sl].astype(BF16) for sl in sls]
        a_all = [jnp.exp(x) for x in gl]
        att = [jnp.where(causal, _dot(qd[hd], kt[hd], NT), 0.0).astype(BF16) for hd in heads]
        kv = [[_dot(vb[hd][r], kd[hd][r], TN) for r in rows] for hd in heads]
        o = [_dot(att[hd], vb[hd], NN) for hd in heads]
        before = []
        for hd in heads:
            st = st_ref[hd]
            sck_ref[0, hd] = st
            per_chunk = []
            for c in range(NCH):
                per_chunk.append(st.astype(BF16))
                st = st * a_all[hd][c * CH:c * CH + 1, :] + kv[hd][c]
            st_ref[hd] = st
            before.append(per_chunk)
        inter = [[_dot(qd[hd][rows[c]], before[hd][c], NT) for c in range(NCH)] for hd in heads]
        for hd in heads:
            oa_ref[:, sls[hd]] = o[hd] + jnp.concatenate(inter[hd], axis=0)

    blk = lambda c: pl.BlockSpec((TB, 512), lambda n, c=c: (n, c))
    return _call(
        body, name=name, grid=(nb,),
        in_specs=[blk(C_QA), blk(C_FA), blk(C_IA), pl.BlockSpec((1, 512), lambda n: (0, 0))],
        out_specs=[pl.BlockSpec((TB, 512), lambda n: (n, 0)), pl.BlockSpec((1, HA, TB, TB), lambda n: (n, 0, 0, 0))],
        out_shape=[jax.ShapeDtypeStruct((p, 512), F32), jax.ShapeDtypeStruct((nb, HA, TB, TB), F32)],
        scratch_shapes=[pltpu.VMEM((HA, TB, TB), F32)],
        args=(proj, proj, proj, lb), sem=("arbitrary",), carry=carry)


def _hgrn_bwd(proj, lb, sck, d_oa, *, name, carry=None):
    p = proj.shape[0]
    nb = p // TB

    def body(qa_ref, fa_ref, ia_ref, lb_ref, sck_ref, do_ref, dq_ref, dz_ref, dv_ref, dlb_ref, dst_ref):
        i = pl.program_id(0)
        n = nb - 1 - i

        @pl.when(i == 0)
        def _():
            dst_ref[...] = jnp.zeros_like(dst_ref)
            dlb_ref[...] = jnp.zeros_like(dlb_ref)

        causal, lower, upper, ones = _chunk_masks()
        m = ((n * TB + _iota((TB, 1), 0)) >= PAD_FRONT).astype(F32)
        heads = range(HA)
        sls = [slice(hd * DK_A, (hd + 1) * DK_A) for hd in heads]
        rows = [slice(c * CH, (c + 1) * CH) for c in range(NCH)]
        a_row = lambda a, c: a[c * CH:c * CH + 1, :]
        gates = [_hgrn_gates(qa_ref[:, sl], fa_ref[:, sl], lb_ref[:, sl], m) for sl in sls]
        g = [_xdot(lower, t[2]) for t in gates]
        gl = [_xdot(ones, t[2]) for t in gates]
        e_g = [jnp.exp(x) for x in g]
        e_ng = [jnp.exp(-x) for x in g]
        e_d = [jnp.exp(gl[hd] - g[hd]) for hd in heads]
        a_all = [jnp.exp(x) for x in gl]
        qd_f = [gates[hd][5] * e_g[hd] for hd in heads]
        kt_f = [gates[hd][3] * e_ng[hd] for hd in heads]
        kd_f = [gates[hd][3] * e_d[hd] for hd in heads]
        qd, kt, kd = ([x.astype(BF16) for x in xs] for xs in (qd_f, kt_f, kd_f))
        vb = [ia_ref[:, sl].astype(BF16) for sl in sls]
        dob = [do_ref[:, sl].astype(BF16) for sl in sls]
        att = [jnp.where(causal, _dot(qd[hd], kt[hd], NT), 0.0).astype(BF16) for hd in heads]
        d_att = [jnp.where(causal, _dot(dob[hd], vb[hd], NT), 0.0).astype(BF16) for hd in heads]
        kv = [[_dot(vb[hd][r], kd[hd][r], TN) for r in rows] for hd in heads]
        dqk = [[_dot(dob[hd][r], qd[hd][r], TN) for r in rows] for hd in heads]
        d_v = [_dot(att[hd], dob[hd], TN) for hd in heads]
        d_qd = [_dot(d_att[hd], kt[hd], NN) for hd in heads]
        d_kt = [_dot(d_att[hd], qd[hd], TN) for hd in heads]
        stc, dsc = [], []
        for hd in heads:
            st, before = sck_ref[0, hd], []
            for c in range(NCH):
                before.append(st)
                if c + 1 < NCH:
                    st = st * a_row(a_all[hd], c) + kv[hd][c]
            dst, after = dst_ref[hd], [None] * NCH
            for c in range(NCH - 1, -1, -1):
                after[c] = dst
                dst = dst * a_row(a_all[hd], c) + dqk[hd][c]
            dst_ref[hd] = dst
            stc.append(before)
            dsc.append(after)
        dscb = [[x.astype(BF16) for x in dsc[hd]] for hd in heads]
        dvs = [[_dot(kd[hd][rows[c]], dscb[hd][c], NT) for c in range(NCH)] for hd in heads]
        dkd = [[_dot(vb[hd][rows[c]], dscb[hd][c], NN) for c in range(NCH)] for hd in heads]
        dqd = [[_dot(dob[hd][rows[c]], stc[hd][c].astype(BF16), NN) for c in range(NCH)] for hd in heads]
        dgl = [[jnp.broadcast_to(jnp.sum(dsc[hd][c] * stc[hd][c], axis=0, keepdims=True) * a_row(a_all[hd], c), (CH, TB))
                for c in range(NCH)] for hd in heads]
        d_qd = [d_qd[hd] + jnp.concatenate(dqd[hd], axis=0) for hd in heads]
        d_kd = [jnp.concatenate(dkd[hd], axis=0) for hd in heads]
        kd_term = [d_kd[hd] * kd_f[hd] for hd in heads]
        d_g = [d_qd[hd] * qd_f[hd] - d_kt[hd] * kt_f[hd] - kd_term[hd] for hd in heads]
        d_lf = [_xdot(upper, d_g[hd]) + _xdot(ones, kd_term[hd]) + jnp.concatenate(dgl[hd], axis=0) for hd in heads]
        for hd in heads:
            sl = sls[hd]
            sig, f, _, _, sq, _ = gates[hd]
            q, lbh = qa_ref[:, sl], lb_ref[:, sl]
            d_kk = (d_kt[hd] * e_ng[hd] + d_kd[hd] * e_d[hd]) * m
            t1 = d_lf[hd] * m * jnp.where(f > LOG_FLOOR, 1.0 / f, 0.0)
            dq_ref[:, sl] = (d_qd[hd] * e_g[hd] * (sq * (1.0 + q * (1.0 - sq)))).astype(BF16)
            dz_ref[:, sl] = ((t1 - d_kk) * (1.0 - lbh) * sig * (1.0 - sig)).astype(BF16)
            dv_ref[:, sl] = (d_v[hd] + jnp.concatenate(dvs[hd], axis=0)).astype(BF16)
            dlb_ref[:, sl] += jnp.sum((t1 - d_kk) * (1.0 - sig), axis=0, keepdims=True)

    blk = lambda c: pl.BlockSpec((TB, 512), lambda i, c=c: (nb - 1 - i, c))
    out_blk = pl.BlockSpec((TB, 512), lambda i: (nb - 1 - i, 0))
    vec = pl.BlockSpec((1, 512), lambda i: (0, 0))
    return _call(
        body, name=name, grid=(nb,),
        in_specs=[blk(C_QA), blk(C_FA), blk(C_IA), vec,
                  pl.BlockSpec((1, HA, TB, TB), lambda i: (nb - 1 - i, 0, 0, 0)), out_blk],
        out_specs=[out_blk, out_blk, out_blk, vec],
        out_shape=[jax.ShapeDtypeStruct((p, 512), BF16)] * 3 + [jax.ShapeDtypeStruct((1, 512), F32)],
        scratch_shapes=[pltpu.VMEM((HA, TB, TB), F32)],
        args=(proj, proj, proj, lb, sck, d_oa), sem=("arbitrary",), carry=carry)


def _lane():
    return _iota((1, TB), 1)


def _swap_halves(y):
    first = (_lane() & 63) < 32
    return jnp.where(first, pltpu.roll(y, 96, 1), pltpu.roll(y, 32, 1))


def _head_ones():
    ri, ci = _iota((TB, TB), 0), _iota((TB, TB), 1)
    return jnp.where((ri >> 6) == (ci >> 6), 1.0, 0.0).astype(BF16)


def _norm_rope(x, g, cos, sin, bd):
    r = lax.rsqrt(_xdot_r(x * x, bd) * (1.0 / DH) + RMS_EPS)
    y = x * r * g
    return y * cos + _swap_halves(y) * sin


def _norm_rope_bwd(d_out, x, g, cos, sin, bd):
    d = d_out * cos - _swap_halves(d_out) * sin
    r = lax.rsqrt(_xdot_r(x * x, bd) * (1.0 / DH) + RMS_EPS)
    xh = x * r
    dyn = d * g
    dx = r * (dyn - xh * (_xdot_r(dyn * xh, bd) * (1.0 / DH)))
    return dx, jnp.sum(d * xh, axis=0, keepdims=True)


def _dup_heads(k):
    first = _lane() < DH
    r = pltpu.roll(k, DH, 1)
    return jnp.where(first, k, r), jnp.where(first, r, k)


def _qk_prep(proj, gq, gk, cos, sin, *, name):
    p = proj.shape[0]

    def body(qb_ref, kb_ref, vb_ref, gq_ref, gk_ref, cos_ref, sin_ref, qh_ref, k2_ref, v2_ref):
        bd = _head_ones()
        cos_v, sin_v = cos_ref[...], sin_ref[...]
        for j in range(HB // 2):
            sl = slice(j * TB, (j + 1) * TB)
            qh_ref[:, sl] = _norm_rope(qb_ref[:, sl], gq_ref[...], cos_v, sin_v, bd).astype(BF16)
        k0, k1 = _dup_heads(_norm_rope(kb_ref[...], gk_ref[...], cos_v, sin_v, bd))
        k2_ref[:, 0:TB] = k0.astype(BF16)
        k2_ref[:, TB:2 * TB] = k1.astype(BF16)
        v0, v1 = _dup_heads(vb_ref[...])
        v2_ref[:, 0:TB] = v0.astype(BF16)
        v2_ref[:, TB:2 * TB] = v1.astype(BF16)

    vec = pl.BlockSpec((1, TB), lambda n: (0, 0))
    tab = pl.BlockSpec((TB, TB), lambda n: (n, 0))
    return pl.pallas_call(
        body, name=name, grid=(p // TB,),
        in_specs=[pl.BlockSpec((TB, 1024), lambda n: (n, C_QB)), pl.BlockSpec((TB, TB), lambda n: (n, C_KB)),
                  pl.BlockSpec((TB, TB), lambda n: (n, C_VB)), vec, vec, tab, tab],
        out_specs=[pl.BlockSpec((TB, 1024), lambda n: (n, 0)), pl.BlockSpec((TB, 256), lambda n: (n, 0)),
                   pl.BlockSpec((TB, 256), lambda n: (n, 0))],
        out_shape=[jax.ShapeDtypeStruct((p, 1024), BF16), jax.ShapeDtypeStruct((p, 256), BF16),
                   jax.ShapeDtypeStruct((p, 256), BF16)],
        compiler_params=_params("parallel"),
    )(proj, proj, proj, gq, gk, cos, sin)


NKEY = N_META + 2 * TB


def _attn_mask(n):
    r = _iota((TB, NKEY), 0)
    j = _iota((TB, NKEY), 1)
    meta = (j < N_META) & ((n >= 1) | (j + PAD_FRONT <= r))
    prev = (j >= N_META) & (j < N_META + TB) & (n >= 2) & (j - N_META > r)
    cur = (j >= N_META + TB) & (n >= 1) & (j - (N_META + TB) <= r)
    return meta | prev | cur


def _attn_specs():
    cur = lambda w: pl.BlockSpec((TB, w), lambda n: (n, 0))
    prev = pl.BlockSpec((TB, 256), lambda n: (jnp.maximum(n - 1, 0), 0))
    meta = pl.BlockSpec((N_META, 256), lambda n: (PAD_FRONT // N_META, 0))
    sink = pl.BlockSpec(memory_space=pltpu.SMEM)
    return cur, prev, meta, sink


ATTN_GROUP = 4


def _head_queries(q_ref, kv, first):
    out = []
    for jj in range(GRP // 2):
        j = kv * (GRP // 2) + jj
        qj = q_ref[:, j * TB:(j + 1) * TB] * (DH ** -0.5)
        for half in range(2):
            out.append((j, half, jnp.where(first if half == 0 else ~first, qj, jnp.zeros_like(qj))))
    return out


def _attn_fwd(qh, k2, v2, sinks, *, name, carry=None):
    p = qh.shape[0]

    def body(sink_ref, q_ref, kc_ref, kp_ref, km_ref, vc_ref, vp_ref, vm_ref, o_ref, lse_ref):
        n = pl.program_id(0)
        mask = _attn_mask(n)
        lane = _lane()
        first = lane < DH
        lse_tile = jnp.zeros((TB, TB), F32)
        for kv in range(KVH):
            ks = slice(kv * TB, (kv + 1) * TB)
            kall = jnp.concatenate([km_ref[:, ks], kp_ref[:, ks], kc_ref[:, ks]], axis=0)
            vall = jnp.concatenate([vm_ref[:, ks], vp_ref[:, ks], vc_ref[:, ks]], axis=0)
            hq = _head_queries(q_ref, kv, first)
            for g0 in range(0, GRP, ATTN_GROUP):
                grp = hq[g0:g0 + ATTN_GROUP]
                idx = range(len(grp))
                s = [jnp.where(mask, _dot(qm, kall, NT), NEG_INF) for _, _, qm in grp]
                sink = [sink_ref[2 * j + half] for j, half, _ in grp]
                mx = [jnp.maximum(jnp.max(s[i], axis=1, keepdims=True), sink[i]) for i in idx]
                pr = [jnp.exp(s[i] - mx[i]) for i in idx]
                den = [jnp.sum(pr[i], axis=1, keepdims=True) + jnp.exp(sink[i] - mx[i]) for i in idx]
                o = [_dot(pr[i].astype(BF16), vall, NN) * (1.0 / den[i]) for i in idx]
                for i, (j, half, _) in enumerate(grp):
                    lse_tile = lse_tile + jnp.where(lane == 2 * j + half, mx[i] + jnp.log(den[i]), 0.0)
                for i in range(0, len(grp), 2):
                    j = grp[i][0]
                    o_ref[:, j * TB:(j + 1) * TB] = jnp.where(first, o[i], o[i + 1])
        lse_ref[...] = lse_tile

    cur, prev, meta, sink = _attn_specs()
    return _call(
        body, name=name, grid=(p // TB,),
        in_specs=[sink, cur(1024), cur(256), prev, meta, cur(256), prev, meta],
        out_specs=[cur(1024), cur(TB)],
        out_shape=[jax.ShapeDtypeStruct((p, 1024), F32), jax.ShapeDtypeStruct((p, TB), F32)],
        args=(sinks, qh, k2, k2, k2, v2, v2, v2), sem=("parallel",), carry=carry)


def _attn_bwd(qh, k2, v2, sinks, o, lse, d_o, *, name, carry=None):
    p = qh.shape[0]

    def body(sink_ref, q_ref, kc_ref, kp_ref, km_ref, vc_ref, vp_ref, vm_ref, o_ref, lse_ref, do_ref,
             dq_ref, dkc_ref, dkp_ref, dvc_ref, dvp_ref, dkm_ref, dvm_ref, dsink_ref):
        n = pl.program_id(0)

        @pl.when(n == 0)
        def _():
            dkm_ref[...] = jnp.zeros_like(dkm_ref)
            dvm_ref[...] = jnp.zeros_like(dvm_ref)
            dsink_ref[...] = jnp.zeros_like(dsink_ref)

        mask = _attn_mask(n)
        lane = _lane()
        first = lane < DH
        lse_tile = lse_ref[...]
        sink_acc = jnp.zeros((TB, TB), F32)
        for kv in range(KVH):
            ks = slice(kv * TB, (kv + 1) * TB)
            kall = jnp.concatenate([km_ref[:, ks], kp_ref[:, ks], kc_ref[:, ks]], axis=0)
            vall = jnp.concatenate([vm_ref[:, ks], vp_ref[:, ks], vc_ref[:, ks]], axis=0)
            d_kall = jnp.zeros((NKEY, TB), F32)
            d_vall = jnp.zeros((NKEY, TB), F32)
            hq = _head_queries(q_ref, kv, first)
            for g0 in range(0, GRP, ATTN_GROUP):
                grp = hq[g0:g0 + ATTN_GROUP]
                idx = range(len(grp))
                s = [jnp.where(mask, _dot(qm, kall, NT), NEG_INF) for _, _, qm in grp]
                dom = [jnp.where(first if half == 0 else ~first, do_ref[:, j * TB:(j + 1) * TB], 0.0) for j, half, _ in grp]
                domb = [x.astype(BF16) for x in dom]
                d_w = [_dot(x, vall, NT) for x in domb]
                delta = [jnp.sum(dom[i] * o_ref[:, grp[i][0] * TB:(grp[i][0] + 1) * TB], axis=1, keepdims=True) for i in idx]
                lse_h = [jnp.sum(jnp.where(lane == 2 * j + half, lse_tile, 0.0), axis=1, keepdims=True) for j, half, _ in grp]
                w = [jnp.exp(s[i] - lse_h[i]) for i in idx]
                for i, (j, half, _) in enumerate(grp):
                    w_sink = jnp.exp(sink_ref[2 * j + half] - lse_h[i])
                    sink_acc = sink_acc + jnp.where(lane == 2 * j + half, -(w_sink * delta[i]), 0.0)
                dsb = [(w[i] * (d_w[i] - delta[i])).astype(BF16) for i in idx]
                d_q = [_dot(x, kall, NN) * (DH ** -0.5) for x in dsb]
                d_k = [_dot(dsb[i], grp[i][2], TN) for i in idx]
                d_v = [_dot(w[i].astype(BF16), domb[i], TN) for i in idx]
                for i in idx:
                    d_kall = d_kall + d_k[i]
                    d_vall = d_vall + d_v[i]
                for i in range(0, len(grp), 2):
                    j = grp[i][0]
                    dq_ref[:, j * TB:(j + 1) * TB] = jnp.where(first, d_q[i], d_q[i + 1])
            dkm_ref[:, ks] += d_kall[0:N_META]
            dkp_ref[:, ks] = d_kall[N_META:N_META + TB]
            dkc_ref[:, ks] = d_kall[N_META + TB:NKEY]
            dvm_ref[:, ks] += d_vall[0:N_META]
            dvp_ref[:, ks] = d_vall[N_META:N_META + TB]
            dvc_ref[:, ks] = d_vall[N_META + TB:NKEY]
        dsink_ref[...] += jnp.sum(sink_acc, axis=0, keepdims=True)

    cur, prev, meta, sink = _attn_specs()
    acc = lambda r: pl.BlockSpec((r, 256), lambda n: (0, 0))
    return _call(
        body, name=name, grid=(p // TB,),
        in_specs=[sink, cur(1024), cur(256), prev, meta, cur(256), prev, meta, cur(1024), cur(TB), cur(1024)],
        out_specs=[cur(1024), cur(256), cur(256), cur(256), cur(256), acc(N_META), acc(N_META),
                   pl.BlockSpec((1, TB), lambda n: (0, 0))],
        out_shape=[jax.ShapeDtypeStruct((p, 1024), F32)] + [jax.ShapeDtypeStruct((p, 256), F32)] * 4
        + [jax.ShapeDtypeStruct((N_META, 256), F32)] * 2 + [jax.ShapeDtypeStruct((1, TB), F32)],
        args=(sinks, qh, k2, k2, k2, v2, v2, v2, o, lse, d_o), sem=("arbitrary",), carry=carry)


def _qk_post(proj, gq, gk, cos, sin, dqh, dkc, dkp, dkm, dvc, dvp, dvm, *, name):
    p = proj.shape[0]
    nb = p // TB

    def body(qb_ref, kb_ref, gq_ref, gk_ref, cos_ref, sin_ref, dqh_ref, dkc_ref, dkp_ref, dkm_ref,
             dvc_ref, dvp_ref, dvm_ref, dqb_ref, dkb_ref, dvb_ref, dgq_ref, dgk_ref, tk_ref, tv_ref):
        n = pl.program_id(0)

        @pl.when(n == 0)
        def _():
            dgq_ref[...] = jnp.zeros_like(dgq_ref)
            dgk_ref[...] = jnp.zeros_like(dgk_ref)

        keep = jnp.where(n == nb - 1, 0.0, 1.0)
        tk_ref[...] = dkc_ref[...] + keep * dkp_ref[...]
        tv_ref[...] = dvc_ref[...] + keep * dvp_ref[...]

        @pl.when(n == 0)
        def _():
            tk_ref[PAD_FRONT:TB, :] += dkm_ref[...]
            tv_ref[PAD_FRONT:TB, :] += dvm_ref[...]

        first = _lane() < DH

        def fold(t_ref):
            t0, t1 = t_ref[:, 0:TB], t_ref[:, TB:2 * TB]
            return jnp.where(first, t0 + pltpu.roll(t0, DH, 1), t1 + pltpu.roll(t1, DH, 1))

        bd = _head_ones()
        cos_v, sin_v = cos_ref[...], sin_ref[...]
        dvb_ref[...] = fold(tv_ref).astype(BF16)
        dkb, dgk = _norm_rope_bwd(fold(tk_ref), kb_ref[...], gk_ref[...], cos_v, sin_v, bd)
        dkb_ref[...] = dkb.astype(BF16)
        dgk_ref[...] += dgk
        dgq = jnp.zeros((1, TB), F32)
        for j in range(HB // 2):
            sl = slice(j * TB, (j + 1) * TB)
            dqb, dg = _norm_rope_bwd(dqh_ref[:, sl], qb_ref[:, sl], gq_ref[...], cos_v, sin_v, bd)
            dqb_ref[:, sl] = dqb.astype(BF16)
            dgq = dgq + dg
        dgq_ref[...] += dgq

    vec = pl.BlockSpec((1, TB), lambda n: (0, 0))
    tab = pl.BlockSpec((TB, TB), lambda n: (n, 0))
    cur = lambda w: pl.BlockSpec((TB, w), lambda n: (n, 0))
    nxt = pl.BlockSpec((TB, 256), lambda n: (jnp.minimum(n + 1, nb - 1), 0))
    meta = pl.BlockSpec((N_META, 256), lambda n: (0, 0))
    return pl.pallas_call(
        body, name=name, grid=(nb,),
        in_specs=[pl.BlockSpec((TB, 1024), lambda n: (n, C_QB)), pl.BlockSpec((TB, TB), lambda n: (n, C_KB)),
                  vec, vec, tab, tab, cur(1024), cur(256), nxt, meta, cur(256), nxt, meta],
        out_specs=[cur(1024), cur(TB), cur(TB), vec, vec],
        out_shape=[jax.ShapeDtypeStruct((p, 1024), BF16), jax.ShapeDtypeStruct((p, TB), BF16),
                   jax.ShapeDtypeStruct((p, TB), BF16), jax.ShapeDtypeStruct((1, TB), F32),
                   jax.ShapeDtypeStruct((1, TB), F32)],
        scratch_shapes=[pltpu.VMEM((TB, 256), F32), pltpu.VMEM((TB, 256), F32)],
        compiler_params=_params("arbitrary"),
    )(proj, proj, gq, gk, cos, sin, dqh, dkc, dkp, dkm, dvc, dvp, dvm)


EXT = TB + N_META


def _pool_count_inv(n, w):
    t = n * TB + _iota((TB, 1), 0)
    cnt = jnp.clip(t - (PAD_FRONT - 1), 1, w)
    return 1.0 / cnt.astype(F32)


def _silu_parts(gate):
    s = _sigmoid(gate)
    return gate * s, s * (1.0 + gate * (1.0 - s))


def _mix_fwd(proj, oa, yb, hg, pool_w, pool_scale, *, name):
    p = proj.shape[0]

    def body(ga_ref, gb_ref, uc_ref, up_ref, gc_ref, oa_ref, yb_ref, hg_ref, pw_ref, ps_ref, mx_ref, pooled_ref):
        n = pl.program_id(0)
        valid = ((n * TB + _iota((TB, 1), 0)) >= PAD_FRONT).astype(F32)
        for hd in range(HA):
            sl = slice(hd * TB, (hd + 1) * TB)
            o = oa_ref[:, sl]
            r = lax.rsqrt(jnp.mean(o * o, axis=-1, keepdims=True) + RMS_EPS)
            act, _ = _silu_parts(ga_ref[:, sl])
            mx_ref[:, sl] = (o * r * hg_ref[...] * act).astype(BF16)
        for j in range(HB // 2):
            sl = slice(j * TB, (j + 1) * TB)
            act, _ = _silu_parts(gb_ref[:, sl])
            mx_ref[:, 512 + j * TB:512 + (j + 1) * TB] = (yb_ref[:, sl] * act).astype(BF16)
        ri, ci = _iota((TB, EXT), 0), _iota((TB, EXT), 1)
        has_prev = jnp.where(n == 0, 0.0, 1.0)
        for gi, w in enumerate(POOL_WINDOWS):
            sl = slice(gi * TB, (gi + 1) * TB)
            ug = uc_ref[:, sl] * valid
            ext = jnp.concatenate([up_ref[:, sl] * has_prev, ug], axis=0)
            band = jnp.where((ci <= ri + N_META) & (ci > ri + N_META - w), 1.0, 0.0).astype(BF16)
            pooled = (_xdot(band, ext) * _pool_count_inv(n, w) - ug) * valid
            pooled_ref[:, sl] = pooled
            yc = _dot(pooled.astype(BF16), pw_ref[gi], NN) * ps_ref[:, sl]
            act, _ = _silu_parts(gc_ref[:, sl])
            mx_ref[:, 1536 + gi * TB:1536 + (gi + 1) * TB] = (yc * act).astype(BF16)

    cur = lambda w, c=0: pl.BlockSpec((TB, w), lambda n, c=c: (n, c))
    prev16 = pl.BlockSpec((N_META, 512), lambda n: (jnp.maximum(n * (TB // N_META) - 1, 0), C_UC))
    return pl.pallas_call(
        body, name=name, grid=(p // TB,),
        in_specs=[cur(512, C_GA), cur(1024, C_GB), cur(512, C_UC), prev16, cur(512, C_GC), cur(512), cur(1024),
                  pl.BlockSpec((1, TB), lambda n: (0, 0)), pl.BlockSpec((4, TB, TB), lambda n: (0, 0, 0)),
                  pl.BlockSpec((1, 512), lambda n: (0, 0))],
        out_specs=[cur(MIX), cur(512)],
        out_shape=[jax.ShapeDtypeStruct((p, MIX), BF16), jax.ShapeDtypeStruct((p, 512), F32)],
        compiler_params=_params("parallel"),
    )(proj, proj, proj, proj, proj, oa, yb, hg, pool_w, pool_scale)


def _mix_bwd(proj, oa, yb, pooled, hg, pool_w, pool_scale, d_mixed, *, name, carry=None):
    p = proj.shape[0]

    def body(ga_ref, gb_ref, gc_ref, oa_ref, yb_ref, pooled_ref, hg_ref, pw_ref, ps_ref, dm_ref,
             doa_ref, dyb_ref, dga_ref, dgb_ref, dgc_ref, dp_ref, dhg_ref, dpw_ref, dps_ref):
        n = pl.program_id(0)

        @pl.when(n == 0)
        def _():
            dhg_ref[...] = jnp.zeros_like(dhg_ref)
            dpw_ref[...] = jnp.zeros_like(dpw_ref)
            dps_ref[...] = jnp.zeros_like(dps_ref)

        valid = ((n * TB + _iota((TB, 1), 0)) >= PAD_FRONT).astype(F32)
        dhg = jnp.zeros((1, TB), F32)
        for hd in range(HA):
            sl = slice(hd * TB, (hd + 1) * TB)
            o = oa_ref[:, sl]
            r = lax.rsqrt(jnp.mean(o * o, axis=-1, keepdims=True) + RMS_EPS)
            on = o * r
            gate = ga_ref[:, sl]
            act, dact = _silu_parts(gate)
            dmx = dm_ref[:, sl]
            d_ya = dmx * act
            dga_ref[:, sl] = (dmx * on * hg_ref[...] * dact).astype(BF16)
            dyn = d_ya * hg_ref[...]
            doa_ref[:, sl] = r * (dyn - on * jnp.mean(dyn * on, axis=-1, keepdims=True))
            dhg = dhg + jnp.sum(d_ya * on, axis=0, keepdims=True)
        dhg_ref[...] += dhg
        for j in range(HB // 2):
            sl = slice(j * TB, (j + 1) * TB)
            act, dact = _silu_parts(gb_ref[:, sl])
            dmx = dm_ref[:, 512 + j * TB:512 + (j + 1) * TB]
            dyb_ref[:, sl] = dmx * act
            dgb_ref[:, sl] = (dmx * yb_ref[:, sl] * dact).astype(BF16)
        for gi in range(len(POOL_WINDOWS)):
            sl = slice(gi * TB, (gi + 1) * TB)
            pooled = pooled_ref[:, sl]
            pooled_b = pooled.astype(BF16)
            t = _dot(pooled_b, pw_ref[gi], NN)
            act, dact = _silu_parts(gc_ref[:, sl])
            dmx = dm_ref[:, 1536 + gi * TB:1536 + (gi + 1) * TB]
            d_yc = dmx * act
            dgc_ref[:, sl] = (dmx * t * ps_ref[:, sl] * dact).astype(BF16)
            dps_ref[:, sl] += jnp.sum(d_yc * t, axis=0, keepdims=True)
            d_t = (d_yc * ps_ref[:, sl]).astype(BF16)
            dp_ref[:, sl] = _dot(d_t, pw_ref[gi], NT) * valid
            dpw_ref[gi] += _dot(pooled_b, d_t, TN)

    cur = lambda w, c=0: pl.BlockSpec((TB, w), lambda n, c=c: (n, c))
    return _call(
        body, name=name, grid=(p // TB,),
        in_specs=[cur(512, C_GA), cur(1024, C_GB), cur(512, C_GC), cur(512), cur(1024), cur(512),
                  pl.BlockSpec((1, TB), lambda n: (0, 0)), pl.BlockSpec((4, TB, TB), lambda n: (0, 0, 0)),
                  pl.BlockSpec((1, 512), lambda n: (0, 0)), cur(MIX)],
        out_specs=[cur(512), cur(1024), cur(512), cur(1024), cur(512), cur(512),
                   pl.BlockSpec((1, TB), lambda n: (0, 0)), pl.BlockSpec((4, TB, TB), lambda n: (0, 0, 0)),
                   pl.BlockSpec((1, 512), lambda n: (0, 0))],
        out_shape=[jax.ShapeDtypeStruct((p, 512), F32), jax.ShapeDtypeStruct((p, 1024), F32),
                   jax.ShapeDtypeStruct((p, 512), BF16), jax.ShapeDtypeStruct((p, 1024), BF16),
                   jax.ShapeDtypeStruct((p, 512), BF16), jax.ShapeDtypeStruct((p, 512), F32),
                   jax.ShapeDtypeStruct((1, TB), F32), jax.ShapeDtypeStruct((4, TB, TB), F32),
                   jax.ShapeDtypeStruct((1, 512), F32)],
        args=(proj, proj, proj, oa, yb, pooled, hg, pool_w, pool_scale, d_mixed), sem=("arbitrary",), carry=carry)


def _pool_bwd(dp, *, name):
    p = dp.shape[0]
    nb = p // TB

    def body(dp_ref, dn_ref, duc_ref):
        n = pl.program_id(0)
        valid = ((n * TB + _iota((TB, 1), 0)) >= PAD_FRONT).astype(F32)
        has_next = jnp.where(n == nb - 1, 0.0, 1.0)
        ri, ci = _iota((TB, EXT), 0), _iota((TB, EXT), 1)
        for gi, w in enumerate(POOL_WINDOWS):
            sl = slice(gi * TB, (gi + 1) * TB)
            d_p = dp_ref[:, sl]
            ext = jnp.concatenate([d_p * _pool_count_inv(n, w), dn_ref[:, sl] * (has_next / w)], axis=0)
            band = jnp.where((ci >= ri) & (ci < ri + w), 1.0, 0.0).astype(BF16)
            duc_ref[:, sl] = ((_xdot(band, ext) - d_p) * valid).astype(BF16)

    return pl.pallas_call(
        body, name=name, grid=(nb,),
        in_specs=[pl.BlockSpec((TB, 512), lambda n: (n, 0)),
                  pl.BlockSpec((N_META, 512), lambda n: (jnp.minimum(n + 1, nb - 1) * (TB // N_META), 0))],
        out_specs=pl.BlockSpec((TB, 512), lambda n: (n, 0)),
        out_shape=jax.ShapeDtypeStruct((p, 512), BF16),
        compiler_params=_params("parallel"),
    )(dp, dp)


def _carried(carries, key, local, fn, *args, **kw):
    if key not in carries:
        return fn(*args, **kw)
    make_src, gather, done = carries[key]
    *outs, stack = fn(*args, carry=(make_src(local), gather), **kw)
    done(stack)
    return outs


def _layer_fwd(h, w, tag, carries):
    xn, = _carried(carries, "rmsnorm_fwd", None, _rmsnorm_fwd, h, w["norm_g"], name=f"rmsnorm_fwd{tag}")
    proj, = _carried(carries, "in_proj", None, _mm_nn, xn, w["w_in"], name=f"in_proj{tag}")
    oa, sck = _carried(carries, "hgrn_fwd", None, _hgrn_fwd, proj, w["lb"], name=f"hgrn_fwd{tag}")
    qh, k2, v2 = _qk_prep(proj, w["gq"], w["gk"], w["cos"], w["sin"], name=f"qk_prep{tag}")
    yb, lse = _carried(carries, "attn_fwd", None, _attn_fwd, qh, k2, v2, w["sinks"], name=f"attn_fwd{tag}")
    mixed, pooled = _mix_fwd(proj, oa, yb, w["hg"], w["pool_w"], w["pool_scale"], name=f"mix_fwd{tag}")
    h_next, = _mm_nn(mixed, w["w_out"], h, name=f"out_proj{tag}", tn=512)
    saved = dict(h=h, xn=xn, proj=proj, oa=oa, sck=sck, qh=qh, k2=k2, v2=v2, yb=yb, lse=lse, mixed=mixed, pooled=pooled)
    return h_next, saved


def _layer_bwd(dh_out, dhb, s, w, tag, carries, first_layer=False):
    g = {}
    d_mixed, = _carried(carries, "d_mixed", g, _mm_nt, dhb, w["w_out"], name=f"d_mixed{tag}", tm=1664, tn=512)
    g["w_out"], = _mm_tn(s["mixed"], dhb, name=f"dw_out{tag}", tn=1024)
    d_oa, d_yb, d_ga, d_gb, d_gc, d_p, g["hg"], g["pool_w"], g["pool_scale"] = _carried(
        carries, "mix_bwd", g, _mix_bwd,
        s["proj"], s["oa"], s["yb"], s["pooled"], w["hg"], w["pool_w"], w["pool_scale"], d_mixed, name=f"mix_bwd{tag}")
    d_uc = _pool_bwd(d_p, name=f"pool_bwd{tag}")
    d_qh, dkc, dkp, dvc, dvp, dkm, dvm, g["sinks"] = _carried(
        carries, "attn_bwd", g, _attn_bwd, s["qh"], s["k2"], s["v2"], w["sinks"], s["yb"], s["lse"], d_yb,
        name=f"attn_bwd{tag}")
    d_qb, d_kb, d_vb, g["gq"], g["gk"] = _qk_post(s["proj"], w["gq"], w["gk"], w["cos"], w["sin"], d_qh, dkc, dkp, dkm,
                                                  dvc, dvp, dvm, name=f"qk_post{tag}")
    d_qa, d_fa, d_ia, g["lb"] = _carried(carries, "hgrn_bwd", g, _hgrn_bwd, s["proj"], w["lb"], s["sck"], d_oa,
                                         name=f"hgrn_bwd{tag}")
    d_proj = jnp.concatenate([d_qa, d_fa, d_ia, d_ga, d_qb, d_gb, d_uc, d_gc, d_kb, d_vb], axis=1)
    d_xn, = _mm_nt(d_proj, w["w_in"], name=f"d_xn{tag}", tk=PROJ_COLS)
    out, aux, g["norm_g"] = _rmsnorm_bwd(d_xn, s["h"], w["norm_g"], dh_out, name=f"rmsnorm_bwd{tag}", first_layer=first_layer)
    g["w_in_parts"] = []
    for i in range(W_IN_PARTS):
        part, = _carried(carries, f"dw_in_{i}", g, _mm_tn, s["xn"], d_proj, name=f"dw_in_{i}{tag}",
                         m_part=(i, W_IN_PARTS), tm=D_MODEL // W_IN_PARTS)
        g["w_in_parts"].append(part)
    return out, aux, g


def _peers():
    x, y, c = lax.axis_index("x"), lax.axis_index("y"), lax.axis_index("c")
    out = []
    for k in range(1, N_DEV):
        kx, ky, kc = (k >> 2) & 1, (k >> 1) & 1, k & 1
        px, py, pc = x ^ kx, y ^ ky, c ^ kc
        out.append(((px, py, pc), 4 * px + 2 * py + pc))
    return 4 * x + 2 * y + c, out


def _exchange_copies(src_ref, out_ref, send_sems, recv_sems, local_sem):
    me, peers = _peers()
    mine = pltpu.make_async_copy(src_ref.at[me], out_ref.at[me], local_sem)
    copies = []
    for k, (dev, idx) in enumerate(peers):
        copies.append(pltpu.make_async_remote_copy(
            src_ref=src_ref.at[idx], dst_ref=out_ref.at[me],
            send_sem=send_sems.at[k], recv_sem=recv_sems.at[k],
            device_id=dev, device_id_type=pl.DeviceIdType.MESH))
    return mine, copies


def _gather_copies(src_ref, out_ref, send_sems, recv_sems, local_sem):
    x, y, c = lax.axis_index("x"), lax.axis_index("y"), lax.axis_index("c")
    slot = lambda px, py, pc: out_ref.at[4 * px + 2 * py + pc]
    sibling = (x, y, 1 - c)
    chips = [(1 - x, y), (x, 1 - y), (1 - x, 1 - y)]

    def copy(k, src, block, to):
        return pltpu.make_async_remote_copy(src_ref=src, dst_ref=slot(*block), send_sem=send_sems.at[k],
                                            recv_sem=recv_sems.at[k], device_id=to, device_id_type=pl.DeviceIdType.MESH)

    mine = lambda: pltpu.make_async_copy(src_ref, slot(x, y, c), local_sem)
    own = lambda: ([copy(0, src_ref, (x, y, c), sibling)]
                   + [copy(1 + j, src_ref, (x, y, c), (*chip, c)) for j, chip in enumerate(chips)])
    passing = lambda: [copy(4 + j, slot(*chip, c), (*chip, c), sibling) for j, chip in enumerate(chips)]
    arrivals = lambda: ([copy(0, src_ref, sibling, sibling)]
                        + [copy(1 + j, src_ref, (*chip, c), sibling) for j, chip in enumerate(chips)]
                        + [copy(4 + j, src_ref, (*chip, 1 - c), sibling) for j, chip in enumerate(chips)])
    return mine, own, passing, arrivals


def _exchange_start(*refs, gather):
    if gather:
        mine, own, _, _ = _gather_copies(*refs)
        mine().start()
        for cp in own():
            cp.start()
        return
    mine, copies = _exchange_copies(*refs)
    mine.start()
    for cp in copies:
        cp.start()


def _exchange_wait(*refs, gather):
    if gather:
        mine, own, passing, arrivals = _gather_copies(*refs)
        passing, arrivals = passing(), arrivals()
        for j, cp in enumerate(passing):
            arrivals[1 + j].wait_recv()
            cp.start()
        arrivals[0].wait_recv()
        for cp in arrivals[4:]:
            cp.wait_recv()
        for cp in own() + passing:
            cp.wait_send()
        mine().wait()
        return
    mine, copies = _exchange_copies(*refs)
    for cp in copies:
        cp.wait_recv()
    for cp in copies:
        cp.wait_send()
    mine.wait()


def _exchange_scratch():
    return [pltpu.SemaphoreType.DMA((N_DEV - 1,)), pltpu.SemaphoreType.DMA((N_DEV - 1,)), pltpu.SemaphoreType.DMA]


def _exchange(src, *, gather, name):
    rows, cols = src.shape[-2:]

    def body(src_ref, out_ref, send_sems, recv_sems, local_sem):
        _exchange_start(src_ref, out_ref, send_sems, recv_sems, local_sem, gather=gather)
        _exchange_wait(src_ref, out_ref, send_sems, recv_sems, local_sem, gather=gather)

    return pl.pallas_call(
        body, name=name,
        in_specs=[pl.BlockSpec(memory_space=pl.ANY)], out_specs=pl.BlockSpec(memory_space=pl.ANY),
        out_shape=jax.ShapeDtypeStruct((N_DEV, rows, cols), src.dtype),
        scratch_shapes=_exchange_scratch(),
    )(src)


def _call(body, *, name, grid, in_specs, out_specs, out_shape, args, sem, scratch_shapes=(), carry=None):
    if carry is None:
        return pl.pallas_call(
            body, name=name, grid=grid, in_specs=list(in_specs), out_specs=list(out_specs), out_shape=list(out_shape),
            scratch_shapes=list(scratch_shapes), compiler_params=_params(*sem))(*args)
    src, gather = carry
    n_in, n_out, n_scr = len(in_specs), len(out_specs), len(scratch_shapes)
    rows, cols = src.shape[-2:]

    def carrying(*refs):
        ins, src_ref = refs[:n_in], refs[n_in]
        outs, dst_ref = refs[n_in + 1:n_in + 1 + n_out], refs[n_in + 1 + n_out]
        scr = refs[n_in + 2 + n_out:]
        exch = (src_ref, dst_ref) + tuple(scr[n_scr:])
        first, last = None, None
        for a, size in enumerate(grid):
            f, l = pl.program_id(a) == 0, pl.program_id(a) == size - 1
            first = f if first is None else first & f
            last = l if last is None else last & l

        @pl.when(first)
        def _():
            _exchange_start(*exch, gather=gather)

        body(*ins, *outs, *scr[:n_scr])

        @pl.when(last)
        def _():
            _exchange_wait(*exch, gather=gather)

    hbm = pl.BlockSpec(memory_space=pl.ANY)
    return pl.pallas_call(
        carrying, name=name, grid=grid, in_specs=list(in_specs) + [hbm], out_specs=list(out_specs) + [hbm],
        out_shape=list(out_shape) + [jax.ShapeDtypeStruct((N_DEV, rows, cols), src.dtype)],
        scratch_shapes=list(scratch_shapes) + _exchange_scratch(),
        compiler_params=_params(*(("arbitrary",) * len(grid))))(*args, src)


def _adamw(stacks, w, m, v, *, name, carry=None):
    nl = len(stacks)
    rows, cols = stacks[0].shape[1:]
    tr = rows
    stack_block_bytes = 8 * 1024 * 1024 // nl
    for cand in (256, 128, 64, 32, 16):
        if rows % cand == 0 and N_DEV * cand * cols * stacks[0].dtype.itemsize <= stack_block_bytes:
            tr = cand
            break
    nt = rows // tr

    def body(*refs):
        s_refs = refs[:nl]
        w_ref, m_ref, v_ref, g_ref, d_ref, nm_ref, nv_ref = refs[nl:]
        for l, s_ref in enumerate(s_refs):
            @pl.when(pl.program_id(0) == l)
            def _(s_ref=s_ref):
                acc = s_ref[0].astype(F32)
                for d in range(1, N_DEV):
                    acc = acc + s_ref[d].astype(F32)
                g_ref[...] = acc

        g = g_ref[...]
        nm = ADAM_B1 * m_ref[...] + (1.0 - ADAM_B1) * g
        nv = ADAM_B2 * v_ref[...] + (1.0 - ADAM_B2) * (g * g)
        m_hat = nm / (1.0 - ADAM_B1 ** ADAM_STEP)
        v_hat = nv / (1.0 - ADAM_B2 ** ADAM_STEP)
        d_ref[...] = -ADAM_LR * (m_hat / (jnp.sqrt(v_hat) + ADAM_EPS) + ADAM_WD * w_ref[...])
        nm_ref[...] = nm
        nv_ref[...] = nv

    blk = pl.BlockSpec((tr, cols), lambda l, i: (l * nt + i, 0))
    return _call(
        body, name=name, grid=(nl, nt),
        in_specs=[pl.BlockSpec((N_DEV, tr, cols), lambda l, i, k=k: (0, jnp.where(l == k, i, 0), 0)) for k in range(nl)]
        + [blk, blk, blk],
        out_specs=[blk] * 4, out_shape=[jax.ShapeDtypeStruct((nl * rows, cols), F32)] * 4,
        args=(*stacks, w, m, v), sem=("arbitrary", "arbitrary"), carry=carry)


def _lb_all(lb_logits):
    sm = jax.nn.softmax(lb_logits.astype(F32), axis=0)
    return jnp.cumsum(sm, axis=0) - sm[0:1]


def _rope_tables(p):
    half = DH // 2
    inv = jnp.power(ROPE_THETA, -jnp.arange(half, dtype=F32) * 2.0 / DH)
    pos = (jnp.arange(p) - PAD_FRONT).astype(F32)
    ang = pos[:, None] * inv[None, :]
    cos, sin = jnp.cos(ang), jnp.sin(ang)
    return jnp.tile(cos, (1, 4)), jnp.tile(jnp.concatenate([-sin, sin], axis=1), (1, 2))


def _permute_cols(w):
    return jnp.concatenate([w[:, :3072], w[:, 3328:], w[:, 3072:3328]], axis=1)


def _unpermute_cols(w):
    return jnp.concatenate([w[:, :3072], w[:, 5120:], w[:, 3072:5120]], axis=1)


SMALL = (("lb_logits", (DEPTH, 512)), ("norm_g", (DEPTH, D_MODEL)), ("q_norm_g", (DEPTH, DH)),
         ("k_norm_g", (DEPTH, DH)), ("attn_sinks", (DEPTH, HB)), ("hgrn_norm_g", (DEPTH, 128)),
         ("pool_w", (DEPTH, 4, 128, 128)), ("pool_scale", (DEPTH, 512)))


def _pack_small(d):
    flat = jnp.concatenate([d[k].astype(F32).reshape(-1) for k, _ in SMALL])
    pad = (-flat.shape[0]) % (8 * 128)
    return jnp.pad(flat, (0, pad)).reshape(-1, 128)


def _unpack_small(a):
    flat = a.reshape(-1)
    out, off = {}, 0
    for k, shp in SMALL:
        n = int(np.prod(shp))
        out[k] = flat[off:off + n].reshape(shp)
        off += n
    return out


def kernel(x, meta_tokens, lb_logits, norm_g, w_in, q_norm_g, k_norm_g, attn_sinks, hgrn_norm_g, pool_w, pool_scale, w_out, loss_target, m_meta_tokens, m_lb_logits, m_norm_g, m_w_in, m_q_norm_g, m_k_norm_g, m_attn_sinks, m_hgrn_norm_g, m_pool_w, m_pool_scale, m_w_out, v_meta_tokens, v_lb_logits, v_norm_g, v_w_in, v_q_norm_g, v_k_norm_g, v_attn_sinks, v_hgrn_norm_g, v_pool_w, v_pool_scale, v_w_out):
    seq = x.shape[1]
    p = seq + TB
    cs = PROJ_COLS // N_DEV
    rs = MIX // N_DEV
    ms = D_MODEL // N_DEV

    full_w_in = lambda st: _permute_cols(st.transpose(1, 0, 2).reshape(D_MODEL, PROJ_COLS))
    dw_in_blocks = lambda rows: _unpermute_cols(rows).reshape(rows.shape[0], N_DEV, cs).transpose(1, 0, 2).astype(BF16)
    dw_out_blocks = lambda g: g["w_out"].reshape(N_DEV, rs, D_MODEL).astype(BF16)
    w_in_bf = w_in.astype(BF16)

    lb_all, lb_vjp = jax.vjp(_lb_all, lb_logits)
    cos, sin = _rope_tables(p)
    layers = []
    for l in range(DEPTH):
        layers.append(dict(
            norm_g=norm_g[l][None], lb=lb_all[l][None],
            gq=jnp.tile(q_norm_g[l], 2)[None], gk=jnp.tile(k_norm_g[l], 2)[None], sinks=attn_sinks[l],
            hg=hgrn_norm_g[l][None], pool_w=pool_w[l].astype(BF16), pool_scale=pool_scale[l][None], cos=cos, sin=sin))
    meta_all = _exchange(meta_tokens, gather=True, name="gather_meta")
    meta_full = meta_all.transpose(1, 0, 2).reshape(N_META, D_MODEL)

    def got_w_out(st):
        st = st.reshape(N_DEV, DEPTH, rs, D_MODEL)
        for l in range(DEPTH):
            layers[l]["w_out"] = st[:, l].reshape(MIX, D_MODEL)

    def got_w_in(l):
        def done(st):
            layers[l]["w_in"] = full_w_in(st)
        return done

    fwd_carries = [
        dict(rmsnorm_fwd=(lambda _: w_in_bf[0], True, got_w_in(0)),
             in_proj=(lambda _: w_in_bf[1], True, got_w_in(1)),
             hgrn_fwd=(lambda _: w_out.reshape(DEPTH * rs, D_MODEL).astype(BF16), True, got_w_out)),
        {}]
    h = jnp.concatenate([jnp.zeros((PAD_FRONT, D_MODEL), F32), meta_full, x[0]], axis=0)
    saved = []
    for l in range(DEPTH):
        h, s = _layer_fwd(h, layers[l], f"_l{l}", fwd_carries[l])
        saved.append(s)
    dh, dhb, sq = _loss_grad(h, loss_target[0], name="loss_grad")
    loss = lax.psum(0.5 * jnp.sum(sq) / D_MODEL, ("x", "y", "c"))

    grads = [None] * DEPTH
    win_stacks, wout_stacks, small_stacks = [None] * (W_IN_PARTS * DEPTH), [None] * DEPTH, [None]

    def into(stacks, i):
        def done(st):
            stacks[i] = st
        return done

    def small_grads(g0):
        both = [g0, grads[1]]
        stk = lambda k: jnp.stack([both[l][k][0] for l in range(DEPTH)])
        fold = lambda a: a[:, :DH] + a[:, DH:]
        return _pack_small(dict(
            lb_logits=lb_vjp(stk("lb"))[0], norm_g=stk("norm_g"), q_norm_g=fold(stk("gq")), k_norm_g=fold(stk("gk")),
            attn_sinks=stk("sinks")[:, :HB], hgrn_norm_g=stk("hg"),
            pool_w=jnp.stack([both[l]["pool_w"] for l in range(DEPTH)]), pool_scale=stk("pool_scale")))

    def part(l, i, own_layer=True):
        src = lambda g: dw_in_blocks((g if own_layer else grads[l])["w_in_parts"][i])
        return src, False, into(win_stacks, W_IN_PARTS * l + i)

    bwd_carries = [
        dict(d_mixed=part(1, 3, own_layer=False), hgrn_bwd=(dw_out_blocks, False, into(wout_stacks, 0)),
             dw_in_0=(small_grads, True, into(small_stacks, 0)), dw_in_1=part(0, 0), dw_in_2=part(0, 1), dw_in_3=part(0, 2)),
        dict(attn_bwd=(dw_out_blocks, False, into(wout_stacks, 1)), dw_in_1=part(1, 0), dw_in_2=part(1, 1), dw_in_3=part(1, 2))]
    dh, dhb, grads[1] = _layer_bwd(dh, dhb, saved[1], layers[1], "_l1", bwd_carries[1])
    grad_x, d_front, grads[0] = _layer_bwd(dh, dhb, saved[0], layers[0], "_l0", bwd_carries[0], first_layer=True)
    grad_x = grad_x[None]
    dmeta = d_front.reshape(N_META, N_DEV, ms).transpose(1, 0, 2)
    meta_stack = _exchange(dmeta, gather=False, name="scatter_dmeta")
    small_stack = small_stacks[0]

    *adam_wout, last_stack = _adamw(wout_stacks, w_out.reshape(DEPTH * rs, D_MODEL), m_w_out.reshape(DEPTH * rs, D_MODEL),
                                    v_w_out.reshape(DEPTH * rs, D_MODEL), name="adamw_w_out",
                                    carry=(dw_in_blocks(grads[0]["w_in_parts"][W_IN_PARTS - 1]), False))
    g_wout, d_wout, nm_wout, nv_wout = adam_wout
    win_stacks[W_IN_PARTS - 1] = last_stack
    g_win, d_win, nm_win, nv_win = _adamw(win_stacks, w_in.reshape(DEPTH * D_MODEL, cs), m_w_in.reshape(DEPTH * D_MODEL, cs),
                                          v_w_in.reshape(DEPTH * D_MODEL, cs), name="adamw_w_in")
    g_meta, d_meta, nm_meta, nv_meta = _adamw([meta_stack], meta_tokens, m_meta_tokens, v_meta_tokens, name="adamw_meta")
    small_w = dict(lb_logits=lb_logits, norm_g=norm_g, q_norm_g=q_norm_g, k_norm_g=k_norm_g, attn_sinks=attn_sinks,
                   hgrn_norm_g=hgrn_norm_g, pool_w=pool_w, pool_scale=pool_scale)
    small_m = dict(lb_logits=m_lb_logits, norm_g=m_norm_g, q_norm_g=m_q_norm_g, k_norm_g=m_k_norm_g, attn_sinks=m_attn_sinks,
                   hgrn_norm_g=m_hgrn_norm_g, pool_w=m_pool_w, pool_scale=m_pool_scale)
    small_v = dict(lb_logits=v_lb_logits, norm_g=v_norm_g, q_norm_g=v_q_norm_g, k_norm_g=v_k_norm_g, attn_sinks=v_attn_sinks,
                   hgrn_norm_g=v_hgrn_norm_g, pool_w=v_pool_w, pool_scale=v_pool_scale)
    small_out = [_unpack_small(a) for a in _adamw([small_stack], _pack_small(small_w), _pack_small(small_m),
                                                  _pack_small(small_v), name="adamw_small")]

    big = dict(
        meta_tokens=(g_meta, d_meta, nm_meta, nv_meta),
        w_in=tuple(a.reshape(DEPTH, D_MODEL, cs) for a in (g_win, d_win, nm_win, nv_win)),
        w_out=tuple(a.reshape(DEPTH, rs, D_MODEL) for a in (g_wout, d_wout, nm_wout, nv_wout)))
    order = ("meta_tokens", "lb_logits", "norm_g", "w_in", "q_norm_g", "k_norm_g", "attn_sinks", "hgrn_norm_g",
             "pool_w", "pool_scale", "w_out")
    outs = [loss, grad_x]
    for kind in range(4):
        for k in order:
            outs.append(big[k][kind] if k in big else small_out[kind][k])
    return tuple(outs)
```

```python
import functools

import numpy as np
import jax
import jax.numpy as jnp
from jax import lax
from jax.experimental import pallas as pl
from jax.experimental.pallas import tpu as pltpu

F32, BF16 = jnp.float32, jnp.bfloat16

D_MODEL = 2048
DEPTH = 2
N_META = 16
TB = 128
PAD_FRONT = TB - N_META
RMS_EPS = 1e-6
NEG_INF = -1e30
LOG_FLOOR = 1e-30
HA, DK_A = 4, 128
CH = 16
NCH = TB // CH
HB, KVH, DH = 16, 2, 64
GRP = HB // KVH
ROPE_THETA = 10000.0
POOL_WINDOWS = (2, 4, 8, 16)
PROJ_COLS = 5376
MIX = 2048
N_DEV = 8
W_IN_PARTS = 2
C_QA, C_FA, C_IA, C_GA = 0, 1, 2, 3
C_QB, C_GB = 2, 3
C_UC, C_GC = 8, 9
C_KB, C_VB = 40, 41

ADAM_LR, ADAM_B1, ADAM_B2, ADAM_EPS, ADAM_WD, ADAM_STEP = 0.001, 0.9, 0.999, 1e-08, 0.01, 10

VMEM_LIMIT = 48 * 1024 * 1024

NN = ((1,), (0,))
NT = ((1,), (1,))
TN = ((0,), (0,))


def _dot(a, b, dims):
    return lax.dot_general(a, b, (dims, ((), ())), preferred_element_type=F32)


def _split3(x):
    hi = x.astype(BF16)
    r = x - hi.astype(F32)
    mid = r.astype(BF16)
    lo = (r - mid.astype(F32)).astype(BF16)
    return hi, mid, lo


def _xdot(m01, x):
    hi, mid, lo = _split3(x)
    return _dot(m01, hi, NN) + _dot(m01, mid, NN) + _dot(m01, lo, NN)


def _xdot_r(x, m01):
    hi, mid, lo = _split3(x)
    return _dot(hi, m01, NN) + _dot(mid, m01, NN) + _dot(lo, m01, NN)


def _iota(shape, dim):
    return lax.broadcasted_iota(jnp.int32, shape, dim)


def _params(*sem):
    return pltpu.CompilerParams(dimension_semantics=sem, vmem_limit_bytes=VMEM_LIMIT)


def _row_tile(p, target):
    best = TB
    t = TB
    while t <= target:
        if p % t == 0:
            best = t
        t += TB
    return best


def _col_tile(n, target):
    best = 128
    t = 128
    while t <= target:
        if n % t == 0:
            best = t
        t += 128
    return best


def _sigmoid(x):
    return 1.0 / (1.0 + jnp.exp(-x))


def _mm_nn(a, b, res=None, *, name, tm=1664, tn=768, carry=None):
    m, k = a.shape
    n = b.shape[1]
    tm, tn = _row_tile(m, tm), _col_tile(n, tn)

    def body(*refs):
        if res is None:
            a_ref, b_ref, o_ref = refs
            o_ref[...] = _dot(a_ref[...], b_ref[...], NN)
        else:
            a_ref, b_ref, r_ref, o_ref = refs
            o_ref[...] = r_ref[...] + _dot(a_ref[...], b_ref[...], NN)

    in_specs = [pl.BlockSpec((tm, k), lambda j, i: (i, 0)), pl.BlockSpec((k, tn), lambda j, i: (0, j))]
    args = [a, b]
    if res is not None:
        in_specs.append(pl.BlockSpec((tm, tn), lambda j, i: (i, j)))
        args.append(res)
    return _call(
        body, name=name, grid=(n // tn, m // tm), in_specs=in_specs,
        out_specs=[pl.BlockSpec((tm, tn), lambda j, i: (i, j))],
        out_shape=[jax.ShapeDtypeStruct((m, n), F32)],
        args=args, sem=("parallel", "parallel"), carry=carry)


def _mm_nt(a, b, *, name, tm=640, tn=512, tk=2048, carry=None):
    m, k = a.shape
    n = b.shape[0]
    tm, tn, tk = _row_tile(m, tm), _col_tile(n, tn), _col_tile(k, tk)

    def body(a_ref, b_ref, o_ref):
        @pl.when(pl.program_id(2) == 0)
        def _():
            o_ref[...] = jnp.zeros_like(o_ref)

        o_ref[...] += _dot(a_ref[...], b_ref[...], NT)

    return _call(
        body, name=name, grid=(n // tn, m // tm, k // tk),
        in_specs=[pl.BlockSpec((tm, tk), lambda j, i, kk: (i, kk)), pl.BlockSpec((tn, tk), lambda j, i, kk: (j, kk))],
        out_specs=[pl.BlockSpec((tm, tn), lambda j, i, kk: (i, j))],
        out_shape=[jax.ShapeDtypeStruct((m, n), F32)],
        args=(a, b), sem=("parallel", "parallel", "arbitrary"), carry=carry)


def _mm_tn(a, b, *, name, tm=1024, tn=672, tk=4160, m_part=None, carry=None):
    k, m = a.shape
    n = b.shape[1]
    first, m = (0, m) if m_part is None else (m_part[0], m // m_part[1])
    tm, tn, tk = _col_tile(m, tm), _col_tile(n, tn), _row_tile(k, tk)
    first *= m // tm

    def body(a_ref, b_ref, o_ref):
        @pl.when(pl.program_id(2) == 0)
        def _():
            o_ref[...] = jnp.zeros_like(o_ref)

        o_ref[...] += _dot(a_ref[...], b_ref[...], TN)

    return _call(
        body, name=name, grid=(m // tm, n // tn, k // tk),
        in_specs=[pl.BlockSpec((tk, tm), lambda i, j, kk: (kk, first + i)), pl.BlockSpec((tk, tn), lambda i, j, kk: (kk, j))],
        out_specs=[pl.BlockSpec((tm, tn), lambda i, j, kk: (i, j))],
        out_shape=[jax.ShapeDtypeStruct((m, n), F32)],
        args=(a, b), sem=("parallel", "parallel", "arbitrary"), carry=carry)


def _rmsnorm_fwd(h, g, *, name, carry=None):
    p, dm = h.shape
    tm = _row_tile(p, 640)

    def body(h_ref, g_ref, xn_ref):
        hv = h_ref[...]
        r = lax.rsqrt(jnp.mean(hv * hv, axis=-1, keepdims=True) + RMS_EPS)
        xn_ref[...] = (hv * r * g_ref[...]).astype(BF16)

    return _call(
        body, name=name, grid=(p // tm,),
        in_specs=[pl.BlockSpec((tm, dm), lambda i: (i, 0)), pl.BlockSpec((1, dm), lambda i: (0, 0))],
        out_specs=[pl.BlockSpec((tm, dm), lambda i: (i, 0))],
        out_shape=[jax.ShapeDtypeStruct((p, dm), BF16)],
        args=(h, g), sem=("parallel",), carry=carry)


def _rmsnorm_bwd(dxn, h, g, dh_out, *, name, first_layer=False, carry=None):
    p, dm = h.shape
    tm = TB if first_layer else _row_tile(p, 384)

    def body(dxn_ref, h_ref, g_ref, dho_ref, out_ref, aux_ref, dg_ref):
        hv = h_ref[...]
        r = lax.rsqrt(jnp.mean(hv * hv, axis=-1, keepdims=True) + RMS_EPS)
        xh = hv * r
        dy = dxn_ref[...]
        dyn = dy * g_ref[...]
        dh = dho_ref[...] + r * (dyn - xh * jnp.mean(dyn * xh, axis=-1, keepdims=True))
        out_ref[...] = dh

        @pl.when(pl.program_id(0) == 0)
        def _():
            dg_ref[...] = jnp.zeros_like(dg_ref)
            if first_layer:
                aux_ref[...] = dh[PAD_FRONT:TB]

        if not first_layer:
            aux_ref[...] = dh.astype(BF16)
        dg_ref[...] += jnp.sum(dy * xh, axis=0, keepdims=True)

    row = pl.BlockSpec((tm, dm), lambda i: (i, 0))
    vec = pl.BlockSpec((1, dm), lambda i: (0, 0))
    if first_layer:
        out_specs = [pl.BlockSpec((TB, dm), lambda i: (jnp.maximum(i - 1, 0), 0)), pl.BlockSpec((N_META, dm), lambda i: (0, 0)), vec]
        out_shape = [jax.ShapeDtypeStruct((p - TB, dm), F32), jax.ShapeDtypeStruct((N_META, dm), F32)]
    else:
        out_specs = [row, row, vec]
        out_shape = [jax.ShapeDtypeStruct((p, dm), F32), jax.ShapeDtypeStruct((p, dm), BF16)]
    return _call(
        body, name=name, grid=(p // tm,),
        in_specs=[row, row, vec, row], out_specs=out_specs,
        out_shape=out_shape + [jax.ShapeDtypeStruct((1, dm), F32)],
        args=(dxn, h, g, dh_out), sem=("arbitrary",), carry=carry)


def _loss_grad(h, target, *, name):
    p, dm = h.shape

    def body(h_ref, t_ref, dh_ref, dhb_ref, sq_ref):
        n = pl.program_id(0)

        @pl.when(n == 0)
        def _():
            dh_ref[...] = jnp.zeros_like(dh_ref)
            dhb_ref[...] = jnp.zeros_like(dhb_ref)
            sq_ref[...] = jnp.zeros_like(sq_ref)

        @pl.when(n > 0)
        def _():
            err = h_ref[...] - t_ref[...]
            dh = err * (1.0 / dm)
            dh_ref[...] = dh
            dhb_ref[...] = dh.astype(BF16)
            sq_ref[...] += jnp.sum(err * err, axis=0, keepdims=True)

    row = pl.BlockSpec((TB, dm), lambda n: (n, 0))
    return pl.pallas_call(
        body, name=name, grid=(p // TB,),
        in_specs=[row, pl.BlockSpec((TB, dm), lambda n: (jnp.maximum(n - 1, 0), 0))],
        out_specs=[row, row, pl.BlockSpec((1, dm), lambda n: (0, 0))],
        out_shape=[jax.ShapeDtypeStruct((p, dm), F32), jax.ShapeDtypeStruct((p, dm), BF16), jax.ShapeDtypeStruct((1, dm), F32)],
        compiler_params=_params("arbitrary"),
    )(h, target)


def _chunk_masks():
    ri, ci = _iota((TB, TB), 0), _iota((TB, TB), 1)
    same = (ri >> 4) == (ci >> 4)
    causal = same & (ci <= ri)
    lower = jnp.where(causal, 1.0, 0.0).astype(BF16)
    upper = jnp.where(same & (ci >= ri), 1.0, 0.0).astype(BF16)
    ones = jnp.where(same, 1.0, 0.0).astype(BF16)
    return causal, lower, upper, ones


def _hgrn_gates(q, z, lbh, m):
    sig = _sigmoid(z)
    f = lbh + (1.0 - lbh) * sig
    lf = jnp.log(jnp.maximum(f, LOG_FLOOR)) * m
    kk = (1.0 - lbh) * (1.0 - sig) * m
    sq = _sigmoid(q)
    return sig, f, lf, kk, sq, q * sq


def _hgrn_fwd(proj, lb, *, name, carry=None):
    p = proj.shape[0]
    nb = p // TB

    def body(qa_ref, fa_ref, ia_ref, lb_ref, oa_ref, sck_ref, st_ref):
        n = pl.program_id(0)

        @pl.when(n == 0)
        def _():
            st_ref[...] = jnp.zeros_like(st_ref)

        causal, lower, _, ones = _chunk_masks()
        m = ((n * TB + _iota((TB, 1), 0)) >= PAD_FRONT).astype(F32)
        heads = range(HA)
        sls = [slice(hd * DK_A, (hd + 1) * DK_A) for hd in heads]
        rows = [slice(c * CH, (c + 1) * CH) for c in range(NCH)]
        gates = [_hgrn_gates(qa_ref[:, sl], fa_ref[:, sl], lb_ref[:, sl], m) for sl in sls]
        lf = [t[2] for t in gates]
        g = [_xdot(lower, x) for x in lf]
        gl = [_xdot(ones, x) for x in lf]
        qd = [(gates[hd][5] * jnp.exp(g[hd])).astype(BF16) for hd in heads]
        kt = [(gates[hd][3] * jnp.exp(-g[hd])).astype(BF16) for hd in heads]
        kd = [(gates[hd][3] * jnp.exp(gl[hd] - g[hd])).astype(BF16) for hd in heads]
        vb = [ia_ref[:, sl].astype(BF16) for sl in sls]
        a_all = [jnp.exp(x) for x in gl]
        att = [jnp.where(causal, _dot(qd[hd], kt[hd], NT), 0.0).astype(BF16) for hd in heads]
        kv = [[_dot(vb[hd][r], kd[hd][r], TN) for r in rows] for hd in heads]
        o = [_dot(att[hd], vb[hd], NN) for hd in heads]
        before = []
        for hd in heads:
            st = st_ref[hd]
            sck_ref[0, hd] = st
            per_chunk = []
            for c in range(NCH):
                per_chunk.append(st.astype(BF16))
                st = st * a_all[hd][c * CH:c * CH + 1, :] + kv[hd][c]
            st_ref[hd] = st
            before.append(per_chunk)
        inter = [[_dot(qd[hd][rows[c]], before[hd][c], NT) for c in range(NCH)] for hd in heads]
        for hd in heads:
            oa_ref[:, sls[hd]] = o[hd] + jnp.concatenate(inter[hd], axis=0)

    blk = lambda c: pl.BlockSpec((TB, 512), lambda n, c=c: (n, c))
    return _call(
        body, name=name, grid=(nb,),
        in_specs=[blk(C_QA), blk(C_FA), blk(C_IA), pl.BlockSpec((1, 512), lambda n: (0, 0))],
        out_specs=[pl.BlockSpec((TB, 512), lambda n: (n, 0)), pl.BlockSpec((1, HA, TB, TB), lambda n: (n, 0, 0, 0))],
        out_shape=[jax.ShapeDtypeStruct((p, 512), F32), jax.ShapeDtypeStruct((nb, HA, TB, TB), F32)],
        scratch_shapes=[pltpu.VMEM((HA, TB, TB), F32)],
        args=(proj, proj, proj, lb), sem=("arbitrary",), carry=carry)


def _hgrn_bwd(proj, lb, sck, d_oa, *, name, carry=None):
    p = proj.shape[0]
    nb = p // TB

    def body(qa_ref, fa_ref, ia_ref, lb_ref, sck_ref, do_ref, dq_ref, dz_ref, dv_ref, dlb_ref, dst_ref):
        i = pl.program_id(0)
        n = nb - 1 - i

        @pl.when(i == 0)
        def _():
            dst_ref[...] = jnp.zeros_like(dst_ref)
            dlb_ref[...] = jnp.zeros_like(dlb_ref)

        causal, lower, upper, ones = _chunk_masks()
        m = ((n * TB + _iota((TB, 1), 0)) >= PAD_FRONT).astype(F32)
        heads = range(HA)
        sls = [slice(hd * DK_A, (hd + 1) * DK_A) for hd in heads]
        rows = [slice(c * CH, (c + 1) * CH) for c in range(NCH)]
        a_row = lambda a, c: a[c * CH:c * CH + 1, :]
        gates = [_hgrn_gates(qa_ref[:, sl], fa_ref[:, sl], lb_ref[:, sl], m) for sl in sls]
        g = [_xdot(lower, t[2]) for t in gates]
        gl = [_xdot(ones, t[2]) for t in gates]
        e_g = [jnp.exp(x) for x in g]
        e_ng = [jnp.exp(-x) for x in g]
        e_d = [jnp.exp(gl[hd] - g[hd]) for hd in heads]
        a_all = [jnp.exp(x) for x in gl]
        qd_f = [gates[hd][5] * e_g[hd] for hd in heads]
        kt_f = [gates[hd][3] * e_ng[hd] for hd in heads]
        kd_f = [gates[hd][3] * e_d[hd] for hd in heads]
        qd, kt, kd = ([x.astype(BF16) for x in xs] for xs in (qd_f, kt_f, kd_f))
        vb = [ia_ref[:, sl].astype(BF16) for sl in sls]
        dob = [do_ref[:, sl].astype(BF16) for sl in sls]
        att = [jnp.where(causal, _dot(qd[hd], kt[hd], NT), 0.0).astype(BF16) for hd in heads]
        d_att = [jnp.where(causal, _dot(dob[hd], vb[hd], NT), 0.0).astype(BF16) for hd in heads]
        kv = [[_dot(vb[hd][r], kd[hd][r], TN) for r in rows] for hd in heads]
        dqk = [[_dot(dob[hd][r], qd[hd][r], TN) for r in rows] for hd in heads]
        d_v = [_dot(att[hd], dob[hd], TN) for hd in heads]
        d_qd = [_dot(d_att[hd], kt[hd], NN) for hd in heads]
        d_kt = [_dot(d_att[hd], qd[hd], TN) for hd in heads]
        stc, dsc = [], []
        for hd in heads:
            st, before = sck_ref[0, hd], []
            for c in range(NCH):
                before.append(st)
                if c + 1 < NCH:
                    st = st * a_row(a_all[hd], c) + kv[hd][c]
            dst, after = dst_ref[hd], [None] * NCH
            for c in range(NCH - 1, -1, -1):
                after[c] = dst
                dst = dst * a_row(a_all[hd], c) + dqk[hd][c]
            dst_ref[hd] = dst
            stc.append(before)
            dsc.append(after)
        dscb = [[x.astype(BF16) for x in dsc[hd]] for hd in heads]
        dvs = [[_dot(kd[hd][rows[c]], dscb[hd][c], NT) for c in range(NCH)] for hd in heads]
        dkd = [[_dot(vb[hd][rows[c]], dscb[hd][c], NN) for c in range(NCH)] for hd in heads]
        dqd = [[_dot(dob[hd][rows[c]], stc[hd][c].astype(BF16), NN) for c in range(NCH)] for hd in heads]
        dgl = [[jnp.broadcast_to(jnp.sum(dsc[hd][c] * stc[hd][c], axis=0, keepdims=True) * a_row(a_all[hd], c), (CH, TB))
                for c in range(NCH)] for hd in heads]
        d_qd = [d_qd[hd] + jnp.concatenate(dqd[hd], axis=0) for hd in heads]
        d_kd = [jnp.concatenate(dkd[hd], axis=0) for hd in heads]
        kd_term = [d_kd[hd] * kd_f[hd] for hd in heads]
        d_g = [d_qd[hd] * qd_f[hd] - d_kt[hd] * kt_f[hd] - kd_term[hd] for hd in heads]
        d_lf = [_xdot(upper, d_g[hd]) + _xdot(ones, kd_term[hd]) + jnp.concatenate(dgl[hd], axis=0) for hd in heads]
        for hd in heads:
            sl = sls[hd]
            sig, f, _, _, sq, _ = gates[hd]
            q, lbh = qa_ref[:, sl], lb_ref[:, sl]
            d_kk = (d_kt[hd] * e_ng[hd] + d_kd[hd] * e_d[hd]) * m
            t1 = d_lf[hd] * m * jnp.where(f > LOG_FLOOR, 1.0 / f, 0.0)
            dq_ref[:, sl] = (d_qd[hd] * e_g[hd] * (sq * (1.0 + q * (1.0 - sq)))).astype(BF16)
            dz_ref[:, sl] = ((t1 - d_kk) * (1.0 - lbh) * sig * (1.0 - sig)).astype(BF16)
            dv_ref[:, sl] = (d_v[hd] + jnp.concatenate(dvs[hd], axis=0)).astype(BF16)
            dlb_ref[:, sl] += jnp.sum((t1 - d_kk) * (1.0 - sig), axis=0, keepdims=True)

    blk = lambda c: pl.BlockSpec((TB, 512), lambda i, c=c: (nb - 1 - i, c))
    out_blk = pl.BlockSpec((TB, 512), lambda i: (nb - 1 - i, 0))
    vec = pl.BlockSpec((1, 512), lambda i: (0, 0))
    return _call(
        body, name=name, grid=(nb,),
        in_specs=[blk(C_QA), blk(C_FA), blk(C_IA), vec,
                  pl.BlockSpec((1, HA, TB, TB), lambda i: (nb - 1 - i, 0, 0, 0)), out_blk],
        out_specs=[out_blk, out_blk, out_blk, vec],
        out_shape=[jax.ShapeDtypeStruct((p, 512), BF16)] * 3 + [jax.ShapeDtypeStruct((1, 512), F32)],
        scratch_shapes=[pltpu.VMEM((HA, TB, TB), F32)],
        args=(proj, proj, proj, lb, sck, d_oa), sem=("arbitrary",), carry=carry)


def _lane():
    return _iota((1, TB), 1)


def _swap_halves(y):
    first = (_lane() & 63) < 32
    return jnp.where(first, pltpu.roll(y, 96, 1), pltpu.roll(y, 32, 1))


def _head_ones():
    ri, ci = _iota((TB, TB), 0), _iota((TB, TB), 1)
    return jnp.where((ri >> 6) == (ci >> 6), 1.0, 0.0).astype(BF16)


def _norm_rope(x, g, cos, sin, bd):
    r = lax.rsqrt(_xdot_r(x * x, bd) * (1.0 / DH) + RMS_EPS)
    y = x * r * g
    return y * cos + _swap_halves(y) * sin


def _norm_rope_bwd(d_out, x, g, cos, sin, bd):
    d = d_out * cos - _swap_halves(d_out) * sin
    r = lax.rsqrt(_xdot_r(x * x, bd) * (1.0 / DH) + RMS_EPS)
    xh = x * r
    dyn = d * g
    dx = r * (dyn - xh * (_xdot_r(dyn * xh, bd) * (1.0 / DH)))
    return dx, jnp.sum(d * xh, axis=0, keepdims=True)


def _dup_heads(k):
    first = _lane() < DH
    r = pltpu.roll(k, DH, 1)
    return jnp.where(first, k, r), jnp.where(first, r, k)


def _qk_prep(proj, gq, gk, cos, sin, *, name):
    p = proj.shape[0]

    def body(qb_ref, kb_ref, vb_ref, gq_ref, gk_ref, cos_ref, sin_ref, qh_ref, k2_ref, v2_ref):
        bd = _head_ones()
        cos_v, sin_v = cos_ref[...], sin_ref[...]
        for j in range(HB // 2):
            sl = slice(j * TB, (j + 1) * TB)
            qh_ref[:, sl] = _norm_rope(qb_ref[:, sl], gq_ref[...], cos_v, sin_v, bd).astype(BF16)
        k0, k1 = _dup_heads(_norm_rope(kb_ref[...], gk_ref[...], cos_v, sin_v, bd))
        k2_ref[:, 0:TB] = k0.astype(BF16)
        k2_ref[:, TB:2 * TB] = k1.astype(BF16)
        v0, v1 = _dup_heads(vb_ref[...])
        v2_ref[:, 0:TB] = v0.astype(BF16)
        v2_ref[:, TB:2 * TB] = v1.astype(BF16)

    vec = pl.BlockSpec((1, TB), lambda n: (0, 0))
    tab = pl.BlockSpec((TB, TB), lambda n: (n, 0))
    return pl.pallas_call(
        body, name=name, grid=(p // TB,),
        in_specs=[pl.BlockSpec((TB, 1024), lambda n: (n, C_QB)), pl.BlockSpec((TB, TB), lambda n: (n, C_KB)),
                  pl.BlockSpec((TB, TB), lambda n: (n, C_VB)), vec, vec, tab, tab],
        out_specs=[pl.BlockSpec((TB, 1024), lambda n: (n, 0)), pl.BlockSpec((TB, 256), lambda n: (n, 0)),
                   pl.BlockSpec((TB, 256), lambda n: (n, 0))],
        out_shape=[jax.ShapeDtypeStruct((p, 1024), BF16), jax.ShapeDtypeStruct((p, 256), BF16),
                   jax.ShapeDtypeStruct((p, 256), BF16)],
        compiler_params=_params("parallel"),
    )(proj, proj, proj, gq, gk, cos, sin)


NKEY = N_META + 2 * TB


def _attn_mask(n):
    r = _iota((TB, NKEY), 0)
    j = _iota((TB, NKEY), 1)
    meta = (j < N_META) & ((n >= 1) | (j + PAD_FRONT <= r))
    prev = (j >= N_META) & (j < N_META + TB) & (n >= 2) & (j - N_META > r)
    cur = (j >= N_META + TB) & (n >= 1) & (j - (N_META + TB) <= r)
    return meta | prev | cur


def _attn_specs():
    cur = lambda w: pl.BlockSpec((TB, w), lambda n: (n, 0))
    prev = pl.BlockSpec((TB, 256), lambda n: (jnp.maximum(n - 1, 0), 0))
    meta = pl.BlockSpec((N_META, 256), lambda n: (PAD_FRONT // N_META, 0))
    sink = pl.BlockSpec(memory_space=pltpu.SMEM)
    return cur, prev, meta, sink


ATTN_GROUP = 4


def _head_queries(q_ref, kv, first):
    out = []
    for jj in range(GRP // 2):
        j = kv * (GRP // 2) + jj
        qj = q_ref[:, j * TB:(j + 1) * TB] * (DH ** -0.5)
        for half in range(2):
            out.append((j, half, jnp.where(first if half == 0 else ~first, qj, jnp.zeros_like(qj))))
    return out


def _attn_fwd(qh, k2, v2, sinks, *, name, carry=None):
    p = qh.shape[0]

    def body(sink_ref, q_ref, kc_ref, kp_ref, km_ref, vc_ref, vp_ref, vm_ref, o_ref, lse_ref):
        n = pl.program_id(0)
        mask = _attn_mask(n)
        lane = _lane()
        first = lane < DH
        lse_tile = jnp.zeros((TB, TB), F32)
        for kv in range(KVH):
            ks = slice(kv * TB, (kv + 1) * TB)
            kall = jnp.concatenate([km_ref[:, ks], kp_ref[:, ks], kc_ref[:, ks]], axis=0)
            vall = jnp.concatenate([vm_ref[:, ks], vp_ref[:, ks], vc_ref[:, ks]], axis=0)
            hq = _head_queries(q_ref, kv, first)
            for g0 in range(0, GRP, ATTN_GROUP):
                grp = hq[g0:g0 + ATTN_GROUP]
                idx = range(len(grp))
                s = [jnp.where(mask, _dot(qm, kall, NT), NEG_INF) for _, _, qm in grp]
                sink = [sink_ref[2 * j + half] for j, half, _ in grp]
                mx = [jnp.maximum(jnp.max(s[i], axis=1, keepdims=True), sink[i]) for i in idx]
                pr = [jnp.exp(s[i] - mx[i]) for i in idx]
                den = [jnp.sum(pr[i], axis=1, keepdims=True) + jnp.exp(sink[i] - mx[i]) for i in idx]
                o = [_dot(pr[i].astype(BF16), vall, NN) * (1.0 / den[i]) for i in idx]
                for i, (j, half, _) in enumerate(grp):
                    lse_tile = lse_tile + jnp.where(lane == 2 * j + half, mx[i] + jnp.log(den[i]), 0.0)
                for i in range(0, len(grp), 2):
                    j = grp[i][0]
                    o_ref[:, j * TB:(j + 1) * TB] = jnp.where(first, o[i], o[i + 1])
        lse_ref[...] = lse_tile

    cur, prev, meta, sink = _attn_specs()
    return _call(
        body, name=name, grid=(p // TB,),
        in_specs=[sink, cur(1024), cur(256), prev, meta, cur(256), prev, meta],
        out_specs=[cur(1024), cur(TB)],
        out_shape=[jax.ShapeDtypeStruct((p, 1024), F32), jax.ShapeDtypeStruct((p, TB), F32)],
        args=(sinks, qh, k2, k2, k2, v2, v2, v2), sem=("parallel",), carry=carry)


def _attn_bwd(qh, k2, v2, sinks, o, lse, d_o, *, name, carry=None):
    p = qh.shape[0]

    def body(sink_ref, q_ref, kc_ref, kp_ref, km_ref, vc_ref, vp_ref, vm_ref, o_ref, lse_ref, do_ref,
             dq_ref, dkc_ref, dkp_ref, dvc_ref, dvp_ref, dkm_ref, dvm_ref, dsink_ref):
        n = pl.program_id(0)

        @pl.when(n == 0)
        def _():
            dkm_ref[...] = jnp.zeros_like(dkm_ref)
            dvm_ref[...] = jnp.zeros_like(dvm_ref)
            dsink_ref[...] = jnp.zeros_like(dsink_ref)

        mask = _attn_mask(n)
        lane = _lane()
        first = lane < DH
        lse_tile = lse_ref[...]
        sink_acc = jnp.zeros((TB, TB), F32)
        for kv in range(KVH):
            ks = slice(kv * TB, (kv + 1) * TB)
            kall = jnp.concatenate([km_ref[:, ks], kp_ref[:, ks], kc_ref[:, ks]], axis=0)
            vall = jnp.concatenate([vm_ref[:, ks], vp_ref[:, ks], vc_ref[:, ks]], axis=0)
            d_kall = jnp.zeros((NKEY, TB), F32)
            d_vall = jnp.zeros((NKEY, TB), F32)
            hq = _head_queries(q_ref, kv, first)
            for g0 in range(0, GRP, ATTN_GROUP):
                grp = hq[g0:g0 + ATTN_GROUP]
                idx = range(len(grp))
                s = [jnp.where(mask, _dot(qm, kall, NT), NEG_INF) for _, _, qm in grp]
                dom = [jnp.where(first if half == 0 else ~first, do_ref[:, j * TB:(j + 1) * TB], 0.0) for j, half, _ in grp]
                domb = [x.astype(BF16) for x in dom]
                d_w = [_dot(x, vall, NT) for x in domb]
                delta = [jnp.sum(dom[i] * o_ref[:, grp[i][0] * TB:(grp[i][0] + 1) * TB], axis=1, keepdims=True) for i in idx]
                lse_h = [jnp.sum(jnp.where(lane == 2 * j + half, lse_tile, 0.0), axis=1, keepdims=True) for j, half, _ in grp]
                w = [jnp.exp(s[i] - lse_h[i]) for i in idx]
                for i, (j, half, _) in enumerate(grp):
                    w_sink = jnp.exp(sink_ref[2 * j + half] - lse_h[i])
                    sink_acc = sink_acc + jnp.where(lane == 2 * j + half, -(w_sink * delta[i]), 0.0)
                dsb = [(w[i] * (d_w[i] - delta[i])).astype(BF16) for i in idx]
                d_q = [_dot(x, kall, NN) * (DH ** -0.5) for x in dsb]
                d_k = [_dot(dsb[i], grp[i][2], TN) for i in idx]
                d_v = [_dot(w[i].astype(BF16), domb[i], TN) for i in idx]
                for i in idx:
                    d_kall = d_kall + d_k[i]
                    d_vall = d_vall + d_v[i]
                for i in range(0, len(grp), 2):
                    j = grp[i][0]
                    dq_ref[:, j * TB:(j + 1) * TB] = jnp.where(first, d_q[i], d_q[i + 1])
            dkm_ref[:, ks] += d_kall[0:N_META]
            dkp_ref[:, ks] = d_kall[N_META:N_META + TB]
            dkc_ref[:, ks] = d_kall[N_META + TB:NKEY]
            dvm_ref[:, ks] += d_vall[0:N_META]
            dvp_ref[:, ks] = d_vall[N_META:N_META + TB]
            dvc_ref[:, ks] = d_vall[N_META + TB:NKEY]
        dsink_ref[...] += jnp.sum(sink_acc, axis=0, keepdims=True)

    cur, prev, meta, sink = _attn_specs()
    acc = lambda r: pl.BlockSpec((r, 256), lambda n: (0, 0))
    return _call(
        body, name=name, grid=(p // TB,),
        in_specs=[sink, cur(1024), cur(256), prev, meta, cur(256), prev, meta, cur(1024), cur(TB), cur(1024)],
        out_specs=[cur(1024), cur(256), cur(256), cur(256), cur(256), acc(N_META), acc(N_META),
                   pl.BlockSpec((1, TB), lambda n: (0, 0))],
        out_shape=[jax.ShapeDtypeStruct((p, 1024), F32)] + [jax.ShapeDtypeStruct((p, 256), F32)] * 4
        + [jax.ShapeDtypeStruct((N_META, 256), F32)] * 2 + [jax.ShapeDtypeStruct((1, TB), F32)],
        args=(sinks, qh, k2, k2, k2, v2, v2, v2, o, lse, d_o), sem=("arbitrary",), carry=carry)


def _qk_post(proj, gq, gk, cos, sin, dqh, dkc, dkp, dkm, dvc, dvp, dvm, *, name):
    p = proj.shape[0]
    nb = p // TB

    def body(qb_ref, kb_ref, gq_ref, gk_ref, cos_ref, sin_ref, dqh_ref, dkc_ref, dkp_ref, dkm_ref,
             dvc_ref, dvp_ref, dvm_ref, dqb_ref, dkb_ref, dvb_ref, dgq_ref, dgk_ref, tk_ref, tv_ref):
        n = pl.program_id(0)

        @pl.when(n == 0)
        def _():
            dgq_ref[...] = jnp.zeros_like(dgq_ref)
            dgk_ref[...] = jnp.zeros_like(dgk_ref)

        keep = jnp.where(n == nb - 1, 0.0, 1.0)
        tk_ref[...] = dkc_ref[...] + keep * dkp_ref[...]
        tv_ref[...] = dvc_ref[...] + keep * dvp_ref[...]

        @pl.when(n == 0)
        def _():
            tk_ref[PAD_FRONT:TB, :] += dkm_ref[...]
            tv_ref[PAD_FRONT:TB, :] += dvm_ref[...]

        first = _lane() < DH

        def fold(t_ref):
            t0, t1 = t_ref[:, 0:TB], t_ref[:, TB:2 * TB]
            return jnp.where(first, t0 + pltpu.roll(t0, DH, 1), t1 + pltpu.roll(t1, DH, 1))

        bd = _head_ones()
        cos_v, sin_v = cos_ref[...], sin_ref[...]
        dvb_ref[...] = fold(tv_ref).astype(BF16)
        dkb, dgk = _norm_rope_bwd(fold(tk_ref), kb_ref[...], gk_ref[...], cos_v, sin_v, bd)
        dkb_ref[...] = dkb.astype(BF16)
        dgk_ref[...] += dgk
        dgq = jnp.zeros((1, TB), F32)
        for j in range(HB // 2):
            sl = slice(j * TB, (j + 1) * TB)
            dqb, dg = _norm_rope_bwd(dqh_ref[:, sl], qb_ref[:, sl], gq_ref[...], cos_v, sin_v, bd)
            dqb_ref[:, sl] = dqb.astype(BF16)
            dgq = dgq + dg
        dgq_ref[...] += dgq

    vec = pl.BlockSpec((1, TB), lambda n: (0, 0))
    tab = pl.BlockSpec((TB, TB), lambda n: (n, 0))
    cur = lambda w: pl.BlockSpec((TB, w), lambda n: (n, 0))
    nxt = pl.BlockSpec((TB, 256), lambda n: (jnp.minimum(n + 1, nb - 1), 0))
    meta = pl.BlockSpec((N_META, 256), lambda n: (0, 0))
    return pl.pallas_call(
        body, name=name, grid=(nb,),
        in_specs=[pl.BlockSpec((TB, 1024), lambda n: (n, C_QB)), pl.BlockSpec((TB, TB), lambda n: (n, C_KB)),
                  vec, vec, tab, tab, cur(1024), cur(256), nxt, meta, cur(256), nxt, meta],
        out_specs=[cur(1024), cur(TB), cur(TB), vec, vec],
        out_shape=[jax.ShapeDtypeStruct((p, 1024), BF16), jax.ShapeDtypeStruct((p, TB), BF16),
                   jax.ShapeDtypeStruct((p, TB), BF16), jax.ShapeDtypeStruct((1, TB), F32),
                   jax.ShapeDtypeStruct((1, TB), F32)],
        scratch_shapes=[pltpu.VMEM((TB, 256), F32), pltpu.VMEM((TB, 256), F32)],
        compiler_params=_params("arbitrary"),
    )(proj, proj, gq, gk, cos, sin, dqh, dkc, dkp, dkm, dvc, dvp, dvm)


EXT = TB + N_META


def _pool_count_inv(n, w):
    t = n * TB + _iota((TB, 1), 0)
    cnt = jnp.clip(t - (PAD_FRONT - 1), 1, w)
    return 1.0 / cnt.astype(F32)


def _silu_parts(gate):
    s = _sigmoid(gate)
    return gate * s, s * (1.0 + gate * (1.0 - s))


def _mix_fwd(proj, oa, yb, hg, pool_w, pool_scale, *, name):
    p = proj.shape[0]

    def body(ga_ref, gb_ref, uc_ref, up_ref, gc_ref, oa_ref, yb_ref, hg_ref, pw_ref, ps_ref, mx_ref, pooled_ref):
        n = pl.program_id(0)
        valid = ((n * TB + _iota((TB, 1), 0)) >= PAD_FRONT).astype(F32)
        for hd in range(HA):
            sl = slice(hd * TB, (hd + 1) * TB)
            o = oa_ref[:, sl]
            r = lax.rsqrt(jnp.mean(o * o, axis=-1, keepdims=True) + RMS_EPS)
            act, _ = _silu_parts(ga_ref[:, sl])
            mx_ref[:, sl] = (o * r * hg_ref[...] * act).astype(BF16)
        for j in range(HB // 2):
            sl = slice(j * TB, (j + 1) * TB)
            act, _ = _silu_parts(gb_ref[:, sl])
            mx_ref[:, 512 + j * TB:512 + (j + 1) * TB] = (yb_ref[:, sl] * act).astype(BF16)
        ri, ci = _iota((TB, EXT), 0), _iota((TB, EXT), 1)
        has_prev = jnp.where(n == 0, 0.0, 1.0)
        for gi, w in enumerate(POOL_WINDOWS):
            sl = slice(gi * TB, (gi + 1) * TB)
            ug = uc_ref[:, sl] * valid
            ext = jnp.concatenate([up_ref[:, sl] * has_prev, ug], axis=0)
            band = jnp.where((ci <= ri + N_META) & (ci > ri + N_META - w), 1.0, 0.0).astype(BF16)
            pooled = (_xdot(band, ext) * _pool_count_inv(n, w) - ug) * valid
            pooled_ref[:, sl] = pooled
            yc = _dot(pooled.astype(BF16), pw_ref[gi], NN) * ps_ref[:, sl]
            act, _ = _silu_parts(gc_ref[:, sl])
            mx_ref[:, 1536 + gi * TB:1536 + (gi + 1) * TB] = (yc * act).astype(BF16)

    cur = lambda w, c=0: pl.BlockSpec((TB, w), lambda n, c=c: (n, c))
    prev16 = pl.BlockSpec((N_META, 512), lambda n: (jnp.maximum(n * (TB // N_META) - 1, 0), C_UC))
    return pl.pallas_call(
        body, name=name, grid=(p // TB,),
        in_specs=[cur(512, C_GA), cur(1024, C_GB), cur(512, C_UC), prev16, cur(512, C_GC), cur(512), cur(1024),
                  pl.BlockSpec((1, TB), lambda n: (0, 0)), pl.BlockSpec((4, TB, TB), lambda n: (0, 0, 0)),
                  pl.BlockSpec((1, 512), lambda n: (0, 0))],
        out_specs=[cur(MIX), cur(512)],
        out_shape=[jax.ShapeDtypeStruct((p, MIX), BF16), jax.ShapeDtypeStruct((p, 512), F32)],
        compiler_params=_params("parallel"),
    )(proj, proj, proj, proj, proj, oa, yb, hg, pool_w, pool_scale)


def _mix_bwd(proj, oa, yb, pooled, hg, pool_w, pool_scale, d_mixed, *, name, carry=None):
    p = proj.shape[0]

    def body(ga_ref, gb_ref, gc_ref, oa_ref, yb_ref, pooled_ref, hg_ref, pw_ref, ps_ref, dm_ref,
             doa_ref, dyb_ref, dga_ref, dgb_ref, dgc_ref, dp_ref, dhg_ref, dpw_ref, dps_ref):
        n = pl.program_id(0)

        @pl.when(n == 0)
        def _():
            dhg_ref[...] = jnp.zeros_like(dhg_ref)
            dpw_ref[...] = jnp.zeros_like(dpw_ref)
            dps_ref[...] = jnp.zeros_like(dps_ref)

        valid = ((n * TB + _iota((TB, 1), 0)) >= PAD_FRONT).astype(F32)
        dhg = jnp.zeros((1, TB), F32)
        for hd in range(HA):
            sl = slice(hd * TB, (hd + 1) * TB)
            o = oa_ref[:, sl]
            r = lax.rsqrt(jnp.mean(o * o, axis=-1, keepdims=True) + RMS_EPS)
            on = o * r
            gate = ga_ref[:, sl]
            act, dact = _silu_parts(gate)
            dmx = dm_ref[:, sl]
            d_ya = dmx * act
            dga_ref[:, sl] = (dmx * on * hg_ref[...] * dact).astype(BF16)
            dyn = d_ya * hg_ref[...]
            doa_ref[:, sl] = r * (dyn - on * jnp.mean(dyn * on, axis=-1, keepdims=True))
            dhg = dhg + jnp.sum(d_ya * on, axis=0, keepdims=True)
        dhg_ref[...] += dhg
        for j in range(HB // 2):
            sl = slice(j * TB, (j + 1) * TB)
            act, dact = _silu_parts(gb_ref[:, sl])
            dmx = dm_ref[:, 512 + j * TB:512 + (j + 1) * TB]
            dyb_ref[:, sl] = dmx * act
            dgb_ref[:, sl] = (dmx * yb_ref[:, sl] * dact).astype(BF16)
        for gi in range(len(POOL_WINDOWS)):
            sl = slice(gi * TB, (gi + 1) * TB)
            pooled = pooled_ref[:, sl]
            pooled_b = pooled.astype(BF16)
            t = _dot(pooled_b, pw_ref[gi], NN)
            act, dact = _silu_parts(gc_ref[:, sl])
            dmx = dm_ref[:, 1536 + gi * TB:1536 + (gi + 1) * TB]
            d_yc = dmx * act
            dgc_ref[:, sl] = (dmx * t * ps_ref[:, sl] * dact).astype(BF16)
            dps_ref[:, sl] += jnp.sum(d_yc * t, axis=0, keepdims=True)
            d_t = (d_yc * ps_ref[:, sl]).astype(BF16)
            dp_ref[:, sl] = _dot(d_t, pw_ref[gi], NT) * valid
            dpw_ref[gi] += _dot(pooled_b, d_t, TN)

    cur = lambda w, c=0: pl.BlockSpec((TB, w), lambda n, c=c: (n, c))
    return _call(
        body, name=name, grid=(p // TB,),
        in_specs=[cur(512, C_GA), cur(1024, C_GB), cur(512, C_GC), cur(512), cur(1024), cur(512),
                  pl.BlockSpec((1, TB), lambda n: (0, 0)), pl.BlockSpec((4, TB, TB), lambda n: (0, 0, 0)),
                  pl.BlockSpec((1, 512), lambda n: (0, 0)), cur(MIX)],
        out_specs=[cur(512), cur(1024), cur(512), cur(1024), cur(512), cur(512),
                   pl.BlockSpec((1, TB), lambda n: (0, 0)), pl.BlockSpec((4, TB, TB), lambda n: (0, 0, 0)),
                   pl.BlockSpec((1, 512), lambda n: (0, 0))],
        out_shape=[jax.ShapeDtypeStruct((p, 512), F32), jax.ShapeDtypeStruct((p, 1024), F32),
                   jax.ShapeDtypeStruct((p, 512), BF16), jax.ShapeDtypeStruct((p, 1024), BF16),
                   jax.ShapeDtypeStruct((p, 512), BF16), jax.ShapeDtypeStruct((p, 512), F32),
                   jax.ShapeDtypeStruct((1, TB), F32), jax.ShapeDtypeStruct((4, TB, TB), F32),
                   jax.ShapeDtypeStruct((1, 512), F32)],
        args=(proj, proj, proj, oa, yb, pooled, hg, pool_w, pool_scale, d_mixed), sem=("arbitrary",), carry=carry)


def _pool_bwd(dp, *, name):
    p = dp.shape[0]
    nb = p // TB

    def body(dp_ref, dn_ref, duc_ref):
        n = pl.program_id(0)
        valid = ((n * TB + _iota((TB, 1), 0)) >= PAD_FRONT).astype(F32)
        has_next = jnp.where(n == nb - 1, 0.0, 1.0)
        ri, ci = _iota((TB, EXT), 0), _iota((TB, EXT), 1)
        for gi, w in enumerate(POOL_WINDOWS):
            sl = slice(gi * TB, (gi + 1) * TB)
            d_p = dp_ref[:, sl]
            ext = jnp.concatenate([d_p * _pool_count_inv(n, w), dn_ref[:, sl] * (has_next / w)], axis=0)
            band = jnp.where((ci >= ri) & (ci < ri + w), 1.0, 0.0).astype(BF16)
            duc_ref[:, sl] = ((_xdot(band, ext) - d_p) * valid).astype(BF16)

    return pl.pallas_call(
        body, name=name, grid=(nb,),
        in_specs=[pl.BlockSpec((TB, 512), lambda n: (n, 0)),
                  pl.BlockSpec((N_META, 512), lambda n: (jnp.minimum(n + 1, nb - 1) * (TB // N_META), 0))],
        out_specs=pl.BlockSpec((TB, 512), lambda n: (n, 0)),
        out_shape=jax.ShapeDtypeStruct((p, 512), BF16),
        compiler_params=_params("parallel"),
    )(dp, dp)


def _carried(carries, key, local, fn, *args, **kw):
    if key not in carries:
        return fn(*args, **kw)
    make_src, gather, done = carries[key]
    *outs, stack = fn(*args, carry=(make_src(local), gather), **kw)
    done(stack)
    return outs


def _layer_fwd(h, w, tag, carries):
    xn, = _carried(carries, "rmsnorm_fwd", None, _rmsnorm_fwd, h, w["norm_g"], name=f"rmsnorm_fwd{tag}")
    proj, = _carried(carries, "in_proj", None, _mm_nn, xn, w["w_in"], name=f"in_proj{tag}")
    oa, sck = _carried(carries, "hgrn_fwd", None, _hgrn_fwd, proj, w["lb"], name=f"hgrn_fwd{tag}")
    qh, k2, v2 = _qk_prep(proj, w["gq"], w["gk"], w["cos"], w["sin"], name=f"qk_prep{tag}")
    yb, lse = _carried(carries, "attn_fwd", None, _attn_fwd, qh, k2, v2, w["sinks"], name=f"attn_fwd{tag}")
    mixed, pooled = _mix_fwd(proj, oa, yb, w["hg"], w["pool_w"], w["pool_scale"], name=f"mix_fwd{tag}")
    h_next, = _mm_nn(mixed, w["w_out"], h, name=f"out_proj{tag}", tn=512)
    saved = dict(h=h, xn=xn, proj=proj, oa=oa, sck=sck, qh=qh, k2=k2, v2=v2, yb=yb, lse=lse, mixed=mixed, pooled=pooled)
    return h_next, saved


def _layer_bwd(dh_out, dhb, s, w, tag, carries, first_layer=False):
    g = {}
    d_mixed, = _carried(carries, "d_mixed", g, _mm_nt, dhb, w["w_out"], name=f"d_mixed{tag}", tm=1664, tn=512)
    g["w_out"], = _mm_tn(s["mixed"], dhb, name=f"dw_out{tag}", tn=512)
    d_oa, d_yb, d_ga, d_gb, d_gc, d_p, g["hg"], g["pool_w"], g["pool_scale"] = _carried(
        carries, "mix_bwd", g, _mix_bwd,
        s["proj"], s["oa"], s["yb"], s["pooled"], w["hg"], w["pool_w"], w["pool_scale"], d_mixed, name=f"mix_bwd{tag}")
    d_uc = _pool_bwd(d_p, name=f"pool_bwd{tag}")
    d_qh, dkc, dkp, dvc, dvp, dkm, dvm, g["sinks"] = _carried(
        carries, "attn_bwd", g, _attn_bwd, s["qh"], s["k2"], s["v2"], w["sinks"], s["yb"], s["lse"], d_yb,
        name=f"attn_bwd{tag}")
    d_qb, d_kb, d_vb, g["gq"], g["gk"] = _qk_post(s["proj"], w["gq"], w["gk"], w["cos"], w["sin"], d_qh, dkc, dkp, dkm,
                                                  dvc, dvp, dvm, name=f"qk_post{tag}")
    d_qa, d_fa, d_ia, g["lb"] = _carried(carries, "hgrn_bwd", g, _hgrn_bwd, s["proj"], w["lb"], s["sck"], d_oa,
                                         name=f"hgrn_bwd{tag}")
    d_proj = jnp.concatenate([d_qa, d_fa, d_ia, d_ga, d_qb, d_gb, d_uc, d_gc, d_kb, d_vb], axis=1)
    g["w_in_parts"] = []
    for i in range(W_IN_PARTS):
        part, = _carried(carries, f"dw_in_{i}", g, _mm_tn, s["xn"], d_proj, name=f"dw_in_{i}{tag}",
                         m_part=(i, W_IN_PARTS))
        g["w_in_parts"].append(part)
    d_xn, = _carried(carries, "d_xn", g, _mm_nt, d_proj, w["w_in"], name=f"d_xn{tag}", tk=PROJ_COLS)
    out, aux, g["norm_g"] = _rmsnorm_bwd(d_xn, s["h"], w["norm_g"], dh_out, name=f"rmsnorm_bwd{tag}", first_layer=first_layer)
    return out, aux, g


def _peers():
    x, y, c = lax.axis_index("x"), lax.axis_index("y"), lax.axis_index("c")
    out = []
    for k in range(1, N_DEV):
        kx, ky, kc = (k >> 2) & 1, (k >> 1) & 1, k & 1
        px, py, pc = x ^ kx, y ^ ky, c ^ kc
        out.append(((px, py, pc), 4 * px + 2 * py + pc))
    return 4 * x + 2 * y + c, out


def _exchange_copies(src_ref, out_ref, send_sems, recv_sems, local_sem):
    me, peers = _peers()
    mine = pltpu.make_async_copy(src_ref.at[me], out_ref.at[me], local_sem)
    copies = []
    for k, (dev, idx) in enumerate(peers):
        copies.append(pltpu.make_async_remote_copy(
            src_ref=src_ref.at[idx], dst_ref=out_ref.at[me],
            send_sem=send_sems.at[k], recv_sem=recv_sems.at[k],
            device_id=dev, device_id_type=pl.DeviceIdType.MESH))
    return mine, copies


def _gather_copies(src_ref, out_ref, send_sems, recv_sems, local_sem):
    x, y, c = lax.axis_index("x"), lax.axis_index("y"), lax.axis_index("c")
    slot = lambda px, py, pc: out_ref.at[4 * px + 2 * py + pc]
    sibling = (x, y, 1 - c)
    chips = [(1 - x, y), (x, 1 - y), (1 - x, 1 - y)]

    def copy(k, src, block, to):
        return pltpu.make_async_remote_copy(src_ref=src, dst_ref=slot(*block), send_sem=send_sems.at[k],
                                            recv_sem=recv_sems.at[k], device_id=to, device_id_type=pl.DeviceIdType.MESH)

    mine = lambda: pltpu.make_async_copy(src_ref, slot(x, y, c), local_sem)
    own = lambda: ([copy(0, src_ref, (x, y, c), sibling)]
                   + [copy(1 + j, src_ref, (x, y, c), (*chip, c)) for j, chip in enumerate(chips)])
    passing = lambda: [copy(4 + j, slot(*chip, c), (*chip, c), sibling) for j, chip in enumerate(chips)]
    arrivals = lambda: ([copy(0, src_ref, sibling, sibling)]
                        + [copy(1 + j, src_ref, (*chip, c), sibling) for j, chip in enumerate(chips)]
                        + [copy(4 + j, src_ref, (*chip, 1 - c), sibling) for j, chip in enumerate(chips)])
    return mine, own, passing, arrivals


def _exchange_start(*refs, gather):
    if gather:
        mine, own, _, _ = _gather_copies(*refs)
        mine().start()
        for cp in own():
            cp.start()
        return
    mine, copies = _exchange_copies(*refs)
    mine.start()
    for cp in copies:
        cp.start()


def _exchange_wait(*refs, gather):
    if gather:
        mine, own, passing, arrivals = _gather_copies(*refs)
        passing, arrivals = passing(), arrivals()
        for j, cp in enumerate(passing):
            arrivals[1 + j].wait_recv()
            cp.start()
        arrivals[0].wait_recv()
        for cp in arrivals[4:]:
            cp.wait_recv()
        for cp in own() + passing:
            cp.wait_send()
        mine().wait()
        return
    mine, copies = _exchange_copies(*refs)
    for cp in copies:
        cp.wait_recv()
    for cp in copies:
        cp.wait_send()
    mine.wait()


def _exchange_scratch():
    return [pltpu.SemaphoreType.DMA((N_DEV - 1,)), pltpu.SemaphoreType.DMA((N_DEV - 1,)), pltpu.SemaphoreType.DMA]


def _exchange(src, *, gather, name):
    rows, cols = src.shape[-2:]

    def body(src_ref, out_ref, send_sems, recv_sems, local_sem):
        _exchange_start(src_ref, out_ref, send_sems, recv_sems, local_sem, gather=gather)
        _exchange_wait(src_ref, out_ref, send_sems, recv_sems, local_sem, gather=gather)

    return pl.pallas_call(
        body, name=name,
        in_specs=[pl.BlockSpec(memory_space=pl.ANY)], out_specs=pl.BlockSpec(memory_space=pl.ANY),
        out_shape=jax.ShapeDtypeStruct((N_DEV, rows, cols), src.dtype),
        scratch_shapes=_exchange_scratch(),
    )(src)


def _call(body, *, name, grid, in_specs, out_specs, out_shape, args, sem, scratch_shapes=(), carry=None):
    if carry is None:
        return pl.pallas_call(
            body, name=name, grid=grid, in_specs=list(in_specs), out_specs=list(out_specs), out_shape=list(out_shape),
            scratch_shapes=list(scratch_shapes), compiler_params=_params(*sem))(*args)
    src, gather = carry
    n_in, n_out, n_scr = len(in_specs), len(out_specs), len(scratch_shapes)
    rows, cols = src.shape[-2:]

    def carrying(*refs):
        ins, src_ref = refs[:n_in], refs[n_in]
        outs, dst_ref = refs[n_in + 1:n_in + 1 + n_out], refs[n_in + 1 + n_out]
        scr = refs[n_in + 2 + n_out:]
        exch = (src_ref, dst_ref) + tuple(scr[n_scr:])
        first, last = None, None
        for a, size in enumerate(grid):
            f, l = pl.program_id(a) == 0, pl.program_id(a) == size - 1
            first = f if first is None else first & f
            last = l if last is None else last & l

        @pl.when(first)
        def _():
            _exchange_start(*exch, gather=gather)

        body(*ins, *outs, *scr[:n_scr])

        @pl.when(last)
        def _():
            _exchange_wait(*exch, gather=gather)

    hbm = pl.BlockSpec(memory_space=pl.ANY)
    return pl.pallas_call(
        carrying, name=name, grid=grid, in_specs=list(in_specs) + [hbm], out_specs=list(out_specs) + [hbm],
        out_shape=list(out_shape) + [jax.ShapeDtypeStruct((N_DEV, rows, cols), src.dtype)],
        scratch_shapes=list(scratch_shapes) + _exchange_scratch(),
        compiler_params=_params(*(("arbitrary",) * len(grid))))(*args, src)


def _adamw(stacks, w, m, v, *, name, carry=None):
    nl = len(stacks)
    rows, cols = stacks[0].shape[1:]
    tr = rows
    stack_block_bytes = 8 * 1024 * 1024 // nl
    for cand in (256, 128, 64, 32, 16):
        if rows % cand == 0 and N_DEV * cand * cols * stacks[0].dtype.itemsize <= stack_block_bytes:
            tr = cand
            break
    nt = rows // tr

    def body(*refs):
        s_refs = refs[:nl]
        w_ref, m_ref, v_ref, g_ref, d_ref, nm_ref, nv_ref = refs[nl:]
        for l, s_ref in enumerate(s_refs):
            @pl.when(pl.program_id(0) == l)
            def _(s_ref=s_ref):
                acc = s_ref[0].astype(F32)
                for d in range(1, N_DEV):
                    acc = acc + s_ref[d].astype(F32)
                g_ref[...] = acc

        g = g_ref[...]
        nm = ADAM_B1 * m_ref[...] + (1.0 - ADAM_B1) * g
        nv = ADAM_B2 * v_ref[...] + (1.0 - ADAM_B2) * (g * g)
        m_hat = nm / (1.0 - ADAM_B1 ** ADAM_STEP)
        v_hat = nv / (1.0 - ADAM_B2 ** ADAM_STEP)
        d_ref[...] = -ADAM_LR * (m_hat / (jnp.sqrt(v_hat) + ADAM_EPS) + ADAM_WD * w_ref[...])
        nm_ref[...] = nm
        nv_ref[...] = nv

    blk = pl.BlockSpec((tr, cols), lambda l, i: (l * nt + i, 0))
    return _call(
        body, name=name, grid=(nl, nt),
        in_specs=[pl.BlockSpec((N_DEV, tr, cols), lambda l, i, k=k: (0, jnp.where(l == k, i, 0), 0)) for k in range(nl)]
        + [blk, blk, blk],
        out_specs=[blk] * 4, out_shape=[jax.ShapeDtypeStruct((nl * rows, cols), F32)] * 4,
        args=(*stacks, w, m, v), sem=("arbitrary", "arbitrary"), carry=carry)


def _lb_all(lb_logits):
    sm = jax.nn.softmax(lb_logits.astype(F32), axis=0)
    return jnp.cumsum(sm, axis=0) - sm[0:1]


def _rope_tables(p):
    half = DH // 2
    inv = jnp.power(ROPE_THETA, -jnp.arange(half, dtype=F32) * 2.0 / DH)
    pos = (jnp.arange(p) - PAD_FRONT).astype(F32)
    ang = pos[:, None] * inv[None, :]
    cos, sin = jnp.cos(ang), jnp.sin(ang)
    return jnp.tile(cos, (1, 4)), jnp.tile(jnp.concatenate([-sin, sin], axis=1), (1, 2))


def _permute_cols(w):
    return jnp.concatenate([w[:, :3072], w[:, 3328:], w[:, 3072:3328]], axis=1)


def _unpermute_cols(w):
    return jnp.concatenate([w[:, :3072], w[:, 5120:], w[:, 3072:5120]], axis=1)


SMALL = (("lb_logits", (DEPTH, 512)), ("q_norm_g", (DEPTH, DH)),
         ("k_norm_g", (DEPTH, DH)), ("attn_sinks", (DEPTH, HB)), ("hgrn_norm_g", (DEPTH, 128)),
         ("pool_w", (DEPTH, 4, 128, 128)), ("pool_scale", (DEPTH, 512)))


def _pack_small(d):
    flat = jnp.concatenate([d[k].astype(F32).reshape(-1) for k, _ in SMALL])
    pad = (-flat.shape[0]) % (8 * 128)
    return jnp.pad(flat, (0, pad)).reshape(-1, 128)


def _unpack_small(a):
    flat = a.reshape(-1)
    out, off = {}, 0
    for k, shp in SMALL:
        n = int(np.prod(shp))
        out[k] = flat[off:off + n].reshape(shp)
        off += n
    return out


def kernel(x, meta_tokens, lb_logits, norm_g, w_in, q_norm_g, k_norm_g, attn_sinks, hgrn_norm_g, pool_w, pool_scale, w_out, loss_target, m_meta_tokens, m_lb_logits, m_norm_g, m_w_in, m_q_norm_g, m_k_norm_g, m_attn_sinks, m_hgrn_norm_g, m_pool_w, m_pool_scale, m_w_out, v_meta_tokens, v_lb_logits, v_norm_g, v_w_in, v_q_norm_g, v_k_norm_g, v_attn_sinks, v_hgrn_norm_g, v_pool_w, v_pool_scale, v_w_out):
    seq = x.shape[1]
    p = seq + TB
    cs = PROJ_COLS // N_DEV
    rs = MIX // N_DEV
    ms = D_MODEL // N_DEV

    full_w_in = lambda st: _permute_cols(st.transpose(1, 0, 2).reshape(D_MODEL, PROJ_COLS))
    dw_in_blocks = lambda rows: _unpermute_cols(rows).reshape(rows.shape[0], N_DEV, cs).transpose(1, 0, 2).astype(BF16)
    dw_out_blocks = lambda g: g["w_out"].reshape(N_DEV, rs, D_MODEL).astype(BF16)
    w_in_bf = w_in.astype(BF16)

    lb_all, lb_vjp = jax.vjp(_lb_all, lb_logits)
    cos, sin = _rope_tables(p)
    layers = []
    for l in range(DEPTH):
        layers.append(dict(
            norm_g=norm_g[l][None], lb=lb_all[l][None],
            gq=jnp.tile(q_norm_g[l], 2)[None], gk=jnp.tile(k_norm_g[l], 2)[None], sinks=attn_sinks[l],
            hg=hgrn_norm_g[l][None], pool_w=pool_w[l].astype(BF16), pool_scale=pool_scale[l][None], cos=cos, sin=sin))
    meta_all = _exchange(meta_tokens, gather=True, name="gather_meta")
    meta_full = meta_all.transpose(1, 0, 2).reshape(N_META, D_MODEL)

    def got_w_out(st):
        st = st.reshape(N_DEV, DEPTH, rs, D_MODEL)
        for l in range(DEPTH):
            layers[l]["w_out"] = st[:, l].reshape(MIX, D_MODEL)

    def got_w_in(l):
        def done(st):
            layers[l]["w_in"] = full_w_in(st)
        return done

    fwd_carries = [
        dict(rmsnorm_fwd=(lambda _: w_in_bf[0], True, got_w_in(0)),
             in_proj=(lambda _: w_in_bf[1], True, got_w_in(1)),
             hgrn_fwd=(lambda _: w_out.reshape(DEPTH * rs, D_MODEL).astype(BF16), True, got_w_out)),
        {}]
    h = jnp.concatenate([jnp.zeros((PAD_FRONT, D_MODEL), F32), meta_full, x[0]], axis=0)
    saved = []
    for l in range(DEPTH):
        h, s = _layer_fwd(h, layers[l], f"_l{l}", fwd_carries[l])
        saved.append(s)
    dh, dhb, sq = _loss_grad(h, loss_target[0], name="loss_grad")
    loss = lax.psum(0.5 * jnp.sum(sq) / D_MODEL, ("x", "y", "c"))

    grads = [None] * DEPTH
    win_stacks, wout_stacks, small_stacks = [None] * (W_IN_PARTS * DEPTH), [None] * DEPTH, [None]

    def into(stacks, i):
        def done(st):
            stacks[i] = st
        return done

    def small_grads(g0):
        both = [g0, grads[1]]
        stk = lambda k: jnp.stack([both[l][k][0] for l in range(DEPTH)])
        fold = lambda a: a[:, :DH] + a[:, DH:]
        return _pack_small(dict(
            lb_logits=lb_vjp(stk("lb"))[0], q_norm_g=fold(stk("gq")), k_norm_g=fold(stk("gk")),
            attn_sinks=stk("sinks")[:, :HB], hgrn_norm_g=stk("hg"),
            pool_w=jnp.stack([both[l]["pool_w"] for l in range(DEPTH)]), pool_scale=stk("pool_scale")))

    def part(l, i):
        return (lambda g: dw_in_blocks(g["w_in_parts"][i])), False, into(win_stacks, W_IN_PARTS * l + i)

    bwd_carries = [
        dict(hgrn_bwd=(dw_out_blocks, False, into(wout_stacks, 0)), dw_in_0=(small_grads, True, into(small_stacks, 0)),
             dw_in_1=part(0, 0), d_xn=part(0, 1)),
        dict(attn_bwd=(dw_out_blocks, False, into(wout_stacks, 1)), dw_in_1=part(1, 0), d_xn=part(1, 1))]
    dh, dhb, grads[1] = _layer_bwd(dh, dhb, saved[1], layers[1], "_l1", bwd_carries[1])
    grad_x, d_front, grads[0] = _layer_bwd(dh, dhb, saved[0], layers[0], "_l0", bwd_carries[0], first_layer=True)
    grad_x = grad_x[None]
    dmeta = d_front.reshape(N_META, N_DEV, ms).transpose(1, 0, 2)
    meta_stack = _exchange(dmeta, gather=False, name="scatter_dmeta")
    small_stack = small_stacks[0]
    d_norm_g = jnp.stack([grads[l]["norm_g"][0] for l in range(DEPTH)]).reshape(-1, 128)
    norm_stack = _exchange(d_norm_g, gather=True, name="gather_norm_g_grad")
    adam_norm = [a.reshape(DEPTH, D_MODEL) for a in _adamw(
        [norm_stack], norm_g.reshape(-1, 128), m_norm_g.reshape(-1, 128), v_norm_g.reshape(-1, 128), name="adamw_norm_g")]

    g_wout, d_wout, nm_wout, nv_wout = _adamw(wout_stacks, w_out.reshape(DEPTH * rs, D_MODEL), m_w_out.reshape(DEPTH * rs, D_MODEL),
                                              v_w_out.reshape(DEPTH * rs, D_MODEL), name="adamw_w_out")
    g_win, d_win, nm_win, nv_win = _adamw(win_stacks, w_in.reshape(DEPTH * D_MODEL, cs), m_w_in.reshape(DEPTH * D_MODEL, cs),
                                          v_w_in.reshape(DEPTH * D_MODEL, cs), name="adamw_w_in")
    g_meta, d_meta, nm_meta, nv_meta = _adamw([meta_stack], meta_tokens, m_meta_tokens, v_meta_tokens, name="adamw_meta")
    small_w = dict(lb_logits=lb_logits, norm_g=norm_g, q_norm_g=q_norm_g, k_norm_g=k_norm_g, attn_sinks=attn_sinks,
                   hgrn_norm_g=hgrn_norm_g, pool_w=pool_w, pool_scale=pool_scale)
    small_m = dict(lb_logits=m_lb_logits, norm_g=m_norm_g, q_norm_g=m_q_norm_g, k_norm_g=m_k_norm_g, attn_sinks=m_attn_sinks,
                   hgrn_norm_g=m_hgrn_norm_g, pool_w=m_pool_w, pool_scale=m_pool_scale)
    small_v = dict(lb_logits=v_lb_logits, norm_g=v_norm_g, q_norm_g=v_q_norm_g, k_norm_g=v_k_norm_g, attn_sinks=v_attn_sinks,
                   hgrn_norm_g=v_hgrn_norm_g, pool_w=v_pool_w, pool_scale=v_pool_scale)
    small_out = [_unpack_small(a) for a in _adamw([small_stack], _pack_small(small_w), _pack_small(small_m),
                                                  _pack_small(small_v), name="adamw_small")]

    big = dict(
        meta_tokens=(g_meta, d_meta, nm_meta, nv_meta), norm_g=tuple(adam_norm),
        w_in=tuple(a.reshape(DEPTH, D_MODEL, cs) for a in (g_win, d_win, nm_win, nv_win)),
        w_out=tuple(a.reshape(DEPTH, rs, D_MODEL) for a in (g_wout, d_wout, nm_wout, nv_wout)))
    order = ("meta_tokens", "lb_logits", "norm_g", "w_in", "q_norm_g", "k_norm_g", "attn_sinks", "hgrn_norm_g",
             "pool_w", "pool_scale", "w_out")
    outs = [loss, grad_x]
    for kind in range(4):
        for k in order:
            outs.append(big[k][kind] if k in big else small_out[kind][k])
    return tuple(outs)
```

```python
import functools

import numpy as np
import jax
import jax.numpy as jnp
from jax import lax
from jax.experimental import pallas as pl
from jax.experimental.pallas import tpu as pltpu

F32, BF16 = jnp.float32, jnp.bfloat16

D_MODEL = 2048
DEPTH = 2
N_META = 16
TB = 128
PAD_FRONT = TB - N_META
RMS_EPS = 1e-6
NEG_INF = -1e30
LOG_FLOOR = 1e-30
HA, DK_A = 4, 128
CH = 16
NCH = TB // CH
HB, KVH, DH = 16, 2, 64
GRP = HB // KVH
ROPE_THETA = 10000.0
POOL_WINDOWS = (2, 4, 8, 16)
PROJ_COLS = 5376
MIX = 2048
N_DEV = 8
W_IN_PARTS = 2
C_QA, C_FA, C_IA, C_GA = 0, 1, 2, 3
C_QB, C_GB = 2, 3
C_UC, C_GC = 8, 9
C_KB, C_VB = 40, 41

ADAM_LR, ADAM_B1, ADAM_B2, ADAM_EPS, ADAM_WD, ADAM_STEP = 0.001, 0.9, 0.999, 1e-08, 0.01, 10

VMEM_LIMIT = 48 * 1024 * 1024

NN = ((1,), (0,))
NT = ((1,), (1,))
TN = ((0,), (0,))


def _dot(a, b, dims):
    return lax.dot_general(a, b, (dims, ((), ())), preferred_element_type=F32)


def _split3(x):
    hi = x.astype(BF16)
    r = x - hi.astype(F32)
    mid = r.astype(BF16)
    lo = (r - mid.astype(F32)).astype(BF16)
    return hi, mid, lo


def _xdot(m01, x):
    hi, mid, lo = _split3(x)
    return _dot(m01, hi, NN) + _dot(m01, mid, NN) + _dot(m01, lo, NN)


def _xdot_r(x, m01):
    hi, mid, lo = _split3(x)
    return _dot(hi, m01, NN) + _dot(mid, m01, NN) + _dot(lo, m01, NN)


def _iota(shape, dim):
    return lax.broadcasted_iota(jnp.int32, shape, dim)


def _params(*sem):
    return pltpu.CompilerParams(dimension_semantics=sem, vmem_limit_bytes=VMEM_LIMIT)


def _row_tile(p, target):
    best = TB
    t = TB
    while t <= target:
        if p % t == 0:
            best = t
        t += TB
    return best


def _col_tile(n, target):
    best = 128
    t = 128
    while t <= target:
        if n % t == 0:
            best = t
        t += 128
    return best


def _sigmoid(x):
    return 1.0 / (1.0 + jnp.exp(-x))


def _mm_nn(a, b, res=None, *, name, tm=1664, tn=768, carry=None):
    m, k = a.shape
    n = b.shape[1]
    tm, tn = _row_tile(m, tm), _col_tile(n, tn)

    def body(*refs):
        if res is None:
            a_ref, b_ref, o_ref = refs
            o_ref[...] = _dot(a_ref[...], b_ref[...], NN)
        else:
            a_ref, b_ref, r_ref, o_ref = refs
            o_ref[...] = r_ref[...] + _dot(a_ref[...], b_ref[...], NN)

    in_specs = [pl.BlockSpec((tm, k), lambda j, i: (i, 0)), pl.BlockSpec((k, tn), lambda j, i: (0, j))]
    args = [a, b]
    if res is not None:
        in_specs.append(pl.BlockSpec((tm, tn), lambda j, i: (i, j)))
        args.append(res)
    return _call(
        body, name=name, grid=(n // tn, m // tm), in_specs=in_specs,
        out_specs=[pl.BlockSpec((tm, tn), lambda j, i: (i, j))],
        out_shape=[jax.ShapeDtypeStruct((m, n), F32)],
        args=args, sem=("parallel", "parallel"), carry=carry)


def _mm_nt(a, b, *, name, tm=640, tn=512, tk=2048, carry=None):
    m, k = a.shape
    n = b.shape[0]
    tm, tn, tk = _row_tile(m, tm), _col_tile(n, tn), _col_tile(k, tk)

    def body(a_ref, b_ref, o_ref):
        @pl.when(pl.program_id(2) == 0)
        def _():
            o_ref[...] = jnp.zeros_like(o_ref)

        o_ref[...] += _dot(a_ref[...], b_ref[...], NT)

    return _call(
        body, name=name, grid=(n // tn, m // tm, k // tk),
        in_specs=[pl.BlockSpec((tm, tk), lambda j, i, kk: (i, kk)), pl.BlockSpec((tn, tk), lambda j, i, kk: (j, kk))],
        out_specs=[pl.BlockSpec((tm, tn), lambda j, i, kk: (i, j))],
        out_shape=[jax.ShapeDtypeStruct((m, n), F32)],
        args=(a, b), sem=("parallel", "parallel", "arbitrary"), carry=carry)


def _mm_tn(a, b, *, name, tm=1024, tn=1344, tk=1664, m_part=None, carry=None):
    k, m = a.shape
    n = b.shape[1]
    first, m = (0, m) if m_part is None else (m_part[0], m // m_part[1])
    tm, tn, tk = _col_tile(m, tm), _col_tile(n, tn), _row_tile(k, tk)
    first *= m // tm

    def body(a_ref, b_ref, o_ref):
        @pl.when(pl.program_id(2) == 0)
        def _():
            o_ref[...] = jnp.zeros_like(o_ref)

        o_ref[...] += _dot(a_ref[...], b_ref[...], TN)

    return _call(
        body, name=name, grid=(m // tm, n // tn, k // tk),
        in_specs=[pl.BlockSpec((tk, tm), lambda i, j, kk: (kk, first + i)), pl.BlockSpec((tk, tn), lambda i, j, kk: (kk, j))],
        out_specs=[pl.BlockSpec((tm, tn), lambda i, j, kk: (i, j))],
        out_shape=[jax.ShapeDtypeStruct((m, n), F32)],
        args=(a, b), sem=("parallel", "parallel", "arbitrary"), carry=carry)


def _rmsnorm_fwd(h, g, *, name, carry=None):
    p, dm = h.shape
    tm = _row_tile(p, 640)

    def body(h_ref, g_ref, xn_ref):
        hv = h_ref[...]
        r = lax.rsqrt(jnp.mean(hv * hv, axis=-1, keepdims=True) + RMS_EPS)
        xn_ref[...] = (hv * r * g_ref[...]).astype(BF16)

    return _call(
        body, name=name, grid=(p // tm,),
        in_specs=[pl.BlockSpec((tm, dm), lambda i: (i, 0)), pl.BlockSpec((1, dm), lambda i: (0, 0))],
        out_specs=[pl.BlockSpec((tm, dm), lambda i: (i, 0))],
        out_shape=[jax.ShapeDtypeStruct((p, dm), BF16)],
        args=(h, g), sem=("parallel",), carry=carry)


def _embed_rmsnorm_fwd(x, meta, g, *, name, carry=None):
    seq, dm = x.shape
    p = seq + TB

    def body(x_ref, meta_ref, g_ref, h_ref, xn_ref):
        @pl.when(pl.program_id(0) == 0)
        def _():
            h_ref[...] = jnp.zeros_like(h_ref)
            h_ref[PAD_FRONT:TB, :] = meta_ref[...]

        @pl.when(pl.program_id(0) > 0)
        def _():
            h_ref[...] = x_ref[...]

        hv = h_ref[...]
        r = lax.rsqrt(jnp.mean(hv * hv, axis=-1, keepdims=True) + RMS_EPS)
        xn_ref[...] = (hv * r * g_ref[...]).astype(BF16)

    row = pl.BlockSpec((TB, dm), lambda i: (i, 0))
    return _call(
        body, name=name, grid=(p // TB,),
        in_specs=[pl.BlockSpec((TB, dm), lambda i: (jnp.maximum(i - 1, 0), 0)), pl.BlockSpec((N_META, dm), lambda i: (0, 0)),
                  pl.BlockSpec((1, dm), lambda i: (0, 0))],
        out_specs=[row, row],
        out_shape=[jax.ShapeDtypeStruct((p, dm), F32), jax.ShapeDtypeStruct((p, dm), BF16)],
        args=(x, meta, g), sem=("arbitrary",), carry=carry)


def _rmsnorm_bwd(dxn, h, g, dh_out, *, name, first_layer=False, carry=None):
    p, dm = h.shape
    tm = TB if first_layer else _row_tile(p, 384)

    def body(dxn_ref, h_ref, g_ref, dho_ref, out_ref, aux_ref, dg_ref):
        hv = h_ref[...]
        r = lax.rsqrt(jnp.mean(hv * hv, axis=-1, keepdims=True) + RMS_EPS)
        xh = hv * r
        dy = dxn_ref[...]
        dyn = dy * g_ref[...]
        dh = dho_ref[...] + r * (dyn - xh * jnp.mean(dyn * xh, axis=-1, keepdims=True))
        out_ref[...] = dh

        @pl.when(pl.program_id(0) == 0)
        def _():
            dg_ref[...] = jnp.zeros_like(dg_ref)
            if first_layer:
                aux_ref[...] = dh[PAD_FRONT:TB]

        if not first_layer:
            aux_ref[...] = dh.astype(BF16)
        dg_ref[...] += jnp.sum(dy * xh, axis=0, keepdims=True)

    row = pl.BlockSpec((tm, dm), lambda i: (i, 0))
    vec = pl.BlockSpec((1, dm), lambda i: (0, 0))
    if first_layer:
        out_specs = [pl.BlockSpec((TB, dm), lambda i: (jnp.maximum(i - 1, 0), 0)), pl.BlockSpec((N_META, dm), lambda i: (0, 0)), vec]
        out_shape = [jax.ShapeDtypeStruct((p - TB, dm), F32), jax.ShapeDtypeStruct((N_META, dm), F32)]
    else:
        out_specs = [row, row, vec]
        out_shape = [jax.ShapeDtypeStruct((p, dm), F32), jax.ShapeDtypeStruct((p, dm), BF16)]
    return _call(
        body, name=name, grid=(p // tm,),
        in_specs=[row, row, vec, row], out_specs=out_specs,
        out_shape=out_shape + [jax.ShapeDtypeStruct((1, dm), F32)],
        args=(dxn, h, g, dh_out), sem=("arbitrary",), carry=carry)


def _loss_grad(h, target, *, name):
    p, dm = h.shape

    def body(h_ref, t_ref, dh_ref, dhb_ref, sq_ref):
        n = pl.program_id(0)

        @pl.when(n == 0)
        def _():
            dh_ref[...] = jnp.zeros_like(dh_ref)
            dhb_ref[...] = jnp.zeros_like(dhb_ref)
            sq_ref[...] = jnp.zeros_like(sq_ref)

        @pl.when(n > 0)
        def _():
            err = h_ref[...] - t_ref[...]
            dh = err * (1.0 / dm)
            dh_ref[...] = dh
            dhb_ref[...] = dh.astype(BF16)
            sq_ref[...] += jnp.sum(err * err, axis=0, keepdims=True)

    row = pl.BlockSpec((TB, dm), lambda n: (n, 0))
    return pl.pallas_call(
        body, name=name, grid=(p // TB,),
        in_specs=[row, pl.BlockSpec((TB, dm), lambda n: (jnp.maximum(n - 1, 0), 0))],
        out_specs=[row, row, pl.BlockSpec((1, dm), lambda n: (0, 0))],
        out_shape=[jax.ShapeDtypeStruct((p, dm), F32), jax.ShapeDtypeStruct((p, dm), BF16), jax.ShapeDtypeStruct((1, dm), F32)],
        compiler_params=_params("arbitrary"),
    )(h, target)


def _chunk_masks():
    ri, ci = _iota((TB, TB), 0), _iota((TB, TB), 1)
    same = (ri >> 4) == (ci >> 4)
    causal = same & (ci <= ri)
    lower = jnp.where(causal, 1.0, 0.0).astype(BF16)
    upper = jnp.where(same & (ci >= ri), 1.0, 0.0).astype(BF16)
    ones = jnp.where(same, 1.0, 0.0).astype(BF16)
    return causal, lower, upper, ones


def _hgrn_gates(q, z, lbh, m):
    sig = _sigmoid(z)
    f = lbh + (1.0 - lbh) * sig
    lf = jnp.log(jnp.maximum(f, LOG_FLOOR)) * m
    kk = (1.0 - lbh) * (1.0 - sig) * m
    sq = _sigmoid(q)
    return sig, f, lf, kk, sq, q * sq


def _hgrn_fwd(proj, lb, *, name, carry=None):
    p = proj.shape[0]
    nb = p // TB

    def body(qa_ref, fa_ref, ia_ref, lb_ref, oa_ref, sck_ref, st_ref):
        n = pl.program_id(0)

        @pl.when(n == 0)
        def _():
            st_ref[...] = jnp.zeros_like(st_ref)

        causal, lower, _, ones = _chunk_masks()
        m = ((n * TB + _iota((TB, 1), 0)) >= PAD_FRONT).astype(F32)
        heads = range(HA)
        sls = [slice(hd * DK_A, (hd + 1) * DK_A) for hd in heads]
        rows = [slice(c * CH, (c + 1) * CH) for c in range(NCH)]
        gates = [_hgrn_gates(qa_ref[:, sl], fa_ref[:, sl], lb_ref[:, sl], m) for sl in sls]
        lf = [t[2] for t in gates]
        g = [_xdot(lower, x) for x in lf]
        gl = [_xdot(ones, x) for x in lf]
        qd = [(gates[hd][5] * jnp.exp(g[hd])).astype(BF16) for hd in heads]
        kt = [(gates[hd][3] * jnp.exp(-g[hd])).astype(BF16) for hd in heads]
        kd = [(gates[hd][3] * jnp.exp(gl[hd] - g[hd])).astype(BF16) for hd in heads]
        vb = [ia_ref[:, sl].astype(BF16) for sl in sls]
        a_all = [jnp.exp(x) for x in gl]
        att = [jnp.where(causal, _dot(qd[hd], kt[hd], NT), 0.0).astype(BF16) for hd in heads]
        kv = [[_dot(vb[hd][r], kd[hd][r], TN) for r in rows] for hd in heads]
        o = [_dot(att[hd], vb[hd], NN) for hd in heads]
        before = []
        for hd in heads:
            st = st_ref[hd]
            sck_ref[0, hd] = st
            per_chunk = []
            for c in range(NCH):
                per_chunk.append(st.astype(BF16))
                st = st * a_all[hd][c * CH:c * CH + 1, :] + kv[hd][c]
            st_ref[hd] = st
            before.append(per_chunk)
        inter = [[_dot(qd[hd][rows[c]], before[hd][c], NT) for c in range(NCH)] for hd in heads]
        for hd in heads:
            oa_ref[:, sls[hd]] = o[hd] + jnp.concatenate(inter[hd], axis=0)

    blk = lambda c: pl.BlockSpec((TB, 512), lambda n, c=c: (n, c))
    return _call(
        body, name=name, grid=(nb,),
        in_specs=[blk(C_QA), blk(C_FA), blk(C_IA), pl.BlockSpec((1, 512), lambda n: (0, 0))],
        out_specs=[pl.BlockSpec((TB, 512), lambda n: (n, 0)), pl.BlockSpec((1, HA, TB, TB), lambda n: (n, 0, 0, 0))],
        out_shape=[jax.ShapeDtypeStruct((p, 512), F32), jax.ShapeDtypeStruct((nb, HA, TB, TB), F32)],
        scratch_shapes=[pltpu.VMEM((HA, TB, TB), F32)],
        args=(proj, proj, proj, lb), sem=("arbitrary",), carry=carry)


def _hgrn_bwd(proj, lb, sck, d_oa, *, name, carry=None):
    p = proj.shape[0]
    nb = p // TB

    def body(qa_ref, fa_ref, ia_ref, lb_ref, sck_ref, do_ref, dq_ref, dz_ref, dv_ref, dlb_ref, dst_ref):
        i = pl.program_id(0)
        n = nb - 1 - i

        @pl.when(i == 0)
        def _():
            dst_ref[...] = jnp.zeros_like(dst_ref)
            dlb_ref[...] = jnp.zeros_like(dlb_ref)

        causal, lower, upper, ones = _chunk_masks()
        m = ((n * TB + _iota((TB, 1), 0)) >= PAD_FRONT).astype(F32)
        heads = range(HA)
        sls = [slice(hd * DK_A, (hd + 1) * DK_A) for hd in heads]
        rows = [slice(c * CH, (c + 1) * CH) for c in range(NCH)]
        a_row = lambda a, c: a[c * CH:c * CH + 1, :]
        gates = [_hgrn_gates(qa_ref[:, sl], fa_ref[:, sl], lb_ref[:, sl], m) for sl in sls]
        g = [_xdot(lower, t[2]) for t in gates]
        gl = [_xdot(ones, t[2]) for t in gates]
        e_g = [jnp.exp(x) for x in g]
        e_ng = [jnp.exp(-x) for x in g]
        e_d = [jnp.exp(gl[hd] - g[hd]) for hd in heads]
        a_all = [jnp.exp(x) for x in gl]
        qd_f = [gates[hd][5] * e_g[hd] for hd in heads]
        kt_f = [gates[hd][3] * e_ng[hd] for hd in heads]
        kd_f = [gates[hd][3] * e_d[hd] for hd in heads]
        qd, kt, kd = ([x.astype(BF16) for x in xs] for xs in (qd_f, kt_f, kd_f))
        vb = [ia_ref[:, sl].astype(BF16) for sl in sls]
        dob = [do_ref[:, sl].astype(BF16) for sl in sls]
        att = [jnp.where(causal, _dot(qd[hd], kt[hd], NT), 0.0).astype(BF16) for hd in heads]
        d_att = [jnp.where(causal, _dot(dob[hd], vb[hd], NT), 0.0).astype(BF16) for hd in heads]
        kv = [[_dot(vb[hd][r], kd[hd][r], TN) for r in rows] for hd in heads]
        dqk = [[_dot(dob[hd][r], qd[hd][r], TN) for r in rows] for hd in heads]
        d_v = [_dot(att[hd], dob[hd], TN) for hd in heads]
        d_qd = [_dot(d_att[hd], kt[hd], NN) for hd in heads]
        d_kt = [_dot(d_att[hd], qd[hd], TN) for hd in heads]
        stc, dsc = [], []
        for hd in heads:
            st, before = sck_ref[0, hd], []
            for c in range(NCH):
                before.append(st)
                if c + 1 < NCH:
                    st = st * a_row(a_all[hd], c) + kv[hd][c]
            dst, after = dst_ref[hd], [None] * NCH
            for c in range(NCH - 1, -1, -1):
                after[c] = dst
                dst = dst * a_row(a_all[hd], c) + dqk[hd][c]
            dst_ref[hd] = dst
            stc.append(before)
            dsc.append(after)
        dscb = [[x.astype(BF16) for x in dsc[hd]] for hd in heads]
        dvs = [[_dot(kd[hd][rows[c]], dscb[hd][c], NT) for c in range(NCH)] for hd in heads]
        dkd = [[_dot(vb[hd][rows[c]], dscb[hd][c], NN) for c in range(NCH)] for hd in heads]
        dqd = [[_dot(dob[hd][rows[c]], stc[hd][c].astype(BF16), NN) for c in range(NCH)] for hd in heads]
        dgl = [[jnp.broadcast_to(jnp.sum(dsc[hd][c] * stc[hd][c], axis=0, keepdims=True) * a_row(a_all[hd], c), (CH, TB))
                for c in range(NCH)] for hd in heads]
        d_qd = [d_qd[hd] + jnp.concatenate(dqd[hd], axis=0) for hd in heads]
        d_kd = [jnp.concatenate(dkd[hd], axis=0) for hd in heads]
        kd_term = [d_kd[hd] * kd_f[hd] for hd in heads]
        d_g = [d_qd[hd] * qd_f[hd] - d_kt[hd] * kt_f[hd] - kd_term[hd] for hd in heads]
        d_lf = [_xdot(upper, d_g[hd]) + _xdot(ones, kd_term[hd]) + jnp.concatenate(dgl[hd], axis=0) for hd in heads]
        for hd in heads:
            sl = sls[hd]
            sig, f, _, _, sq, _ = gates[hd]
            q, lbh = qa_ref[:, sl], lb_ref[:, sl]
            d_kk = (d_kt[hd] * e_ng[hd] + d_kd[hd] * e_d[hd]) * m
            t1 = d_lf[hd] * m * jnp.where(f > LOG_FLOOR, 1.0 / f, 0.0)
            dq_ref[:, sl] = (d_qd[hd] * e_g[hd] * (sq * (1.0 + q * (1.0 - sq)))).astype(BF16)
            dz_ref[:, sl] = ((t1 - d_kk) * (1.0 - lbh) * sig * (1.0 - sig)).astype(BF16)
            dv_ref[:, sl] = (d_v[hd] + jnp.concatenate(dvs[hd], axis=0)).astype(BF16)
            dlb_ref[:, sl] += jnp.sum((t1 - d_kk) * (1.0 - sig), axis=0, keepdims=True)

    blk = lambda c: pl.BlockSpec((TB, 512), lambda i, c=c: (nb - 1 - i, c))
    out_blk = pl.BlockSpec((TB, 512), lambda i: (nb - 1 - i, 0))
    vec = pl.BlockSpec((1, 512), lambda i: (0, 0))
    return _call(
        body, name=name, grid=(nb,),
        in_specs=[blk(C_QA), blk(C_FA), blk(C_IA), vec,
                  pl.BlockSpec((1, HA, TB, TB), lambda i: (nb - 1 - i, 0, 0, 0)), out_blk],
        out_specs=[out_blk, out_blk, out_blk, vec],
        out_shape=[jax.ShapeDtypeStruct((p, 512), BF16)] * 3 + [jax.ShapeDtypeStruct((1, 512), F32)],
        scratch_shapes=[pltpu.VMEM((HA, TB, TB), F32)],
        args=(proj, proj, proj, lb, sck, d_oa), sem=("arbitrary",), carry=carry)


def _lane():
    return _iota((1, TB), 1)


def _swap_halves(y):
    first = (_lane() & 63) < 32
    return jnp.where(first, pltpu.roll(y, 96, 1), pltpu.roll(y, 32, 1))


def _head_ones():
    ri, ci = _iota((TB, TB), 0), _iota((TB, TB), 1)
    return jnp.where((ri >> 6) == (ci >> 6), 1.0, 0.0).astype(BF16)


def _norm_rope(x, g, cos, sin, bd):
    r = lax.rsqrt(_xdot_r(x * x, bd) * (1.0 / DH) + RMS_EPS)
    y = x * r * g
    return y * cos + _swap_halves(y) * sin


def _norm_rope_bwd(d_out, x, g, cos, sin, bd):
    d = d_out * cos - _swap_halves(d_out) * sin
    r = lax.rsqrt(_xdot_r(x * x, bd) * (1.0 / DH) + RMS_EPS)
    xh = x * r
    dyn = d * g
    dx = r * (dyn - xh * (_xdot_r(dyn * xh, bd) * (1.0 / DH)))
    return dx, jnp.sum(d * xh, axis=0, keepdims=True)


def _dup_heads(k):
    first = _lane() < DH
    r = pltpu.roll(k, DH, 1)
    return jnp.where(first, k, r), jnp.where(first, r, k)


def _qk_prep(proj, gq, gk, cos, sin, *, name):
    p = proj.shape[0]

    def body(qb_ref, kb_ref, vb_ref, gq_ref, gk_ref, cos_ref, sin_ref, qh_ref, k2_ref, v2_ref):
        bd = _head_ones()
        cos_v, sin_v = cos_ref[...], sin_ref[...]
        for j in range(HB // 2):
            sl = slice(j * TB, (j + 1) * TB)
            qh_ref[:, sl] = _norm_rope(qb_ref[:, sl], gq_ref[...], cos_v, sin_v, bd).astype(BF16)
        k0, k1 = _dup_heads(_norm_rope(kb_ref[...], gk_ref[...], cos_v, sin_v, bd))
        k2_ref[:, 0:TB] = k0.astype(BF16)
        k2_ref[:, TB:2 * TB] = k1.astype(BF16)
        v0, v1 = _dup_heads(vb_ref[...])
        v2_ref[:, 0:TB] = v0.astype(BF16)
        v2_ref[:, TB:2 * TB] = v1.astype(BF16)

    vec = pl.BlockSpec((1, TB), lambda n: (0, 0))
    tab = pl.BlockSpec((TB, TB), lambda n: (n, 0))
    return pl.pallas_call(
        body, name=name, grid=(p // TB,),
        in_specs=[pl.BlockSpec((TB, 1024), lambda n: (n, C_QB)), pl.BlockSpec((TB, TB), lambda n: (n, C_KB)),
                  pl.BlockSpec((TB, TB), lambda n: (n, C_VB)), vec, vec, tab, tab],
        out_specs=[pl.BlockSpec((TB, 1024), lambda n: (n, 0)), pl.BlockSpec((TB, 256), lambda n: (n, 0)),
                   pl.BlockSpec((TB, 256), lambda n: (n, 0))],
        out_shape=[jax.ShapeDtypeStruct((p, 1024), BF16), jax.ShapeDtypeStruct((p, 256), BF16),
                   jax.ShapeDtypeStruct((p, 256), BF16)],
        compiler_params=_params("parallel"),
    )(proj, proj, proj, gq, gk, cos, sin)


NKEY = N_META + 2 * TB


def _attn_mask(n):
    r = _iota((TB, NKEY), 0)
    j = _iota((TB, NKEY), 1)
    meta = (j < N_META) & ((n >= 1) | (j + PAD_FRONT <= r))
    prev = (j >= N_META) & (j < N_META + TB) & (n >= 2) & (j - N_META > r)
    cur = (j >= N_META + TB) & (n >= 1) & (j - (N_META + TB) <= r)
    return meta | prev | cur


def _attn_specs():
    cur = lambda w: pl.BlockSpec((TB, w), lambda n: (n, 0))
    prev = pl.BlockSpec((TB, 256), lambda n: (jnp.maximum(n - 1, 0), 0))
    meta = pl.BlockSpec((N_META, 256), lambda n: (PAD_FRONT // N_META, 0))
    sink = pl.BlockSpec(memory_space=pltpu.SMEM)
    return cur, prev, meta, sink


ATTN_GROUP = 4


def _head_queries(q_ref, kv, first):
    out = []
    for jj in range(GRP // 2):
        j = kv * (GRP // 2) + jj
        qj = q_ref[:, j * TB:(j + 1) * TB] * (DH ** -0.5)
        for half in range(2):
            out.append((j, half, jnp.where(first if half == 0 else ~first, qj, jnp.zeros_like(qj))))
    return out


def _attn_fwd(qh, k2, v2, sinks, *, name, carry=None):
    p = qh.shape[0]

    def body(sink_ref, q_ref, kc_ref, kp_ref, km_ref, vc_ref, vp_ref, vm_ref, o_ref, lse_ref):
        n = pl.program_id(0)
        mask = _attn_mask(n)
        lane = _lane()
        first = lane < DH
        lse_tile = jnp.zeros((TB, TB), F32)
        for kv in range(KVH):
            ks = slice(kv * TB, (kv + 1) * TB)
            kall = jnp.concatenate([km_ref[:, ks], kp_ref[:, ks], kc_ref[:, ks]], axis=0)
            vall = jnp.concatenate([vm_ref[:, ks], vp_ref[:, ks], vc_ref[:, ks]], axis=0)
            hq = _head_queries(q_ref, kv, first)
            for g0 in range(0, GRP, ATTN_GROUP):
                grp = hq[g0:g0 + ATTN_GROUP]
                idx = range(len(grp))
                s = [jnp.where(mask, _dot(qm, kall, NT), NEG_INF) for _, _, qm in grp]
                sink = [sink_ref[2 * j + half] for j, half, _ in grp]
                mx = [jnp.maximum(jnp.max(s[i], axis=1, keepdims=True), sink[i]) for i in idx]
                pr = [jnp.exp(s[i] - mx[i]) for i in idx]
                den = [jnp.sum(pr[i], axis=1, keepdims=True) + jnp.exp(sink[i] - mx[i]) for i in idx]
                o = [_dot(pr[i].astype(BF16), vall, NN) * (1.0 / den[i]) for i in idx]
                for i, (j, half, _) in enumerate(grp):
                    lse_tile = lse_tile + jnp.where(lane == 2 * j + half, mx[i] + jnp.log(den[i]), 0.0)
                for i in range(0, len(grp), 2):
                    j = grp[i][0]
                    o_ref[:, j * TB:(j + 1) * TB] = jnp.where(first, o[i], o[i + 1])
        lse_ref[...] = lse_tile

    cur, prev, meta, sink = _attn_specs()
    return _call(
        body, name=name, grid=(p // TB,),
        in_specs=[sink, cur(1024), cur(256), prev, meta, cur(256), prev, meta],
        out_specs=[cur(1024), cur(TB)],
        out_shape=[jax.ShapeDtypeStruct((p, 1024), F32), jax.ShapeDtypeStruct((p, TB), F32)],
        args=(sinks, qh, k2, k2, k2, v2, v2, v2), sem=("parallel",), carry=carry)


def _attn_bwd(qh, k2, v2, sinks, o, lse, d_o, *, name, carry=None):
    p = qh.shape[0]

    def body(sink_ref, q_ref, kc_ref, kp_ref, km_ref, vc_ref, vp_ref, vm_ref, o_ref, lse_ref, do_ref,
             dq_ref, dkc_ref, dkp_ref, dvc_ref, dvp_ref, dkm_ref, dvm_ref, dsink_ref):
        n = pl.program_id(0)

        @pl.when(n == 0)
        def _():
            dkm_ref[...] = jnp.zeros_like(dkm_ref)
            dvm_ref[...] = jnp.zeros_like(dvm_ref)
            dsink_ref[...] = jnp.zeros_like(dsink_ref)

        mask = _attn_mask(n)
        lane = _lane()
        first = lane < DH
        lse_tile = lse_ref[...]
        sink_acc = jnp.zeros((TB, TB), F32)
        for kv in range(KVH):
            ks = slice(kv * TB, (kv + 1) * TB)
            kall = jnp.concatenate([km_ref[:, ks], kp_ref[:, ks], kc_ref[:, ks]], axis=0)
            vall = jnp.concatenate([vm_ref[:, ks], vp_ref[:, ks], vc_ref[:, ks]], axis=0)
            d_kall = jnp.zeros((NKEY, TB), F32)
            d_vall = jnp.zeros((NKEY, TB), F32)
            hq = _head_queries(q_ref, kv, first)
            for g0 in range(0, GRP, ATTN_GROUP):
                grp = hq[g0:g0 + ATTN_GROUP]
                idx = range(len(grp))
                s = [jnp.where(mask, _dot(qm, kall, NT), NEG_INF) for _, _, qm in grp]
                dom = [jnp.where(first if half == 0 else ~first, do_ref[:, j * TB:(j + 1) * TB], 0.0) for j, half, _ in grp]
                domb = [x.astype(BF16) for x in dom]
                d_w = [_dot(x, vall, NT) for x in domb]
                delta = [jnp.sum(dom[i] * o_ref[:, grp[i][0] * TB:(grp[i][0] + 1) * TB], axis=1, keepdims=True) for i in idx]
                lse_h = [jnp.sum(jnp.where(lane == 2 * j + half, lse_tile, 0.0), axis=1, keepdims=True) for j, half, _ in grp]
                w = [jnp.exp(s[i] - lse_h[i]) for i in idx]
                for i, (j, half, _) in enumerate(grp):
                    w_sink = jnp.exp(sink_ref[2 * j + half] - lse_h[i])
                    sink_acc = sink_acc + jnp.where(lane == 2 * j + half, -(w_sink * delta[i]), 0.0)
                dsb = [(w[i] * (d_w[i] - delta[i])).astype(BF16) for i in idx]
                d_q = [_dot(x, kall, NN) * (DH ** -0.5) for x in dsb]
                d_k = [_dot(dsb[i], grp[i][2], TN) for i in idx]
                d_v = [_dot(w[i].astype(BF16), domb[i], TN) for i in idx]
                for i in idx:
                    d_kall = d_kall + d_k[i]
                    d_vall = d_vall + d_v[i]
                for i in range(0, len(grp), 2):
                    j = grp[i][0]
                    dq_ref[:, j * TB:(j + 1) * TB] = jnp.where(first, d_q[i], d_q[i + 1])
            dkm_ref[:, ks] += d_kall[0:N_META]
            dkp_ref[:, ks] = d_kall[N_META:N_META + TB]
            dkc_ref[:, ks] = d_kall[N_META + TB:NKEY]
            dvm_ref[:, ks] += d_vall[0:N_META]
            dvp_ref[:, ks] = d_vall[N_META:N_META + TB]
            dvc_ref[:, ks] = d_vall[N_META + TB:NKEY]
        dsink_ref[...] += jnp.sum(sink_acc, axis=0, keepdims=True)

    cur, prev, meta, sink = _attn_specs()
    acc = lambda r: pl.BlockSpec((r, 256), lambda n: (0, 0))
    return _call(
        body, name=name, grid=(p // TB,),
        in_specs=[sink, cur(1024), cur(256), prev, meta, cur(256), prev, meta, cur(1024), cur(TB), cur(1024)],
        out_specs=[cur(1024), cur(256), cur(256), cur(256), cur(256), acc(N_META), acc(N_META),
                   pl.BlockSpec((1, TB), lambda n: (0, 0))],
        out_shape=[jax.ShapeDtypeStruct((p, 1024), F32)] + [jax.ShapeDtypeStruct((p, 256), F32)] * 4
        + [jax.ShapeDtypeStruct((N_META, 256), F32)] * 2 + [jax.ShapeDtypeStruct((1, TB), F32)],
        args=(sinks, qh, k2, k2, k2, v2, v2, v2, o, lse, d_o), sem=("arbitrary",), carry=carry)


def _qk_post(proj, gq, gk, cos, sin, dqh, dkc, dkp, dkm, dvc, dvp, dvm, *, name):
    p = proj.shape[0]
    nb = p // TB

    def body(qb_ref, kb_ref, gq_ref, gk_ref, cos_ref, sin_ref, dqh_ref, dkc_ref, dkp_ref, dkm_ref,
             dvc_ref, dvp_ref, dvm_ref, dqb_ref, dkb_ref, dvb_ref, dgq_ref, dgk_ref, tk_ref, tv_ref):
        n = pl.program_id(0)

        @pl.when(n == 0)
        def _():
            dgq_ref[...] = jnp.zeros_like(dgq_ref)
            dgk_ref[...] = jnp.zeros_like(dgk_ref)

        keep = jnp.where(n == nb - 1, 0.0, 1.0)
        tk_ref[...] = dkc_ref[...] + keep * dkp_ref[...]
        tv_ref[...] = dvc_ref[...] + keep * dvp_ref[...]

        @pl.when(n == 0)
        def _():
            tk_ref[PAD_FRONT:TB, :] += dkm_ref[...]
            tv_ref[PAD_FRONT:TB, :] += dvm_ref[...]

        first = _lane() < DH

        def fold(t_ref):
            t0, t1 = t_ref[:, 0:TB], t_ref[:, TB:2 * TB]
            return jnp.where(first, t0 + pltpu.roll(t0, DH, 1), t1 + pltpu.roll(t1, DH, 1))

        bd = _head_ones()
        cos_v, sin_v = cos_ref[...], sin_ref[...]
        dvb_ref[...] = fold(tv_ref).astype(BF16)
        dkb, dgk = _norm_rope_bwd(fold(tk_ref), kb_ref[...], gk_ref[...], cos_v, sin_v, bd)
        dkb_ref[...] = dkb.astype(BF16)
        dgk_ref[...] += dgk
        dgq = jnp.zeros((1, TB), F32)
        for j in range(HB // 2):
            sl = slice(j * TB, (j + 1) * TB)
            dqb, dg = _norm_rope_bwd(dqh_ref[:, sl], qb_ref[:, sl], gq_ref[...], cos_v, sin_v, bd)
            dqb_ref[:, sl] = dqb.astype(BF16)
            dgq = dgq + dg
        dgq_ref[...] += dgq

    vec = pl.BlockSpec((1, TB), lambda n: (0, 0))
    tab = pl.BlockSpec((TB, TB), lambda n: (n, 0))
    cur = lambda w: pl.BlockSpec((TB, w), lambda n: (n, 0))
    nxt = pl.BlockSpec((TB, 256), lambda n: (jnp.minimum(n + 1, nb - 1), 0))
    meta = pl.BlockSpec((N_META, 256), lambda n: (0, 0))
    return pl.pallas_call(
        body, name=name, grid=(nb,),
        in_specs=[pl.BlockSpec((TB, 1024), lambda n: (n, C_QB)), pl.BlockSpec((TB, TB), lambda n: (n, C_KB)),
                  vec, vec, tab, tab, cur(1024), cur(256), nxt, meta, cur(256), nxt, meta],
        out_specs=[cur(1024), cur(TB), cur(TB), vec, vec],
        out_shape=[jax.ShapeDtypeStruct((p, 1024), BF16), jax.ShapeDtypeStruct((p, TB), BF16),
                   jax.ShapeDtypeStruct((p, TB), BF16), jax.ShapeDtypeStruct((1, TB), F32),
                   jax.ShapeDtypeStruct((1, TB), F32)],
        scratch_shapes=[pltpu.VMEM((TB, 256), F32), pltpu.VMEM((TB, 256), F32)],
        compiler_params=_params("arbitrary"),
    )(proj, proj, gq, gk, cos, sin, dqh, dkc, dkp, dkm, dvc, dvp, dvm)


EXT = TB + N_META


def _pool_count_inv(n, w):
    t = n * TB + _iota((TB, 1), 0)
    cnt = jnp.clip(t - (PAD_FRONT - 1), 1, w)
    return 1.0 / cnt.astype(F32)


def _silu_parts(gate):
    s = _sigmoid(gate)
    return gate * s, s * (1.0 + gate * (1.0 - s))


def _mix_fwd(proj, oa, yb, hg, pool_w, pool_scale, *, name):
    p = proj.shape[0]

    def body(ga_ref, gb_ref, uc_ref, up_ref, gc_ref, oa_ref, yb_ref, hg_ref, pw_ref, ps_ref, mx_ref, pooled_ref):
        n = pl.program_id(0)
        valid = ((n * TB + _iota((TB, 1), 0)) >= PAD_FRONT).astype(F32)
        for hd in range(HA):
            sl = slice(hd * TB, (hd + 1) * TB)
            o = oa_ref[:, sl]
            r = lax.rsqrt(jnp.mean(o * o, axis=-1, keepdims=True) + RMS_EPS)
            act, _ = _silu_parts(ga_ref[:, sl])
            mx_ref[:, sl] = (o * r * hg_ref[...] * act).astype(BF16)
        for j in range(HB // 2):
            sl = slice(j * TB, (j + 1) * TB)
            act, _ = _silu_parts(gb_ref[:, sl])
            mx_ref[:, 512 + j * TB:512 + (j + 1) * TB] = (yb_ref[:, sl] * act).astype(BF16)
        ri, ci = _iota((TB, EXT), 0), _iota((TB, EXT), 1)
        has_prev = jnp.where(n == 0, 0.0, 1.0)
        for gi, w in enumerate(POOL_WINDOWS):
            sl = slice(gi * TB, (gi + 1) * TB)
            ug = uc_ref[:, sl] * valid
            ext = jnp.concatenate([up_ref[:, sl] * has_prev, ug], axis=0)
            band = jnp.where((ci <= ri + N_META) & (ci > ri + N_META - w), 1.0, 0.0).astype(BF16)
            pooled = (_xdot(band, ext) * _pool_count_inv(n, w) - ug) * valid
            pooled_ref[:, sl] = pooled
            yc = _dot(pooled.astype(BF16), pw_ref[gi], NN) * ps_ref[:, sl]
            act, _ = _silu_parts(gc_ref[:, sl])
            mx_ref[:, 1536 + gi * TB:1536 + (gi + 1) * TB] = (yc * act).astype(BF16)

    cur = lambda w, c=0: pl.BlockSpec((TB, w), lambda n, c=c: (n, c))
    prev16 = pl.BlockSpec((N_META, 512), lambda n: (jnp.maximum(n * (TB // N_META) - 1, 0), C_UC))
    return pl.pallas_call(
        body, name=name, grid=(p // TB,),
        in_specs=[cur(512, C_GA), cur(1024, C_GB), cur(512, C_UC), prev16, cur(512, C_GC), cur(512), cur(1024),
                  pl.BlockSpec((1, TB), lambda n: (0, 0)), pl.BlockSpec((4, TB, TB), lambda n: (0, 0, 0)),
                  pl.BlockSpec((1, 512), lambda n: (0, 0))],
        out_specs=[cur(MIX), cur(512)],
        out_shape=[jax.ShapeDtypeStruct((p, MIX), BF16), jax.ShapeDtypeStruct((p, 512), F32)],
        compiler_params=_params("parallel"),
    )(proj, proj, proj, proj, proj, oa, yb, hg, pool_w, pool_scale)


def _mix_bwd(proj, oa, yb, pooled, hg, pool_w, pool_scale, d_mixed, *, name, carry=None):
    p = proj.shape[0]

    def body(ga_ref, gb_ref, gc_ref, oa_ref, yb_ref, pooled_ref, hg_ref, pw_ref, ps_ref, dm_ref,
             doa_ref, dyb_ref, dga_ref, dgb_ref, dgc_ref, dp_ref, dhg_ref, dpw_ref, dps_ref):
        n = pl.program_id(0)

        @pl.when(n == 0)
        def _():
            dhg_ref[...] = jnp.zeros_like(dhg_ref)
            dpw_ref[...] = jnp.zeros_like(dpw_ref)
            dps_ref[...] = jnp.zeros_like(dps_ref)

        valid = ((n * TB + _iota((TB, 1), 0)) >= PAD_FRONT).astype(F32)
        dhg = jnp.zeros((1, TB), F32)
        for hd in range(HA):
            sl = slice(hd * TB, (hd + 1) * TB)
            o = oa_ref[:, sl]
            r = lax.rsqrt(jnp.mean(o * o, axis=-1, keepdims=True) + RMS_EPS)
            on = o * r
            gate = ga_ref[:, sl]
            act, dact = _silu_parts(gate)
            dmx = dm_ref[:, sl]
            d_ya = dmx * act
            dga_ref[:, sl] = (dmx * on * hg_ref[...] * dact).astype(BF16)
            dyn = d_ya * hg_ref[...]
            doa_ref[:, sl] = r * (dyn - on * jnp.mean(dyn * on, axis=-1, keepdims=True))
            dhg = dhg + jnp.sum(d_ya * on, axis=0, keepdims=True)
        dhg_ref[...] += dhg
        for j in range(HB // 2):
            sl = slice(j * TB, (j + 1) * TB)
            act, dact = _silu_parts(gb_ref[:, sl])
            dmx = dm_ref[:, 512 + j * TB:512 + (j + 1) * TB]
            dyb_ref[:, sl] = dmx * act
            dgb_ref[:, sl] = (dmx * yb_ref[:, sl] * dact).astype(BF16)
        for gi in range(len(POOL_WINDOWS)):
            sl = slice(gi * TB, (gi + 1) * TB)
            pooled = pooled_ref[:, sl]
            pooled_b = pooled.astype(BF16)
            t = _dot(pooled_b, pw_ref[gi], NN)
            act, dact = _silu_parts(gc_ref[:, sl])
            dmx = dm_ref[:, 1536 + gi * TB:1536 + (gi + 1) * TB]
            d_yc = dmx * act
            dgc_ref[:, sl] = (dmx * t * ps_ref[:, sl] * dact).astype(BF16)
            dps_ref[:, sl] += jnp.sum(d_yc * t, axis=0, keepdims=True)
            d_t = (d_yc * ps_ref[:, sl]).astype(BF16)
            dp_ref[:, sl] = _dot(d_t, pw_ref[gi], NT) * valid
            dpw_ref[gi] += _dot(pooled_b, d_t, TN)

    cur = lambda w, c=0: pl.BlockSpec((TB, w), lambda n, c=c: (n, c))
    return _call(
        body, name=name, grid=(p // TB,),
        in_specs=[cur(512, C_GA), cur(1024, C_GB), cur(512, C_GC), cur(512), cur(1024), cur(512),
                  pl.BlockSpec((1, TB), lambda n: (0, 0)), pl.BlockSpec((4, TB, TB), lambda n: (0, 0, 0)),
                  pl.BlockSpec((1, 512), lambda n: (0, 0)), cur(MIX)],
        out_specs=[cur(512), cur(1024), cur(512), cur(1024), cur(512), cur(512),
                   pl.BlockSpec((1, TB), lambda n: (0, 0)), pl.BlockSpec((4, TB, TB), lambda n: (0, 0, 0)),
                   pl.BlockSpec((1, 512), lambda n: (0, 0))],
        out_shape=[jax.ShapeDtypeStruct((p, 512), F32), jax.ShapeDtypeStruct((p, 1024), F32),
                   jax.ShapeDtypeStruct((p, 512), BF16), jax.ShapeDtypeStruct((p, 1024), BF16),
                   jax.ShapeDtypeStruct((p, 512), BF16), jax.ShapeDtypeStruct((p, 512), F32),
                   jax.ShapeDtypeStruct((1, TB), F32), jax.ShapeDtypeStruct((4, TB, TB), F32),
                   jax.ShapeDtypeStruct((1, 512), F32)],
        args=(proj, proj, proj, oa, yb, pooled, hg, pool_w, pool_scale, d_mixed), sem=("arbitrary",), carry=carry)


def _pool_bwd(dp, *, name):
    p = dp.shape[0]
    nb = p // TB

    def body(dp_ref, dn_ref, duc_ref):
        n = pl.program_id(0)
        valid = ((n * TB + _iota((TB, 1), 0)) >= PAD_FRONT).astype(F32)
        has_next = jnp.where(n == nb - 1, 0.0, 1.0)
        ri, ci = _iota((TB, EXT), 0), _iota((TB, EXT), 1)
        for gi, w in enumerate(POOL_WINDOWS):
            sl = slice(gi * TB, (gi + 1) * TB)
            d_p = dp_ref[:, sl]
            ext = jnp.concatenate([d_p * _pool_count_inv(n, w), dn_ref[:, sl] * (has_next / w)], axis=0)
            band = jnp.where((ci >= ri) & (ci < ri + w), 1.0, 0.0).astype(BF16)
            duc_ref[:, sl] = ((_xdot(band, ext) - d_p) * valid).astype(BF16)

    return pl.pallas_call(
        body, name=name, grid=(nb,),
        in_specs=[pl.BlockSpec((TB, 512), lambda n: (n, 0)),
                  pl.BlockSpec((N_META, 512), lambda n: (jnp.minimum(n + 1, nb - 1) * (TB // N_META), 0))],
        out_specs=pl.BlockSpec((TB, 512), lambda n: (n, 0)),
        out_shape=jax.ShapeDtypeStruct((p, 512), BF16),
        compiler_params=_params("parallel"),
    )(dp, dp)


def _carried(carries, key, local, fn, *args, **kw):
    if key not in carries:
        return fn(*args, **kw)
    make_src, gather, done = carries[key]
    *outs, stack = fn(*args, carry=(make_src(local), gather), **kw)
    done(stack)
    return outs


def _layer_fwd(h, w, tag, carries, xn=None):
    if xn is None:
        xn, = _rmsnorm_fwd(h, w["norm_g"], name=f"rmsnorm_fwd{tag}")
    proj, = _carried(carries, "in_proj", None, _mm_nn, xn, w["w_in"], name=f"in_proj{tag}")
    oa, sck = _carried(carries, "hgrn_fwd", None, _hgrn_fwd, proj, w["lb"], name=f"hgrn_fwd{tag}")
    qh, k2, v2 = _qk_prep(proj, w["gq"], w["gk"], w["cos"], w["sin"], name=f"qk_prep{tag}")
    yb, lse = _carried(carries, "attn_fwd", None, _attn_fwd, qh, k2, v2, w["sinks"], name=f"attn_fwd{tag}")
    mixed, pooled = _mix_fwd(proj, oa, yb, w["hg"], w["pool_w"], w["pool_scale"], name=f"mix_fwd{tag}")
    h_next, = _mm_nn(mixed, w["w_out"], h, name=f"out_proj{tag}", tn=512)
    saved = dict(h=h, xn=xn, proj=proj, oa=oa, sck=sck, qh=qh, k2=k2, v2=v2, yb=yb, lse=lse, mixed=mixed, pooled=pooled)
    return h_next, saved


def _layer_bwd(dh_out, dhb, s, w, tag, carries, first_layer=False):
    g = {}
    d_mixed, = _carried(carries, "d_mixed", g, _mm_nt, dhb, w["w_out"], name=f"d_mixed{tag}", tm=1664, tn=512)
    g["w_out"], = _mm_tn(s["mixed"], dhb, name=f"dw_out{tag}", tn=1024)
    d_oa, d_yb, d_ga, d_gb, d_gc, d_p, g["hg"], g["pool_w"], g["pool_scale"] = _carried(
        carries, "mix_bwd", g, _mix_bwd,
        s["proj"], s["oa"], s["yb"], s["pooled"], w["hg"], w["pool_w"], w["pool_scale"], d_mixed, name=f"mix_bwd{tag}")
    d_uc = _pool_bwd(d_p, name=f"pool_bwd{tag}")
    d_qh, dkc, dkp, dvc, dvp, dkm, dvm, g["sinks"] = _carried(
        carries, "attn_bwd", g, _attn_bwd, s["qh"], s["k2"], s["v2"], w["sinks"], s["yb"], s["lse"], d_yb,
        name=f"attn_bwd{tag}")
    d_qb, d_kb, d_vb, g["gq"], g["gk"] = _qk_post(s["proj"], w["gq"], w["gk"], w["cos"], w["sin"], d_qh, dkc, dkp, dkm,
                                                  dvc, dvp, dvm, name=f"qk_post{tag}")
    d_qa, d_fa, d_ia, g["lb"] = _carried(carries, "hgrn_bwd", g, _hgrn_bwd, s["proj"], w["lb"], s["sck"], d_oa,
                                         name=f"hgrn_bwd{tag}")
    d_proj = jnp.concatenate([d_qa, d_fa, d_ia, d_ga, d_qb, d_gb, d_uc, d_gc, d_kb, d_vb], axis=1)
    g["w_in_parts"] = []
    for i in range(W_IN_PARTS):
        part, = _carried(carries, f"dw_in_{i}", g, _mm_tn, s["xn"], d_proj, name=f"dw_in_{i}{tag}",
                         m_part=(i, W_IN_PARTS))
        g["w_in_parts"].append(part)
    d_xn, = _carried(carries, "d_xn", g, _mm_nt, d_proj, w["w_in"], name=f"d_xn{tag}", tk=PROJ_COLS)
    out, aux, g["norm_g"] = _rmsnorm_bwd(d_xn, s["h"], w["norm_g"], dh_out, name=f"rmsnorm_bwd{tag}", first_layer=first_layer)
    return out, aux, g


def _peers():
    x, y, c = lax.axis_index("x"), lax.axis_index("y"), lax.axis_index("c")
    out = []
    for k in range(1, N_DEV):
        kx, ky, kc = (k >> 2) & 1, (k >> 1) & 1, k & 1
        px, py, pc = x ^ kx, y ^ ky, c ^ kc
        out.append(((px, py, pc), 4 * px + 2 * py + pc))
    return 4 * x + 2 * y + c, out


def _exchange_copies(src_ref, out_ref, send_sems, recv_sems, local_sem):
    me, peers = _peers()
    mine = pltpu.make_async_copy(src_ref.at[me], out_ref.at[me], local_sem)
    copies = []
    for k, (dev, idx) in enumerate(peers):
        copies.append(pltpu.make_async_remote_copy(
            src_ref=src_ref.at[idx], dst_ref=out_ref.at[me],
            send_sem=send_sems.at[k], recv_sem=recv_sems.at[k],
            device_id=dev, device_id_type=pl.DeviceIdType.MESH))
    return mine, copies


def _gather_copies(src_ref, out_ref, send_sems, recv_sems, local_sem):
    x, y, c = lax.axis_index("x"), lax.axis_index("y"), lax.axis_index("c")
    slot = lambda px, py, pc: out_ref.at[4 * px + 2 * py + pc]
    sibling = (x, y, 1 - c)
    chips = [(1 - x, y), (x, 1 - y), (1 - x, 1 - y)]

    def copy(k, src, block, to):
        return pltpu.make_async_remote_copy(src_ref=src, dst_ref=slot(*block), send_sem=send_sems.at[k],
                                            recv_sem=recv_sems.at[k], device_id=to, device_id_type=pl.DeviceIdType.MESH)

    mine = lambda: pltpu.make_async_copy(src_ref, slot(x, y, c), local_sem)
    own = lambda: ([copy(0, src_ref, (x, y, c), sibling)]
                   + [copy(1 + j, src_ref, (x, y, c), (*chip, c)) for j, chip in enumerate(chips)])
    passing = lambda: [copy(4 + j, slot(*chip, c), (*chip, c), sibling) for j, chip in enumerate(chips)]
    arrivals = lambda: ([copy(0, src_ref, sibling, sibling)]
                        + [copy(1 + j, src_ref, (*chip, c), sibling) for j, chip in enumerate(chips)]
                        + [copy(4 + j, src_ref, (*chip, 1 - c), sibling) for j, chip in enumerate(chips)])
    return mine, own, passing, arrivals


def _exchange_start(*refs, gather):
    if gather:
        mine, own, _, _ = _gather_copies(*refs)
        mine().start()
        for cp in own():
            cp.start()
        return
    mine, copies = _exchange_copies(*refs)
    mine.start()
    for cp in copies:
        cp.start()


def _exchange_wait(*refs, gather):
    if gather:
        mine, own, passing, arrivals = _gather_copies(*refs)
        passing, arrivals = passing(), arrivals()
        for j, cp in enumerate(passing):
            arrivals[1 + j].wait_recv()
            cp.start()
        arrivals[0].wait_recv()
        for cp in arrivals[4:]:
            cp.wait_recv()
        for cp in own() + passing:
            cp.wait_send()
        mine().wait()
        return
    mine, copies = _exchange_copies(*refs)
    for cp in copies:
        cp.wait_recv()
    for cp in copies:
        cp.wait_send()
    mine.wait()


def _exchange_scratch():
    return [pltpu.SemaphoreType.DMA((N_DEV - 1,)), pltpu.SemaphoreType.DMA((N_DEV - 1,)), pltpu.SemaphoreType.DMA]


def _exchange(src, *, gather, name):
    rows, cols = src.shape[-2:]

    def body(src_ref, out_ref, send_sems, recv_sems, local_sem):
        _exchange_start(src_ref, out_ref, send_sems, recv_sems, local_sem, gather=gather)
        _exchange_wait(src_ref, out_ref, send_sems, recv_sems, local_sem, gather=gather)

    return pl.pallas_call(
        body, name=name,
        in_specs=[pl.BlockSpec(memory_space=pl.ANY)], out_specs=pl.BlockSpec(memory_space=pl.ANY),
        out_shape=jax.ShapeDtypeStruct((N_DEV, rows, cols), src.dtype),
        scratch_shapes=_exchange_scratch(),
    )(src)


def _call(body, *, name, grid, in_specs, out_specs, out_shape, args, sem, scratch_shapes=(), carry=None):
    if carry is None:
        return pl.pallas_call(
            body, name=name, grid=grid, in_specs=list(in_specs), out_specs=list(out_specs), out_shape=list(out_shape),
            scratch_shapes=list(scratch_shapes), compiler_params=_params(*sem))(*args)
    src, gather = carry
    n_in, n_out, n_scr = len(in_specs), len(out_specs), len(scratch_shapes)
    rows, cols = src.shape[-2:]

    def carrying(*refs):
        ins, src_ref = refs[:n_in], refs[n_in]
        outs, dst_ref = refs[n_in + 1:n_in + 1 + n_out], refs[n_in + 1 + n_out]
        scr = refs[n_in + 2 + n_out:]
        exch = (src_ref, dst_ref) + tuple(scr[n_scr:])
        first, last = None, None
        for a, size in enumerate(grid):
            f, l = pl.program_id(a) == 0, pl.program_id(a) == size - 1
            first = f if first is None else first & f
            last = l if last is None else last & l

        @pl.when(first)
        def _():
            _exchange_start(*exch, gather=gather)

        body(*ins, *outs, *scr[:n_scr])

        @pl.when(last)
        def _():
            _exchange_wait(*exch, gather=gather)

    hbm = pl.BlockSpec(memory_space=pl.ANY)
    return pl.pallas_call(
        carrying, name=name, grid=grid, in_specs=list(in_specs) + [hbm], out_specs=list(out_specs) + [hbm],
        out_shape=list(out_shape) + [jax.ShapeDtypeStruct((N_DEV, rows, cols), src.dtype)],
        scratch_shapes=list(scratch_shapes) + _exchange_scratch(),
        compiler_params=_params(*(("arbitrary",) * len(grid))))(*args, src)


def _adamw(stacks, w, m, v, *, name, carry=None):
    nl = len(stacks)
    rows, cols = stacks[0].shape[1:]
    tr = rows
    stack_block_bytes = 8 * 1024 * 1024 // nl
    for cand in (256, 128, 64, 32, 16):
        if rows % cand == 0 and N_DEV * cand * cols * stacks[0].dtype.itemsize <= stack_block_bytes:
            tr = cand
            break
    nt = rows // tr

    def body(*refs):
        s_refs = refs[:nl]
        w_ref, m_ref, v_ref, g_ref, d_ref, nm_ref, nv_ref = refs[nl:]
        for l, s_ref in enumerate(s_refs):
            @pl.when(pl.program_id(0) == l)
            def _(s_ref=s_ref):
                acc = s_ref[0].astype(F32)
                for d in range(1, N_DEV):
                    acc = acc + s_ref[d].astype(F32)
                g_ref[...] = acc

        g = g_ref[...]
        nm = ADAM_B1 * m_ref[...] + (1.0 - ADAM_B1) * g
        nv = ADAM_B2 * v_ref[...] + (1.0 - ADAM_B2) * (g * g)
        m_hat = nm / (1.0 - ADAM_B1 ** ADAM_STEP)
        v_hat = nv / (1.0 - ADAM_B2 ** ADAM_STEP)
        d_ref[...] = -ADAM_LR * (m_hat / (jnp.sqrt(v_hat) + ADAM_EPS) + ADAM_WD * w_ref[...])
        nm_ref[...] = nm
        nv_ref[...] = nv

    blk = pl.BlockSpec((tr, cols), lambda l, i: (l * nt + i, 0))
    return _call(
        body, name=name, grid=(nl, nt),
        in_specs=[pl.BlockSpec((N_DEV, tr, cols), lambda l, i, k=k: (0, jnp.where(l == k, i, 0), 0)) for k in range(nl)]
        + [blk, blk, blk],
        out_specs=[blk] * 4, out_shape=[jax.ShapeDtypeStruct((nl * rows, cols), F32)] * 4,
        args=(*stacks, w, m, v), sem=("arbitrary", "arbitrary"), carry=carry)


def _lb_all(lb_logits):
    sm = jax.nn.softmax(lb_logits.astype(F32), axis=0)
    return jnp.cumsum(sm, axis=0) - sm[0:1]


def _rope_tables(p):
    half = DH // 2
    inv = jnp.power(ROPE_THETA, -jnp.arange(half, dtype=F32) * 2.0 / DH)
    pos = (jnp.arange(p) - PAD_FRONT).astype(F32)
    ang = pos[:, None] * inv[None, :]
    cos, sin = jnp.cos(ang), jnp.sin(ang)
    return jnp.tile(cos, (1, 4)), jnp.tile(jnp.concatenate([-sin, sin], axis=1), (1, 2))


def _permute_cols(w):
    return jnp.concatenate([w[:, :3072], w[:, 3328:], w[:, 3072:3328]], axis=1)


def _unpermute_cols(w):
    return jnp.concatenate([w[:, :3072], w[:, 5120:], w[:, 3072:5120]], axis=1)


SMALL = (("lb_logits", (DEPTH, 512)), ("q_norm_g", (DEPTH, DH)),
         ("k_norm_g", (DEPTH, DH)), ("attn_sinks", (DEPTH, HB)), ("hgrn_norm_g", (DEPTH, 128)),
         ("pool_w", (DEPTH, 4, 128, 128)), ("pool_scale", (DEPTH, 512)))


def _pack_small(d):
    flat = jnp.concatenate([d[k].astype(F32).reshape(-1) for k, _ in SMALL])
    pad = (-flat.shape[0]) % (8 * 128)
    return jnp.pad(flat, (0, pad)).reshape(-1, 128)


def _unpack_small(a):
    flat = a.reshape(-1)
    out, off = {}, 0
    for k, shp in SMALL:
        n = int(np.prod(shp))
        out[k] = flat[off:off + n].reshape(shp)
        off += n
    return out


def kernel(x, meta_tokens, lb_logits, norm_g, w_in, q_norm_g, k_norm_g, attn_sinks, hgrn_norm_g, pool_w, pool_scale, w_out, loss_target, m_meta_tokens, m_lb_logits, m_norm_g, m_w_in, m_q_norm_g, m_k_norm_g, m_attn_sinks, m_hgrn_norm_g, m_pool_w, m_pool_scale, m_w_out, v_meta_tokens, v_lb_logits, v_norm_g, v_w_in, v_q_norm_g, v_k_norm_g, v_attn_sinks, v_hgrn_norm_g, v_pool_w, v_pool_scale, v_w_out):
    seq = x.shape[1]
    p = seq + TB
    cs = PROJ_COLS // N_DEV
    rs = MIX // N_DEV
    ms = D_MODEL // N_DEV

    full_w_in = lambda st: _permute_cols(st.transpose(1, 0, 2).reshape(D_MODEL, PROJ_COLS))
    dw_in_blocks = lambda rows: _unpermute_cols(rows).reshape(rows.shape[0], N_DEV, cs).transpose(1, 0, 2).astype(BF16)
    dw_out_blocks = lambda g: g["w_out"].reshape(N_DEV, rs, D_MODEL).astype(BF16)
    w_in_bf = w_in.astype(BF16)

    lb_all, lb_vjp = jax.vjp(_lb_all, lb_logits)
    cos, sin = _rope_tables(p)
    layers = []
    for l in range(DEPTH):
        layers.append(dict(
            norm_g=norm_g[l][None], lb=lb_all[l][None],
            gq=jnp.tile(q_norm_g[l], 2)[None], gk=jnp.tile(k_norm_g[l], 2)[None], sinks=attn_sinks[l],
            hg=hgrn_norm_g[l][None], pool_w=pool_w[l].astype(BF16), pool_scale=pool_scale[l][None], cos=cos, sin=sin))
    meta_all = _exchange(meta_tokens, gather=True, name="gather_meta")
    meta_full = meta_all.transpose(1, 0, 2).reshape(N_META, D_MODEL)

    def got_w_out(st):
        st = st.reshape(N_DEV, DEPTH, rs, D_MODEL)
        for l in range(DEPTH):
            layers[l]["w_out"] = st[:, l].reshape(MIX, D_MODEL)

    def got_w_in(l):
        def done(st):
            layers[l]["w_in"] = full_w_in(st)
        return done

    fwd_carries = [
        dict(in_proj=(lambda _: w_in_bf[1], True, got_w_in(1)),
             hgrn_fwd=(lambda _: w_out.reshape(DEPTH * rs, D_MODEL).astype(BF16), True, got_w_out)),
        {}]
    h, xn, w_in_l0 = _embed_rmsnorm_fwd(x[0], meta_full, layers[0]["norm_g"], name="embed_rmsnorm_fwd_l0",
                                        carry=(w_in_bf[0], True))
    got_w_in(0)(w_in_l0)
    saved = []
    for l in range(DEPTH):
        h, s = _layer_fwd(h, layers[l], f"_l{l}", fwd_carries[l], xn=xn if l == 0 else None)
        saved.append(s)
    dh, dhb, sq = _loss_grad(h, loss_target[0], name="loss_grad")
    loss = lax.psum(0.5 * jnp.sum(sq) / D_MODEL, ("x", "y", "c"))

    grads = [None] * DEPTH
    win_stacks, wout_stacks, small_stacks = [None] * (W_IN_PARTS * DEPTH), [None] * DEPTH, [None]

    def into(stacks, i):
        def done(st):
            stacks[i] = st
        return done

    def small_grads(g0):
        both = [g0, grads[1]]
        stk = lambda k: jnp.stack([both[l][k][0] for l in range(DEPTH)])
        fold = lambda a: a[:, :DH] + a[:, DH:]
        return _pack_small(dict(
            lb_logits=lb_vjp(stk("lb"))[0], q_norm_g=fold(stk("gq")), k_norm_g=fold(stk("gk")),
            attn_sinks=stk("sinks")[:, :HB], hgrn_norm_g=stk("hg"),
            pool_w=jnp.stack([both[l]["pool_w"] for l in range(DEPTH)]), pool_scale=stk("pool_scale")))

    def part(l, i):
        return (lambda g: dw_in_blocks(g["w_in_parts"][i])), False, into(win_stacks, W_IN_PARTS * l + i)

    bwd_carries = [
        dict(hgrn_bwd=(dw_out_blocks, False, into(wout_stacks, 0)), dw_in_0=(small_grads, True, into(small_stacks, 0)),
             dw_in_1=part(0, 0), d_xn=part(0, 1)),
        dict(attn_bwd=(dw_out_blocks, False, into(wout_stacks, 1)), dw_in_1=part(1, 0), d_xn=part(1, 1))]
    dh, dhb, grads[1] = _layer_bwd(dh, dhb, saved[1], layers[1], "_l1", bwd_carries[1])
    grad_x, d_front, grads[0] = _layer_bwd(dh, dhb, saved[0], layers[0], "_l0", bwd_carries[0], first_layer=True)
    grad_x = grad_x[None]
    dmeta = d_front.reshape(N_META, N_DEV, ms).transpose(1, 0, 2)
    meta_stack = _exchange(dmeta, gather=False, name="scatter_dmeta")
    small_stack = small_stacks[0]
    d_norm_g = jnp.stack([grads[l]["norm_g"][0] for l in range(DEPTH)]).reshape(-1, 128)
    norm_stack = _exchange(d_norm_g, gather=True, name="gather_norm_g_grad")
    adam_norm = [a.reshape(DEPTH, D_MODEL) for a in _adamw(
        [norm_stack], norm_g.reshape(-1, 128), m_norm_g.reshape(-1, 128), v_norm_g.reshape(-1, 128), name="adamw_norm_g")]

    g_wout, d_wout, nm_wout, nv_wout = _adamw(wout_stacks, w_out.reshape(DEPTH * rs, D_MODEL), m_w_out.reshape(DEPTH * rs, D_MODEL),
                                              v_w_out.reshape(DEPTH * rs, D_MODEL), name="adamw_w_out")
    g_win, d_win, nm_win, nv_win = _adamw(win_stacks, w_in.reshape(DEPTH * D_MODEL, cs), m_w_in.reshape(DEPTH * D_MODEL, cs),
                                          v_w_in.reshape(DEPTH * D_MODEL, cs), name="adamw_w_in")
    g_meta, d_meta, nm_meta, nv_meta = _adamw([meta_stack], meta_tokens, m_meta_tokens, v_meta_tokens, name="adamw_meta")
    small_w = dict(lb_logits=lb_logits, norm_g=norm_g, q_norm_g=q_norm_g, k_norm_g=k_norm_g, attn_sinks=attn_sinks,
                   hgrn_norm_g=hgrn_norm_g, pool_w=pool_w, pool_scale=pool_scale)
    small_m = dict(lb_logits=m_lb_logits, norm_g=m_norm_g, q_norm_g=m_q_norm_g, k_norm_g=m_k_norm_g, attn_sinks=m_attn_sinks,
                   hgrn_norm_g=m_hgrn_norm_g, pool_w=m_pool_w, pool_scale=m_pool_scale)
    small_v = dict(lb_logits=v_lb_logits, norm_g=v_norm_g, q_norm_g=v_q_norm_g, k_norm_g=v_k_norm_g, attn_sinks=v_attn_sinks,
                   hgrn_norm_g=v_hgrn_norm_g, pool_w=v_pool_w, pool_scale=v_pool_scale)
    small_out = [_unpack_small(a) for a in _adamw([small_stack], _pack_small(small_w), _pack_small(small_m),
                                                  _pack_small(small_v), name="adamw_small")]

    big = dict(
        meta_tokens=(g_meta, d_meta, nm_meta, nv_meta), norm_g=tuple(adam_norm),
        w_in=tuple(a.reshape(DEPTH, D_MODEL, cs) for a in (g_win, d_win, nm_win, nv_win)),
        w_out=tuple(a.reshape(DEPTH, rs, D_MODEL) for a in (g_wout, d_wout, nm_wout, nv_wout)))
    order = ("meta_tokens", "lb_logits", "norm_g", "w_in", "q_norm_g", "k_norm_g", "attn_sinks", "hgrn_norm_g",
             "pool_w", "pool_scale", "w_out")
    outs = [loss, grad_x]
    for kind in range(4):
        for k in order:
            outs.append(big[k][kind] if k in big else small_out[kind][k])
    return tuple(outs)
```

```python
import functools

import numpy as np
import jax
import jax.numpy as jnp
from jax import lax
from jax.experimental import pallas as pl
from jax.experimental.pallas import tpu as pltpu

F32, BF16 = jnp.float32, jnp.bfloat16

D_MODEL = 2048
DEPTH = 2
N_META = 16
TB = 128
PAD_FRONT = TB - N_META
RMS_EPS = 1e-6
NEG_INF = -1e30
LOG_FLOOR = 1e-30
HA, DK_A = 4, 128
CH = 16
NCH = TB // CH
HB, KVH, DH = 16, 2, 64
GRP = HB // KVH
ROPE_THETA = 10000.0
POOL_WINDOWS = (2, 4, 8, 16)
PROJ_COLS = 5376
MIX = 2048
N_DEV = 8
W_IN_PARTS = 2
C_QA, C_FA, C_IA, C_GA = 0, 1, 2, 3
C_QB = 2
C_KB, C_VB = 24, 25
C_GB, C_UC, C_GC = (13, 14, 15, 16), (17, 18), (19, 20)


class _Cols:
    def __init__(self, refs):
        self.refs, self.width = refs, refs[0].shape[1]

    def __getitem__(self, idx):
        rows, sl = idx
        k, off = divmod(sl.start, self.width)
        return self.refs[k][rows, off:off + sl.stop - sl.start]

ADAM_LR, ADAM_B1, ADAM_B2, ADAM_EPS, ADAM_WD, ADAM_STEP = 0.001, 0.9, 0.999, 1e-08, 0.01, 10

VMEM_LIMIT = 48 * 1024 * 1024

NN = ((1,), (0,))
NT = ((1,), (1,))
TN = ((0,), (0,))


def _dot(a, b, dims):
    return lax.dot_general(a, b, (dims, ((), ())), preferred_element_type=F32)


def _split3(x):
    hi = x.astype(BF16)
    r = x - hi.astype(F32)
    mid = r.astype(BF16)
    lo = (r - mid.astype(F32)).astype(BF16)
    return hi, mid, lo


def _xdot(m01, x):
    hi, mid, lo = _split3(x)
    return _dot(m01, hi, NN) + _dot(m01, mid, NN) + _dot(m01, lo, NN)


def _xdot_r(x, m01):
    hi, mid, lo = _split3(x)
    return _dot(hi, m01, NN) + _dot(mid, m01, NN) + _dot(lo, m01, NN)


def _iota(shape, dim):
    return lax.broadcasted_iota(jnp.int32, shape, dim)


def _params(*sem):
    return pltpu.CompilerParams(dimension_semantics=sem, vmem_limit_bytes=VMEM_LIMIT)


def _row_tile(p, target):
    best = TB
    t = TB
    while t <= target:
        if p % t == 0:
            best = t
        t += TB
    return best


def _col_tile(n, target):
    best = 128
    t = 128
    while t <= target:
        if n % t == 0:
            best = t
        t += 128
    return best


def _sigmoid(x):
    return 1.0 / (1.0 + jnp.exp(-x))


def _mm_nn(a, b, res=None, *, name, tm=1664, tn=768, carry=None):
    m, k = a.shape
    n = b.shape[1]
    tm, tn = _row_tile(m, tm), _col_tile(n, tn)

    def body(*refs):
        if res is None:
            a_ref, b_ref, o_ref = refs
            o_ref[...] = _dot(a_ref[...], b_ref[...], NN)
        else:
            a_ref, b_ref, r_ref, o_ref = refs
            o_ref[...] = r_ref[...] + _dot(a_ref[...], b_ref[...], NN)

    in_specs = [pl.BlockSpec((tm, k), lambda j, i: (i, 0)), pl.BlockSpec((k, tn), lambda j, i: (0, j))]
    args = [a, b]
    if res is not None:
        in_specs.append(pl.BlockSpec((tm, tn), lambda j, i: (i, j)))
        args.append(res)
    return _call(
        body, name=name, grid=(n // tn, m // tm), in_specs=in_specs,
        out_specs=[pl.BlockSpec((tm, tn), lambda j, i: (i, j))],
        out_shape=[jax.ShapeDtypeStruct((m, n), F32)],
        args=args, sem=("parallel", "parallel"), carry=carry)


def _mm_nt(a, b, *, name, tm=640, tn=512, tk=2048, carry=None):
    m, k = a.shape
    n = b.shape[0]
    tm, tn, tk = _row_tile(m, tm), _col_tile(n, tn), _col_tile(k, tk)

    def body(a_ref, b_ref, o_ref):
        @pl.when(pl.program_id(2) == 0)
        def _():
            o_ref[...] = jnp.zeros_like(o_ref)

        o_ref[...] += _dot(a_ref[...], b_ref[...], NT)

    return _call(
        body, name=name, grid=(n // tn, m // tm, k // tk),
        in_specs=[pl.BlockSpec((tm, tk), lambda j, i, kk: (i, kk)), pl.BlockSpec((tn, tk), lambda j, i, kk: (j, kk))],
        out_specs=[pl.BlockSpec((tm, tn), lambda j, i, kk: (i, j))],
        out_shape=[jax.ShapeDtypeStruct((m, n), F32)],
        args=(a, b), sem=("parallel", "parallel", "arbitrary"), carry=carry)


def _mm_tn(a, b, *, name, tm=1024, tn=1344, tk=1664, m_part=None, carry=None):
    k, m = a.shape
    n = b.shape[1]
    first, m = (0, m) if m_part is None else (m_part[0], m // m_part[1])
    tm, tn, tk = _col_tile(m, tm), _col_tile(n, tn), _row_tile(k, tk)
    first *= m // tm
    nk = k // tk

    def body(a_ref, b_ref, o_ref, acc_ref):
        @pl.when(pl.program_id(2) == 0)
        def _():
            acc_ref[...] = jnp.zeros_like(acc_ref)

        acc_ref[...] += _dot(a_ref[...], b_ref[...], TN)

        @pl.when(pl.program_id(2) == nk - 1)
        def _():
            o_ref[...] = acc_ref[...].astype(BF16)

    return _call(
        body, name=name, grid=(m // tm, n // tn, nk),
        in_specs=[pl.BlockSpec((tk, tm), lambda i, j, kk: (kk, first + i)), pl.BlockSpec((tk, tn), lambda i, j, kk: (kk, j))],
        out_specs=[pl.BlockSpec((tm, tn), lambda i, j, kk: (i, j))],
        out_shape=[jax.ShapeDtypeStruct((m, n), BF16)], scratch_shapes=[pltpu.VMEM((tm, tn), F32)],
        args=(a, b), sem=("parallel", "parallel", "arbitrary"), carry=carry)


def _rmsnorm_fwd(h, g, *, name, carry=None):
    p, dm = h.shape
    tm = _row_tile(p, 640)

    def body(h_ref, g_ref, xn_ref):
        hv = h_ref[...]
        r = lax.rsqrt(jnp.mean(hv * hv, axis=-1, keepdims=True) + RMS_EPS)
        xn_ref[...] = (hv * r * g_ref[...]).astype(BF16)

    return _call(
        body, name=name, grid=(p // tm,),
        in_specs=[pl.BlockSpec((tm, dm), lambda i: (i, 0)), pl.BlockSpec((1, dm), lambda i: (0, 0))],
        out_specs=[pl.BlockSpec((tm, dm), lambda i: (i, 0))],
        out_shape=[jax.ShapeDtypeStruct((p, dm), BF16)],
        args=(h, g), sem=("parallel",), carry=carry)


def _embed_rmsnorm_fwd(x, meta, g, *, name, carry=None):
    seq, dm = x.shape
    p = seq + TB

    def body(x_ref, meta_ref, g_ref, h_ref, xn_ref):
        @pl.when(pl.program_id(0) == 0)
        def _():
            h_ref[...] = jnp.zeros_like(h_ref)
            h_ref[PAD_FRONT:TB, :] = meta_ref[...]

        @pl.when(pl.program_id(0) > 0)
        def _():
            h_ref[...] = x_ref[...]

        hv = h_ref[...]
        r = lax.rsqrt(jnp.mean(hv * hv, axis=-1, keepdims=True) + RMS_EPS)
        xn_ref[...] = (hv * r * g_ref[...]).astype(BF16)

    row = pl.BlockSpec((TB, dm), lambda i: (i, 0))
    return _call(
        body, name=name, grid=(p // TB,),
        in_specs=[pl.BlockSpec((TB, dm), lambda i: (jnp.maximum(i - 1, 0), 0)), pl.BlockSpec((N_META, dm), lambda i: (0, 0)),
                  pl.BlockSpec((1, dm), lambda i: (0, 0))],
        out_specs=[row, row],
        out_shape=[jax.ShapeDtypeStruct((p, dm), F32), jax.ShapeDtypeStruct((p, dm), BF16)],
        args=(x, meta, g), sem=("arbitrary",), carry=carry)


def _rmsnorm_bwd(dxn, h, g, dh_out, *, name, first_layer=False, carry=None):
    p, dm = h.shape
    tm = TB if first_layer else _row_tile(p, 384)

    def body(dxn_ref, h_ref, g_ref, dho_ref, out_ref, aux_ref, dg_ref):
        hv = h_ref[...]
        r = lax.rsqrt(jnp.mean(hv * hv, axis=-1, keepdims=True) + RMS_EPS)
        xh = hv * r
        dy = dxn_ref[...]
        dyn = dy * g_ref[...]
        dh = dho_ref[...] + r * (dyn - xh * jnp.mean(dyn * xh, axis=-1, keepdims=True))
        out_ref[...] = dh

        @pl.when(pl.program_id(0) == 0)
        def _():
            dg_ref[...] = jnp.zeros_like(dg_ref)
            if first_layer:
                aux_ref[...] = dh[PAD_FRONT:TB]

        if not first_layer:
            aux_ref[...] = dh.astype(BF16)
        dg_ref[...] += jnp.sum(dy * xh, axis=0, keepdims=True)

    row = pl.BlockSpec((tm, dm), lambda i: (i, 0))
    vec = pl.BlockSpec((1, dm), lambda i: (0, 0))
    if first_layer:
        out_specs = [pl.BlockSpec((TB, dm), lambda i: (jnp.maximum(i - 1, 0), 0)), pl.BlockSpec((N_META, dm), lambda i: (0, 0)), vec]
        out_shape = [jax.ShapeDtypeStruct((p - TB, dm), F32), jax.ShapeDtypeStruct((N_META, dm), F32)]
    else:
        out_specs = [row, row, vec]
        out_shape = [jax.ShapeDtypeStruct((p, dm), F32), jax.ShapeDtypeStruct((p, dm), BF16)]
    return _call(
        body, name=name, grid=(p // tm,),
        in_specs=[row, row, vec, row], out_specs=out_specs,
        out_shape=out_shape + [jax.ShapeDtypeStruct((1, dm), F32)],
        args=(dxn, h, g, dh_out), sem=("arbitrary",), carry=carry)


def _loss_grad(h, target, *, name):
    p, dm = h.shape

    def body(h_ref, t_ref, dh_ref, dhb_ref, sq_ref):
        n = pl.program_id(0)

        @pl.when(n == 0)
        def _():
            dh_ref[...] = jnp.zeros_like(dh_ref)
            dhb_ref[...] = jnp.zeros_like(dhb_ref)
            sq_ref[...] = jnp.zeros_like(sq_ref)

        @pl.when(n > 0)
        def _():
            err = h_ref[...] - t_ref[...]
            dh = err * (1.0 / dm)
            dh_ref[...] = dh
            dhb_ref[...] = dh.astype(BF16)
            sq_ref[...] += jnp.sum(err * err, axis=0, keepdims=True)

    row = pl.BlockSpec((TB, dm), lambda n: (n, 0))
    return pl.pallas_call(
        body, name=name, grid=(p // TB,),
        in_specs=[row, pl.BlockSpec((TB, dm), lambda n: (jnp.maximum(n - 1, 0), 0))],
        out_specs=[row, row, pl.BlockSpec((1, dm), lambda n: (0, 0))],
        out_shape=[jax.ShapeDtypeStruct((p, dm), F32), jax.ShapeDtypeStruct((p, dm), BF16), jax.ShapeDtypeStruct((1, dm), F32)],
        compiler_params=_params("arbitrary"),
    )(h, target)


def _chunk_masks():
    ri, ci = _iota((TB, TB), 0), _iota((TB, TB), 1)
    same = (ri >> 4) == (ci >> 4)
    causal = same & (ci <= ri)
    lower = jnp.where(causal, 1.0, 0.0).astype(BF16)
    upper = jnp.where(same & (ci >= ri), 1.0, 0.0).astype(BF16)
    ones = jnp.where(same, 1.0, 0.0).astype(BF16)
    return causal, lower, upper, ones


def _hgrn_gates(q, z, lbh, m):
    sig = _sigmoid(z)
    f = lbh + (1.0 - lbh) * sig
    lf = jnp.log(jnp.maximum(f, LOG_FLOOR)) * m
    kk = (1.0 - lbh) * (1.0 - sig) * m
    sq = _sigmoid(q)
    return sig, f, lf, kk, sq, q * sq


def _hgrn_fwd(proj, lb, *, name, carry=None):
    p = proj.shape[0]
    nb = p // TB

    def body(qa_ref, fa_ref, ia_ref, lb_ref, oa_ref, sck_ref, st_ref):
        n = pl.program_id(0)

        @pl.when(n == 0)
        def _():
            st_ref[...] = jnp.zeros_like(st_ref)

        causal, lower, _, ones = _chunk_masks()
        m = ((n * TB + _iota((TB, 1), 0)) >= PAD_FRONT).astype(F32)
        heads = range(HA)
        sls = [slice(hd * DK_A, (hd + 1) * DK_A) for hd in heads]
        rows = [slice(c * CH, (c + 1) * CH) for c in range(NCH)]
        gates = [_hgrn_gates(qa_ref[:, sl], fa_ref[:, sl], lb_ref[:, sl], m) for sl in sls]
        lf = [t[2] for t in gates]
        g = [_xdot(lower, x) for x in lf]
        gl = [_xdot(ones, x) for x in lf]
        qd = [(gates[hd][5] * jnp.exp(g[hd])).astype(BF16) for hd in heads]
        kt = [(gates[hd][3] * jnp.exp(-g[hd])).astype(BF16) for hd in heads]
        kd = [(gates[hd][3] * jnp.exp(gl[hd] - g[hd])).astype(BF16) for hd in heads]
        vb = [ia_ref[:, sl].astype(BF16) for sl in sls]
        a_all = [jnp.exp(x) for x in gl]
        att = [jnp.where(causal, _dot(qd[hd], kt[hd], NT), 0.0).astype(BF16) for hd in heads]
        kv = [[_dot(vb[hd][r], kd[hd][r], TN) for r in rows] for hd in heads]
        o = [_dot(att[hd], vb[hd], NN) for hd in heads]
        before = []
        for hd in heads:
            st = st_ref[hd]
            sck_ref[0, hd] = st
            per_chunk = []
            for c in range(NCH):
                per_chunk.append(st.astype(BF16))
                st = st * a_all[hd][c * CH:c * CH + 1, :] + kv[hd][c]
            st_ref[hd] = st
            before.append(per_chunk)
        inter = [[_dot(qd[hd][rows[c]], before[hd][c], NT) for c in range(NCH)] for hd in heads]
        for hd in heads:
            oa_ref[:, sls[hd]] = o[hd] + jnp.concatenate(inter[hd], axis=0)

    blk = lambda c: pl.BlockSpec((TB, 512), lambda n, c=c: (n, c))
    return _call(
        body, name=name, grid=(nb,),
        in_specs=[blk(C_QA), blk(C_FA), blk(C_IA), pl.BlockSpec((1, 512), lambda n: (0, 0))],
        out_specs=[pl.BlockSpec((TB, 512), lambda n: (n, 0)), pl.BlockSpec((1, HA, TB, TB), lambda n: (n, 0, 0, 0))],
        out_shape=[jax.ShapeDtypeStruct((p, 512), F32), jax.ShapeDtypeStruct((nb, HA, TB, TB), F32)],
        scratch_shapes=[pltpu.VMEM((HA, TB, TB), F32)],
        args=(proj, proj, proj, lb), sem=("arbitrary",), carry=carry)


def _hgrn_bwd(proj, lb, sck, d_oa, *, name, carry=None):
    p = proj.shape[0]
    nb = p // TB

    def body(qa_ref, fa_ref, ia_ref, lb_ref, sck_ref, do_ref, dq_ref, dz_ref, dv_ref, dlb_ref, dst_ref):
        i = pl.program_id(0)
        n = nb - 1 - i

        @pl.when(i == 0)
        def _():
            dst_ref[...] = jnp.zeros_like(dst_ref)
            dlb_ref[...] = jnp.zeros_like(dlb_ref)

        causal, lower, upper, ones = _chunk_masks()
        m = ((n * TB + _iota((TB, 1), 0)) >= PAD_FRONT).astype(F32)
        heads = range(HA)
        sls = [slice(hd * DK_A, (hd + 1) * DK_A) for hd in heads]
        rows = [slice(c * CH, (c + 1) * CH) for c in range(NCH)]
        a_row = lambda a, c: a[c * CH:c * CH + 1, :]
        gates = [_hgrn_gates(qa_ref[:, sl], fa_ref[:, sl], lb_ref[:, sl], m) for sl in sls]
        g = [_xdot(lower, t[2]) for t in gates]
        gl = [_xdot(ones, t[2]) for t in gates]
        e_g = [jnp.exp(x) for x in g]
        e_ng = [jnp.exp(-x) for x in g]
        e_d = [jnp.exp(gl[hd] - g[hd]) for hd in heads]
        a_all = [jnp.exp(x) for x in gl]
        qd_f = [gates[hd][5] * e_g[hd] for hd in heads]
        kt_f = [gates[hd][3] * e_ng[hd] for hd in heads]
        kd_f = [gates[hd][3] * e_d[hd] for hd in heads]
        qd, kt, kd = ([x.astype(BF16) for x in xs] for xs in (qd_f, kt_f, kd_f))
        vb = [ia_ref[:, sl].astype(BF16) for sl in sls]
        dob = [do_ref[:, sl].astype(BF16) for sl in sls]
        att = [jnp.where(causal, _dot(qd[hd], kt[hd], NT), 0.0).astype(BF16) for hd in heads]
        d_att = [jnp.where(causal, _dot(dob[hd], vb[hd], NT), 0.0).astype(BF16) for hd in heads]
        kv = [[_dot(vb[hd][r], kd[hd][r], TN) for r in rows] for hd in heads]
        dqk = [[_dot(dob[hd][r], qd[hd][r], TN) for r in rows] for hd in heads]
        d_v = [_dot(att[hd], dob[hd], TN) for hd in heads]
        d_qd = [_dot(d_att[hd], kt[hd], NN) for hd in heads]
        d_kt = [_dot(d_att[hd], qd[hd], TN) for hd in heads]
        stc, dsc = [], []
        for hd in heads:
            st, before = sck_ref[0, hd], []
            for c in range(NCH):
                before.append(st)
                if c + 1 < NCH:
                    st = st * a_row(a_all[hd], c) + kv[hd][c]
            dst, after = dst_ref[hd], [None] * NCH
            for c in range(NCH - 1, -1, -1):
                after[c] = dst
                dst = dst * a_row(a_all[hd], c) + dqk[hd][c]
            dst_ref[hd] = dst
            stc.append(before)
            dsc.append(after)
        dscb = [[x.astype(BF16) for x in dsc[hd]] for hd in heads]
        dvs = [[_dot(kd[hd][rows[c]], dscb[hd][c], NT) for c in range(NCH)] for hd in heads]
        dkd = [[_dot(vb[hd][rows[c]], dscb[hd][c], NN) for c in range(NCH)] for hd in heads]
        dqd = [[_dot(dob[hd][rows[c]], stc[hd][c].astype(BF16), NN) for c in range(NCH)] for hd in heads]
        dgl = [[jnp.broadcast_to(jnp.sum(dsc[hd][c] * stc[hd][c], axis=0, keepdims=True) * a_row(a_all[hd], c), (CH, TB))
                for c in range(NCH)] for hd in heads]
        d_qd = [d_qd[hd] + jnp.concatenate(dqd[hd], axis=0) for hd in heads]
        d_kd = [jnp.concatenate(dkd[hd], axis=0) for hd in heads]
        kd_term = [d_kd[hd] * kd_f[hd] for hd in heads]
        d_g = [d_qd[hd] * qd_f[hd] - d_kt[hd] * kt_f[hd] - kd_term[hd] for hd in heads]
        d_lf = [_xdot(upper, d_g[hd]) + _xdot(ones, kd_term[hd]) + jnp.concatenate(dgl[hd], axis=0) for hd in heads]
        for hd in heads:
            sl = sls[hd]
            sig, f, _, _, sq, _ = gates[hd]
            q, lbh = qa_ref[:, sl], lb_ref[:, sl]
            d_kk = (d_kt[hd] * e_ng[hd] + d_kd[hd] * e_d[hd]) * m
            t1 = d_lf[hd] * m * jnp.where(f > LOG_FLOOR, 1.0 / f, 0.0)
            dq_ref[:, sl] = (d_qd[hd] * e_g[hd] * (sq * (1.0 + q * (1.0 - sq)))).astype(BF16)
            dz_ref[:, sl] = ((t1 - d_kk) * (1.0 - lbh) * sig * (1.0 - sig)).astype(BF16)
            dv_ref[:, sl] = (d_v[hd] + jnp.concatenate(dvs[hd], axis=0)).astype(BF16)
            dlb_ref[:, sl] += jnp.sum((t1 - d_kk) * (1.0 - sig), axis=0, keepdims=True)

    blk = lambda c: pl.BlockSpec((TB, 512), lambda i, c=c: (nb - 1 - i, c))
    out_blk = pl.BlockSpec((TB, 512), lambda i: (nb - 1 - i, 0))
    vec = pl.BlockSpec((1, 512), lambda i: (0, 0))
    return _call(
        body, name=name, grid=(nb,),
        in_specs=[blk(C_QA), blk(C_FA), blk(C_IA), vec,
                  pl.BlockSpec((1, HA, TB, TB), lambda i: (nb - 1 - i, 0, 0, 0)), out_blk],
        out_specs=[out_blk, out_blk, out_blk, vec],
        out_shape=[jax.ShapeDtypeStruct((p, 512), BF16)] * 3 + [jax.ShapeDtypeStruct((1, 512), F32)],
        scratch_shapes=[pltpu.VMEM((HA, TB, TB), F32)],
        args=(proj, proj, proj, lb, sck, d_oa), sem=("arbitrary",), carry=carry)


def _lane():
    return _iota((1, TB), 1)


def _swap_halves(y):
    first = (_lane() & 63) < 32
    return jnp.where(first, pltpu.roll(y, 96, 1), pltpu.roll(y, 32, 1))


def _head_ones():
    ri, ci = _iota((TB, TB), 0), _iota((TB, TB), 1)
    return jnp.where((ri >> 6) == (ci >> 6), 1.0, 0.0).astype(BF16)


def _norm_rope(x, g, cos, sin, bd):
    r = lax.rsqrt(_xdot_r(x * x, bd) * (1.0 / DH) + RMS_EPS)
    y = x * r * g
    return y * cos + _swap_halves(y) * sin


def _norm_rope_bwd(d_out, x, g, cos, sin, bd):
    d = d_out * cos - _swap_halves(d_out) * sin
    r = lax.rsqrt(_xdot_r(x * x, bd) * (1.0 / DH) + RMS_EPS)
    xh = x * r
    dyn = d * g
    dx = r * (dyn - xh * (_xdot_r(dyn * xh, bd) * (1.0 / DH)))
    return dx, jnp.sum(d * xh, axis=0, keepdims=True)


def _dup_heads(k):
    first = _lane() < DH
    r = pltpu.roll(k, DH, 1)
    return jnp.where(first, k, r), jnp.where(first, r, k)


def _qk_prep(proj, gq, gk, cos, sin, *, name):
    p = proj.shape[0]

    def body(qb_ref, kb_ref, vb_ref, gq_ref, gk_ref, cos_ref, sin_ref, qh_ref, k2_ref, v2_ref):
        bd = _head_ones()
        cos_v, sin_v = cos_ref[...], sin_ref[...]
        for j in range(HB // 2):
            sl = slice(j * TB, (j + 1) * TB)
            qh_ref[:, sl] = _norm_rope(qb_ref[:, sl], gq_ref[...], cos_v, sin_v, bd).astype(BF16)
        k0, k1 = _dup_heads(_norm_rope(kb_ref[...], gk_ref[...], cos_v, sin_v, bd))
        k2_ref[:, 0:TB] = k0.astype(BF16)
        k2_ref[:, TB:2 * TB] = k1.astype(BF16)
        v0, v1 = _dup_heads(vb_ref[...])
        v2_ref[:, 0:TB] = v0.astype(BF16)
        v2_ref[:, TB:2 * TB] = v1.astype(BF16)

    vec = pl.BlockSpec((1, TB), lambda n: (0, 0))
    tab = pl.BlockSpec((TB, TB), lambda n: (n, 0))
    return pl.pallas_call(
        body, name=name, grid=(p // TB,),
        in_specs=[pl.BlockSpec((TB, 1024), lambda n: (n, C_QB)), pl.BlockSpec((TB, TB), lambda n: (n, C_KB)),
                  pl.BlockSpec((TB, TB), lambda n: (n, C_VB)), vec, vec, tab, tab],
        out_specs=[pl.BlockSpec((TB, 1024), lambda n: (n, 0)), pl.BlockSpec((TB, 256), lambda n: (n, 0)),
                   pl.BlockSpec((TB, 256), lambda n: (n, 0))],
        out_shape=[jax.ShapeDtypeStruct((p, 1024), BF16), jax.ShapeDtypeStruct((p, 256), BF16),
                   jax.ShapeDtypeStruct((p, 256), BF16)],
        compiler_params=_params("parallel"),
    )(proj, proj, proj, gq, gk, cos, sin)


NKEY = N_META + 2 * TB


def _attn_mask(n):
    r = _iota((TB, NKEY), 0)
    j = _iota((TB, NKEY), 1)
    meta = (j < N_META) & ((n >= 1) | (j + PAD_FRONT <= r))
    prev = (j >= N_META) & (j < N_META + TB) & (n >= 2) & (j - N_META > r)
    cur = (j >= N_META + TB) & (n >= 1) & (j - (N_META + TB) <= r)
    return meta | prev | cur


def _attn_specs():
    cur = lambda w: pl.BlockSpec((TB, w), lambda n: (n, 0))
    prev = pl.BlockSpec((TB, 256), lambda n: (jnp.maximum(n - 1, 0), 0))
    meta = pl.BlockSpec((N_META, 256), lambda n: (PAD_FRONT // N_META, 0))
    sink = pl.BlockSpec(memory_space=pltpu.SMEM)
    return cur, prev, meta, sink


ATTN_GROUP = 4


def _head_queries(q_ref, kv, first):
    out = []
    for jj in range(GRP // 2):
        j = kv * (GRP // 2) + jj
        qj = q_ref[:, j * TB:(j + 1) * TB] * (DH ** -0.5)
        for half in range(2):
            out.append((j, half, jnp.where(first if half == 0 else ~first, qj, jnp.zeros_like(qj))))
    return out


def _attn_fwd(qh, k2, v2, sinks, *, name, carry=None):
    p = qh.shape[0]

    def body(sink_ref, q_ref, kc_ref, kp_ref, km_ref, vc_ref, vp_ref, vm_ref, o_ref, lse_ref):
        n = pl.program_id(0)
        mask = _attn_mask(n)
        lane = _lane()
        first = lane < DH
        lse_tile = jnp.zeros((TB, TB), F32)
        for kv in range(KVH):
            ks = slice(kv * TB, (kv + 1) * TB)
            kall = jnp.concatenate([km_ref[:, ks], kp_ref[:, ks], kc_ref[:, ks]], axis=0)
            vall = jnp.concatenate([vm_ref[:, ks], vp_ref[:, ks], vc_ref[:, ks]], axis=0)
            hq = _head_queries(q_ref, kv, first)
            for g0 in range(0, GRP, ATTN_GROUP):
                grp = hq[g0:g0 + ATTN_GROUP]
                idx = range(len(grp))
                s = [jnp.where(mask, _dot(qm, kall, NT), NEG_INF) for _, _, qm in grp]
                sink = [sink_ref[2 * j + half] for j, half, _ in grp]
                mx = [jnp.maximum(jnp.max(s[i], axis=1, keepdims=True), sink[i]) for i in idx]
                pr = [jnp.exp(s[i] - mx[i]) for i in idx]
                den = [jnp.sum(pr[i], axis=1, keepdims=True) + jnp.exp(sink[i] - mx[i]) for i in idx]
                o = [_dot(pr[i].astype(BF16), vall, NN) * (1.0 / den[i]) for i in idx]
                for i, (j, half, _) in enumerate(grp):
                    lse_tile = lse_tile + jnp.where(lane == 2 * j + half, mx[i] + jnp.log(den[i]), 0.0)
                for i in range(0, len(grp), 2):
                    j = grp[i][0]
                    o_ref[:, j * TB:(j + 1) * TB] = jnp.where(first, o[i], o[i + 1])
        lse_ref[...] = lse_tile

    cur, prev, meta, sink = _attn_specs()
    return _call(
        body, name=name, grid=(p // TB,),
        in_specs=[sink, cur(1024), cur(256), prev, meta, cur(256), prev, meta],
        out_specs=[cur(1024), cur(TB)],
        out_shape=[jax.ShapeDtypeStruct((p, 1024), F32), jax.ShapeDtypeStruct((p, TB), F32)],
        args=(sinks, qh, k2, k2, k2, v2, v2, v2), sem=("parallel",), carry=carry)


def _attn_bwd(qh, k2, v2, sinks, o, lse, d_o, *, name, carry=None):
    p = qh.shape[0]

    def body(sink_ref, q_ref, kc_ref, kp_ref, km_ref, vc_ref, vp_ref, vm_ref, o_ref, lse_ref, do_ref,
             dq_ref, dkc_ref, dkp_ref, dvc_ref, dvp_ref, dkm_ref, dvm_ref, dsink_ref):
        n = pl.program_id(0)

        @pl.when(n == 0)
        def _():
            dkm_ref[...] = jnp.zeros_like(dkm_ref)
            dvm_ref[...] = jnp.zeros_like(dvm_ref)
            dsink_ref[...] = jnp.zeros_like(dsink_ref)

        mask = _attn_mask(n)
        lane = _lane()
        first = lane < DH
        lse_tile = lse_ref[...]
        sink_acc = jnp.zeros((TB, TB), F32)
        for kv in range(KVH):
            ks = slice(kv * TB, (kv + 1) * TB)
            kall = jnp.concatenate([km_ref[:, ks], kp_ref[:, ks], kc_ref[:, ks]], axis=0)
            vall = jnp.concatenate([vm_ref[:, ks], vp_ref[:, ks], vc_ref[:, ks]], axis=0)
            d_kall = jnp.zeros((NKEY, TB), F32)
            d_vall = jnp.zeros((NKEY, TB), F32)
            hq = _head_queries(q_ref, kv, first)
            for g0 in range(0, GRP, ATTN_GROUP):
                grp = hq[g0:g0 + ATTN_GROUP]
                idx = range(len(grp))
                s = [jnp.where(mask, _dot(qm, kall, NT), NEG_INF) for _, _, qm in grp]
                dom = [jnp.where(first if half == 0 else ~first, do_ref[:, j * TB:(j + 1) * TB], 0.0) for j, half, _ in grp]
                domb = [x.astype(BF16) for x in dom]
                d_w = [_dot(x, vall, NT) for x in domb]
                delta = [jnp.sum(dom[i] * o_ref[:, grp[i][0] * TB:(grp[i][0] + 1) * TB], axis=1, keepdims=True) for i in idx]
                lse_h = [jnp.sum(jnp.where(lane == 2 * j + half, lse_tile, 0.0), axis=1, keepdims=True) for j, half, _ in grp]
                w = [jnp.exp(s[i] - lse_h[i]) for i in idx]
                for i, (j, half, _) in enumerate(grp):
                    w_sink = jnp.exp(sink_ref[2 * j + half] - lse_h[i])
                    sink_acc = sink_acc + jnp.where(lane == 2 * j + half, -(w_sink * delta[i]), 0.0)
                dsb = [(w[i] * (d_w[i] - delta[i])).astype(BF16) for i in idx]
                d_q = [_dot(x, kall, NN) * (DH ** -0.5) for x in dsb]
                d_k = [_dot(dsb[i], grp[i][2], TN) for i in idx]
                d_v = [_dot(w[i].astype(BF16), domb[i], TN) for i in idx]
                for i in idx:
                    d_kall = d_kall + d_k[i]
                    d_vall = d_vall + d_v[i]
                for i in range(0, len(grp), 2):
                    j = grp[i][0]
                    dq_ref[:, j * TB:(j + 1) * TB] = jnp.where(first, d_q[i], d_q[i + 1])
            dkm_ref[:, ks] += d_kall[0:N_META]
            dkp_ref[:, ks] = d_kall[N_META:N_META + TB]
            dkc_ref[:, ks] = d_kall[N_META + TB:NKEY]
            dvm_ref[:, ks] += d_vall[0:N_META]
            dvp_ref[:, ks] = d_vall[N_META:N_META + TB]
            dvc_ref[:, ks] = d_vall[N_META + TB:NKEY]
        dsink_ref[...] += jnp.sum(sink_acc, axis=0, keepdims=True)

    cur, prev, meta, sink = _attn_specs()
    acc = lambda r: pl.BlockSpec((r, 256), lambda n: (0, 0))
    return _call(
        body, name=name, grid=(p // TB,),
        in_specs=[sink, cur(1024), cur(256), prev, meta, cur(256), prev, meta, cur(1024), cur(TB), cur(1024)],
        out_specs=[cur(1024), cur(256), cur(256), cur(256), cur(256), acc(N_META), acc(N_META),
                   pl.BlockSpec((1, TB), lambda n: (0, 0))],
        out_shape=[jax.ShapeDtypeStruct((p, 1024), F32)] + [jax.ShapeDtypeStruct((p, 256), F32)] * 4
        + [jax.ShapeDtypeStruct((N_META, 256), F32)] * 2 + [jax.ShapeDtypeStruct((1, TB), F32)],
        args=(sinks, qh, k2, k2, k2, v2, v2, v2, o, lse, d_o), sem=("arbitrary",), carry=carry)


def _qk_post(proj, gq, gk, cos, sin, dqh, dkc, dkp, dkm, dvc, dvp, dvm, *, name):
    p = proj.shape[0]
    nb = p // TB

    def body(qb_ref, kb_ref, gq_ref, gk_ref, cos_ref, sin_ref, dqh_ref, dkc_ref, dkp_ref, dkm_ref,
             dvc_ref, dvp_ref, dvm_ref, dqb_ref, dkb_ref, dvb_ref, dgq_ref, dgk_ref, tk_ref, tv_ref):
        n = pl.program_id(0)

        @pl.when(n == 0)
        def _():
            dgq_ref[...] = jnp.zeros_like(dgq_ref)
            dgk_ref[...] = jnp.zeros_like(dgk_ref)

        keep = jnp.where(n == nb - 1, 0.0, 1.0)
        tk_ref[...] = dkc_ref[...] + keep * dkp_ref[...]
        tv_ref[...] = dvc_ref[...] + keep * dvp_ref[...]

        @pl.when(n == 0)
        def _():
            tk_ref[PAD_FRONT:TB, :] += dkm_ref[...]
            tv_ref[PAD_FRONT:TB, :] += dvm_ref[...]

        first = _lane() < DH

        def fold(t_ref):
            t0, t1 = t_ref[:, 0:TB], t_ref[:, TB:2 * TB]
            return jnp.where(first, t0 + pltpu.roll(t0, DH, 1), t1 + pltpu.roll(t1, DH, 1))

        bd = _head_ones()
        cos_v, sin_v = cos_ref[...], sin_ref[...]
        dvb_ref[...] = fold(tv_ref).astype(BF16)
        dkb, dgk = _norm_rope_bwd(fold(tk_ref), kb_ref[...], gk_ref[...], cos_v, sin_v, bd)
        dkb_ref[...] = dkb.astype(BF16)
        dgk_ref[...] += dgk
        dgq = jnp.zeros((1, TB), F32)
        for j in range(HB // 2):
            sl = slice(j * TB, (j + 1) * TB)
            dqb, dg = _norm_rope_bwd(dqh_ref[:, sl], qb_ref[:, sl], gq_ref[...], cos_v, sin_v, bd)
            dqb_ref[:, sl] = dqb.astype(BF16)
            dgq = dgq + dg
        dgq_ref[...] += dgq

    vec = pl.BlockSpec((1, TB), lambda n: (0, 0))
    tab = pl.BlockSpec((TB, TB), lambda n: (n, 0))
    cur = lambda w: pl.BlockSpec((TB, w), lambda n: (n, 0))
    nxt = pl.BlockSpec((TB, 256), lambda n: (jnp.minimum(n + 1, nb - 1), 0))
    meta = pl.BlockSpec((N_META, 256), lambda n: (0, 0))
    return pl.pallas_call(
        body, name=name, grid=(nb,),
        in_specs=[pl.BlockSpec((TB, 1024), lambda n: (n, C_QB)), pl.BlockSpec((TB, TB), lambda n: (n, C_KB)),
                  vec, vec, tab, tab, cur(1024), cur(256), nxt, meta, cur(256), nxt, meta],
        out_specs=[cur(1024), cur(TB), cur(TB), vec, vec],
        out_shape=[jax.ShapeDtypeStruct((p, 1024), BF16), jax.ShapeDtypeStruct((p, TB), BF16),
                   jax.ShapeDtypeStruct((p, TB), BF16), jax.ShapeDtypeStruct((1, TB), F32),
                   jax.ShapeDtypeStruct((1, TB), F32)],
        scratch_shapes=[pltpu.VMEM((TB, 256), F32), pltpu.VMEM((TB, 256), F32)],
        compiler_params=_params("arbitrary"),
    )(proj, proj, gq, gk, cos, sin, dqh, dkc, dkp, dkm, dvc, dvp, dvm)


EXT = TB + N_META


def _pool_count_inv(n, w):
    t = n * TB + _iota((TB, 1), 0)
    cnt = jnp.clip(t - (PAD_FRONT - 1), 1, w)
    return 1.0 / cnt.astype(F32)


def _silu_parts(gate):
    s = _sigmoid(gate)
    return gate * s, s * (1.0 + gate * (1.0 - s))


def _mix_fwd(proj, oa, yb, hg, pool_w, pool_scale, *, name):
    p = proj.shape[0]

    def body(ga_ref, gb0, gb1, gb2, gb3, uc0, uc1, up0, up1, gc0, gc1, oa_ref, yb_ref, hg_ref, pw_ref, ps_ref,
             mx_ref, pooled_ref):
        gb_ref, uc_ref, up_ref, gc_ref = _Cols([gb0, gb1, gb2, gb3]), _Cols([uc0, uc1]), _Cols([up0, up1]), _Cols([gc0, gc1])
        n = pl.program_id(0)
        valid = ((n * TB + _iota((TB, 1), 0)) >= PAD_FRONT).astype(F32)
        for hd in range(HA):
            sl = slice(hd * TB, (hd + 1) * TB)
            o = oa_ref[:, sl]
            r = lax.rsqrt(jnp.mean(o * o, axis=-1, keepdims=True) + RMS_EPS)
            act, _ = _silu_parts(ga_ref[:, sl])
            mx_ref[:, sl] = (o * r * hg_ref[...] * act).astype(BF16)
        for j in range(HB // 2):
            sl = slice(j * TB, (j + 1) * TB)
            act, _ = _silu_parts(gb_ref[:, sl])
            mx_ref[:, 512 + j * TB:512 + (j + 1) * TB] = (yb_ref[:, sl] * act).astype(BF16)
        ri, ci = _iota((TB, EXT), 0), _iota((TB, EXT), 1)
        has_prev = jnp.where(n == 0, 0.0, 1.0)
        for gi, w in enumerate(POOL_WINDOWS):
            sl = slice(gi * TB, (gi + 1) * TB)
            ug = uc_ref[:, sl] * valid
            ext = jnp.concatenate([up_ref[:, sl] * has_prev, ug], axis=0)
            band = jnp.where((ci <= ri + N_META) & (ci > ri + N_META - w), 1.0, 0.0).astype(BF16)
            pooled = (_xdot(band, ext) * _pool_count_inv(n, w) - ug) * valid
            pooled_ref[:, sl] = pooled
            yc = _dot(pooled.astype(BF16), pw_ref[gi], NN) * ps_ref[:, sl]
            act, _ = _silu_parts(gc_ref[:, sl])
            mx_ref[:, 1536 + gi * TB:1536 + (gi + 1) * TB] = (yc * act).astype(BF16)

    cur = lambda w, c=0: pl.BlockSpec((TB, w), lambda n, c=c: (n, c))
    prev16 = lambda c: pl.BlockSpec((N_META, 256), lambda n, c=c: (jnp.maximum(n * (TB // N_META) - 1, 0), c))
    return pl.pallas_call(
        body, name=name, grid=(p // TB,),
        in_specs=[cur(512, C_GA)] + [cur(256, c) for c in C_GB] + [cur(256, c) for c in C_UC]
        + [prev16(c) for c in C_UC] + [cur(256, c) for c in C_GC]
        + [cur(512), cur(1024), pl.BlockSpec((1, TB), lambda n: (0, 0)), pl.BlockSpec((4, TB, TB), lambda n: (0, 0, 0)),
           pl.BlockSpec((1, 512), lambda n: (0, 0))],
        out_specs=[cur(MIX), cur(512)],
        out_shape=[jax.ShapeDtypeStruct((p, MIX), BF16), jax.ShapeDtypeStruct((p, 512), F32)],
        compiler_params=_params("parallel"),
    )(*([proj] * 11), oa, yb, hg, pool_w, pool_scale)


def _mix_bwd(proj, oa, yb, pooled, hg, pool_w, pool_scale, d_mixed, *, name, carry=None):
    p = proj.shape[0]

    def body(ga_ref, gb0, gb1, gb2, gb3, gc0, gc1, oa_ref, yb_ref, pooled_ref, hg_ref, pw_ref, ps_ref, dm_ref,
             doa_ref, dyb_ref, dga_ref, dgb_ref, dgc_ref, dp_ref, dhg_ref, dpw_ref, dps_ref):
        gb_ref, gc_ref = _Cols([gb0, gb1, gb2, gb3]), _Cols([gc0, gc1])
        n = pl.program_id(0)

        @pl.when(n == 0)
        def _():
            dhg_ref[...] = jnp.zeros_like(dhg_ref)
            dpw_ref[...] = jnp.zeros_like(dpw_ref)
            dps_ref[...] = jnp.zeros_like(dps_ref)

        valid = ((n * TB + _iota((TB, 1), 0)) >= PAD_FRONT).astype(F32)
        dhg = jnp.zeros((1, TB), F32)
        for hd in range(HA):
            sl = slice(hd * TB, (hd + 1) * TB)
            o = oa_ref[:, sl]
            r = lax.rsqrt(jnp.mean(o * o, axis=-1, keepdims=True) + RMS_EPS)
            on = o * r
            gate = ga_ref[:, sl]
            act, dact = _silu_parts(gate)
            dmx = dm_ref[:, sl]
            d_ya = dmx * act
            dga_ref[:, sl] = (dmx * on * hg_ref[...] * dact).astype(BF16)
            dyn = d_ya * hg_ref[...]
            doa_ref[:, sl] = r * (dyn - on * jnp.mean(dyn * on, axis=-1, keepdims=True))
            dhg = dhg + jnp.sum(d_ya * on, axis=0, keepdims=True)
        dhg_ref[...] += dhg
        for j in range(HB // 2):
            sl = slice(j * TB, (j + 1) * TB)
            act, dact = _silu_parts(gb_ref[:, sl])
            dmx = dm_ref[:, 512 + j * TB:512 + (j + 1) * TB]
            dyb_ref[:, sl] = dmx * act
            dgb_ref[:, sl] = (dmx * yb_ref[:, sl] * dact).astype(BF16)
        for gi in range(len(POOL_WINDOWS)):
            sl = slice(gi * TB, (gi + 1) * TB)
            pooled = pooled_ref[:, sl]
            pooled_b = pooled.astype(BF16)
            t = _dot(pooled_b, pw_ref[gi], NN)
            act, dact = _silu_parts(gc_ref[:, sl])
            dmx = dm_ref[:, 1536 + gi * TB:1536 + (gi + 1) * TB]
            d_yc = dmx * act
            dgc_ref[:, sl] = (dmx * t * ps_ref[:, sl] * dact).astype(BF16)
            dps_ref[:, sl] += jnp.sum(d_yc * t, axis=0, keepdims=True)
            d_t = (d_yc * ps_ref[:, sl]).astype(BF16)
            dp_ref[:, sl] = _dot(d_t, pw_ref[gi], NT) * valid
            dpw_ref[gi] += _dot(pooled_b, d_t, TN)

    cur = lambda w, c=0: pl.BlockSpec((TB, w), lambda n, c=c: (n, c))
    return _call(
        body, name=name, grid=(p // TB,),
        in_specs=[cur(512, C_GA)] + [cur(256, c) for c in C_GB] + [cur(256, c) for c in C_GC]
        + [cur(512), cur(1024), cur(512), pl.BlockSpec((1, TB), lambda n: (0, 0)),
           pl.BlockSpec((4, TB, TB), lambda n: (0, 0, 0)), pl.BlockSpec((1, 512), lambda n: (0, 0)), cur(MIX)],
        out_specs=[cur(512), cur(1024), cur(512), cur(1024), cur(512), cur(512),
                   pl.BlockSpec((1, TB), lambda n: (0, 0)), pl.BlockSpec((4, TB, TB), lambda n: (0, 0, 0)),
                   pl.BlockSpec((1, 512), lambda n: (0, 0))],
        out_shape=[jax.ShapeDtypeStruct((p, 512), F32), jax.ShapeDtypeStruct((p, 1024), F32),
                   jax.ShapeDtypeStruct((p, 512), BF16), jax.ShapeDtypeStruct((p, 1024), BF16),
                   jax.ShapeDtypeStruct((p, 512), BF16), jax.ShapeDtypeStruct((p, 512), F32),
                   jax.ShapeDtypeStruct((1, TB), F32), jax.ShapeDtypeStruct((4, TB, TB), F32),
                   jax.ShapeDtypeStruct((1, 512), F32)],
        args=(*([proj] * 7), oa, yb, pooled, hg, pool_w, pool_scale, d_mixed), sem=("arbitrary",), carry=carry)


def _pool_bwd(dp, *, name):
    p = dp.shape[0]
    nb = p // TB

    def body(dp_ref, dn_ref, duc_ref):
        n = pl.program_id(0)
        valid = ((n * TB + _iota((TB, 1), 0)) >= PAD_FRONT).astype(F32)
        has_next = jnp.where(n == nb - 1, 0.0, 1.0)
        ri, ci = _iota((TB, EXT), 0), _iota((TB, EXT), 1)
        for gi, w in enumerate(POOL_WINDOWS):
            sl = slice(gi * TB, (gi + 1) * TB)
            d_p = dp_ref[:, sl]
            ext = jnp.concatenate([d_p * _pool_count_inv(n, w), dn_ref[:, sl] * (has_next / w)], axis=0)
            band = jnp.where((ci >= ri) & (ci < ri + w), 1.0, 0.0).astype(BF16)
            duc_ref[:, sl] = ((_xdot(band, ext) - d_p) * valid).astype(BF16)

    return pl.pallas_call(
        body, name=name, grid=(nb,),
        in_specs=[pl.BlockSpec((TB, 512), lambda n: (n, 0)),
                  pl.BlockSpec((N_META, 512), lambda n: (jnp.minimum(n + 1, nb - 1) * (TB // N_META), 0))],
        out_specs=pl.BlockSpec((TB, 512), lambda n: (n, 0)),
        out_shape=jax.ShapeDtypeStruct((p, 512), BF16),
        compiler_params=_params("parallel"),
    )(dp, dp)


def _carried(carries, key, local, fn, *args, **kw):
    if key not in carries:
        return fn(*args, **kw)
    make_src, gather, done = carries[key]
    *outs, stack = fn(*args, carry=(make_src(local), gather), **kw)
    done(stack)
    return outs


def _layer_fwd(h, w, tag, carries, xn=None):
    if xn is None:
        xn, = _rmsnorm_fwd(h, w["norm_g"], name=f"rmsnorm_fwd{tag}")
    proj, = _carried(carries, "in_proj", None, _mm_nn, xn, w["w_in"], name=f"in_proj{tag}")
    oa, sck = _carried(carries, "hgrn_fwd", None, _hgrn_fwd, proj, w["lb"], name=f"hgrn_fwd{tag}")
    qh, k2, v2 = _qk_prep(proj, w["gq"], w["gk"], w["cos"], w["sin"], name=f"qk_prep{tag}")
    yb, lse = _carried(carries, "attn_fwd", None, _attn_fwd, qh, k2, v2, w["sinks"], name=f"attn_fwd{tag}")
    mixed, pooled = _mix_fwd(proj, oa, yb, w["hg"], w["pool_w"], w["pool_scale"], name=f"mix_fwd{tag}")
    h_next, = _mm_nn(mixed, w["w_out"], h, name=f"out_proj{tag}", tn=512)
    saved = dict(h=h, xn=xn, proj=proj, oa=oa, sck=sck, qh=qh, k2=k2, v2=v2, yb=yb, lse=lse, mixed=mixed, pooled=pooled)
    return h_next, saved


def _layer_bwd(dh_out, dhb, s, w, tag, carries, first_layer=False):
    g = {}
    d_mixed, = _carried(carries, "d_mixed", g, _mm_nt, dhb, w["w_out"], name=f"d_mixed{tag}", tm=1664, tn=512)
    g["w_out"], = _mm_tn(s["mixed"], dhb, name=f"dw_out{tag}", tn=1024)
    d_oa, d_yb, d_ga, d_gb, d_gc, d_p, g["hg"], g["pool_w"], g["pool_scale"] = _carried(
        carries, "mix_bwd", g, _mix_bwd,
        s["proj"], s["oa"], s["yb"], s["pooled"], w["hg"], w["pool_w"], w["pool_scale"], d_mixed, name=f"mix_bwd{tag}")
    d_uc = _pool_bwd(d_p, name=f"pool_bwd{tag}")
    d_qh, dkc, dkp, dvc, dvp, dkm, dvm, g["sinks"] = _carried(
        carries, "attn_bwd", g, _attn_bwd, s["qh"], s["k2"], s["v2"], w["sinks"], s["yb"], s["lse"], d_yb,
        name=f"attn_bwd{tag}")
    d_qb, d_kb, d_vb, g["gq"], g["gk"] = _qk_post(s["proj"], w["gq"], w["gk"], w["cos"], w["sin"], d_qh, dkc, dkp, dkm,
                                                  dvc, dvp, dvm, name=f"qk_post{tag}")
    d_qa, d_fa, d_ia, g["lb"] = _carried(carries, "hgrn_bwd", g, _hgrn_bwd, s["proj"], w["lb"], s["sck"], d_oa,
                                         name=f"hgrn_bwd{tag}")
    d_proj = jnp.concatenate([d_qa, d_fa, d_ia, d_ga, d_qb, d_kb, d_vb, d_gb, d_uc, d_gc], axis=1)
    g["w_in_parts"] = []
    for i in range(W_IN_PARTS):
        part, = _carried(carries, f"dw_in_{i}", g, _mm_tn, s["xn"], d_proj, name=f"dw_in_{i}{tag}",
                         m_part=(i, W_IN_PARTS))
        g["w_in_parts"].append(part)
    d_xn, = _carried(carries, "d_xn", g, _mm_nt, d_proj, w["w_in"], name=f"d_xn{tag}", tk=PROJ_COLS)
    out, aux, g["norm_g"] = _rmsnorm_bwd(d_xn, s["h"], w["norm_g"], dh_out, name=f"rmsnorm_bwd{tag}", first_layer=first_layer)
    return out, aux, g


def _peers():
    x, y, c = lax.axis_index("x"), lax.axis_index("y"), lax.axis_index("c")
    out = []
    for k in range(1, N_DEV):
        kx, ky, kc = (k >> 2) & 1, (k >> 1) & 1, k & 1
        px, py, pc = x ^ kx, y ^ ky, c ^ kc
        out.append(((px, py, pc), 4 * px + 2 * py + pc))
    return 4 * x + 2 * y + c, out


def _exchange_copies(src_ref, out_ref, send_sems, recv_sems, local_sem):
    me, peers = _peers()
    mine = pltpu.make_async_copy(src_ref.at[me], out_ref.at[me], local_sem)
    copies = []
    for k, (dev, idx) in enumerate(peers):
        copies.append(pltpu.make_async_remote_copy(
            src_ref=src_ref.at[idx], dst_ref=out_ref.at[me],
            send_sem=send_sems.at[k], recv_sem=recv_sems.at[k],
            device_id=dev, device_id_type=pl.DeviceIdType.MESH))
    return mine, copies


def _gather_copies(src_ref, out_ref, send_sems, recv_sems, local_sem):
    x, y, c = lax.axis_index("x"), lax.axis_index("y"), lax.axis_index("c")
    slot = lambda px, py, pc: out_ref.at[4 * px + 2 * py + pc]
    sibling = (x, y, 1 - c)
    chips = [(1 - x, y), (x, 1 - y), (1 - x, 1 - y)]

    def copy(k, src, block, to):
        return pltpu.make_async_remote_copy(src_ref=src, dst_ref=slot(*block), send_sem=send_sems.at[k],
                                            recv_sem=recv_sems.at[k], device_id=to, device_id_type=pl.DeviceIdType.MESH)

    mine = lambda: pltpu.make_async_copy(src_ref, slot(x, y, c), local_sem)
    own = lambda: ([copy(0, src_ref, (x, y, c), sibling)]
                   + [copy(1 + j, src_ref, (x, y, c), (*chip, c)) for j, chip in enumerate(chips)])
    passing = lambda: [copy(4 + j, slot(*chip, c), (*chip, c), sibling) for j, chip in enumerate(chips)]
    arrivals = lambda: ([copy(0, src_ref, sibling, sibling)]
                        + [copy(1 + j, src_ref, (*chip, c), sibling) for j, chip in enumerate(chips)]
                        + [copy(4 + j, src_ref, (*chip, 1 - c), sibling) for j, chip in enumerate(chips)])
    return mine, own, passing, arrivals


def _exchange_start(*refs, gather):
    if gather:
        mine, own, _, _ = _gather_copies(*refs)
        mine().start()
        for cp in own():
            cp.start()
        return
    mine, copies = _exchange_copies(*refs)
    mine.start()
    for cp in copies:
        cp.start()


def _exchange_wait(*refs, gather):
    if gather:
        mine, own, passing, arrivals = _gather_copies(*refs)
        passing, arrivals = passing(), arrivals()
        for j, cp in enumerate(passing):
            arrivals[1 + j].wait_recv()
            cp.start()
        arrivals[0].wait_recv()
        for cp in arrivals[4:]:
            cp.wait_recv()
        for cp in own() + passing:
            cp.wait_send()
        mine().wait()
        return
    mine, copies = _exchange_copies(*refs)
    for cp in copies:
        cp.wait_recv()
    for cp in copies:
        cp.wait_send()
    mine.wait()


def _exchange_scratch():
    return [pltpu.SemaphoreType.DMA((N_DEV - 1,)), pltpu.SemaphoreType.DMA((N_DEV - 1,)), pltpu.SemaphoreType.DMA]


def _exchange(src, *, gather, name):
    rows, cols = src.shape[-2:]

    def body(src_ref, out_ref, send_sems, recv_sems, local_sem):
        _exchange_start(src_ref, out_ref, send_sems, recv_sems, local_sem, gather=gather)
        _exchange_wait(src_ref, out_ref, send_sems, recv_sems, local_sem, gather=gather)

    return pl.pallas_call(
        body, name=name,
        in_specs=[pl.BlockSpec(memory_space=pl.ANY)], out_specs=pl.BlockSpec(memory_space=pl.ANY),
        out_shape=jax.ShapeDtypeStruct((N_DEV, rows, cols), src.dtype),
        scratch_shapes=_exchange_scratch(),
    )(src)


def _call(body, *, name, grid, in_specs, out_specs, out_shape, args, sem, scratch_shapes=(), carry=None):
    if carry is None:
        return pl.pallas_call(
            body, name=name, grid=grid, in_specs=list(in_specs), out_specs=list(out_specs), out_shape=list(out_shape),
            scratch_shapes=list(scratch_shapes), compiler_params=_params(*sem))(*args)
    src, gather = carry
    n_in, n_out, n_scr = len(in_specs), len(out_specs), len(scratch_shapes)
    rows, cols = src.shape[-2:]

    def carrying(*refs):
        ins, src_ref = refs[:n_in], refs[n_in]
        outs, dst_ref = refs[n_in + 1:n_in + 1 + n_out], refs[n_in + 1 + n_out]
        scr = refs[n_in + 2 + n_out:]
        exch = (src_ref, dst_ref) + tuple(scr[n_scr:])
        first, last = None, None
        for a, size in enumerate(grid):
            f, l = pl.program_id(a) == 0, pl.program_id(a) == size - 1
            first = f if first is None else first & f
            last = l if last is None else last & l

        @pl.when(first)
        def _():
            _exchange_start(*exch, gather=gather)

        body(*ins, *outs, *scr[:n_scr])

        @pl.when(last)
        def _():
            _exchange_wait(*exch, gather=gather)

    hbm = pl.BlockSpec(memory_space=pl.ANY)
    return pl.pallas_call(
        carrying, name=name, grid=grid, in_specs=list(in_specs) + [hbm], out_specs=list(out_specs) + [hbm],
        out_shape=list(out_shape) + [jax.ShapeDtypeStruct((N_DEV, rows, cols), src.dtype)],
        scratch_shapes=list(scratch_shapes) + _exchange_scratch(),
        compiler_params=_params(*(("arbitrary",) * len(grid))))(*args, src)


def _adamw(stacks, w, m, v, *, name, carry=None):
    nl = len(stacks)
    rows, cols = stacks[0].shape[1:]
    tr = rows
    stack_block_bytes = 8 * 1024 * 1024 // nl
    for cand in (256, 128, 64, 32, 16):
        if rows % cand == 0 and N_DEV * cand * cols * stacks[0].dtype.itemsize <= stack_block_bytes:
            tr = cand
            break
    nt = rows // tr

    def body(*refs):
        s_refs = refs[:nl]
        w_ref, m_ref, v_ref, g_ref, d_ref, nm_ref, nv_ref = refs[nl:]
        for l, s_ref in enumerate(s_refs):
            @pl.when(pl.program_id(0) == l)
            def _(s_ref=s_ref):
                acc = s_ref[0].astype(F32)
                for d in range(1, N_DEV):
                    acc = acc + s_ref[d].astype(F32)
                g_ref[...] = acc

        g = g_ref[...]
        nm = ADAM_B1 * m_ref[...] + (1.0 - ADAM_B1) * g
        nv = ADAM_B2 * v_ref[...] + (1.0 - ADAM_B2) * (g * g)
        m_hat = nm / (1.0 - ADAM_B1 ** ADAM_STEP)
        v_hat = nv / (1.0 - ADAM_B2 ** ADAM_STEP)
        d_ref[...] = -ADAM_LR * (m_hat / (jnp.sqrt(v_hat) + ADAM_EPS) + ADAM_WD * w_ref[...])
        nm_ref[...] = nm
        nv_ref[...] = nv

    blk = pl.BlockSpec((tr, cols), lambda l, i: (l * nt + i, 0))
    return _call(
        body, name=name, grid=(nl, nt),
        in_specs=[pl.BlockSpec((N_DEV, tr, cols), lambda l, i, k=k: (0, jnp.where(l == k, i, 0), 0)) for k in range(nl)]
        + [blk, blk, blk],
        out_specs=[blk] * 4, out_shape=[jax.ShapeDtypeStruct((nl * rows, cols), F32)] * 4,
        args=(*stacks, w, m, v), sem=("arbitrary", "arbitrary"), carry=carry)


def _lb_all(lb_logits):
    sm = jax.nn.softmax(lb_logits.astype(F32), axis=0)
    return jnp.cumsum(sm, axis=0) - sm[0:1]


def _rope_tables(p):
    half = DH // 2
    inv = jnp.power(ROPE_THETA, -jnp.arange(half, dtype=F32) * 2.0 / DH)
    pos = (jnp.arange(p) - PAD_FRONT).astype(F32)
    ang = pos[:, None] * inv[None, :]
    cos, sin = jnp.cos(ang), jnp.sin(ang)
    return jnp.tile(cos, (1, 4)), jnp.tile(jnp.concatenate([-sin, sin], axis=1), (1, 2))


SMALL = (("lb_logits", (DEPTH, 512)), ("q_norm_g", (DEPTH, DH)),
         ("k_norm_g", (DEPTH, DH)), ("attn_sinks", (DEPTH, HB)), ("hgrn_norm_g", (DEPTH, 128)),
         ("pool_w", (DEPTH, 4, 128, 128)), ("pool_scale", (DEPTH, 512)))


def _pack_small(d):
    flat = jnp.concatenate([d[k].astype(F32).reshape(-1) for k, _ in SMALL])
    pad = (-flat.shape[0]) % (8 * 128)
    return jnp.pad(flat, (0, pad)).reshape(-1, 128)


def _unpack_small(a):
    flat = a.reshape(-1)
    out, off = {}, 0
    for k, shp in SMALL:
        n = int(np.prod(shp))
        out[k] = flat[off:off + n].reshape(shp)
        off += n
    return out


def kernel(x, meta_tokens, lb_logits, norm_g, w_in, q_norm_g, k_norm_g, attn_sinks, hgrn_norm_g, pool_w, pool_scale, w_out, loss_target, m_meta_tokens, m_lb_logits, m_norm_g, m_w_in, m_q_norm_g, m_k_norm_g, m_attn_sinks, m_hgrn_norm_g, m_pool_w, m_pool_scale, m_w_out, v_meta_tokens, v_lb_logits, v_norm_g, v_w_in, v_q_norm_g, v_k_norm_g, v_attn_sinks, v_hgrn_norm_g, v_pool_w, v_pool_scale, v_w_out):
    seq = x.shape[1]
    p = seq + TB
    cs = PROJ_COLS // N_DEV
    rs = MIX // N_DEV
    ms = D_MODEL // N_DEV

    full_w_in = lambda st: st.transpose(1, 0, 2).reshape(D_MODEL, PROJ_COLS)
    dw_in_blocks = lambda rows: rows.reshape(rows.shape[0], N_DEV, cs).transpose(1, 0, 2)
    dw_out_blocks = lambda g: g["w_out"].reshape(N_DEV, rs, D_MODEL)
    w_in_bf = w_in.astype(BF16)

    lb_all, lb_vjp = jax.vjp(_lb_all, lb_logits)
    cos, sin = _rope_tables(p)
    layers = []
    for l in range(DEPTH):
        layers.append(dict(
            norm_g=norm_g[l][None], lb=lb_all[l][None],
            gq=jnp.tile(q_norm_g[l], 2)[None], gk=jnp.tile(k_norm_g[l], 2)[None], sinks=attn_sinks[l],
            hg=hgrn_norm_g[l][None], pool_w=pool_w[l].astype(BF16), pool_scale=pool_scale[l][None], cos=cos, sin=sin))
    meta_all = _exchange(meta_tokens, gather=True, name="gather_meta")
    meta_full = meta_all.transpose(1, 0, 2).reshape(N_META, D_MODEL)

    def got_w_out(st):
        st = st.reshape(N_DEV, DEPTH, rs, D_MODEL)
        for l in range(DEPTH):
            layers[l]["w_out"] = st[:, l].reshape(MIX, D_MODEL)

    def got_w_in(l):
        def done(st):
            layers[l]["w_in"] = full_w_in(st)
        return done

    fwd_carries = [
        dict(in_proj=(lambda _: w_in_bf[1], True, got_w_in(1)),
             hgrn_fwd=(lambda _: w_out.reshape(DEPTH * rs, D_MODEL).astype(BF16), True, got_w_out)),
        {}]
    h, xn, w_in_l0 = _embed_rmsnorm_fwd(x[0], meta_full, layers[0]["norm_g"], name="embed_rmsnorm_fwd_l0",
                                        carry=(w_in_bf[0], True))
    got_w_in(0)(w_in_l0)
    saved = []
    for l in range(DEPTH):
        h, s = _layer_fwd(h, layers[l], f"_l{l}", fwd_carries[l], xn=xn if l == 0 else None)
        saved.append(s)
    dh, dhb, sq = _loss_grad(h, loss_target[0], name="loss_grad")
    loss = lax.psum(0.5 * jnp.sum(sq) / D_MODEL, ("x", "y", "c"))

    grads = [None] * DEPTH
    win_stacks, wout_stacks, small_stacks = [None] * (W_IN_PARTS * DEPTH), [None] * DEPTH, [None]

    def into(stacks, i):
        def done(st):
            stacks[i] = st
        return done

    def small_grads(g0):
        both = [g0, grads[1]]
        stk = lambda k: jnp.stack([both[l][k][0] for l in range(DEPTH)])
        fold = lambda a: a[:, :DH] + a[:, DH:]
        return _pack_small(dict(
            lb_logits=lb_vjp(stk("lb"))[0], q_norm_g=fold(stk("gq")), k_norm_g=fold(stk("gk")),
            attn_sinks=stk("sinks")[:, :HB], hgrn_norm_g=stk("hg"),
            pool_w=jnp.stack([both[l]["pool_w"] for l in range(DEPTH)]), pool_scale=stk("pool_scale")))

    def part(l, i):
        return (lambda g: dw_in_blocks(g["w_in_parts"][i])), False, into(win_stacks, W_IN_PARTS * l + i)

    bwd_carries = [
        dict(hgrn_bwd=(dw_out_blocks, False, into(wout_stacks, 0)), dw_in_0=(small_grads, True, into(small_stacks, 0)),
             dw_in_1=part(0, 0), d_xn=part(0, 1)),
        dict(attn_bwd=(dw_out_blocks, False, into(wout_stacks, 1)), dw_in_1=part(1, 0), d_xn=part(1, 1))]
    dh, dhb, grads[1] = _layer_bwd(dh, dhb, saved[1], layers[1], "_l1", bwd_carries[1])
    grad_x, d_front, grads[0] = _layer_bwd(dh, dhb, saved[0], layers[0], "_l0", bwd_carries[0], first_layer=True)
    grad_x = grad_x[None]
    dmeta = d_front.reshape(N_META, N_DEV, ms).transpose(1, 0, 2)
    meta_stack = _exchange(dmeta, gather=False, name="scatter_dmeta")
    small_stack = small_stacks[0]
    d_norm_g = jnp.stack([grads[l]["norm_g"][0] for l in range(DEPTH)]).reshape(-1, 128)
    norm_stack = _exchange(d_norm_g, gather=True, name="gather_norm_g_grad")
    adam_norm = [a.reshape(DEPTH, D_MODEL) for a in _adamw(
        [norm_stack], norm_g.reshape(-1, 128), m_norm_g.reshape(-1, 128), v_norm_g.reshape(-1, 128), name="adamw_norm_g")]

    g_wout, d_wout, nm_wout, nv_wout = _adamw(wout_stacks, w_out.reshape(DEPTH * rs, D_MODEL), m_w_out.reshape(DEPTH * rs, D_MODEL),
                                              v_w_out.reshape(DEPTH * rs, D_MODEL), name="adamw_w_out")
    g_win, d_win, nm_win, nv_win = _adamw(win_stacks, w_in.reshape(DEPTH * D_MODEL, cs), m_w_in.reshape(DEPTH * D_MODEL, cs),
                                          v_w_in.reshape(DEPTH * D_MODEL, cs), name="adamw_w_in")
    g_meta, d_meta, nm_meta, nv_meta = _adamw([meta_stack], meta_tokens, m_meta_tokens, v_meta_tokens, name="adamw_meta")
    small_w = dict(lb_logits=lb_logits, norm_g=norm_g, q_norm_g=q_norm_g, k_norm_g=k_norm_g, attn_sinks=attn_sinks,
                   hgrn_norm_g=hgrn_norm_g, pool_w=pool_w, pool_scale=pool_scale)
    small_m = dict(lb_logits=m_lb_logits, norm_g=m_norm_g, q_norm_g=m_q_norm_g, k_norm_g=m_k_norm_g, attn_sinks=m_attn_sinks,
                   hgrn_norm_g=m_hgrn_norm_g, pool_w=m_pool_w, pool_scale=m_pool_scale)
    small_v = dict(lb_logits=v_lb_logits, norm_g=v_norm_g, q_norm_g=v_q_norm_g, k_norm_g=v_k_norm_g, attn_sinks=v_attn_sinks,
                   hgrn_norm_g=v_hgrn_norm_g, pool_w=v_pool_w, pool_scale=v_pool_scale)
    small_out = [_unpack_small(a) for a in _adamw([small_stack], _pack_small(small_w), _pack_small(small_m),
                                                  _pack_small(small_v), name="adamw_small")]

    big = dict(
        meta_tokens=(g_meta, d_meta, nm_meta, nv_meta), norm_g=tuple(adam_norm),
        w_in=tuple(a.reshape(DEPTH, D_MODEL, cs) for a in (g_win, d_win, nm_win, nv_win)),
        w_out=tuple(a.reshape(DEPTH, rs, D_MODEL) for a in (g_wout, d_wout, nm_wout, nv_wout)))
    order = ("meta_tokens", "lb_logits", "norm_g", "w_in", "q_norm_g", "k_norm_g", "attn_sinks", "hgrn_norm_g",
             "pool_w", "pool_scale", "w_out")
    outs = [loss, grad_x]
    for kind in range(4):
        for k in order:
            outs.append(big[k][kind] if k in big else small_out[kind][k])
    return tuple(outs)
```

```python
import functools

import numpy as np
import jax
import jax.numpy as jnp
from jax import lax
from jax.experimental import pallas as pl
from jax.experimental.pallas import tpu as pltpu

F32, BF16 = jnp.float32, jnp.bfloat16

D_MODEL = 2048
DEPTH = 2
N_META = 16
TB = 128
PAD_FRONT = TB - N_META
RMS_EPS = 1e-6
NEG_INF = -1e30
LOG_FLOOR = 1e-30
HA, DK_A = 4, 128
CH = 16
NCH = TB // CH
HB, KVH, DH = 16, 2, 64
GRP = HB // KVH
ROPE_THETA = 10000.0
POOL_WINDOWS = (2, 4, 8, 16)
PROJ_COLS = 5376
MIX = 2048
N_DEV = 8
W_IN_PARTS = 2
C_QA, C_FA, C_IA, C_GA = 0, 1, 2, 3
C_QB = 2
C_KB, C_VB = 24, 25
C_GB, C_UC, C_GC = (13, 14, 15, 16), (17, 18), (19, 20)


class _Cols:
    def __init__(self, refs):
        self.refs, self.width = refs, refs[0].shape[1]

    def __getitem__(self, idx):
        rows, sl = idx
        k, off = divmod(sl.start, self.width)
        return self.refs[k][rows, off:off + sl.stop - sl.start]

ADAM_LR, ADAM_B1, ADAM_B2, ADAM_EPS, ADAM_WD, ADAM_STEP = 0.001, 0.9, 0.999, 1e-08, 0.01, 10

VMEM_LIMIT = 48 * 1024 * 1024

NN = ((1,), (0,))
NT = ((1,), (1,))
TN = ((0,), (0,))


def _dot(a, b, dims):
    return lax.dot_general(a, b, (dims, ((), ())), preferred_element_type=F32)


def _split3(x):
    hi = x.astype(BF16)
    r = x - hi.astype(F32)
    mid = r.astype(BF16)
    lo = (r - mid.astype(F32)).astype(BF16)
    return hi, mid, lo


def _xdot(m01, x):
    hi, mid, lo = _split3(x)
    return _dot(m01, hi, NN) + _dot(m01, mid, NN) + _dot(m01, lo, NN)


def _xdot2(m01, x):
    hi = x.astype(BF16)
    lo = (x - hi.astype(F32)).astype(BF16)
    return _dot(m01, hi, NN) + _dot(m01, lo, NN)


def _xdot_r(x, m01):
    hi, mid, lo = _split3(x)
    return _dot(hi, m01, NN) + _dot(mid, m01, NN) + _dot(lo, m01, NN)


def _iota(shape, dim):
    return lax.broadcasted_iota(jnp.int32, shape, dim)


def _params(*sem):
    return pltpu.CompilerParams(dimension_semantics=sem, vmem_limit_bytes=VMEM_LIMIT)


def _row_tile(p, target):
    best = TB
    t = TB
    while t <= target:
        if p % t == 0:
            best = t
        t += TB
    return best


def _col_tile(n, target):
    best = 128
    t = 128
    while t <= target:
        if n % t == 0:
            best = t
        t += 128
    return best


def _sigmoid(x):
    return 1.0 / (1.0 + jnp.exp(-x))


def _mm_nn(a, b, res=None, *, name, tm=1664, tn=768, carry=None):
    m, k = a.shape
    n = b.shape[1]
    tm, tn = _row_tile(m, tm), _col_tile(n, tn)

    def body(*refs):
        if res is None:
            a_ref, b_ref, o_ref = refs
            o_ref[...] = _dot(a_ref[...], b_ref[...], NN)
        else:
            a_ref, b_ref, r_ref, o_ref = refs
            o_ref[...] = r_ref[...] + _dot(a_ref[...], b_ref[...], NN)

    in_specs = [pl.BlockSpec((tm, k), lambda j, i: (i, 0)), pl.BlockSpec((k, tn), lambda j, i: (0, j))]
    args = [a, b]
    if res is not None:
        in_specs.append(pl.BlockSpec((tm, tn), lambda j, i: (i, j)))
        args.append(res)
    return _call(
        body, name=name, grid=(n // tn, m // tm), in_specs=in_specs,
        out_specs=[pl.BlockSpec((tm, tn), lambda j, i: (i, j))],
        out_shape=[jax.ShapeDtypeStruct((m, n), F32)],
        args=args, sem=("parallel", "parallel"), carry=carry)


def _mm_nt(a, b, *, name, tm=640, tn=512, tk=2048, carry=None):
    m, k = a.shape
    n = b.shape[0]
    tm, tn, tk = _row_tile(m, tm), _col_tile(n, tn), _col_tile(k, tk)

    def body(a_ref, b_ref, o_ref):
        @pl.when(pl.program_id(2) == 0)
        def _():
            o_ref[...] = jnp.zeros_like(o_ref)

        o_ref[...] += _dot(a_ref[...], b_ref[...], NT)

    return _call(
        body, name=name, grid=(n // tn, m // tm, k // tk),
        in_specs=[pl.BlockSpec((tm, tk), lambda j, i, kk: (i, kk)), pl.BlockSpec((tn, tk), lambda j, i, kk: (j, kk))],
        out_specs=[pl.BlockSpec((tm, tn), lambda j, i, kk: (i, j))],
        out_shape=[jax.ShapeDtypeStruct((m, n), F32)],
        args=(a, b), sem=("parallel", "parallel", "arbitrary"), carry=carry)


def _mm_tn(a, b, *, name, tm=1024, tn=1344, tk=1664, m_part=None, carry=None):
    k, m = a.shape
    n = b.shape[1]
    first, m = (0, m) if m_part is None else (m_part[0], m // m_part[1])
    tm, tn, tk = _col_tile(m, tm), _col_tile(n, tn), _row_tile(k, tk)
    first *= m // tm
    nk = k // tk

    def body(a_ref, b_ref, o_ref, acc_ref):
        @pl.when(pl.program_id(2) == 0)
        def _():
            acc_ref[...] = jnp.zeros_like(acc_ref)

        acc_ref[...] += _dot(a_ref[...], b_ref[...], TN)

        @pl.when(pl.program_id(2) == nk - 1)
        def _():
            o_ref[...] = acc_ref[...].astype(BF16)

    return _call(
        body, name=name, grid=(m // tm, n // tn, nk),
        in_specs=[pl.BlockSpec((tk, tm), lambda i, j, kk: (kk, first + i)), pl.BlockSpec((tk, tn), lambda i, j, kk: (kk, j))],
        out_specs=[pl.BlockSpec((tm, tn), lambda i, j, kk: (i, j))],
        out_shape=[jax.ShapeDtypeStruct((m, n), BF16)], scratch_shapes=[pltpu.VMEM((tm, tn), F32)],
        args=(a, b), sem=("parallel", "parallel", "arbitrary"), carry=carry)


def _rmsnorm_fwd(h, g, *, name, carry=None):
    p, dm = h.shape
    tm = _row_tile(p, 640)

    def body(h_ref, g_ref, xn_ref):
        hv = h_ref[...]
        r = lax.rsqrt(jnp.mean(hv * hv, axis=-1, keepdims=True) + RMS_EPS)
        xn_ref[...] = (hv * r * g_ref[...]).astype(BF16)

    return _call(
        body, name=name, grid=(p // tm,),
        in_specs=[pl.BlockSpec((tm, dm), lambda i: (i, 0)), pl.BlockSpec((1, dm), lambda i: (0, 0))],
        out_specs=[pl.BlockSpec((tm, dm), lambda i: (i, 0))],
        out_shape=[jax.ShapeDtypeStruct((p, dm), BF16)],
        args=(h, g), sem=("parallel",), carry=carry)


def _embed_rmsnorm_fwd(x, meta, g, *, name, carry=None):
    seq, dm = x.shape
    p = seq + TB

    def body(x_ref, meta_ref, g_ref, h_ref, xn_ref):
        @pl.when(pl.program_id(0) == 0)
        def _():
            h_ref[...] = jnp.zeros_like(h_ref)
            h_ref[PAD_FRONT:TB, :] = meta_ref[...]

        @pl.when(pl.program_id(0) > 0)
        def _():
            h_ref[...] = x_ref[...]

        hv = h_ref[...]
        r = lax.rsqrt(jnp.mean(hv * hv, axis=-1, keepdims=True) + RMS_EPS)
        xn_ref[...] = (hv * r * g_ref[...]).astype(BF16)

    row = pl.BlockSpec((TB, dm), lambda i: (i, 0))
    return _call(
        body, name=name, grid=(p // TB,),
        in_specs=[pl.BlockSpec((TB, dm), lambda i: (jnp.maximum(i - 1, 0), 0)), pl.BlockSpec((N_META, dm), lambda i: (0, 0)),
                  pl.BlockSpec((1, dm), lambda i: (0, 0))],
        out_specs=[row, row],
        out_shape=[jax.ShapeDtypeStruct((p, dm), F32), jax.ShapeDtypeStruct((p, dm), BF16)],
        args=(x, meta, g), sem=("arbitrary",), carry=carry)


def _rmsnorm_bwd(dxn, h, g, dh_out, *, name, first_layer=False, carry=None):
    p, dm = h.shape
    tm = TB if first_layer else _row_tile(p, 384)

    def body(dxn_ref, h_ref, g_ref, dho_ref, out_ref, aux_ref, dg_ref):
        hv = h_ref[...]
        r = lax.rsqrt(jnp.mean(hv * hv, axis=-1, keepdims=True) + RMS_EPS)
        xh = hv * r
        dy = dxn_ref[...]
        dyn = dy * g_ref[...]
        dh = dho_ref[...] + r * (dyn - xh * jnp.mean(dyn * xh, axis=-1, keepdims=True))
        out_ref[...] = dh

        @pl.when(pl.program_id(0) == 0)
        def _():
            dg_ref[...] = jnp.zeros_like(dg_ref)
            if first_layer:
                aux_ref[...] = dh[PAD_FRONT:TB]

        if not first_layer:
            aux_ref[...] = dh.astype(BF16)
        dg_ref[...] += jnp.sum(dy * xh, axis=0, keepdims=True)

    row = pl.BlockSpec((tm, dm), lambda i: (i, 0))
    vec = pl.BlockSpec((1, dm), lambda i: (0, 0))
    if first_layer:
        out_specs = [pl.BlockSpec((TB, dm), lambda i: (jnp.maximum(i - 1, 0), 0)), pl.BlockSpec((N_META, dm), lambda i: (0, 0)), vec]
        out_shape = [jax.ShapeDtypeStruct((p - TB, dm), F32), jax.ShapeDtypeStruct((N_META, dm), F32)]
    else:
        out_specs = [row, row, vec]
        out_shape = [jax.ShapeDtypeStruct((p, dm), F32), jax.ShapeDtypeStruct((p, dm), BF16)]
    return _call(
        body, name=name, grid=(p // tm,),
        in_specs=[row, row, vec, row], out_specs=out_specs,
        out_shape=out_shape + [jax.ShapeDtypeStruct((1, dm), F32)],
        args=(dxn, h, g, dh_out), sem=("arbitrary",), carry=carry)


def _loss_grad(h, target, *, name):
    p, dm = h.shape

    def body(h_ref, t_ref, dh_ref, dhb_ref, sq_ref):
        n = pl.program_id(0)

        @pl.when(n == 0)
        def _():
            dh_ref[...] = jnp.zeros_like(dh_ref)
            dhb_ref[...] = jnp.zeros_like(dhb_ref)
            sq_ref[...] = jnp.zeros_like(sq_ref)

        @pl.when(n > 0)
        def _():
            err = h_ref[...] - t_ref[...]
            dh = err * (1.0 / dm)
            dh_ref[...] = dh
            dhb_ref[...] = dh.astype(BF16)
            sq_ref[...] += jnp.sum(err * err, axis=0, keepdims=True)

    row = pl.BlockSpec((TB, dm), lambda n: (n, 0))
    return pl.pallas_call(
        body, name=name, grid=(p // TB,),
        in_specs=[row, pl.BlockSpec((TB, dm), lambda n: (jnp.maximum(n - 1, 0), 0))],
        out_specs=[row, row, pl.BlockSpec((1, dm), lambda n: (0, 0))],
        out_shape=[jax.ShapeDtypeStruct((p, dm), F32), jax.ShapeDtypeStruct((p, dm), BF16), jax.ShapeDtypeStruct((1, dm), F32)],
        compiler_params=_params("arbitrary"),
    )(h, target)


def _chunk_masks():
    ri, ci = _iota((TB, TB), 0), _iota((TB, TB), 1)
    same = (ri >> 4) == (ci >> 4)
    causal = same & (ci <= ri)
    lower = jnp.where(causal, 1.0, 0.0).astype(BF16)
    upper = jnp.where(same & (ci >= ri), 1.0, 0.0).astype(BF16)
    ones = jnp.where(same, 1.0, 0.0).astype(BF16)
    return causal, lower, upper, ones


def _hgrn_gates(q, z, lbh, m):
    sig = _sigmoid(z)
    f = lbh + (1.0 - lbh) * sig
    lf = jnp.log(jnp.maximum(f, LOG_FLOOR)) * m
    kk = (1.0 - lbh) * (1.0 - sig) * m
    sq = _sigmoid(q)
    return sig, f, lf, kk, sq, q * sq


def _hgrn_fwd(proj, lb, *, name, carry=None):
    p = proj.shape[0]
    nb = p // TB

    def body(qa_ref, fa_ref, ia_ref, lb_ref, oa_ref, sck_ref, st_ref):
        n = pl.program_id(0)

        @pl.when(n == 0)
        def _():
            st_ref[...] = jnp.zeros_like(st_ref)

        causal, lower, _, ones = _chunk_masks()
        m = ((n * TB + _iota((TB, 1), 0)) >= PAD_FRONT).astype(F32)
        heads = range(HA)
        sls = [slice(hd * DK_A, (hd + 1) * DK_A) for hd in heads]
        rows = [slice(c * CH, (c + 1) * CH) for c in range(NCH)]
        gates = [_hgrn_gates(qa_ref[:, sl], fa_ref[:, sl], lb_ref[:, sl], m) for sl in sls]
        lf = [t[2] for t in gates]
        g = [_xdot(lower, x) for x in lf]
        gl = [_xdot(ones, x) for x in lf]
        qd = [(gates[hd][5] * jnp.exp(g[hd])).astype(BF16) for hd in heads]
        kt = [(gates[hd][3] * jnp.exp(-g[hd])).astype(BF16) for hd in heads]
        kd = [(gates[hd][3] * jnp.exp(gl[hd] - g[hd])).astype(BF16) for hd in heads]
        vb = [ia_ref[:, sl].astype(BF16) for sl in sls]
        a_all = [jnp.exp(x) for x in gl]
        att = [jnp.where(causal, _dot(qd[hd], kt[hd], NT), 0.0).astype(BF16) for hd in heads]
        kv = [[_dot(vb[hd][r], kd[hd][r], TN) for r in rows] for hd in heads]
        o = [_dot(att[hd], vb[hd], NN) for hd in heads]
        before = []
        for hd in heads:
            st = st_ref[hd]
            sck_ref[0, hd] = st
            per_chunk = []
            for c in range(NCH):
                per_chunk.append(st.astype(BF16))
                st = st * a_all[hd][c * CH:c * CH + 1, :] + kv[hd][c]
            st_ref[hd] = st
            before.append(per_chunk)
        inter = [[_dot(qd[hd][rows[c]], before[hd][c], NT) for c in range(NCH)] for hd in heads]
        for hd in heads:
            oa_ref[:, sls[hd]] = o[hd] + jnp.concatenate(inter[hd], axis=0)

    blk = lambda c: pl.BlockSpec((TB, 512), lambda n, c=c: (n, c))
    return _call(
        body, name=name, grid=(nb,),
        in_specs=[blk(C_QA), blk(C_FA), blk(C_IA), pl.BlockSpec((1, 512), lambda n: (0, 0))],
        out_specs=[pl.BlockSpec((TB, 512), lambda n: (n, 0)), pl.BlockSpec((1, HA, TB, TB), lambda n: (n, 0, 0, 0))],
        out_shape=[jax.ShapeDtypeStruct((p, 512), F32), jax.ShapeDtypeStruct((nb, HA, TB, TB), F32)],
        scratch_shapes=[pltpu.VMEM((HA, TB, TB), F32)],
        args=(proj, proj, proj, lb), sem=("arbitrary",), carry=carry)


def _hgrn_bwd(proj, lb, sck, d_oa, *, name, carry=None):
    p = proj.shape[0]
    nb = p // TB

    def body(qa_ref, fa_ref, ia_ref, lb_ref, sck_ref, do_ref, dq_ref, dz_ref, dv_ref, dlb_ref, dst_ref):
        i = pl.program_id(0)
        n = nb - 1 - i

        @pl.when(i == 0)
        def _():
            dst_ref[...] = jnp.zeros_like(dst_ref)
            dlb_ref[...] = jnp.zeros_like(dlb_ref)

        causal, lower, upper, ones = _chunk_masks()
        m = ((n * TB + _iota((TB, 1), 0)) >= PAD_FRONT).astype(F32)
        heads = range(HA)
        sls = [slice(hd * DK_A, (hd + 1) * DK_A) for hd in heads]
        rows = [slice(c * CH, (c + 1) * CH) for c in range(NCH)]
        a_row = lambda a, c: a[c * CH:c * CH + 1, :]
        gates = [_hgrn_gates(qa_ref[:, sl], fa_ref[:, sl], lb_ref[:, sl], m) for sl in sls]
        g = [_xdot(lower, t[2]) for t in gates]
        gl = [_xdot(ones, t[2]) for t in gates]
        e_g = [jnp.exp(x) for x in g]
        e_ng = [jnp.exp(-x) for x in g]
        e_d = [jnp.exp(gl[hd] - g[hd]) for hd in heads]
        a_all = [jnp.exp(x) for x in gl]
        qd_f = [gates[hd][5] * e_g[hd] for hd in heads]
        kt_f = [gates[hd][3] * e_ng[hd] for hd in heads]
        kd_f = [gates[hd][3] * e_d[hd] for hd in heads]
        qd, kt, kd = ([x.astype(BF16) for x in xs] for xs in (qd_f, kt_f, kd_f))
        vb = [ia_ref[:, sl].astype(BF16) for sl in sls]
        dob = [do_ref[:, sl].astype(BF16) for sl in sls]
        att = [jnp.where(causal, _dot(qd[hd], kt[hd], NT), 0.0).astype(BF16) for hd in heads]
        d_att = [jnp.where(causal, _dot(dob[hd], vb[hd], NT), 0.0).astype(BF16) for hd in heads]
        kv = [[_dot(vb[hd][r], kd[hd][r], TN) for r in rows] for hd in heads]
        dqk = [[_dot(dob[hd][r], qd[hd][r], TN) for r in rows] for hd in heads]
        d_v = [_dot(att[hd], dob[hd], TN) for hd in heads]
        d_qd = [_dot(d_att[hd], kt[hd], NN) for hd in heads]
        d_kt = [_dot(d_att[hd], qd[hd], TN) for hd in heads]
        stc, dsc = [], []
        for hd in heads:
            st, before = sck_ref[0, hd], []
            for c in range(NCH):
                before.append(st)
                if c + 1 < NCH:
                    st = st * a_row(a_all[hd], c) + kv[hd][c]
            dst, after = dst_ref[hd], [None] * NCH
            for c in range(NCH - 1, -1, -1):
                after[c] = dst
                dst = dst * a_row(a_all[hd], c) + dqk[hd][c]
            dst_ref[hd] = dst
            stc.append(before)
            dsc.append(after)
        dscb = [[x.astype(BF16) for x in dsc[hd]] for hd in heads]
        dvs = [[_dot(kd[hd][rows[c]], dscb[hd][c], NT) for c in range(NCH)] for hd in heads]
        dkd = [[_dot(vb[hd][rows[c]], dscb[hd][c], NN) for c in range(NCH)] for hd in heads]
        dqd = [[_dot(dob[hd][rows[c]], stc[hd][c].astype(BF16), NN) for c in range(NCH)] for hd in heads]
        dgl = [[jnp.broadcast_to(jnp.sum(dsc[hd][c] * stc[hd][c], axis=0, keepdims=True) * a_row(a_all[hd], c), (CH, TB))
                for c in range(NCH)] for hd in heads]
        d_qd = [d_qd[hd] + jnp.concatenate(dqd[hd], axis=0) for hd in heads]
        d_kd = [jnp.concatenate(dkd[hd], axis=0) for hd in heads]
        kd_term = [d_kd[hd] * kd_f[hd] for hd in heads]
        d_g = [d_qd[hd] * qd_f[hd] - d_kt[hd] * kt_f[hd] - kd_term[hd] for hd in heads]
        d_lf = [_xdot2(upper, d_g[hd]) + _xdot2(ones, kd_term[hd]) + jnp.concatenate(dgl[hd], axis=0) for hd in heads]
        for hd in heads:
            sl = sls[hd]
            sig, f, _, _, sq, _ = gates[hd]
            q, lbh = qa_ref[:, sl], lb_ref[:, sl]
            d_kk = (d_kt[hd] * e_ng[hd] + d_kd[hd] * e_d[hd]) * m
            t1 = d_lf[hd] * m * jnp.where(f > LOG_FLOOR, 1.0 / f, 0.0)
            dq_ref[:, sl] = (d_qd[hd] * e_g[hd] * (sq * (1.0 + q * (1.0 - sq)))).astype(BF16)
            dz_ref[:, sl] = ((t1 - d_kk) * (1.0 - lbh) * sig * (1.0 - sig)).astype(BF16)
            dv_ref[:, sl] = (d_v[hd] + jnp.concatenate(dvs[hd], axis=0)).astype(BF16)
            dlb_ref[:, sl] += jnp.sum((t1 - d_kk) * (1.0 - sig), axis=0, keepdims=True)

    blk = lambda c: pl.BlockSpec((TB, 512), lambda i, c=c: (nb - 1 - i, c))
    out_blk = pl.BlockSpec((TB, 512), lambda i: (nb - 1 - i, 0))
    vec = pl.BlockSpec((1, 512), lambda i: (0, 0))
    return _call(
        body, name=name, grid=(nb,),
        in_specs=[blk(C_QA), blk(C_FA), blk(C_IA), vec,
                  pl.BlockSpec((1, HA, TB, TB), lambda i: (nb - 1 - i, 0, 0, 0)), out_blk],
        out_specs=[out_blk, out_blk, out_blk, vec],
        out_shape=[jax.ShapeDtypeStruct((p, 512), BF16)] * 3 + [jax.ShapeDtypeStruct((1, 512), F32)],
        scratch_shapes=[pltpu.VMEM((HA, TB, TB), F32)],
        args=(proj, proj, proj, lb, sck, d_oa), sem=("arbitrary",), carry=carry)


def _lane():
    return _iota((1, TB), 1)


def _swap_halves(y):
    first = (_lane() & 63) < 32
    return jnp.where(first, pltpu.roll(y, 96, 1), pltpu.roll(y, 32, 1))


def _head_ones():
    ri, ci = _iota((TB, TB), 0), _iota((TB, TB), 1)
    return jnp.where((ri >> 6) == (ci >> 6), 1.0, 0.0).astype(BF16)


def _norm_rope(x, g, cos, sin, bd):
    r = lax.rsqrt(_xdot_r(x * x, bd) * (1.0 / DH) + RMS_EPS)
    y = x * r * g
    return y * cos + _swap_halves(y) * sin


def _norm_rope_bwd(d_out, x, g, cos, sin, bd):
    d = d_out * cos - _swap_halves(d_out) * sin
    r = lax.rsqrt(_xdot_r(x * x, bd) * (1.0 / DH) + RMS_EPS)
    xh = x * r
    dyn = d * g
    dx = r * (dyn - xh * (_xdot_r(dyn * xh, bd) * (1.0 / DH)))
    return dx, jnp.sum(d * xh, axis=0, keepdims=True)


def _dup_heads(k):
    first = _lane() < DH
    r = pltpu.roll(k, DH, 1)
    return jnp.where(first, k, r), jnp.where(first, r, k)


def _qk_prep(proj, gq, gk, cos, sin, *, name):
    p = proj.shape[0]

    def body(qb_ref, kb_ref, vb_ref, gq_ref, gk_ref, cos_ref, sin_ref, qh_ref, k2_ref, v2_ref):
        bd = _head_ones()
        cos_v, sin_v = cos_ref[...], sin_ref[...]
        for j in range(HB // 2):
            sl = slice(j * TB, (j + 1) * TB)
            qh_ref[:, sl] = _norm_rope(qb_ref[:, sl], gq_ref[...], cos_v, sin_v, bd).astype(BF16)
        k0, k1 = _dup_heads(_norm_rope(kb_ref[...], gk_ref[...], cos_v, sin_v, bd))
        k2_ref[:, 0:TB] = k0.astype(BF16)
        k2_ref[:, TB:2 * TB] = k1.astype(BF16)
        v0, v1 = _dup_heads(vb_ref[...])
        v2_ref[:, 0:TB] = v0.astype(BF16)
        v2_ref[:, TB:2 * TB] = v1.astype(BF16)

    vec = pl.BlockSpec((1, TB), lambda n: (0, 0))
    tab = pl.BlockSpec((TB, TB), lambda n: (n, 0))
    return pl.pallas_call(
        body, name=name, grid=(p // TB,),
        in_specs=[pl.BlockSpec((TB, 1024), lambda n: (n, C_QB)), pl.BlockSpec((TB, TB), lambda n: (n, C_KB)),
                  pl.BlockSpec((TB, TB), lambda n: (n, C_VB)), vec, vec, tab, tab],
        out_specs=[pl.BlockSpec((TB, 1024), lambda n: (n, 0)), pl.BlockSpec((TB, 256), lambda n: (n, 0)),
                   pl.BlockSpec((TB, 256), lambda n: (n, 0))],
        out_shape=[jax.ShapeDtypeStruct((p, 1024), BF16), jax.ShapeDtypeStruct((p, 256), BF16),
                   jax.ShapeDtypeStruct((p, 256), BF16)],
        compiler_params=_params("parallel"),
    )(proj, proj, proj, gq, gk, cos, sin)


NKEY = N_META + 2 * TB


def _attn_mask(n):
    r = _iota((TB, NKEY), 0)
    j = _iota((TB, NKEY), 1)
    meta = (j < N_META) & ((n >= 1) | (j + PAD_FRONT <= r))
    prev = (j >= N_META) & (j < N_META + TB) & (n >= 2) & (j - N_META > r)
    cur = (j >= N_META + TB) & (n >= 1) & (j - (N_META + TB) <= r)
    return meta | prev | cur


def _attn_specs():
    cur = lambda w: pl.BlockSpec((TB, w), lambda n: (n, 0))
    prev = pl.BlockSpec((TB, 256), lambda n: (jnp.maximum(n - 1, 0), 0))
    meta = pl.BlockSpec((N_META, 256), lambda n: (PAD_FRONT // N_META, 0))
    sink = pl.BlockSpec(memory_space=pltpu.SMEM)
    return cur, prev, meta, sink


ATTN_GROUP_FWD, ATTN_GROUP_BWD = 8, 4


def _head_queries(q_ref, kv, first):
    out = []
    for jj in range(GRP // 2):
        j = kv * (GRP // 2) + jj
        qj = q_ref[:, j * TB:(j + 1) * TB] * (DH ** -0.5)
        for half in range(2):
            out.append((j, half, jnp.where(first if half == 0 else ~first, qj, jnp.zeros_like(qj))))
    return out


def _attn_fwd(qh, k2, v2, sinks, *, name, carry=None):
    p = qh.shape[0]

    def body(sink_ref, q_ref, kc_ref, kp_ref, km_ref, vc_ref, vp_ref, vm_ref, o_ref, lse_ref):
        n = pl.program_id(0)
        mask = _attn_mask(n)
        lane = _lane()
        first = lane < DH
        lse_tile = jnp.zeros((TB, TB), F32)
        for kv in range(KVH):
            ks = slice(kv * TB, (kv + 1) * TB)
            kall = jnp.concatenate([km_ref[:, ks], kp_ref[:, ks], kc_ref[:, ks]], axis=0)
            vall = jnp.concatenate([vm_ref[:, ks], vp_ref[:, ks], vc_ref[:, ks]], axis=0)
            hq = _head_queries(q_ref, kv, first)
            for g0 in range(0, GRP, ATTN_GROUP_FWD):
                grp = hq[g0:g0 + ATTN_GROUP_FWD]
                idx = range(len(grp))
                s = [jnp.where(mask, _dot(qm, kall, NT), NEG_INF) for _, _, qm in grp]
                sink = [sink_ref[2 * j + half] for j, half, _ in grp]
                mx = [jnp.maximum(jnp.max(s[i], axis=1, keepdims=True), sink[i]) for i in idx]
                pr = [jnp.exp(s[i] - mx[i]) for i in idx]
                den = [jnp.sum(pr[i], axis=1, keepdims=True) + jnp.exp(sink[i] - mx[i]) for i in idx]
                o = [_dot(pr[i].astype(BF16), vall, NN) * (1.0 / den[i]) for i in idx]
                for i, (j, half, _) in enumerate(grp):
                    lse_tile = lse_tile + jnp.where(lane == 2 * j + half, mx[i] + jnp.log(den[i]), 0.0)
                for i in range(0, len(grp), 2):
                    j = grp[i][0]
                    o_ref[:, j * TB:(j + 1) * TB] = jnp.where(first, o[i], o[i + 1])
        lse_ref[...] = lse_tile

    cur, prev, meta, sink = _attn_specs()
    return _call(
        body, name=name, grid=(p // TB,),
        in_specs=[sink, cur(1024), cur(256), prev, meta, cur(256), prev, meta],
        out_specs=[cur(1024), cur(TB)],
        out_shape=[jax.ShapeDtypeStruct((p, 1024), F32), jax.ShapeDtypeStruct((p, TB), F32)],
        args=(sinks, qh, k2, k2, k2, v2, v2, v2), sem=("parallel",), carry=carry)


def _attn_bwd(qh, k2, v2, sinks, o, lse, d_o, *, name, carry=None):
    p = qh.shape[0]

    def body(sink_ref, q_ref, kc_ref, kp_ref, km_ref, vc_ref, vp_ref, vm_ref, o_ref, lse_ref, do_ref,
             dq_ref, dkc_ref, dkp_ref, dvc_ref, dvp_ref, dkm_ref, dvm_ref, dsink_ref):
        n = pl.program_id(0)

        @pl.when(n == 0)
        def _():
            dkm_ref[...] = jnp.zeros_like(dkm_ref)
            dvm_ref[...] = jnp.zeros_like(dvm_ref)
            dsink_ref[...] = jnp.zeros_like(dsink_ref)

        mask = _attn_mask(n)
        lane = _lane()
        first = lane < DH
        lse_tile = lse_ref[...]
        sink_acc = jnp.zeros((TB, TB), F32)
        for kv in range(KVH):
            ks = slice(kv * TB, (kv + 1) * TB)
            kall = jnp.concatenate([km_ref[:, ks], kp_ref[:, ks], kc_ref[:, ks]], axis=0)
            vall = jnp.concatenate([vm_ref[:, ks], vp_ref[:, ks], vc_ref[:, ks]], axis=0)
            d_kall = jnp.zeros((NKEY, TB), F32)
            d_vall = jnp.zeros((NKEY, TB), F32)
            hq = _head_queries(q_ref, kv, first)
            for g0 in range(0, GRP, ATTN_GROUP_BWD):
                grp = hq[g0:g0 + ATTN_GROUP_BWD]
                idx = range(len(grp))
                s = [jnp.where(mask, _dot(qm, kall, NT), NEG_INF) for _, _, qm in grp]
                dom = [jnp.where(first if half == 0 else ~first, do_ref[:, j * TB:(j + 1) * TB], 0.0) for j, half, _ in grp]
                domb = [x.astype(BF16) for x in dom]
                d_w = [_dot(x, vall, NT) for x in domb]
                delta = [jnp.sum(dom[i] * o_ref[:, grp[i][0] * TB:(grp[i][0] + 1) * TB], axis=1, keepdims=True) for i in idx]
                lse_h = [jnp.sum(jnp.where(lane == 2 * j + half, lse_tile, 0.0), axis=1, keepdims=True) for j, half, _ in grp]
                w = [jnp.exp(s[i] - lse_h[i]) for i in idx]
                for i, (j, half, _) in enumerate(grp):
                    w_sink = jnp.exp(sink_ref[2 * j + half] - lse_h[i])
                    sink_acc = sink_acc + jnp.where(lane == 2 * j + half, -(w_sink * delta[i]), 0.0)
                dsb = [(w[i] * (d_w[i] - delta[i])).astype(BF16) for i in idx]
                d_q = [_dot(x, kall, NN) * (DH ** -0.5) for x in dsb]
                d_k = [_dot(dsb[i], grp[i][2], TN) for i in idx]
                d_v = [_dot(w[i].astype(BF16), domb[i], TN) for i in idx]
                for i in idx:
                    d_kall = d_kall + d_k[i]
                    d_vall = d_vall + d_v[i]
                for i in range(0, len(grp), 2):
                    j = grp[i][0]
                    dq_ref[:, j * TB:(j + 1) * TB] = jnp.where(first, d_q[i], d_q[i + 1])
            dkm_ref[:, ks] += d_kall[0:N_META]
            dkp_ref[:, ks] = d_kall[N_META:N_META + TB]
            dkc_ref[:, ks] = d_kall[N_META + TB:NKEY]
            dvm_ref[:, ks] += d_vall[0:N_META]
            dvp_ref[:, ks] = d_vall[N_META:N_META + TB]
            dvc_ref[:, ks] = d_vall[N_META + TB:NKEY]
        dsink_ref[...] += jnp.sum(sink_acc, axis=0, keepdims=True)

    cur, prev, meta, sink = _attn_specs()
    acc = lambda r: pl.BlockSpec((r, 256), lambda n: (0, 0))
    return _call(
        body, name=name, grid=(p // TB,),
        in_specs=[sink, cur(1024), cur(256), prev, meta, cur(256), prev, meta, cur(1024), cur(TB), cur(1024)],
        out_specs=[cur(1024), cur(256), cur(256), cur(256), cur(256), acc(N_META), acc(N_META),
                   pl.BlockSpec((1, TB), lambda n: (0, 0))],
        out_shape=[jax.ShapeDtypeStruct((p, 1024), F32)] + [jax.ShapeDtypeStruct((p, 256), F32)] * 4
        + [jax.ShapeDtypeStruct((N_META, 256), F32)] * 2 + [jax.ShapeDtypeStruct((1, TB), F32)],
        args=(sinks, qh, k2, k2, k2, v2, v2, v2, o, lse, d_o), sem=("arbitrary",), carry=carry)


def _qk_post(proj, gq, gk, cos, sin, dqh, dkc, dkp, dkm, dvc, dvp, dvm, *, name):
    p = proj.shape[0]
    nb = p // TB

    def body(qb_ref, kb_ref, gq_ref, gk_ref, cos_ref, sin_ref, dqh_ref, dkc_ref, dkp_ref, dkm_ref,
             dvc_ref, dvp_ref, dvm_ref, dqb_ref, dkb_ref, dvb_ref, dgq_ref, dgk_ref, tk_ref, tv_ref):
        n = pl.program_id(0)

        @pl.when(n == 0)
        def _():
            dgq_ref[...] = jnp.zeros_like(dgq_ref)
            dgk_ref[...] = jnp.zeros_like(dgk_ref)

        keep = jnp.where(n == nb - 1, 0.0, 1.0)
        tk_ref[...] = dkc_ref[...] + keep * dkp_ref[...]
        tv_ref[...] = dvc_ref[...] + keep * dvp_ref[...]

        @pl.when(n == 0)
        def _():
            tk_ref[PAD_FRONT:TB, :] += dkm_ref[...]
            tv_ref[PAD_FRONT:TB, :] += dvm_ref[...]

        first = _lane() < DH

        def fold(t_ref):
            t0, t1 = t_ref[:, 0:TB], t_ref[:, TB:2 * TB]
            return jnp.where(first, t0 + pltpu.roll(t0, DH, 1), t1 + pltpu.roll(t1, DH, 1))

        bd = _head_ones()
        cos_v, sin_v = cos_ref[...], sin_ref[...]
        dvb_ref[...] = fold(tv_ref).astype(BF16)
        dkb, dgk = _norm_rope_bwd(fold(tk_ref), kb_ref[...], gk_ref[...], cos_v, sin_v, bd)
        dkb_ref[...] = dkb.astype(BF16)
        dgk_ref[...] += dgk
        dgq = jnp.zeros((1, TB), F32)
        for j in range(HB // 2):
            sl = slice(j * TB, (j + 1) * TB)
            dqb, dg = _norm_rope_bwd(dqh_ref[:, sl], qb_ref[:, sl], gq_ref[...], cos_v, sin_v, bd)
            dqb_ref[:, sl] = dqb.astype(BF16)
            dgq = dgq + dg
        dgq_ref[...] += dgq

    vec = pl.BlockSpec((1, TB), lambda n: (0, 0))
    tab = pl.BlockSpec((TB, TB), lambda n: (n, 0))
    cur = lambda w: pl.BlockSpec((TB, w), lambda n: (n, 0))
    nxt = pl.BlockSpec((TB, 256), lambda n: (jnp.minimum(n + 1, nb - 1), 0))
    meta = pl.BlockSpec((N_META, 256), lambda n: (0, 0))
    return pl.pallas_call(
        body, name=name, grid=(nb,),
        in_specs=[pl.BlockSpec((TB, 1024), lambda n: (n, C_QB)), pl.BlockSpec((TB, TB), lambda n: (n, C_KB)),
                  vec, vec, tab, tab, cur(1024), cur(256), nxt, meta, cur(256), nxt, meta],
        out_specs=[cur(1024), cur(TB), cur(TB), vec, vec],
        out_shape=[jax.ShapeDtypeStruct((p, 1024), BF16), jax.ShapeDtypeStruct((p, TB), BF16),
                   jax.ShapeDtypeStruct((p, TB), BF16), jax.ShapeDtypeStruct((1, TB), F32),
                   jax.ShapeDtypeStruct((1, TB), F32)],
        scratch_shapes=[pltpu.VMEM((TB, 256), F32), pltpu.VMEM((TB, 256), F32)],
        compiler_params=_params("arbitrary"),
    )(proj, proj, gq, gk, cos, sin, dqh, dkc, dkp, dkm, dvc, dvp, dvm)


EXT = TB + N_META


def _pool_count_inv(n, w):
    t = n * TB + _iota((TB, 1), 0)
    cnt = jnp.clip(t - (PAD_FRONT - 1), 1, w)
    return 1.0 / cnt.astype(F32)


def _silu_parts(gate):
    s = _sigmoid(gate)
    return gate * s, s * (1.0 + gate * (1.0 - s))


def _mix_fwd(proj, oa, yb, hg, pool_w, pool_scale, *, name):
    p = proj.shape[0]

    def body(ga_ref, gb0, gb1, gb2, gb3, uc0, uc1, up0, up1, gc0, gc1, oa_ref, yb_ref, hg_ref, pw_ref, ps_ref,
             mx_ref, pooled_ref):
        gb_ref, uc_ref, up_ref, gc_ref = _Cols([gb0, gb1, gb2, gb3]), _Cols([uc0, uc1]), _Cols([up0, up1]), _Cols([gc0, gc1])
        n = pl.program_id(0)
        valid = ((n * TB + _iota((TB, 1), 0)) >= PAD_FRONT).astype(F32)
        for hd in range(HA):
            sl = slice(hd * TB, (hd + 1) * TB)
            o = oa_ref[:, sl]
            r = lax.rsqrt(jnp.mean(o * o, axis=-1, keepdims=True) + RMS_EPS)
            act, _ = _silu_parts(ga_ref[:, sl])
            mx_ref[:, sl] = (o * r * hg_ref[...] * act).astype(BF16)
        for j in range(HB // 2):
            sl = slice(j * TB, (j + 1) * TB)
            act, _ = _silu_parts(gb_ref[:, sl])
            mx_ref[:, 512 + j * TB:512 + (j + 1) * TB] = (yb_ref[:, sl] * act).astype(BF16)
        ri, ci = _iota((TB, EXT), 0), _iota((TB, EXT), 1)
        has_prev = jnp.where(n == 0, 0.0, 1.0)
        groups = range(len(POOL_WINDOWS))
        sls = [slice(gi * TB, (gi + 1) * TB) for gi in groups]
        ug = [uc_ref[:, sl] * valid for sl in sls]
        ext = [jnp.concatenate([up_ref[:, sls[gi]] * has_prev, ug[gi]], axis=0) for gi in groups]
        band = [jnp.where((ci <= ri + N_META) & (ci > ri + N_META - w), 1.0, 0.0).astype(BF16) for w in POOL_WINDOWS]
        num = [_xdot(band[gi], ext[gi]) for gi in groups]
        pooled = [(num[gi] * _pool_count_inv(n, w) - ug[gi]) * valid for gi, w in enumerate(POOL_WINDOWS)]
        t = [_dot(pooled[gi].astype(BF16), pw_ref[gi], NN) for gi in groups]
        for gi in groups:
            pooled_ref[:, sls[gi]] = pooled[gi]
            act, _ = _silu_parts(gc_ref[:, sls[gi]])
            mx_ref[:, 1536 + gi * TB:1536 + (gi + 1) * TB] = (t[gi] * ps_ref[:, sls[gi]] * act).astype(BF16)

    cur = lambda w, c=0: pl.BlockSpec((TB, w), lambda n, c=c: (n, c))
    prev16 = lambda c: pl.BlockSpec((N_META, 256), lambda n, c=c: (jnp.maximum(n * (TB // N_META) - 1, 0), c))
    return pl.pallas_call(
        body, name=name, grid=(p // TB,),
        in_specs=[cur(512, C_GA)] + [cur(256, c) for c in C_GB] + [cur(256, c) for c in C_UC]
        + [prev16(c) for c in C_UC] + [cur(256, c) for c in C_GC]
        + [cur(512), cur(1024), pl.BlockSpec((1, TB), lambda n: (0, 0)), pl.BlockSpec((4, TB, TB), lambda n: (0, 0, 0)),
           pl.BlockSpec((1, 512), lambda n: (0, 0))],
        out_specs=[cur(MIX), cur(512)],
        out_shape=[jax.ShapeDtypeStruct((p, MIX), BF16), jax.ShapeDtypeStruct((p, 512), F32)],
        compiler_params=_params("parallel"),
    )(*([proj] * 11), oa, yb, hg, pool_w, pool_scale)


def _mix_bwd(proj, oa, yb, pooled, hg, pool_w, pool_scale, d_mixed, *, name, carry=None):
    p = proj.shape[0]

    def body(ga_ref, gb0, gb1, gb2, gb3, gc0, gc1, oa_ref, yb_ref, pooled_ref, hg_ref, pw_ref, ps_ref, dm_ref,
             doa_ref, dyb_ref, dga_ref, dgb_ref, dgc_ref, dp_ref, dhg_ref, dpw_ref, dps_ref):
        gb_ref, gc_ref = _Cols([gb0, gb1, gb2, gb3]), _Cols([gc0, gc1])
        n = pl.program_id(0)

        @pl.when(n == 0)
        def _():
            dhg_ref[...] = jnp.zeros_like(dhg_ref)
            dpw_ref[...] = jnp.zeros_like(dpw_ref)
            dps_ref[...] = jnp.zeros_like(dps_ref)

        valid = ((n * TB + _iota((TB, 1), 0)) >= PAD_FRONT).astype(F32)
        dhg = jnp.zeros((1, TB), F32)
        for hd in range(HA):
            sl = slice(hd * TB, (hd + 1) * TB)
            o = oa_ref[:, sl]
            r = lax.rsqrt(jnp.mean(o * o, axis=-1, keepdims=True) + RMS_EPS)
            on = o * r
            gate = ga_ref[:, sl]
            act, dact = _silu_parts(gate)
            dmx = dm_ref[:, sl]
            d_ya = dmx * act
            dga_ref[:, sl] = (dmx * on * hg_ref[...] * dact).astype(BF16)
            dyn = d_ya * hg_ref[...]
            doa_ref[:, sl] = r * (dyn - on * jnp.mean(dyn * on, axis=-1, keepdims=True))
            dhg = dhg + jnp.sum(d_ya * on, axis=0, keepdims=True)
        dhg_ref[...] += dhg
        for j in range(HB // 2):
            sl = slice(j * TB, (j + 1) * TB)
            act, dact = _silu_parts(gb_ref[:, sl])
            dmx = dm_ref[:, 512 + j * TB:512 + (j + 1) * TB]
            dyb_ref[:, sl] = dmx * act
            dgb_ref[:, sl] = (dmx * yb_ref[:, sl] * dact).astype(BF16)
        groups = range(len(POOL_WINDOWS))
        sls = [slice(gi * TB, (gi + 1) * TB) for gi in groups]
        pooled_b = [pooled_ref[:, sl].astype(BF16) for sl in sls]
        t = [_dot(pooled_b[gi], pw_ref[gi], NN) for gi in groups]
        d_t = []
        for gi in groups:
            sl = sls[gi]
            act, dact = _silu_parts(gc_ref[:, sl])
            dmx = dm_ref[:, 1536 + gi * TB:1536 + (gi + 1) * TB]
            d_yc = dmx * act
            dgc_ref[:, sl] = (dmx * t[gi] * ps_ref[:, sl] * dact).astype(BF16)
            dps_ref[:, sl] += jnp.sum(d_yc * t[gi], axis=0, keepdims=True)
            d_t.append((d_yc * ps_ref[:, sl]).astype(BF16))
        d_p = [_dot(d_t[gi], pw_ref[gi], NT) for gi in groups]
        d_w = [_dot(pooled_b[gi], d_t[gi], TN) for gi in groups]
        for gi in groups:
            dp_ref[:, sls[gi]] = d_p[gi] * valid
            dpw_ref[gi] += d_w[gi]

    cur = lambda w, c=0: pl.BlockSpec((TB, w), lambda n, c=c: (n, c))
    return _call(
        body, name=name, grid=(p // TB,),
        in_specs=[cur(512, C_GA)] + [cur(256, c) for c in C_GB] + [cur(256, c) for c in C_GC]
        + [cur(512), cur(1024), cur(512), pl.BlockSpec((1, TB), lambda n: (0, 0)),
           pl.BlockSpec((4, TB, TB), lambda n: (0, 0, 0)), pl.BlockSpec((1, 512), lambda n: (0, 0)), cur(MIX)],
        out_specs=[cur(512), cur(1024), cur(512), cur(1024), cur(512), cur(512),
                   pl.BlockSpec((1, TB), lambda n: (0, 0)), pl.BlockSpec((4, TB, TB), lambda n: (0, 0, 0)),
                   pl.BlockSpec((1, 512), lambda n: (0, 0))],
        out_shape=[jax.ShapeDtypeStruct((p, 512), F32), jax.ShapeDtypeStruct((p, 1024), F32),
                   jax.ShapeDtypeStruct((p, 512), BF16), jax.ShapeDtypeStruct((p, 1024), BF16),
                   jax.ShapeDtypeStruct((p, 512), BF16), jax.ShapeDtypeStruct((p, 512), F32),
                   jax.ShapeDtypeStruct((1, TB), F32), jax.ShapeDtypeStruct((4, TB, TB), F32),
                   jax.ShapeDtypeStruct((1, 512), F32)],
        args=(*([proj] * 7), oa, yb, pooled, hg, pool_w, pool_scale, d_mixed), sem=("arbitrary",), carry=carry)


def _pool_bwd(dp, *, name):
    p = dp.shape[0]
    nb = p // TB

    def body(dp_ref, dn_ref, duc_ref):
        n = pl.program_id(0)
        valid = ((n * TB + _iota((TB, 1), 0)) >= PAD_FRONT).astype(F32)
        has_next = jnp.where(n == nb - 1, 0.0, 1.0)
        ri, ci = _iota((TB, EXT), 0), _iota((TB, EXT), 1)
        groups = range(len(POOL_WINDOWS))
        sls = [slice(gi * TB, (gi + 1) * TB) for gi in groups]
        d_p = [dp_ref[:, sl] for sl in sls]
        ext = [jnp.concatenate([d_p[gi] * _pool_count_inv(n, w), dn_ref[:, sls[gi]] * (has_next / w)], axis=0)
               for gi, w in enumerate(POOL_WINDOWS)]
        band = [jnp.where((ci >= ri) & (ci < ri + w), 1.0, 0.0).astype(BF16) for w in POOL_WINDOWS]
        back = [_xdot2(band[gi], ext[gi]) for gi in groups]
        for gi in groups:
            duc_ref[:, sls[gi]] = ((back[gi] - d_p[gi]) * valid).astype(BF16)

    return pl.pallas_call(
        body, name=name, grid=(nb,),
        in_specs=[pl.BlockSpec((TB, 512), lambda n: (n, 0)),
                  pl.BlockSpec((N_META, 512), lambda n: (jnp.minimum(n + 1, nb - 1) * (TB // N_META), 0))],
        out_specs=pl.BlockSpec((TB, 512), lambda n: (n, 0)),
        out_shape=jax.ShapeDtypeStruct((p, 512), BF16),
        compiler_params=_params("parallel"),
    )(dp, dp)


def _carried(carries, key, local, fn, *args, **kw):
    if key not in carries:
        return fn(*args, **kw)
    make_src, gather, done = carries[key]
    *outs, stack = fn(*args, carry=(make_src(local), gather), **kw)
    done(stack)
    return outs


def _layer_fwd(h, w, tag, carries, xn=None):
    if xn is None:
        xn, = _rmsnorm_fwd(h, w["norm_g"], name=f"rmsnorm_fwd{tag}")
    proj, = _carried(carries, "in_proj", None, _mm_nn, xn, w["w_in"], name=f"in_proj{tag}")
    oa, sck = _carried(carries, "hgrn_fwd", None, _hgrn_fwd, proj, w["lb"], name=f"hgrn_fwd{tag}")
    qh, k2, v2 = _qk_prep(proj, w["gq"], w["gk"], w["cos"], w["sin"], name=f"qk_prep{tag}")
    yb, lse = _carried(carries, "attn_fwd", None, _attn_fwd, qh, k2, v2, w["sinks"], name=f"attn_fwd{tag}")
    mixed, pooled = _mix_fwd(proj, oa, yb, w["hg"], w["pool_w"], w["pool_scale"], name=f"mix_fwd{tag}")
    h_next, = _mm_nn(mixed, w["w_out"], h, name=f"out_proj{tag}", tn=512)
    saved = dict(h=h, xn=xn, proj=proj, oa=oa, sck=sck, qh=qh, k2=k2, v2=v2, yb=yb, lse=lse, mixed=mixed, pooled=pooled)
    return h_next, saved


def _layer_bwd(dh_out, dhb, s, w, tag, carries, first_layer=False):
    g = {}
    d_mixed, = _carried(carries, "d_mixed", g, _mm_nt, dhb, w["w_out"], name=f"d_mixed{tag}", tm=1664, tn=512)
    g["w_out"], = _mm_tn(s["mixed"], dhb, name=f"dw_out{tag}", tn=1024)
    d_oa, d_yb, d_ga, d_gb, d_gc, d_p, g["hg"], g["pool_w"], g["pool_scale"] = _carried(
        carries, "mix_bwd", g, _mix_bwd,
        s["proj"], s["oa"], s["yb"], s["pooled"], w["hg"], w["pool_w"], w["pool_scale"], d_mixed, name=f"mix_bwd{tag}")
    d_uc = _pool_bwd(d_p, name=f"pool_bwd{tag}")
    d_qh, dkc, dkp, dvc, dvp, dkm, dvm, g["sinks"] = _carried(
        carries, "attn_bwd", g, _attn_bwd, s["qh"], s["k2"], s["v2"], w["sinks"], s["yb"], s["lse"], d_yb,
        name=f"attn_bwd{tag}")
    d_qb, d_kb, d_vb, g["gq"], g["gk"] = _qk_post(s["proj"], w["gq"], w["gk"], w["cos"], w["sin"], d_qh, dkc, dkp, dkm,
                                                  dvc, dvp, dvm, name=f"qk_post{tag}")
    d_qa, d_fa, d_ia, g["lb"] = _carried(carries, "hgrn_bwd", g, _hgrn_bwd, s["proj"], w["lb"], s["sck"], d_oa,
                                         name=f"hgrn_bwd{tag}")
    d_proj = jnp.concatenate([d_qa, d_fa, d_ia, d_ga, d_qb, d_kb, d_vb, d_gb, d_uc, d_gc], axis=1)
    g["w_in_parts"] = []
    for i in range(W_IN_PARTS):
        part, = _carried(carries, f"dw_in_{i}", g, _mm_tn, s["xn"], d_proj, name=f"dw_in_{i}{tag}",
                         m_part=(i, W_IN_PARTS))
        g["w_in_parts"].append(part)
    d_xn, = _carried(carries, "d_xn", g, _mm_nt, d_proj, w["w_in"], name=f"d_xn{tag}", tk=PROJ_COLS)
    out, aux, g["norm_g"] = _rmsnorm_bwd(d_xn, s["h"], w["norm_g"], dh_out, name=f"rmsnorm_bwd{tag}", first_layer=first_layer)
    return out, aux, g


def _peers():
    x, y, c = lax.axis_index("x"), lax.axis_index("y"), lax.axis_index("c")
    out = []
    for k in range(1, N_DEV):
        kx, ky, kc = (k >> 2) & 1, (k >> 1) & 1, k & 1
        px, py, pc = x ^ kx, y ^ ky, c ^ kc
        out.append(((px, py, pc), 4 * px + 2 * py + pc))
    return 4 * x + 2 * y + c, out


def _exchange_copies(src_ref, out_ref, send_sems, recv_sems, local_sem):
    me, peers = _peers()
    mine = pltpu.make_async_copy(src_ref.at[me], out_ref.at[me], local_sem)
    copies = []
    for k, (dev, idx) in enumerate(peers):
        copies.append(pltpu.make_async_remote_copy(
            src_ref=src_ref.at[idx], dst_ref=out_ref.at[me],
            send_sem=send_sems.at[k], recv_sem=recv_sems.at[k],
            device_id=dev, device_id_type=pl.DeviceIdType.MESH))
    return mine, copies


def _gather_copies(src_ref, out_ref, send_sems, recv_sems, local_sem):
    x, y, c = lax.axis_index("x"), lax.axis_index("y"), lax.axis_index("c")
    slot = lambda px, py, pc: out_ref.at[4 * px + 2 * py + pc]
    sibling = (x, y, 1 - c)
    chips = [(1 - x, y), (x, 1 - y), (1 - x, 1 - y)]

    def copy(k, src, block, to):
        return pltpu.make_async_remote_copy(src_ref=src, dst_ref=slot(*block), send_sem=send_sems.at[k],
                                            recv_sem=recv_sems.at[k], device_id=to, device_id_type=pl.DeviceIdType.MESH)

    mine = lambda: pltpu.make_async_copy(src_ref, slot(x, y, c), local_sem)
    own = lambda: ([copy(0, src_ref, (x, y, c), sibling)]
                   + [copy(1 + j, src_ref, (x, y, c), (*chip, c)) for j, chip in enumerate(chips)])
    passing = lambda: [copy(4 + j, slot(*chip, c), (*chip, c), sibling) for j, chip in enumerate(chips)]
    arrivals = lambda: ([copy(0, src_ref, sibling, sibling)]
                        + [copy(1 + j, src_ref, (*chip, c), sibling) for j, chip in enumerate(chips)]
                        + [copy(4 + j, src_ref, (*chip, 1 - c), sibling) for j, chip in enumerate(chips)])
    return mine, own, passing, arrivals


def _exchange_start(*refs, gather):
    if gather:
        mine, own, _, _ = _gather_copies(*refs)
        mine().start()
        for cp in own():
            cp.start()
        return
    mine, copies = _exchange_copies(*refs)
    mine.start()
    for cp in copies:
        cp.start()


def _exchange_wait(*refs, gather):
    if gather:
        mine, own, passing, arrivals = _gather_copies(*refs)
        passing, arrivals = passing(), arrivals()
        for j, cp in enumerate(passing):
            arrivals[1 + j].wait_recv()
            cp.start()
        arrivals[0].wait_recv()
        for cp in arrivals[4:]:
            cp.wait_recv()
        for cp in own() + passing:
            cp.wait_send()
        mine().wait()
        return
    mine, copies = _exchange_copies(*refs)
    for cp in copies:
        cp.wait_recv()
    for cp in copies:
        cp.wait_send()
    mine.wait()


def _exchange_scratch():
    return [pltpu.SemaphoreType.DMA((N_DEV - 1,)), pltpu.SemaphoreType.DMA((N_DEV - 1,)), pltpu.SemaphoreType.DMA]


def _exchange(src, *, gather, name):
    rows, cols = src.shape[-2:]

    def body(src_ref, out_ref, send_sems, recv_sems, local_sem):
        _exchange_start(src_ref, out_ref, send_sems, recv_sems, local_sem, gather=gather)
        _exchange_wait(src_ref, out_ref, send_sems, recv_sems, local_sem, gather=gather)

    return pl.pallas_call(
        body, name=name,
        in_specs=[pl.BlockSpec(memory_space=pl.ANY)], out_specs=pl.BlockSpec(memory_space=pl.ANY),
        out_shape=jax.ShapeDtypeStruct((N_DEV, rows, cols), src.dtype),
        scratch_shapes=_exchange_scratch(),
    )(src)


def _call(body, *, name, grid, in_specs, out_specs, out_shape, args, sem, scratch_shapes=(), carry=None):
    if carry is None:
        return pl.pallas_call(
            body, name=name, grid=grid, in_specs=list(in_specs), out_specs=list(out_specs), out_shape=list(out_shape),
            scratch_shapes=list(scratch_shapes), compiler_params=_params(*sem))(*args)
    src, gather = carry
    n_in, n_out, n_scr = len(in_specs), len(out_specs), len(scratch_shapes)
    rows, cols = src.shape[-2:]

    def carrying(*refs):
        ins, src_ref = refs[:n_in], refs[n_in]
        outs, dst_ref = refs[n_in + 1:n_in + 1 + n_out], refs[n_in + 1 + n_out]
        scr = refs[n_in + 2 + n_out:]
        exch = (src_ref, dst_ref) + tuple(scr[n_scr:])
        first, last = None, None
        for a, size in enumerate(grid):
            f, l = pl.program_id(a) == 0, pl.program_id(a) == size - 1
            first = f if first is None else first & f
            last = l if last is None else last & l

        @pl.when(first)
        def _():
            _exchange_start(*exch, gather=gather)

        body(*ins, *outs, *scr[:n_scr])

        @pl.when(last)
        def _():
            _exchange_wait(*exch, gather=gather)

    hbm = pl.BlockSpec(memory_space=pl.ANY)
    return pl.pallas_call(
        carrying, name=name, grid=grid, in_specs=list(in_specs) + [hbm], out_specs=list(out_specs) + [hbm],
        out_shape=list(out_shape) + [jax.ShapeDtypeStruct((N_DEV, rows, cols), src.dtype)],
        scratch_shapes=list(scratch_shapes) + _exchange_scratch(),
        compiler_params=_params(*(("arbitrary",) * len(grid))))(*args, src)


def _adamw(stacks, w, m, v, *, name, carry=None):
    nl = len(stacks)
    rows, cols = stacks[0].shape[1:]
    tr = rows
    stack_block_bytes = 8 * 1024 * 1024 // nl
    for cand in (256, 128, 64, 32, 16):
        if rows % cand == 0 and N_DEV * cand * cols * stacks[0].dtype.itemsize <= stack_block_bytes:
            tr = cand
            break
    nt = rows // tr

    def body(*refs):
        s_refs = refs[:nl]
        w_ref, m_ref, v_ref, g_ref, d_ref, nm_ref, nv_ref = refs[nl:]
        for l, s_ref in enumerate(s_refs):
            @pl.when(pl.program_id(0) == l)
            def _(s_ref=s_ref):
                acc = s_ref[0].astype(F32)
                for d in range(1, N_DEV):
                    acc = acc + s_ref[d].astype(F32)
                g_ref[...] = acc

        g = g_ref[...]
        nm = ADAM_B1 * m_ref[...] + (1.0 - ADAM_B1) * g
        nv = ADAM_B2 * v_ref[...] + (1.0 - ADAM_B2) * (g * g)
        m_hat = nm / (1.0 - ADAM_B1 ** ADAM_STEP)
        v_hat = nv / (1.0 - ADAM_B2 ** ADAM_STEP)
        d_ref[...] = -ADAM_LR * (m_hat / (jnp.sqrt(v_hat) + ADAM_EPS) + ADAM_WD * w_ref[...])
        nm_ref[...] = nm
        nv_ref[...] = nv

    blk = pl.BlockSpec((tr, cols), lambda l, i: (l * nt + i, 0))
    return _call(
        body, name=name, grid=(nl, nt),
        in_specs=[pl.BlockSpec((N_DEV, tr, cols), lambda l, i, k=k: (0, jnp.where(l == k, i, 0), 0)) for k in range(nl)]
        + [blk, blk, blk],
        out_specs=[blk] * 4, out_shape=[jax.ShapeDtypeStruct((nl * rows, cols), F32)] * 4,
        args=(*stacks, w, m, v), sem=("arbitrary", "arbitrary"), carry=carry)


def _lb_all(lb_logits):
    sm = jax.nn.softmax(lb_logits.astype(F32), axis=0)
    return jnp.cumsum(sm, axis=0) - sm[0:1]


def _rope_tables(p):
    half = DH // 2
    inv = jnp.power(ROPE_THETA, -jnp.arange(half, dtype=F32) * 2.0 / DH)
    pos = (jnp.arange(p) - PAD_FRONT).astype(F32)
    ang = pos[:, None] * inv[None, :]
    cos, sin = jnp.cos(ang), jnp.sin(ang)
    return jnp.tile(cos, (1, 4)), jnp.tile(jnp.concatenate([-sin, sin], axis=1), (1, 2))


SMALL = (("lb_logits", (DEPTH, 512)), ("q_norm_g", (DEPTH, DH)),
         ("k_norm_g", (DEPTH, DH)), ("attn_sinks", (DEPTH, HB)), ("hgrn_norm_g", (DEPTH, 128)),
         ("pool_w", (DEPTH, 4, 128, 128)), ("pool_scale", (DEPTH, 512)))


def _pack_small(d):
    flat = jnp.concatenate([d[k].astype(F32).reshape(-1) for k, _ in SMALL])
    pad = (-flat.shape[0]) % (8 * 128)
    return jnp.pad(flat, (0, pad)).reshape(-1, 128)


def _unpack_small(a):
    flat = a.reshape(-1)
    out, off = {}, 0
    for k, shp in SMALL:
        n = int(np.prod(shp))
        out[k] = flat[off:off + n].reshape(shp)
        off += n
    return out


def kernel(x, meta_tokens, lb_logits, norm_g, w_in, q_norm_g, k_norm_g, attn_sinks, hgrn_norm_g, pool_w, pool_scale, w_out, loss_target, m_meta_tokens, m_lb_logits, m_norm_g, m_w_in, m_q_norm_g, m_k_norm_g, m_attn_sinks, m_hgrn_norm_g, m_pool_w, m_pool_scale, m_w_out, v_meta_tokens, v_lb_logits, v_norm_g, v_w_in, v_q_norm_g, v_k_norm_g, v_attn_sinks, v_hgrn_norm_g, v_pool_w, v_pool_scale, v_w_out):
    seq = x.shape[1]
    p = seq + TB
    cs = PROJ_COLS // N_DEV
    rs = MIX // N_DEV
    ms = D_MODEL // N_DEV

    full_w_in = lambda st: st.transpose(1, 0, 2).reshape(D_MODEL, PROJ_COLS)
    dw_in_blocks = lambda rows: rows.reshape(rows.shape[0], N_DEV, cs).transpose(1, 0, 2)
    dw_out_blocks = lambda g: g["w_out"].reshape(N_DEV, rs, D_MODEL)
    w_in_bf = w_in.astype(BF16)

    lb_all, lb_vjp = jax.vjp(_lb_all, lb_logits)
    cos, sin = _rope_tables(p)
    layers = []
    for l in range(DEPTH):
        layers.append(dict(
            norm_g=norm_g[l][None], lb=lb_all[l][None],
            gq=jnp.tile(q_norm_g[l], 2)[None], gk=jnp.tile(k_norm_g[l], 2)[None], sinks=attn_sinks[l],
            hg=hgrn_norm_g[l][None], pool_w=pool_w[l].astype(BF16), pool_scale=pool_scale[l][None], cos=cos, sin=sin))
    meta_all = _exchange(meta_tokens, gather=True, name="gather_meta")
    meta_full = meta_all.transpose(1, 0, 2).reshape(N_META, D_MODEL)

    def got_w_out(st):
        st = st.reshape(N_DEV, DEPTH, rs, D_MODEL)
        for l in range(DEPTH):
            layers[l]["w_out"] = st[:, l].reshape(MIX, D_MODEL)

    def got_w_in(l):
        def done(st):
            layers[l]["w_in"] = full_w_in(st)
        return done

    fwd_carries = [
        dict(in_proj=(lambda _: w_in_bf[1], True, got_w_in(1)),
             hgrn_fwd=(lambda _: w_out.reshape(DEPTH * rs, D_MODEL).astype(BF16), True, got_w_out)),
        {}]
    h, xn, w_in_l0 = _embed_rmsnorm_fwd(x[0], meta_full, layers[0]["norm_g"], name="embed_rmsnorm_fwd_l0",
                                        carry=(w_in_bf[0], True))
    got_w_in(0)(w_in_l0)
    saved = []
    for l in range(DEPTH):
        h, s = _layer_fwd(h, layers[l], f"_l{l}", fwd_carries[l], xn=xn if l == 0 else None)
        saved.append(s)
    dh, dhb, sq = _loss_grad(h, loss_target[0], name="loss_grad")
    loss = lax.psum(0.5 * jnp.sum(sq) / D_MODEL, ("x", "y", "c"))

    grads = [None] * DEPTH
    win_stacks, wout_stacks, small_stacks = [None] * (W_IN_PARTS * DEPTH), [None] * DEPTH, [None]

    def into(stacks, i):
        def done(st):
            stacks[i] = st
        return done

    def small_grads(g0):
        both = [g0, grads[1]]
        stk = lambda k: jnp.stack([both[l][k][0] for l in range(DEPTH)])
        fold = lambda a: a[:, :DH] + a[:, DH:]
        return _pack_small(dict(
            lb_logits=lb_vjp(stk("lb"))[0], q_norm_g=fold(stk("gq")), k_norm_g=fold(stk("gk")),
            attn_sinks=stk("sinks")[:, :HB], hgrn_norm_g=stk("hg"),
            pool_w=jnp.stack([both[l]["pool_w"] for l in range(DEPTH)]), pool_scale=stk("pool_scale")))

    def part(l, i):
        return (lambda g: dw_in_blocks(g["w_in_parts"][i])), False, into(win_stacks, W_IN_PARTS * l + i)

    bwd_carries = [
        dict(hgrn_bwd=(dw_out_blocks, False, into(wout_stacks, 0)), dw_in_0=(small_grads, True, into(small_stacks, 0)),
             dw_in_1=part(0, 0), d_xn=part(0, 1)),
        dict(attn_bwd=(dw_out_blocks, False, into(wout_stacks, 1)), dw_in_1=part(1, 0), d_xn=part(1, 1))]
    dh, dhb, grads[1] = _layer_bwd(dh, dhb, saved[1], layers[1], "_l1", bwd_carries[1])
    grad_x, d_front, grads[0] = _layer_bwd(dh, dhb, saved[0], layers[0], "_l0", bwd_carries[0], first_layer=True)
    grad_x = grad_x[None]
    dmeta = d_front.reshape(N_META, N_DEV, ms).transpose(1, 0, 2)
    meta_stack = _exchange(dmeta, gather=False, name="scatter_dmeta")
    small_stack = small_stacks[0]
    d_norm_g = jnp.stack([grads[l]["norm_g"][0] for l in range(DEPTH)]).reshape(-1, 128)
    norm_stack = _exchange(d_norm_g, gather=True, name="gather_norm_g_grad")
    adam_norm = [a.reshape(DEPTH, D_MODEL) for a in _adamw(
        [norm_stack], norm_g.reshape(-1, 128), m_norm_g.reshape(-1, 128), v_norm_g.reshape(-1, 128), name="adamw_norm_g")]

    g_wout, d_wout, nm_wout, nv_wout = _adamw(wout_stacks, w_out.reshape(DEPTH * rs, D_MODEL), m_w_out.reshape(DEPTH * rs, D_MODEL),
                                              v_w_out.reshape(DEPTH * rs, D_MODEL), name="adamw_w_out")
    g_win, d_win, nm_win, nv_win = _adamw(win_stacks, w_in.reshape(DEPTH * D_MODEL, cs), m_w_in.reshape(DEPTH * D_MODEL, cs),
                                          v_w_in.reshape(DEPTH * D_MODEL, cs), name="adamw_w_in")
    g_meta, d_meta, nm_meta, nv_meta = _adamw([meta_stack], meta_tokens, m_meta_tokens, v_meta_tokens, name="adamw_meta")
    small_w = dict(lb_logits=lb_logits, norm_g=norm_g, q_norm_g=q_norm_g, k_norm_g=k_norm_g, attn_sinks=attn_sinks,
                   hgrn_norm_g=hgrn_norm_g, pool_w=pool_w, pool_scale=pool_scale)
    small_m = dict(lb_logits=m_lb_logits, norm_g=m_norm_g, q_norm_g=m_q_norm_g, k_norm_g=m_k_norm_g, attn_sinks=m_attn_sinks,
                   hgrn_norm_g=m_hgrn_norm_g, pool_w=m_pool_w, pool_scale=m_pool_scale)
    small_v = dict(lb_logits=v_lb_logits, norm_g=v_norm_g, q_norm_g=v_q_norm_g, k_norm_g=v_k_norm_g, attn_sinks=v_attn_sinks,
                   hgrn_norm_g=v_hgrn_norm_g, pool_w=v_pool_w, pool_scale=v_pool_scale)
    small_out = [_unpack_small(a) for a in _adamw([small_stack], _pack_small(small_w), _pack_small(small_m),
                                                  _pack_small(small_v), name="adamw_small")]

    big = dict(
        meta_tokens=(g_meta, d_meta, nm_meta, nv_meta), norm_g=tuple(adam_norm),
        w_in=tuple(a.reshape(DEPTH, D_MODEL, cs) for a in (g_win, d_win, nm_win, nv_win)),
        w_out=tuple(a.reshape(DEPTH, rs, D_MODEL) for a in (g_wout, d_wout, nm_wout, nv_wout)))
    order = ("meta_tokens", "lb_logits", "norm_g", "w_in", "q_norm_g", "k_norm_g", "attn_sinks", "hgrn_norm_g",
             "pool_w", "pool_scale", "w_out")
    outs = [loss, grad_x]
    for kind in range(4):
        for k in order:
            outs.append(big[k][kind] if k in big else small_out[kind][k])
    return tuple(outs)
```

```python
import functools

import numpy as np
import jax
import jax.numpy as jnp
from jax import lax
from jax.experimental import pallas as pl
from jax.experimental.pallas import tpu as pltpu

F32, BF16 = jnp.float32, jnp.bfloat16

D_MODEL = 2048
DEPTH = 2
N_META = 16
TB = 128
PAD_FRONT = TB - N_META
RMS_EPS = 1e-6
NEG_INF = -1e30
LOG_FLOOR = 1e-30
HA, DK_A = 4, 128
CH = 16
NCH = TB // CH
HB, KVH, DH = 16, 2, 64
GRP = HB // KVH
ROPE_THETA = 10000.0
POOL_WINDOWS = (2, 4, 8, 16)
PROJ_COLS = 5376
MIX = 2048
N_DEV = 8
W_IN_PARTS = 2
C_QA, C_FA, C_IA, C_GA = 0, 1, 2, 3
C_QB = 2
C_KB, C_VB = 24, 25
C_GB, C_UC, C_GC = (13, 14, 15, 16), (17, 18), (19, 20)


class _Cols:
    def __init__(self, refs):
        self.refs, self.width = refs, refs[0].shape[1]

    def __getitem__(self, idx):
        rows, sl = idx
        k, off = divmod(sl.start, self.width)
        return self.refs[k][rows, off:off + sl.stop - sl.start]

ADAM_LR, ADAM_B1, ADAM_B2, ADAM_EPS, ADAM_WD, ADAM_STEP = 0.001, 0.9, 0.999, 1e-08, 0.01, 10

VMEM_LIMIT = 48 * 1024 * 1024

NN = ((1,), (0,))
NT = ((1,), (1,))
TN = ((0,), (0,))


def _dot(a, b, dims):
    return lax.dot_general(a, b, (dims, ((), ())), preferred_element_type=F32)


def _split3(x):
    hi = x.astype(BF16)
    r = x - hi.astype(F32)
    mid = r.astype(BF16)
    lo = (r - mid.astype(F32)).astype(BF16)
    return hi, mid, lo


def _xdot(m01, x):
    hi, mid, lo = _split3(x)
    return _dot(m01, hi, NN) + _dot(m01, mid, NN) + _dot(m01, lo, NN)


def _xdot2(m01, x):
    hi = x.astype(BF16)
    lo = (x - hi.astype(F32)).astype(BF16)
    return _dot(m01, hi, NN) + _dot(m01, lo, NN)


def _xdot_r(x, m01):
    hi, mid, lo = _split3(x)
    return _dot(hi, m01, NN) + _dot(mid, m01, NN) + _dot(lo, m01, NN)


def _iota(shape, dim):
    return lax.broadcasted_iota(jnp.int32, shape, dim)


def _params(*sem):
    return pltpu.CompilerParams(dimension_semantics=sem, vmem_limit_bytes=VMEM_LIMIT)


def _row_tile(p, target):
    best = TB
    t = TB
    while t <= target:
        if p % t == 0:
            best = t
        t += TB
    return best


def _col_tile(n, target):
    best = 128
    t = 128
    while t <= target:
        if n % t == 0:
            best = t
        t += 128
    return best


def _sigmoid(x):
    return 1.0 / (1.0 + jnp.exp(-x))


def _mm_nn(a, b, res=None, *, name, tm=1664, tn=768, carry=None):
    m, k = a.shape
    n = b.shape[1]
    tm, tn = _row_tile(m, tm), _col_tile(n, tn)

    def body(*refs):
        if res is None:
            a_ref, b_ref, o_ref = refs
            o_ref[...] = _dot(a_ref[...], b_ref[...], NN)
        else:
            a_ref, b_ref, r_ref, o_ref = refs
            o_ref[...] = r_ref[...] + _dot(a_ref[...], b_ref[...], NN)

    in_specs = [pl.BlockSpec((tm, k), lambda j, i: (i, 0)), pl.BlockSpec((k, tn), lambda j, i: (0, j))]
    args = [a, b]
    if res is not None:
        in_specs.append(pl.BlockSpec((tm, tn), lambda j, i: (i, j)))
        args.append(res)
    return _call(
        body, name=name, grid=(n // tn, m // tm), in_specs=in_specs,
        out_specs=[pl.BlockSpec((tm, tn), lambda j, i: (i, j))],
        out_shape=[jax.ShapeDtypeStruct((m, n), F32)],
        args=args, sem=("parallel", "parallel"), carry=carry)


def _mm_nt(a, b, *, name, tm=640, tn=512, tk=2048, carry=None):
    m, k = a.shape
    n = b.shape[0]
    tm, tn, tk = _row_tile(m, tm), _col_tile(n, tn), _col_tile(k, tk)

    def body(a_ref, b_ref, o_ref):
        if k == tk:
            o_ref[...] = _dot(a_ref[...], b_ref[...], NT)
            return

        @pl.when(pl.program_id(2) == 0)
        def _():
            o_ref[...] = jnp.zeros_like(o_ref)

        o_ref[...] += _dot(a_ref[...], b_ref[...], NT)

    return _call(
        body, name=name, grid=(n // tn, m // tm, k // tk),
        in_specs=[pl.BlockSpec((tm, tk), lambda j, i, kk: (i, kk)), pl.BlockSpec((tn, tk), lambda j, i, kk: (j, kk))],
        out_specs=[pl.BlockSpec((tm, tn), lambda j, i, kk: (i, j))],
        out_shape=[jax.ShapeDtypeStruct((m, n), F32)],
        args=(a, b), sem=("parallel", "parallel", "arbitrary"), carry=carry)


def _mm_tn(a, b, *, name, tm=1024, tn=1344, tk=1664, m_part=None, carry=None):
    k, m = a.shape
    n = b.shape[1]
    first, m = (0, m) if m_part is None else (m_part[0], m // m_part[1])
    tm, tn, tk = _col_tile(m, tm), _col_tile(n, tn), _row_tile(k, tk)
    first *= m // tm
    nk = k // tk

    def body(a_ref, b_ref, o_ref, acc_ref):
        @pl.when(pl.program_id(2) == 0)
        def _():
            acc_ref[...] = jnp.zeros_like(acc_ref)

        acc_ref[...] += _dot(a_ref[...], b_ref[...], TN)

        @pl.when(pl.program_id(2) == nk - 1)
        def _():
            o_ref[...] = acc_ref[...].astype(BF16)

    return _call(
        body, name=name, grid=(m // tm, n // tn, nk),
        in_specs=[pl.BlockSpec((tk, tm), lambda i, j, kk: (kk, first + i)), pl.BlockSpec((tk, tn), lambda i, j, kk: (kk, j))],
        out_specs=[pl.BlockSpec((tm, tn), lambda i, j, kk: (i, j))],
        out_shape=[jax.ShapeDtypeStruct((m, n), BF16)], scratch_shapes=[pltpu.VMEM((tm, tn), F32)],
        args=(a, b), sem=("parallel", "parallel", "arbitrary"), carry=carry)


def _rmsnorm_fwd(h, g, *, name, carry=None):
    p, dm = h.shape
    tm = _row_tile(p, 640)

    def body(h_ref, g_ref, xn_ref):
        hv = h_ref[...]
        r = lax.rsqrt(jnp.mean(hv * hv, axis=-1, keepdims=True) + RMS_EPS)
        xn_ref[...] = (hv * r * g_ref[...]).astype(BF16)

    return _call(
        body, name=name, grid=(p // tm,),
        in_specs=[pl.BlockSpec((tm, dm), lambda i: (i, 0)), pl.BlockSpec((1, dm), lambda i: (0, 0))],
        out_specs=[pl.BlockSpec((tm, dm), lambda i: (i, 0))],
        out_shape=[jax.ShapeDtypeStruct((p, dm), BF16)],
        args=(h, g), sem=("parallel",), carry=carry)


def _embed_rmsnorm_fwd(x, meta, g, *, name, carry=None):
    seq, dm = x.shape
    p = seq + TB

    def body(x_ref, meta_ref, g_ref, h_ref, xn_ref):
        @pl.when(pl.program_id(0) == 0)
        def _():
            h_ref[...] = jnp.zeros_like(h_ref)
            h_ref[PAD_FRONT:TB, :] = meta_ref[...]

        @pl.when(pl.program_id(0) > 0)
        def _():
            h_ref[...] = x_ref[...]

        hv = h_ref[...]
        r = lax.rsqrt(jnp.mean(hv * hv, axis=-1, keepdims=True) + RMS_EPS)
        xn_ref[...] = (hv * r * g_ref[...]).astype(BF16)

    row = pl.BlockSpec((TB, dm), lambda i: (i, 0))
    return _call(
        body, name=name, grid=(p // TB,),
        in_specs=[pl.BlockSpec((TB, dm), lambda i: (jnp.maximum(i - 1, 0), 0)), pl.BlockSpec((N_META, dm), lambda i: (0, 0)),
                  pl.BlockSpec((1, dm), lambda i: (0, 0))],
        out_specs=[row, row],
        out_shape=[jax.ShapeDtypeStruct((p, dm), F32), jax.ShapeDtypeStruct((p, dm), BF16)],
        args=(x, meta, g), sem=("arbitrary",), carry=carry)


def _rmsnorm_bwd(dxn, h, g, dh_out, *, name, first_layer=False, carry=None):
    p, dm = h.shape
    tm = TB if first_layer else _row_tile(p, 384)

    def body(dxn_ref, h_ref, g_ref, dho_ref, out_ref, aux_ref, dg_ref):
        hv = h_ref[...]
        r = lax.rsqrt(jnp.mean(hv * hv, axis=-1, keepdims=True) + RMS_EPS)
        xh = hv * r
        dy = dxn_ref[...]
        dyn = dy * g_ref[...]
        dh = dho_ref[...] + r * (dyn - xh * jnp.mean(dyn * xh, axis=-1, keepdims=True))
        out_ref[...] = dh

        @pl.when(pl.program_id(0) == 0)
        def _():
            dg_ref[...] = jnp.zeros_like(dg_ref)
            if first_layer:
                aux_ref[...] = dh[PAD_FRONT:TB]

        if not first_layer:
            aux_ref[...] = dh.astype(BF16)
        dg_ref[...] += jnp.sum(dy * xh, axis=0, keepdims=True)

    row = pl.BlockSpec((tm, dm), lambda i: (i, 0))
    vec = pl.BlockSpec((1, dm), lambda i: (0, 0))
    if first_layer:
        out_specs = [pl.BlockSpec((TB, dm), lambda i: (jnp.maximum(i - 1, 0), 0)), pl.BlockSpec((N_META, dm), lambda i: (0, 0)), vec]
        out_shape = [jax.ShapeDtypeStruct((p - TB, dm), F32), jax.ShapeDtypeStruct((N_META, dm), F32)]
    else:
        out_specs = [row, row, vec]
        out_shape = [jax.ShapeDtypeStruct((p, dm), F32), jax.ShapeDtypeStruct((p, dm), BF16)]
    return _call(
        body, name=name, grid=(p // tm,),
        in_specs=[row, row, vec, row], out_specs=out_specs,
        out_shape=out_shape + [jax.ShapeDtypeStruct((1, dm), F32)],
        args=(dxn, h, g, dh_out), sem=("arbitrary",), carry=carry)


def _loss_grad(h, target, *, name):
    p, dm = h.shape

    def body(h_ref, t_ref, dh_ref, dhb_ref, sq_ref):
        n = pl.program_id(0)

        @pl.when(n == 0)
        def _():
            dh_ref[...] = jnp.zeros_like(dh_ref)
            dhb_ref[...] = jnp.zeros_like(dhb_ref)
            sq_ref[...] = jnp.zeros_like(sq_ref)

        @pl.when(n > 0)
        def _():
            err = h_ref[...] - t_ref[...]
            dh = err * (1.0 / dm)
            dh_ref[...] = dh
            dhb_ref[...] = dh.astype(BF16)
            sq_ref[...] += jnp.sum(err * err, axis=0, keepdims=True)

    row = pl.BlockSpec((TB, dm), lambda n: (n, 0))
    return pl.pallas_call(
        body, name=name, grid=(p // TB,),
        in_specs=[row, pl.BlockSpec((TB, dm), lambda n: (jnp.maximum(n - 1, 0), 0))],
        out_specs=[row, row, pl.BlockSpec((1, dm), lambda n: (0, 0))],
        out_shape=[jax.ShapeDtypeStruct((p, dm), F32), jax.ShapeDtypeStruct((p, dm), BF16), jax.ShapeDtypeStruct((1, dm), F32)],
        compiler_params=_params("arbitrary"),
    )(h, target)


def _chunk_masks():
    ri, ci = _iota((TB, TB), 0), _iota((TB, TB), 1)
    same = (ri >> 4) == (ci >> 4)
    causal = same & (ci <= ri)
    lower = jnp.where(causal, 1.0, 0.0).astype(BF16)
    upper = jnp.where(same & (ci >= ri), 1.0, 0.0).astype(BF16)
    ones = jnp.where(same, 1.0, 0.0).astype(BF16)
    return causal, lower, upper, ones


def _hgrn_gates(q, z, lbh, m):
    sig = _sigmoid(z)
    f = lbh + (1.0 - lbh) * sig
    lf = jnp.log(jnp.maximum(f, LOG_FLOOR)) * m
    kk = (1.0 - lbh) * (1.0 - sig) * m
    sq = _sigmoid(q)
    return sig, f, lf, kk, sq, q * sq


def _hgrn_fwd(proj, lb, *, name, carry=None):
    p = proj.shape[0]
    nb = p // TB

    def body(qa_ref, fa_ref, ia_ref, lb_ref, oa_ref, sck_ref, st_ref):
        n = pl.program_id(0)

        @pl.when(n == 0)
        def _():
            st_ref[...] = jnp.zeros_like(st_ref)

        causal, lower, _, ones = _chunk_masks()
        m = ((n * TB + _iota((TB, 1), 0)) >= PAD_FRONT).astype(F32)
        heads = range(HA)
        sls = [slice(hd * DK_A, (hd + 1) * DK_A) for hd in heads]
        rows = [slice(c * CH, (c + 1) * CH) for c in range(NCH)]
        gates = [_hgrn_gates(qa_ref[:, sl], fa_ref[:, sl], lb_ref[:, sl], m) for sl in sls]
        lf = [t[2] for t in gates]
        g = [_xdot(lower, x) for x in lf]
        gl = [_xdot(ones, x) for x in lf]
        qd = [(gates[hd][5] * jnp.exp(g[hd])).astype(BF16) for hd in heads]
        kt = [(gates[hd][3] * jnp.exp(-g[hd])).astype(BF16) for hd in heads]
        kd = [(gates[hd][3] * jnp.exp(gl[hd] - g[hd])).astype(BF16) for hd in heads]
        vb = [ia_ref[:, sl].astype(BF16) for sl in sls]
        a_all = [jnp.exp(x) for x in gl]
        att = [jnp.where(causal, _dot(qd[hd], kt[hd], NT), 0.0).astype(BF16) for hd in heads]
        kv = [[_dot(vb[hd][r], kd[hd][r], TN) for r in rows] for hd in heads]
        o = [_dot(att[hd], vb[hd], NN) for hd in heads]
        before = []
        for hd in heads:
            st = st_ref[hd]
            sck_ref[0, hd] = st
            per_chunk = []
            for c in range(NCH):
                per_chunk.append(st.astype(BF16))
                st = st * a_all[hd][c * CH:c * CH + 1, :] + kv[hd][c]
            st_ref[hd] = st
            before.append(per_chunk)
        inter = [[_dot(qd[hd][rows[c]], before[hd][c], NT) for c in range(NCH)] for hd in heads]
        for hd in heads:
            oa_ref[:, sls[hd]] = o[hd] + jnp.concatenate(inter[hd], axis=0)

    blk = lambda c: pl.BlockSpec((TB, 512), lambda n, c=c: (n, c))
    return _call(
        body, name=name, grid=(nb,),
        in_specs=[blk(C_QA), blk(C_FA), blk(C_IA), pl.BlockSpec((1, 512), lambda n: (0, 0))],
        out_specs=[pl.BlockSpec((TB, 512), lambda n: (n, 0)), pl.BlockSpec((1, HA, TB, TB), lambda n: (n, 0, 0, 0))],
        out_shape=[jax.ShapeDtypeStruct((p, 512), F32), jax.ShapeDtypeStruct((nb, HA, TB, TB), F32)],
        scratch_shapes=[pltpu.VMEM((HA, TB, TB), F32)],
        args=(proj, proj, proj, lb), sem=("arbitrary",), carry=carry)


def _hgrn_bwd(proj, lb, sck, d_oa, *, name, carry=None):
    p = proj.shape[0]
    nb = p // TB

    def body(qa_ref, fa_ref, ia_ref, lb_ref, sck_ref, do_ref, dq_ref, dz_ref, dv_ref, dlb_ref, dst_ref):
        i = pl.program_id(0)
        n = nb - 1 - i

        @pl.when(i == 0)
        def _():
            dst_ref[...] = jnp.zeros_like(dst_ref)
            dlb_ref[...] = jnp.zeros_like(dlb_ref)

        causal, lower, upper, ones = _chunk_masks()
        m = ((n * TB + _iota((TB, 1), 0)) >= PAD_FRONT).astype(F32)
        heads = range(HA)
        sls = [slice(hd * DK_A, (hd + 1) * DK_A) for hd in heads]
        rows = [slice(c * CH, (c + 1) * CH) for c in range(NCH)]
        a_row = lambda a, c: a[c * CH:c * CH + 1, :]
        gates = [_hgrn_gates(qa_ref[:, sl], fa_ref[:, sl], lb_ref[:, sl], m) for sl in sls]
        g = [_xdot(lower, t[2]) for t in gates]
        gl = [_xdot(ones, t[2]) for t in gates]
        e_g = [jnp.exp(x) for x in g]
        e_ng = [jnp.exp(-x) for x in g]
        e_d = [jnp.exp(gl[hd] - g[hd]) for hd in heads]
        a_all = [jnp.exp(x) for x in gl]
        qd_f = [gates[hd][5] * e_g[hd] for hd in heads]
        kt_f = [gates[hd][3] * e_ng[hd] for hd in heads]
        kd_f = [gates[hd][3] * e_d[hd] for hd in heads]
        qd, kt, kd = ([x.astype(BF16) for x in xs] for xs in (qd_f, kt_f, kd_f))
        vb = [ia_ref[:, sl].astype(BF16) for sl in sls]
        dob = [do_ref[:, sl].astype(BF16) for sl in sls]
        att = [jnp.where(causal, _dot(qd[hd], kt[hd], NT), 0.0).astype(BF16) for hd in heads]
        d_att = [jnp.where(causal, _dot(dob[hd], vb[hd], NT), 0.0).astype(BF16) for hd in heads]
        kv = [[_dot(vb[hd][r], kd[hd][r], TN) for r in rows] for hd in heads]
        dqk = [[_dot(dob[hd][r], qd[hd][r], TN) for r in rows] for hd in heads]
        d_v = [_dot(att[hd], dob[hd], TN) for hd in heads]
        d_qd = [_dot(d_att[hd], kt[hd], NN) for hd in heads]
        d_kt = [_dot(d_att[hd], qd[hd], TN) for hd in heads]
        stc, dsc = [], []
        for hd in heads:
            st, before = sck_ref[0, hd], []
            for c in range(NCH):
                before.append(st)
                if c + 1 < NCH:
                    st = st * a_row(a_all[hd], c) + kv[hd][c]
            dst, after = dst_ref[hd], [None] * NCH
            for c in range(NCH - 1, -1, -1):
                after[c] = dst
                dst = dst * a_row(a_all[hd], c) + dqk[hd][c]
            dst_ref[hd] = dst
            stc.append(before)
            dsc.append(after)
        dscb = [[x.astype(BF16) for x in dsc[hd]] for hd in heads]
        dvs = [[_dot(kd[hd][rows[c]], dscb[hd][c], NT) for c in range(NCH)] for hd in heads]
        dkd = [[_dot(vb[hd][rows[c]], dscb[hd][c], NN) for c in range(NCH)] for hd in heads]
        dqd = [[_dot(dob[hd][rows[c]], stc[hd][c].astype(BF16), NN) for c in range(NCH)] for hd in heads]
        dgl = [[jnp.broadcast_to(jnp.sum(dsc[hd][c] * stc[hd][c], axis=0, keepdims=True) * a_row(a_all[hd], c), (CH, TB))
                for c in range(NCH)] for hd in heads]
        d_qd = [d_qd[hd] + jnp.concatenate(dqd[hd], axis=0) for hd in heads]
        d_kd = [jnp.concatenate(dkd[hd], axis=0) for hd in heads]
        kd_term = [d_kd[hd] * kd_f[hd] for hd in heads]
        d_g = [d_qd[hd] * qd_f[hd] - d_kt[hd] * kt_f[hd] - kd_term[hd] for hd in heads]
        d_lf = [_xdot2(upper, d_g[hd]) + _xdot2(ones, kd_term[hd]) + jnp.concatenate(dgl[hd], axis=0) for hd in heads]
        for hd in heads:
            sl = sls[hd]
            sig, f, _, _, sq, _ = gates[hd]
            q, lbh = qa_ref[:, sl], lb_ref[:, sl]
            d_kk = (d_kt[hd] * e_ng[hd] + d_kd[hd] * e_d[hd]) * m
            t1 = d_lf[hd] * m * jnp.where(f > LOG_FLOOR, 1.0 / f, 0.0)
            dq_ref[:, sl] = (d_qd[hd] * e_g[hd] * (sq * (1.0 + q * (1.0 - sq)))).astype(BF16)
            dz_ref[:, sl] = ((t1 - d_kk) * (1.0 - lbh) * sig * (1.0 - sig)).astype(BF16)
            dv_ref[:, sl] = (d_v[hd] + jnp.concatenate(dvs[hd], axis=0)).astype(BF16)
            dlb_ref[:, sl] += jnp.sum((t1 - d_kk) * (1.0 - sig), axis=0, keepdims=True)

    blk = lambda c: pl.BlockSpec((TB, 512), lambda i, c=c: (nb - 1 - i, c))
    out_blk = pl.BlockSpec((TB, 512), lambda i: (nb - 1 - i, 0))
    vec = pl.BlockSpec((1, 512), lambda i: (0, 0))
    return _call(
        body, name=name, grid=(nb,),
        in_specs=[blk(C_QA), blk(C_FA), blk(C_IA), vec,
                  pl.BlockSpec((1, HA, TB, TB), lambda i: (nb - 1 - i, 0, 0, 0)), out_blk],
        out_specs=[out_blk, out_blk, out_blk, vec],
        out_shape=[jax.ShapeDtypeStruct((p, 512), BF16)] * 3 + [jax.ShapeDtypeStruct((1, 512), F32)],
        scratch_shapes=[pltpu.VMEM((HA, TB, TB), F32)],
        args=(proj, proj, proj, lb, sck, d_oa), sem=("arbitrary",), carry=carry)


def _lane():
    return _iota((1, TB), 1)


def _swap_halves(y):
    first = (_lane() & 63) < 32
    return jnp.where(first, pltpu.roll(y, 96, 1), pltpu.roll(y, 32, 1))


def _head_ones():
    ri, ci = _iota((TB, TB), 0), _iota((TB, TB), 1)
    return jnp.where((ri >> 6) == (ci >> 6), 1.0, 0.0).astype(BF16)


def _norm_rope(x, g, cos, sin, bd):
    r = lax.rsqrt(_xdot_r(x * x, bd) * (1.0 / DH) + RMS_EPS)
    y = x * r * g
    return y * cos + _swap_halves(y) * sin


def _norm_rope_bwd(d_out, x, g, cos, sin, bd):
    d = d_out * cos - _swap_halves(d_out) * sin
    r = lax.rsqrt(_xdot_r(x * x, bd) * (1.0 / DH) + RMS_EPS)
    xh = x * r
    dyn = d * g
    dx = r * (dyn - xh * (_xdot_r(dyn * xh, bd) * (1.0 / DH)))
    return dx, jnp.sum(d * xh, axis=0, keepdims=True)


def _dup_heads(k):
    first = _lane() < DH
    r = pltpu.roll(k, DH, 1)
    return jnp.where(first, k, r), jnp.where(first, r, k)


def _qk_prep(proj, gq, gk, cos, sin, *, name):
    p = proj.shape[0]

    def body(qb_ref, kb_ref, vb_ref, gq_ref, gk_ref, cos_ref, sin_ref, qh_ref, k2_ref, v2_ref):
        bd = _head_ones()
        cos_v, sin_v = cos_ref[...], sin_ref[...]
        for j in range(HB // 2):
            sl = slice(j * TB, (j + 1) * TB)
            qh_ref[:, sl] = _norm_rope(qb_ref[:, sl], gq_ref[...], cos_v, sin_v, bd).astype(BF16)
        k0, k1 = _dup_heads(_norm_rope(kb_ref[...], gk_ref[...], cos_v, sin_v, bd))
        k2_ref[:, 0:TB] = k0.astype(BF16)
        k2_ref[:, TB:2 * TB] = k1.astype(BF16)
        v0, v1 = _dup_heads(vb_ref[...])
        v2_ref[:, 0:TB] = v0.astype(BF16)
        v2_ref[:, TB:2 * TB] = v1.astype(BF16)

    vec = pl.BlockSpec((1, TB), lambda n: (0, 0))
    tab = pl.BlockSpec((TB, TB), lambda n: (n, 0))
    return pl.pallas_call(
        body, name=name, grid=(p // TB,),
        in_specs=[pl.BlockSpec((TB, 1024), lambda n: (n, C_QB)), pl.BlockSpec((TB, TB), lambda n: (n, C_KB)),
                  pl.BlockSpec((TB, TB), lambda n: (n, C_VB)), vec, vec, tab, tab],
        out_specs=[pl.BlockSpec((TB, 1024), lambda n: (n, 0)), pl.BlockSpec((TB, 256), lambda n: (n, 0)),
                   pl.BlockSpec((TB, 256), lambda n: (n, 0))],
        out_shape=[jax.ShapeDtypeStruct((p, 1024), BF16), jax.ShapeDtypeStruct((p, 256), BF16),
                   jax.ShapeDtypeStruct((p, 256), BF16)],
        compiler_params=_params("parallel"),
    )(proj, proj, proj, gq, gk, cos, sin)


NKEY = N_META + 2 * TB


def _attn_mask(n):
    r = _iota((TB, NKEY), 0)
    j = _iota((TB, NKEY), 1)
    meta = (j < N_META) & ((n >= 1) | (j + PAD_FRONT <= r))
    prev = (j >= N_META) & (j < N_META + TB) & (n >= 2) & (j - N_META > r)
    cur = (j >= N_META + TB) & (n >= 1) & (j - (N_META + TB) <= r)
    return meta | prev | cur


def _attn_specs():
    cur = lambda w: pl.BlockSpec((TB, w), lambda n: (n, 0))
    prev = pl.BlockSpec((TB, 256), lambda n: (jnp.maximum(n - 1, 0), 0))
    meta = pl.BlockSpec((N_META, 256), lambda n: (PAD_FRONT // N_META, 0))
    sink = pl.BlockSpec(memory_space=pltpu.SMEM)
    return cur, prev, meta, sink


ATTN_GROUP_FWD, ATTN_GROUP_BWD = 8, 4


def _head_queries(q_ref, kv, first):
    out = []
    for jj in range(GRP // 2):
        j = kv * (GRP // 2) + jj
        qj = q_ref[:, j * TB:(j + 1) * TB] * (DH ** -0.5)
        for half in range(2):
            out.append((j, half, jnp.where(first if half == 0 else ~first, qj, jnp.zeros_like(qj))))
    return out


def _attn_fwd(qh, k2, v2, sinks, *, name, carry=None):
    p = qh.shape[0]

    def body(sink_ref, q_ref, kc_ref, kp_ref, km_ref, vc_ref, vp_ref, vm_ref, o_ref, lse_ref):
        n = pl.program_id(0)
        mask = _attn_mask(n)
        lane = _lane()
        first = lane < DH
        lse_tile = jnp.zeros((TB, TB), F32)
        for kv in range(KVH):
            ks = slice(kv * TB, (kv + 1) * TB)
            kall = jnp.concatenate([km_ref[:, ks], kp_ref[:, ks], kc_ref[:, ks]], axis=0)
            vall = jnp.concatenate([vm_ref[:, ks], vp_ref[:, ks], vc_ref[:, ks]], axis=0)
            hq = _head_queries(q_ref, kv, first)
            for g0 in range(0, GRP, ATTN_GROUP_FWD):
                grp = hq[g0:g0 + ATTN_GROUP_FWD]
                idx = range(len(grp))
                s = [jnp.where(mask, _dot(qm, kall, NT), NEG_INF) for _, _, qm in grp]
                sink = [sink_ref[2 * j + half] for j, half, _ in grp]
                mx = [jnp.maximum(jnp.max(s[i], axis=1, keepdims=True), sink[i]) for i in idx]
                pr = [jnp.exp(s[i] - mx[i]) for i in idx]
                den = [jnp.sum(pr[i], axis=1, keepdims=True) + jnp.exp(sink[i] - mx[i]) for i in idx]
                o = [_dot(pr[i].astype(BF16), vall, NN) * (1.0 / den[i]) for i in idx]
                for i, (j, half, _) in enumerate(grp):
                    lse_tile = lse_tile + jnp.where(lane == 2 * j + half, mx[i] + jnp.log(den[i]), 0.0)
                for i in range(0, len(grp), 2):
                    j = grp[i][0]
                    o_ref[:, j * TB:(j + 1) * TB] = jnp.where(first, o[i], o[i + 1])
        lse_ref[...] = lse_tile

    cur, prev, meta, sink = _attn_specs()
    return _call(
        body, name=name, grid=(p // TB,),
        in_specs=[sink, cur(1024), cur(256), prev, meta, cur(256), prev, meta],
        out_specs=[cur(1024), cur(TB)],
        out_shape=[jax.ShapeDtypeStruct((p, 1024), F32), jax.ShapeDtypeStruct((p, TB), F32)],
        args=(sinks, qh, k2, k2, k2, v2, v2, v2), sem=("parallel",), carry=carry)


def _attn_bwd(qh, k2, v2, sinks, o, lse, d_o, *, name, carry=None):
    p = qh.shape[0]

    def body(sink_ref, q_ref, kc_ref, kp_ref, km_ref, vc_ref, vp_ref, vm_ref, o_ref, lse_ref, do_ref,
             dq_ref, dkc_ref, dkp_ref, dvc_ref, dvp_ref, dkm_ref, dvm_ref, dsink_ref):
        n = pl.program_id(0)

        @pl.when(n == 0)
        def _():
            dkm_ref[...] = jnp.zeros_like(dkm_ref)
            dvm_ref[...] = jnp.zeros_like(dvm_ref)
            dsink_ref[...] = jnp.zeros_like(dsink_ref)

        mask = _attn_mask(n)
        lane = _lane()
        first = lane < DH
        lse_tile = lse_ref[...]
        sink_acc = jnp.zeros((TB, TB), F32)
        for kv in range(KVH):
            ks = slice(kv * TB, (kv + 1) * TB)
            kall = jnp.concatenate([km_ref[:, ks], kp_ref[:, ks], kc_ref[:, ks]], axis=0)
            vall = jnp.concatenate([vm_ref[:, ks], vp_ref[:, ks], vc_ref[:, ks]], axis=0)
            d_kall = jnp.zeros((NKEY, TB), F32)
            d_vall = jnp.zeros((NKEY, TB), F32)
            hq = _head_queries(q_ref, kv, first)
            for g0 in range(0, GRP, ATTN_GROUP_BWD):
                grp = hq[g0:g0 + ATTN_GROUP_BWD]
                idx = range(len(grp))
                s = [jnp.where(mask, _dot(qm, kall, NT), NEG_INF) for _, _, qm in grp]
                dom = [jnp.where(first if half == 0 else ~first, do_ref[:, j * TB:(j + 1) * TB], 0.0) for j, half, _ in grp]
                domb = [x.astype(BF16) for x in dom]
                d_w = [_dot(x, vall, NT) for x in domb]
                delta = [jnp.sum(dom[i] * o_ref[:, grp[i][0] * TB:(grp[i][0] + 1) * TB], axis=1, keepdims=True) for i in idx]
                lse_h = [jnp.sum(jnp.where(lane == 2 * j + half, lse_tile, 0.0), axis=1, keepdims=True) for j, half, _ in grp]
                w = [jnp.exp(s[i] - lse_h[i]) for i in idx]
                for i, (j, half, _) in enumerate(grp):
                    w_sink = jnp.exp(sink_ref[2 * j + half] - lse_h[i])
                    sink_acc = sink_acc + jnp.where(lane == 2 * j + half, -(w_sink * delta[i]), 0.0)
                dsb = [(w[i] * (d_w[i] - delta[i])).astype(BF16) for i in idx]
                d_q = [_dot(x, kall, NN) * (DH ** -0.5) for x in dsb]
                d_k = [_dot(dsb[i], grp[i][2], TN) for i in idx]
                d_v = [_dot(w[i].astype(BF16), domb[i], TN) for i in idx]
                for i in idx:
                    d_kall = d_kall + d_k[i]
                    d_vall = d_vall + d_v[i]
                for i in range(0, len(grp), 2):
                    j = grp[i][0]
                    dq_ref[:, j * TB:(j + 1) * TB] = jnp.where(first, d_q[i], d_q[i + 1])
            dkm_ref[:, ks] += d_kall[0:N_META]
            dkp_ref[:, ks] = d_kall[N_META:N_META + TB]
            dkc_ref[:, ks] = d_kall[N_META + TB:NKEY]
            dvm_ref[:, ks] += d_vall[0:N_META]
            dvp_ref[:, ks] = d_vall[N_META:N_META + TB]
            dvc_ref[:, ks] = d_vall[N_META + TB:NKEY]
        dsink_ref[...] += jnp.sum(sink_acc, axis=0, keepdims=True)

    cur, prev, meta, sink = _attn_specs()
    acc = lambda r: pl.BlockSpec((r, 256), lambda n: (0, 0))
    return _call(
        body, name=name, grid=(p // TB,),
        in_specs=[sink, cur(1024), cur(256), prev, meta, cur(256), prev, meta, cur(1024), cur(TB), cur(1024)],
        out_specs=[cur(1024), cur(256), cur(256), cur(256), cur(256), acc(N_META), acc(N_META),
                   pl.BlockSpec((1, TB), lambda n: (0, 0))],
        out_shape=[jax.ShapeDtypeStruct((p, 1024), F32)] + [jax.ShapeDtypeStruct((p, 256), F32)] * 4
        + [jax.ShapeDtypeStruct((N_META, 256), F32)] * 2 + [jax.ShapeDtypeStruct((1, TB), F32)],
        args=(sinks, qh, k2, k2, k2, v2, v2, v2, o, lse, d_o), sem=("arbitrary",), carry=carry)


def _qk_post(proj, gq, gk, cos, sin, dqh, dkc, dkp, dkm, dvc, dvp, dvm, *, name):
    p = proj.shape[0]
    nb = p // TB

    def body(qb_ref, kb_ref, gq_ref, gk_ref, cos_ref, sin_ref, dqh_ref, dkc_ref, dkp_ref, dkm_ref,
             dvc_ref, dvp_ref, dvm_ref, dqb_ref, dkb_ref, dvb_ref, dgq_ref, dgk_ref, tk_ref, tv_ref):
        n = pl.program_id(0)

        @pl.when(n == 0)
        def _():
            dgq_ref[...] = jnp.zeros_like(dgq_ref)
            dgk_ref[...] = jnp.zeros_like(dgk_ref)

        keep = jnp.where(n == nb - 1, 0.0, 1.0)
        tk_ref[...] = dkc_ref[...] + keep * dkp_ref[...]
        tv_ref[...] = dvc_ref[...] + keep * dvp_ref[...]

        @pl.when(n == 0)
        def _():
            tk_ref[PAD_FRONT:TB, :] += dkm_ref[...]
            tv_ref[PAD_FRONT:TB, :] += dvm_ref[...]

        first = _lane() < DH

        def fold(t_ref):
            t0, t1 = t_ref[:, 0:TB], t_ref[:, TB:2 * TB]
            return jnp.where(first, t0 + pltpu.roll(t0, DH, 1), t1 + pltpu.roll(t1, DH, 1))

        bd = _head_ones()
        cos_v, sin_v = cos_ref[...], sin_ref[...]
        dvb_ref[...] = fold(tv_ref).astype(BF16)
        dkb, dgk = _norm_rope_bwd(fold(tk_ref), kb_ref[...], gk_ref[...], cos_v, sin_v, bd)
        dkb_ref[...] = dkb.astype(BF16)
        dgk_ref[...] += dgk
        dgq = jnp.zeros((1, TB), F32)
        for j in range(HB // 2):
            sl = slice(j * TB, (j + 1) * TB)
            dqb, dg = _norm_rope_bwd(dqh_ref[:, sl], qb_ref[:, sl], gq_ref[...], cos_v, sin_v, bd)
            dqb_ref[:, sl] = dqb.astype(BF16)
            dgq = dgq + dg
        dgq_ref[...] += dgq

    vec = pl.BlockSpec((1, TB), lambda n: (0, 0))
    tab = pl.BlockSpec((TB, TB), lambda n: (n, 0))
    cur = lambda w: pl.BlockSpec((TB, w), lambda n: (n, 0))
    nxt = pl.BlockSpec((TB, 256), lambda n: (jnp.minimum(n + 1, nb - 1), 0))
    meta = pl.BlockSpec((N_META, 256), lambda n: (0, 0))
    return pl.pallas_call(
        body, name=name, grid=(nb,),
        in_specs=[pl.BlockSpec((TB, 1024), lambda n: (n, C_QB)), pl.BlockSpec((TB, TB), lambda n: (n, C_KB)),
                  vec, vec, tab, tab, cur(1024), cur(256), nxt, meta, cur(256), nxt, meta],
        out_specs=[cur(1024), cur(TB), cur(TB), vec, vec],
        out_shape=[jax.ShapeDtypeStruct((p, 1024), BF16), jax.ShapeDtypeStruct((p, TB), BF16),
                   jax.ShapeDtypeStruct((p, TB), BF16), jax.ShapeDtypeStruct((1, TB), F32),
                   jax.ShapeDtypeStruct((1, TB), F32)],
        scratch_shapes=[pltpu.VMEM((TB, 256), F32), pltpu.VMEM((TB, 256), F32)],
        compiler_params=_params("arbitrary"),
    )(proj, proj, gq, gk, cos, sin, dqh, dkc, dkp, dkm, dvc, dvp, dvm)


EXT = TB + N_META


def _pool_count_inv(n, w):
    t = n * TB + _iota((TB, 1), 0)
    cnt = jnp.clip(t - (PAD_FRONT - 1), 1, w)
    return 1.0 / cnt.astype(F32)


def _silu_parts(gate):
    s = _sigmoid(gate)
    return gate * s, s * (1.0 + gate * (1.0 - s))


def _mix_fwd(proj, oa, yb, hg, pool_w, pool_scale, *, name):
    p = proj.shape[0]

    def body(ga_ref, gb0, gb1, gb2, gb3, uc0, uc1, up0, up1, gc0, gc1, oa_ref, yb_ref, hg_ref, pw_ref, ps_ref,
             mx_ref, pooled_ref):
        gb_ref, uc_ref, up_ref, gc_ref = _Cols([gb0, gb1, gb2, gb3]), _Cols([uc0, uc1]), _Cols([up0, up1]), _Cols([gc0, gc1])
        n = pl.program_id(0)
        valid = ((n * TB + _iota((TB, 1), 0)) >= PAD_FRONT).astype(F32)
        for hd in range(HA):
            sl = slice(hd * TB, (hd + 1) * TB)
            o = oa_ref[:, sl]
            r = lax.rsqrt(jnp.mean(o * o, axis=-1, keepdims=True) + RMS_EPS)
            act, _ = _silu_parts(ga_ref[:, sl])
            mx_ref[:, sl] = (o * r * hg_ref[...] * act).astype(BF16)
        for j in range(HB // 2):
            sl = slice(j * TB, (j + 1) * TB)
            act, _ = _silu_parts(gb_ref[:, sl])
            mx_ref[:, 512 + j * TB:512 + (j + 1) * TB] = (yb_ref[:, sl] * act).astype(BF16)
        ri, ci = _iota((TB, EXT), 0), _iota((TB, EXT), 1)
        has_prev = jnp.where(n == 0, 0.0, 1.0)
        groups = range(len(POOL_WINDOWS))
        sls = [slice(gi * TB, (gi + 1) * TB) for gi in groups]
        ug = [uc_ref[:, sl] * valid for sl in sls]
        ext = [jnp.concatenate([up_ref[:, sls[gi]] * has_prev, ug[gi]], axis=0) for gi in groups]
        band = [jnp.where((ci <= ri + N_META) & (ci > ri + N_META - w), 1.0, 0.0).astype(BF16) for w in POOL_WINDOWS]
        num = [_xdot(band[gi], ext[gi]) for gi in groups]
        pooled = [(num[gi] * _pool_count_inv(n, w) - ug[gi]) * valid for gi, w in enumerate(POOL_WINDOWS)]
        t = [_dot(pooled[gi].astype(BF16), pw_ref[gi], NN) for gi in groups]
        for gi in groups:
            pooled_ref[:, sls[gi]] = pooled[gi]
            act, _ = _silu_parts(gc_ref[:, sls[gi]])
            mx_ref[:, 1536 + gi * TB:1536 + (gi + 1) * TB] = (t[gi] * ps_ref[:, sls[gi]] * act).astype(BF16)

    cur = lambda w, c=0: pl.BlockSpec((TB, w), lambda n, c=c: (n, c))
    prev16 = lambda c: pl.BlockSpec((N_META, 256), lambda n, c=c: (jnp.maximum(n * (TB // N_META) - 1, 0), c))
    return pl.pallas_call(
        body, name=name, grid=(p // TB,),
        in_specs=[cur(512, C_GA)] + [cur(256, c) for c in C_GB] + [cur(256, c) for c in C_UC]
        + [prev16(c) for c in C_UC] + [cur(256, c) for c in C_GC]
        + [cur(512), cur(1024), pl.BlockSpec((1, TB), lambda n: (0, 0)), pl.BlockSpec((4, TB, TB), lambda n: (0, 0, 0)),
           pl.BlockSpec((1, 512), lambda n: (0, 0))],
        out_specs=[cur(MIX), cur(512)],
        out_shape=[jax.ShapeDtypeStruct((p, MIX), BF16), jax.ShapeDtypeStruct((p, 512), F32)],
        compiler_params=_params("parallel"),
    )(*([proj] * 11), oa, yb, hg, pool_w, pool_scale)


def _mix_bwd(proj, oa, yb, pooled, hg, pool_w, pool_scale, d_mixed, *, name, carry=None):
    p = proj.shape[0]

    def body(ga_ref, gb0, gb1, gb2, gb3, gc0, gc1, oa_ref, yb_ref, pooled_ref, hg_ref, pw_ref, ps_ref, dm_ref,
             doa_ref, dyb_ref, dga_ref, dgb_ref, dgc_ref, dp_ref, dhg_ref, dpw_ref, dps_ref):
        gb_ref, gc_ref = _Cols([gb0, gb1, gb2, gb3]), _Cols([gc0, gc1])
        n = pl.program_id(0)

        @pl.when(n == 0)
        def _():
            dhg_ref[...] = jnp.zeros_like(dhg_ref)
            dpw_ref[...] = jnp.zeros_like(dpw_ref)
            dps_ref[...] = jnp.zeros_like(dps_ref)

        valid = ((n * TB + _iota((TB, 1), 0)) >= PAD_FRONT).astype(F32)
        dhg = jnp.zeros((1, TB), F32)
        for hd in range(HA):
            sl = slice(hd * TB, (hd + 1) * TB)
            o = oa_ref[:, sl]
            r = lax.rsqrt(jnp.mean(o * o, axis=-1, keepdims=True) + RMS_EPS)
            on = o * r
            gate = ga_ref[:, sl]
            act, dact = _silu_parts(gate)
            dmx = dm_ref[:, sl]
            d_ya = dmx * act
            dga_ref[:, sl] = (dmx * on * hg_ref[...] * dact).astype(BF16)
            dyn = d_ya * hg_ref[...]
            doa_ref[:, sl] = r * (dyn - on * jnp.mean(dyn * on, axis=-1, keepdims=True))
            dhg = dhg + jnp.sum(d_ya * on, axis=0, keepdims=True)
        dhg_ref[...] += dhg
        for j in range(HB // 2):
            sl = slice(j * TB, (j + 1) * TB)
            act, dact = _silu_parts(gb_ref[:, sl])
            dmx = dm_ref[:, 512 + j * TB:512 + (j + 1) * TB]
            dyb_ref[:, sl] = dmx * act
            dgb_ref[:, sl] = (dmx * yb_ref[:, sl] * dact).astype(BF16)
        groups = range(len(POOL_WINDOWS))
        sls = [slice(gi * TB, (gi + 1) * TB) for gi in groups]
        pooled_b = [pooled_ref[:, sl].astype(BF16) for sl in sls]
        t = [_dot(pooled_b[gi], pw_ref[gi], NN) for gi in groups]
        d_t = []
        for gi in groups:
            sl = sls[gi]
            act, dact = _silu_parts(gc_ref[:, sl])
            dmx = dm_ref[:, 1536 + gi * TB:1536 + (gi + 1) * TB]
            d_yc = dmx * act
            dgc_ref[:, sl] = (dmx * t[gi] * ps_ref[:, sl] * dact).astype(BF16)
            dps_ref[:, sl] += jnp.sum(d_yc * t[gi], axis=0, keepdims=True)
            d_t.append((d_yc * ps_ref[:, sl]).astype(BF16))
        d_p = [_dot(d_t[gi], pw_ref[gi], NT) for gi in groups]
        d_w = [_dot(pooled_b[gi], d_t[gi], TN) for gi in groups]
        for gi in groups:
            dp_ref[:, sls[gi]] = d_p[gi] * valid
            dpw_ref[gi] += d_w[gi]

    cur = lambda w, c=0: pl.BlockSpec((TB, w), lambda n, c=c: (n, c))
    return _call(
        body, name=name, grid=(p // TB,),
        in_specs=[cur(512, C_GA)] + [cur(256, c) for c in C_GB] + [cur(256, c) for c in C_GC]
        + [cur(512), cur(1024), cur(512), pl.BlockSpec((1, TB), lambda n: (0, 0)),
           pl.BlockSpec((4, TB, TB), lambda n: (0, 0, 0)), pl.BlockSpec((1, 512), lambda n: (0, 0)), cur(MIX)],
        out_specs=[cur(512), cur(1024), cur(512), cur(1024), cur(512), cur(512),
                   pl.BlockSpec((1, TB), lambda n: (0, 0)), pl.BlockSpec((4, TB, TB), lambda n: (0, 0, 0)),
                   pl.BlockSpec((1, 512), lambda n: (0, 0))],
        out_shape=[jax.ShapeDtypeStruct((p, 512), F32), jax.ShapeDtypeStruct((p, 1024), F32),
                   jax.ShapeDtypeStruct((p, 512), BF16), jax.ShapeDtypeStruct((p, 1024), BF16),
                   jax.ShapeDtypeStruct((p, 512), BF16), jax.ShapeDtypeStruct((p, 512), F32),
                   jax.ShapeDtypeStruct((1, TB), F32), jax.ShapeDtypeStruct((4, TB, TB), F32),
                   jax.ShapeDtypeStruct((1, 512), F32)],
        args=(*([proj] * 7), oa, yb, pooled, hg, pool_w, pool_scale, d_mixed), sem=("arbitrary",), carry=carry)


def _pool_bwd(dp, *, name):
    p = dp.shape[0]
    nb = p // TB

    def body(dp_ref, dn_ref, duc_ref):
        n = pl.program_id(0)
        valid = ((n * TB + _iota((TB, 1), 0)) >= PAD_FRONT).astype(F32)
        has_next = jnp.where(n == nb - 1, 0.0, 1.0)
        ri, ci = _iota((TB, EXT), 0), _iota((TB, EXT), 1)
        groups = range(len(POOL_WINDOWS))
        sls = [slice(gi * TB, (gi + 1) * TB) for gi in groups]
        d_p = [dp_ref[:, sl] for sl in sls]
        ext = [jnp.concatenate([d_p[gi] * _pool_count_inv(n, w), dn_ref[:, sls[gi]] * (has_next / w)], axis=0)
               for gi, w in enumerate(POOL_WINDOWS)]
        band = [jnp.where((ci >= ri) & (ci < ri + w), 1.0, 0.0).astype(BF16) for w in POOL_WINDOWS]
        back = [_xdot2(band[gi], ext[gi]) for gi in groups]
        for gi in groups:
            duc_ref[:, sls[gi]] = ((back[gi] - d_p[gi]) * valid).astype(BF16)

    return pl.pallas_call(
        body, name=name, grid=(nb,),
        in_specs=[pl.BlockSpec((TB, 512), lambda n: (n, 0)),
                  pl.BlockSpec((N_META, 512), lambda n: (jnp.minimum(n + 1, nb - 1) * (TB // N_META), 0))],
        out_specs=pl.BlockSpec((TB, 512), lambda n: (n, 0)),
        out_shape=jax.ShapeDtypeStruct((p, 512), BF16),
        compiler_params=_params("parallel"),
    )(dp, dp)


def _carried(carries, key, local, fn, *args, **kw):
    if key not in carries:
        return fn(*args, **kw)
    make_src, gather, done = carries[key]
    *outs, stack = fn(*args, carry=(make_src(local), gather), **kw)
    done(stack)
    return outs


def _layer_fwd(h, w, tag, carries, xn=None):
    if xn is None:
        xn, = _rmsnorm_fwd(h, w["norm_g"], name=f"rmsnorm_fwd{tag}")
    proj, = _carried(carries, "in_proj", None, _mm_nn, xn, w["w_in"], name=f"in_proj{tag}")
    oa, sck = _carried(carries, "hgrn_fwd", None, _hgrn_fwd, proj, w["lb"], name=f"hgrn_fwd{tag}")
    qh, k2, v2 = _qk_prep(proj, w["gq"], w["gk"], w["cos"], w["sin"], name=f"qk_prep{tag}")
    yb, lse = _carried(carries, "attn_fwd", None, _attn_fwd, qh, k2, v2, w["sinks"], name=f"attn_fwd{tag}")
    mixed, pooled = _mix_fwd(proj, oa, yb, w["hg"], w["pool_w"], w["pool_scale"], name=f"mix_fwd{tag}")
    h_next, = _mm_nn(mixed, w["w_out"], h, name=f"out_proj{tag}", tm=640, tn=D_MODEL)
    saved = dict(h=h, xn=xn, proj=proj, oa=oa, sck=sck, qh=qh, k2=k2, v2=v2, yb=yb, lse=lse, mixed=mixed, pooled=pooled)
    return h_next, saved


def _layer_bwd(dh_out, dhb, s, w, tag, carries, first_layer=False):
    g = {}
    d_mixed, = _carried(carries, "d_mixed", g, _mm_nt, dhb, w["w_out"], name=f"d_mixed{tag}", tm=640, tn=MIX)
    g["w_out"], = _mm_tn(s["mixed"], dhb, name=f"dw_out{tag}", tn=1024)
    d_oa, d_yb, d_ga, d_gb, d_gc, d_p, g["hg"], g["pool_w"], g["pool_scale"] = _carried(
        carries, "mix_bwd", g, _mix_bwd,
        s["proj"], s["oa"], s["yb"], s["pooled"], w["hg"], w["pool_w"], w["pool_scale"], d_mixed, name=f"mix_bwd{tag}")
    d_uc = _pool_bwd(d_p, name=f"pool_bwd{tag}")
    d_qh, dkc, dkp, dvc, dvp, dkm, dvm, g["sinks"] = _carried(
        carries, "attn_bwd", g, _attn_bwd, s["qh"], s["k2"], s["v2"], w["sinks"], s["yb"], s["lse"], d_yb,
        name=f"attn_bwd{tag}")
    d_qb, d_kb, d_vb, g["gq"], g["gk"] = _qk_post(s["proj"], w["gq"], w["gk"], w["cos"], w["sin"], d_qh, dkc, dkp, dkm,
                                                  dvc, dvp, dvm, name=f"qk_post{tag}")
    d_qa, d_fa, d_ia, g["lb"] = _carried(carries, "hgrn_bwd", g, _hgrn_bwd, s["proj"], w["lb"], s["sck"], d_oa,
                                         name=f"hgrn_bwd{tag}")
    d_proj = jnp.concatenate([d_qa, d_fa, d_ia, d_ga, d_qb, d_kb, d_vb, d_gb, d_uc, d_gc], axis=1)
    g["w_in_parts"] = []
    for i in range(W_IN_PARTS):
        part, = _carried(carries, f"dw_in_{i}", g, _mm_tn, s["xn"], d_proj, name=f"dw_in_{i}{tag}",
                         m_part=(i, W_IN_PARTS))
        g["w_in_parts"].append(part)
    d_xn, = _carried(carries, "d_xn", g, _mm_nt, d_proj, w["w_in"], name=f"d_xn{tag}", tk=PROJ_COLS)
    out, aux, g["norm_g"] = _rmsnorm_bwd(d_xn, s["h"], w["norm_g"], dh_out, name=f"rmsnorm_bwd{tag}", first_layer=first_layer)
    return out, aux, g


def _peers():
    x, y, c = lax.axis_index("x"), lax.axis_index("y"), lax.axis_index("c")
    out = []
    for k in range(1, N_DEV):
        kx, ky, kc = (k >> 2) & 1, (k >> 1) & 1, k & 1
        px, py, pc = x ^ kx, y ^ ky, c ^ kc
        out.append(((px, py, pc), 4 * px + 2 * py + pc))
    return 4 * x + 2 * y + c, out


def _exchange_copies(src_ref, out_ref, send_sems, recv_sems, local_sem):
    me, peers = _peers()
    mine = pltpu.make_async_copy(src_ref.at[me], out_ref.at[me], local_sem)
    copies = []
    for k, (dev, idx) in enumerate(peers):
        copies.append(pltpu.make_async_remote_copy(
            src_ref=src_ref.at[idx], dst_ref=out_ref.at[me],
            send_sem=send_sems.at[k], recv_sem=recv_sems.at[k],
            device_id=dev, device_id_type=pl.DeviceIdType.MESH))
    return mine, copies


def _gather_copies(src_ref, out_ref, send_sems, recv_sems, local_sem):
    x, y, c = lax.axis_index("x"), lax.axis_index("y"), lax.axis_index("c")
    slot = lambda px, py, pc: out_ref.at[4 * px + 2 * py + pc]
    sibling = (x, y, 1 - c)
    chips = [(1 - x, y), (x, 1 - y), (1 - x, 1 - y)]

    def copy(k, src, block, to):
        return pltpu.make_async_remote_copy(src_ref=src, dst_ref=slot(*block), send_sem=send_sems.at[k],
                                            recv_sem=recv_sems.at[k], device_id=to, device_id_type=pl.DeviceIdType.MESH)

    mine = lambda: pltpu.make_async_copy(src_ref, slot(x, y, c), local_sem)
    own = lambda: ([copy(0, src_ref, (x, y, c), sibling)]
                   + [copy(1 + j, src_ref, (x, y, c), (*chip, c)) for j, chip in enumerate(chips)])
    passing = lambda: [copy(4 + j, slot(*chip, c), (*chip, c), sibling) for j, chip in enumerate(chips)]
    arrivals = lambda: ([copy(0, src_ref, sibling, sibling)]
                        + [copy(1 + j, src_ref, (*chip, c), sibling) for j, chip in enumerate(chips)]
                        + [copy(4 + j, src_ref, (*chip, 1 - c), sibling) for j, chip in enumerate(chips)])
    return mine, own, passing, arrivals


def _exchange_start(*refs, gather):
    if gather:
        mine, own, _, _ = _gather_copies(*refs)
        mine().start()
        for cp in own():
            cp.start()
        return
    mine, copies = _exchange_copies(*refs)
    mine.start()
    for cp in copies:
        cp.start()


def _exchange_wait(*refs, gather):
    if gather:
        mine, own, passing, arrivals = _gather_copies(*refs)
        passing, arrivals = passing(), arrivals()
        for j, cp in enumerate(passing):
            arrivals[1 + j].wait_recv()
            cp.start()
        arrivals[0].wait_recv()
        for cp in arrivals[4:]:
            cp.wait_recv()
        for cp in own() + passing:
            cp.wait_send()
        mine().wait()
        return
    mine, copies = _exchange_copies(*refs)
    for cp in copies:
        cp.wait_recv()
    for cp in copies:
        cp.wait_send()
    mine.wait()


def _exchange_scratch():
    return [pltpu.SemaphoreType.DMA((N_DEV - 1,)), pltpu.SemaphoreType.DMA((N_DEV - 1,)), pltpu.SemaphoreType.DMA]


def _exchange(src, *, gather, name):
    rows, cols = src.shape[-2:]

    def body(src_ref, out_ref, send_sems, recv_sems, local_sem):
        _exchange_start(src_ref, out_ref, send_sems, recv_sems, local_sem, gather=gather)
        _exchange_wait(src_ref, out_ref, send_sems, recv_sems, local_sem, gather=gather)

    return pl.pallas_call(
        body, name=name,
        in_specs=[pl.BlockSpec(memory_space=pl.ANY)], out_specs=pl.BlockSpec(memory_space=pl.ANY),
        out_shape=jax.ShapeDtypeStruct((N_DEV, rows, cols), src.dtype),
        scratch_shapes=_exchange_scratch(),
    )(src)


def _call(body, *, name, grid, in_specs, out_specs, out_shape, args, sem, scratch_shapes=(), carry=None):
    if carry is None:
        return pl.pallas_call(
            body, name=name, grid=grid, in_specs=list(in_specs), out_specs=list(out_specs), out_shape=list(out_shape),
            scratch_shapes=list(scratch_shapes), compiler_params=_params(*sem))(*args)
    src, gather = carry
    n_in, n_out, n_scr = len(in_specs), len(out_specs), len(scratch_shapes)
    rows, cols = src.shape[-2:]

    def carrying(*refs):
        ins, src_ref = refs[:n_in], refs[n_in]
        outs, dst_ref = refs[n_in + 1:n_in + 1 + n_out], refs[n_in + 1 + n_out]
        scr = refs[n_in + 2 + n_out:]
        exch = (src_ref, dst_ref) + tuple(scr[n_scr:])
        first, last = None, None
        for a, size in enumerate(grid):
            f, l = pl.program_id(a) == 0, pl.program_id(a) == size - 1
            first = f if first is None else first & f
            last = l if last is None else last & l

        @pl.when(first)
        def _():
            _exchange_start(*exch, gather=gather)

        body(*ins, *outs, *scr[:n_scr])

        @pl.when(last)
        def _():
            _exchange_wait(*exch, gather=gather)

    hbm = pl.BlockSpec(memory_space=pl.ANY)
    return pl.pallas_call(
        carrying, name=name, grid=grid, in_specs=list(in_specs) + [hbm], out_specs=list(out_specs) + [hbm],
        out_shape=list(out_shape) + [jax.ShapeDtypeStruct((N_DEV, rows, cols), src.dtype)],
        scratch_shapes=list(scratch_shapes) + _exchange_scratch(),
        compiler_params=_params(*(("arbitrary",) * len(grid))))(*args, src)


def _adamw(stacks, w, m, v, *, name, carry=None):
    nl = len(stacks)
    rows, cols = stacks[0].shape[1:]
    tr = rows
    stack_block_bytes = 8 * 1024 * 1024 // nl
    for cand in (256, 128, 64, 32, 16):
        if rows % cand == 0 and N_DEV * cand * cols * stacks[0].dtype.itemsize <= stack_block_bytes:
            tr = cand
            break
    nt = rows // tr

    def body(*refs):
        s_refs = refs[:nl]
        w_ref, m_ref, v_ref, g_ref, d_ref, nm_ref, nv_ref = refs[nl:]
        for l, s_ref in enumerate(s_refs):
            @pl.when(pl.program_id(0) == l)
            def _(s_ref=s_ref):
                acc = s_ref[0].astype(F32)
                for d in range(1, N_DEV):
                    acc = acc + s_ref[d].astype(F32)
                g_ref[...] = acc

        g = g_ref[...]
        nm = ADAM_B1 * m_ref[...] + (1.0 - ADAM_B1) * g
        nv = ADAM_B2 * v_ref[...] + (1.0 - ADAM_B2) * (g * g)
        m_hat = nm / (1.0 - ADAM_B1 ** ADAM_STEP)
        v_hat = nv / (1.0 - ADAM_B2 ** ADAM_STEP)
        d_ref[...] = -ADAM_LR * (m_hat / (jnp.sqrt(v_hat) + ADAM_EPS) + ADAM_WD * w_ref[...])
        nm_ref[...] = nm
        nv_ref[...] = nv

    blk = pl.BlockSpec((tr, cols), lambda l, i: (l * nt + i, 0))
    return _call(
        body, name=name, grid=(nl, nt),
        in_specs=[pl.BlockSpec((N_DEV, tr, cols), lambda l, i, k=k: (0, jnp.where(l == k, i, 0), 0)) for k in range(nl)]
        + [blk, blk, blk],
        out_specs=[blk] * 4, out_shape=[jax.ShapeDtypeStruct((nl * rows, cols), F32)] * 4,
        args=(*stacks, w, m, v), sem=("arbitrary", "arbitrary"), carry=carry)


def _lb_all(lb_logits):
    sm = jax.nn.softmax(lb_logits.astype(F32), axis=0)
    return jnp.cumsum(sm, axis=0) - sm[0:1]


def _rope_tables(p):
    half = DH // 2
    inv = jnp.power(ROPE_THETA, -jnp.arange(half, dtype=F32) * 2.0 / DH)
    pos = (jnp.arange(p) - PAD_FRONT).astype(F32)
    ang = pos[:, None] * inv[None, :]
    cos, sin = jnp.cos(ang), jnp.sin(ang)
    return jnp.tile(cos, (1, 4)), jnp.tile(jnp.concatenate([-sin, sin], axis=1), (1, 2))


SMALL = (("lb_logits", (DEPTH, 512)), ("q_norm_g", (DEPTH, DH)),
         ("k_norm_g", (DEPTH, DH)), ("attn_sinks", (DEPTH, HB)), ("hgrn_norm_g", (DEPTH, 128)),
         ("pool_w", (DEPTH, 4, 128, 128)), ("pool_scale", (DEPTH, 512)))


def _pack_small(d):
    flat = jnp.concatenate([d[k].astype(F32).reshape(-1) for k, _ in SMALL])
    pad = (-flat.shape[0]) % (8 * 128)
    return jnp.pad(flat, (0, pad)).reshape(-1, 128)


def _unpack_small(a):
    flat = a.reshape(-1)
    out, off = {}, 0
    for k, shp in SMALL:
        n = int(np.prod(shp))
        out[k] = flat[off:off + n].reshape(shp)
        off += n
    return out


def kernel(x, meta_tokens, lb_logits, norm_g, w_in, q_norm_g, k_norm_g, attn_sinks, hgrn_norm_g, pool_w, pool_scale, w_out, loss_target, m_meta_tokens, m_lb_logits, m_norm_g, m_w_in, m_q_norm_g, m_k_norm_g, m_attn_sinks, m_hgrn_norm_g, m_pool_w, m_pool_scale, m_w_out, v_meta_tokens, v_lb_logits, v_norm_g, v_w_in, v_q_norm_g, v_k_norm_g, v_attn_sinks, v_hgrn_norm_g, v_pool_w, v_pool_scale, v_w_out):
    seq = x.shape[1]
    p = seq + TB
    cs = PROJ_COLS // N_DEV
    rs = MIX // N_DEV
    ms = D_MODEL // N_DEV

    full_w_in = lambda st: st.transpose(1, 0, 2).reshape(D_MODEL, PROJ_COLS)
    dw_in_blocks = lambda rows: rows.reshape(rows.shape[0], N_DEV, cs).transpose(1, 0, 2)
    dw_out_blocks = lambda g: g["w_out"].reshape(N_DEV, rs, D_MODEL)
    w_in_bf = w_in.astype(BF16)

    lb_all, lb_vjp = jax.vjp(_lb_all, lb_logits)
    cos, sin = _rope_tables(p)
    layers = []
    for l in range(DEPTH):
        layers.append(dict(
            norm_g=norm_g[l][None], lb=lb_all[l][None],
            gq=jnp.tile(q_norm_g[l], 2)[None], gk=jnp.tile(k_norm_g[l], 2)[None], sinks=attn_sinks[l],
            hg=hgrn_norm_g[l][None], pool_w=pool_w[l].astype(BF16), pool_scale=pool_scale[l][None], cos=cos, sin=sin))
    meta_all = _exchange(meta_tokens, gather=True, name="gather_meta")
    meta_full = meta_all.transpose(1, 0, 2).reshape(N_META, D_MODEL)

    def got_w_out(st):
        st = st.reshape(N_DEV, DEPTH, rs, D_MODEL)
        for l in range(DEPTH):
            layers[l]["w_out"] = st[:, l].reshape(MIX, D_MODEL)

    def got_w_in(l):
        def done(st):
            layers[l]["w_in"] = full_w_in(st)
        return done

    fwd_carries = [
        dict(in_proj=(lambda _: w_in_bf[1], True, got_w_in(1)),
             hgrn_fwd=(lambda _: w_out.reshape(DEPTH * rs, D_MODEL).astype(BF16), True, got_w_out)),
        {}]
    h, xn, w_in_l0 = _embed_rmsnorm_fwd(x[0], meta_full, layers[0]["norm_g"], name="embed_rmsnorm_fwd_l0",
                                        carry=(w_in_bf[0], True))
    got_w_in(0)(w_in_l0)
    saved = []
    for l in range(DEPTH):
        h, s = _layer_fwd(h, layers[l], f"_l{l}", fwd_carries[l], xn=xn if l == 0 else None)
        saved.append(s)
    dh, dhb, sq = _loss_grad(h, loss_target[0], name="loss_grad")
    loss = lax.psum(0.5 * jnp.sum(sq) / D_MODEL, ("x", "y", "c"))

    grads = [None] * DEPTH
    win_stacks, wout_stacks, small_stacks = [None] * (W_IN_PARTS * DEPTH), [None] * DEPTH, [None]

    def into(stacks, i):
        def done(st):
            stacks[i] = st
        return done

    def small_grads(g0):
        both = [g0, grads[1]]
        stk = lambda k: jnp.stack([both[l][k][0] for l in range(DEPTH)])
        fold = lambda a: a[:, :DH] + a[:, DH:]
        return _pack_small(dict(
            lb_logits=lb_vjp(stk("lb"))[0], q_norm_g=fold(stk("gq")), k_norm_g=fold(stk("gk")),
            attn_sinks=stk("sinks")[:, :HB], hgrn_norm_g=stk("hg"),
            pool_w=jnp.stack([both[l]["pool_w"] for l in range(DEPTH)]), pool_scale=stk("pool_scale")))

    def part(l, i):
        return (lambda g: dw_in_blocks(g["w_in_parts"][i])), False, into(win_stacks, W_IN_PARTS * l + i)

    bwd_carries = [
        dict(hgrn_bwd=(dw_out_blocks, False, into(wout_stacks, 0)), dw_in_0=(small_grads, True, into(small_stacks, 0)),
             dw_in_1=part(0, 0), d_xn=part(0, 1)),
        dict(attn_bwd=(dw_out_blocks, False, into(wout_stacks, 1)), dw_in_1=part(1, 0), d_xn=part(1, 1))]
    dh, dhb, grads[1] = _layer_bwd(dh, dhb, saved[1], layers[1], "_l1", bwd_carries[1])
    grad_x, d_front, grads[0] = _layer_bwd(dh, dhb, saved[0], layers[0], "_l0", bwd_carries[0], first_layer=True)
    grad_x = grad_x[None]
    dmeta = d_front.reshape(N_META, N_DEV, ms).transpose(1, 0, 2)
    meta_stack = _exchange(dmeta, gather=False, name="scatter_dmeta")
    small_stack = small_stacks[0]
    d_norm_g = jnp.stack([grads[l]["norm_g"][0] for l in range(DEPTH)]).reshape(-1, 128)
    norm_stack = _exchange(d_norm_g, gather=True, name="gather_norm_g_grad")
    adam_norm = [a.reshape(DEPTH, D_MODEL) for a in _adamw(
        [norm_stack], norm_g.reshape(-1, 128), m_norm_g.reshape(-1, 128), v_norm_g.reshape(-1, 128), name="adamw_norm_g")]

    g_wout, d_wout, nm_wout, nv_wout = _adamw(wout_stacks, w_out.reshape(DEPTH * rs, D_MODEL), m_w_out.reshape(DEPTH * rs, D_MODEL),
                                              v_w_out.reshape(DEPTH * rs, D_MODEL), name="adamw_w_out")
    g_win, d_win, nm_win, nv_win = _adamw(win_stacks, w_in.reshape(DEPTH * D_MODEL, cs), m_w_in.reshape(DEPTH * D_MODEL, cs),
                                          v_w_in.reshape(DEPTH * D_MODEL, cs), name="adamw_w_in")
    g_meta, d_meta, nm_meta, nv_meta = _adamw([meta_stack], meta_tokens, m_meta_tokens, v_meta_tokens, name="adamw_meta")
    small_w = dict(lb_logits=lb_logits, norm_g=norm_g, q_norm_g=q_norm_g, k_norm_g=k_norm_g, attn_sinks=attn_sinks,
                   hgrn_norm_g=hgrn_norm_g, pool_w=pool_w, pool_scale=pool_scale)
    small_m = dict(lb_logits=m_lb_logits, norm_g=m_norm_g, q_norm_g=m_q_norm_g, k_norm_g=m_k_norm_g, attn_sinks=m_attn_sinks,
                   hgrn_norm_g=m_hgrn_norm_g, pool_w=m_pool_w, pool_scale=m_pool_scale)
    small_v = dict(lb_logits=v_lb_logits, norm_g=v_norm_g, q_norm_g=v_q_norm_g, k_norm_g=v_k_norm_g, attn_sinks=v_attn_sinks,
                   hgrn_norm_g=v_hgrn_norm_g, pool_w=v_pool_w, pool_scale=v_pool_scale)
    small_out = [_unpack_small(a) for a in _adamw([small_stack], _pack_small(small_w), _pack_small(small_m),
                                                  _pack_small(small_v), name="adamw_small")]

    big = dict(
        meta_tokens=(g_meta, d_meta, nm_meta, nv_meta), norm_g=tuple(adam_norm),
        w_in=tuple(a.reshape(DEPTH, D_MODEL, cs) for a in (g_win, d_win, nm_win, nv_win)),
        w_out=tuple(a.reshape(DEPTH, rs, D_MODEL) for a in (g_wout, d_wout, nm_wout, nv_wout)))
    order = ("meta_tokens", "lb_logits", "norm_g", "w_in", "q_norm_g", "k_norm_g", "attn_sinks", "hgrn_norm_g",
             "pool_w", "pool_scale", "w_out")
    outs = [loss, grad_x]
    for kind in range(4):
        for k in order:
            outs.append(big[k][kind] if k in big else small_out[kind][k])
    return tuple(outs)
```

```python
import functools

import numpy as np
import jax
import jax.numpy as jnp
from jax import lax
from jax.experimental import pallas as pl
from jax.experimental.pallas import tpu as pltpu

F32, BF16 = jnp.float32, jnp.bfloat16

D_MODEL = 2048
DEPTH = 2
N_META = 16
TB = 128
PAD_FRONT = TB - N_META
RMS_EPS = 1e-6
NEG_INF = -1e30
LOG_FLOOR = 1e-30
HA, DK_A = 4, 128
CH = 16
NCH = TB // CH
HB, KVH, DH = 16, 2, 64
GRP = HB // KVH
ROPE_THETA = 10000.0
POOL_WINDOWS = (2, 4, 8, 16)
PROJ_COLS = 5376
MIX = 2048
N_DEV = 8
W_IN_PARTS = 2
C_QA, C_FA, C_IA, C_GA = 0, 1, 2, 3
C_QB = 2
C_KB, C_VB = 24, 25
C_GB, C_UC, C_GC = (13, 14, 15, 16), (17, 18), (19, 20)


class _Cols:
    def __init__(self, refs):
        self.refs, self.width = refs, refs[0].shape[1]

    def __getitem__(self, idx):
        rows, sl = idx
        k, off = divmod(sl.start, self.width)
        return self.refs[k][rows, off:off + sl.stop - sl.start]

ADAM_LR, ADAM_B1, ADAM_B2, ADAM_EPS, ADAM_WD, ADAM_STEP = 0.001, 0.9, 0.999, 1e-08, 0.01, 10

VMEM_LIMIT = 48 * 1024 * 1024

NN = ((1,), (0,))
NT = ((1,), (1,))
TN = ((0,), (0,))


def _dot(a, b, dims):
    return lax.dot_general(a, b, (dims, ((), ())), preferred_element_type=F32)


def _split3(x):
    hi = x.astype(BF16)
    r = x - hi.astype(F32)
    mid = r.astype(BF16)
    lo = (r - mid.astype(F32)).astype(BF16)
    return hi, mid, lo


def _xdot(m01, x):
    hi, mid, lo = _split3(x)
    return _dot(m01, hi, NN) + _dot(m01, mid, NN) + _dot(m01, lo, NN)


def _xdot2(m01, x):
    hi = x.astype(BF16)
    lo = (x - hi.astype(F32)).astype(BF16)
    return _dot(m01, hi, NN) + _dot(m01, lo, NN)


def _xdot_r(x, m01):
    hi, mid, lo = _split3(x)
    return _dot(hi, m01, NN) + _dot(mid, m01, NN) + _dot(lo, m01, NN)


def _iota(shape, dim):
    return lax.broadcasted_iota(jnp.int32, shape, dim)


def _params(*sem):
    return pltpu.CompilerParams(dimension_semantics=sem, vmem_limit_bytes=VMEM_LIMIT)


def _row_tile(p, target):
    best = TB
    t = TB
    while t <= target:
        if p % t == 0:
            best = t
        t += TB
    return best


def _col_tile(n, target):
    best = 128
    t = 128
    while t <= target:
        if n % t == 0:
            best = t
        t += 128
    return best


def _sigmoid(x):
    return 1.0 / (1.0 + jnp.exp(-x))


def _mm_nn(a, b, res=None, *, name, tm=1664, tn=768, carry=None):
    m, k = a.shape
    n = b.shape[1]
    tm, tn = _row_tile(m, tm), _col_tile(n, tn)

    def body(*refs):
        if res is None:
            a_ref, b_ref, o_ref = refs
            o_ref[...] = _dot(a_ref[...], b_ref[...], NN)
        else:
            a_ref, b_ref, r_ref, o_ref = refs
            o_ref[...] = r_ref[...] + _dot(a_ref[...], b_ref[...], NN)

    in_specs = [pl.BlockSpec((tm, k), lambda j, i: (i, 0)), pl.BlockSpec((k, tn), lambda j, i: (0, j))]
    args = [a, b]
    if res is not None:
        in_specs.append(pl.BlockSpec((tm, tn), lambda j, i: (i, j)))
        args.append(res)
    return _call(
        body, name=name, grid=(n // tn, m // tm), in_specs=in_specs,
        out_specs=[pl.BlockSpec((tm, tn), lambda j, i: (i, j))],
        out_shape=[jax.ShapeDtypeStruct((m, n), F32)],
        args=args, sem=("parallel", "parallel"), carry=carry)


def _mm_nt(a, b, *, name, tm=640, tn=512, tk=2048, carry=None):
    m, k = a.shape
    n = b.shape[0]
    tm, tn, tk = _row_tile(m, tm), _col_tile(n, tn), _col_tile(k, tk)

    def body(a_ref, b_ref, o_ref):
        if k == tk:
            o_ref[...] = _dot(a_ref[...], b_ref[...], NT)
            return

        @pl.when(pl.program_id(2) == 0)
        def _():
            o_ref[...] = jnp.zeros_like(o_ref)

        o_ref[...] += _dot(a_ref[...], b_ref[...], NT)

    return _call(
        body, name=name, grid=(n // tn, m // tm, k // tk),
        in_specs=[pl.BlockSpec((tm, tk), lambda j, i, kk: (i, kk)), pl.BlockSpec((tn, tk), lambda j, i, kk: (j, kk))],
        out_specs=[pl.BlockSpec((tm, tn), lambda j, i, kk: (i, j))],
        out_shape=[jax.ShapeDtypeStruct((m, n), F32)],
        args=(a, b), sem=("parallel", "parallel", "arbitrary"), carry=carry)


def _mm_tn(a, b, *, name, tm=1024, tn=1344, tk=1664, m_part=None, carry=None):
    k, m = a.shape
    n = b.shape[1]
    first, m = (0, m) if m_part is None else (m_part[0], m // m_part[1])
    tm, tn, tk = _col_tile(m, tm), _col_tile(n, tn), _row_tile(k, tk)
    first *= m // tm
    nk = k // tk

    def body(a_ref, b_ref, o_ref, acc_ref):
        @pl.when(pl.program_id(2) == 0)
        def _():
            acc_ref[...] = jnp.zeros_like(acc_ref)

        acc_ref[...] += _dot(a_ref[...], b_ref[...], TN)

        @pl.when(pl.program_id(2) == nk - 1)
        def _():
            o_ref[...] = acc_ref[...].astype(BF16)

    return _call(
        body, name=name, grid=(m // tm, n // tn, nk),
        in_specs=[pl.BlockSpec((tk, tm), lambda i, j, kk: (kk, first + i)), pl.BlockSpec((tk, tn), lambda i, j, kk: (kk, j))],
        out_specs=[pl.BlockSpec((tm, tn), lambda i, j, kk: (i, j))],
        out_shape=[jax.ShapeDtypeStruct((m, n), BF16)], scratch_shapes=[pltpu.VMEM((tm, tn), F32)],
        args=(a, b), sem=("parallel", "parallel", "arbitrary"), carry=carry)


def _rmsnorm_fwd(h, g, *, name, carry=None):
    p, dm = h.shape
    tm = _row_tile(p, 640)

    def body(h_ref, g_ref, xn_ref):
        hv = h_ref[...]
        r = lax.rsqrt(jnp.mean(hv * hv, axis=-1, keepdims=True) + RMS_EPS)
        xn_ref[...] = (hv * r * g_ref[...]).astype(BF16)

    return _call(
        body, name=name, grid=(p // tm,),
        in_specs=[pl.BlockSpec((tm, dm), lambda i: (i, 0)), pl.BlockSpec((1, dm), lambda i: (0, 0))],
        out_specs=[pl.BlockSpec((tm, dm), lambda i: (i, 0))],
        out_shape=[jax.ShapeDtypeStruct((p, dm), BF16)],
        args=(h, g), sem=("parallel",), carry=carry)


def _embed_rmsnorm_fwd(x, meta, g, *, name, carry=None):
    seq, dm = x.shape
    p = seq + TB

    def body(x_ref, meta_ref, g_ref, h_ref, xn_ref):
        @pl.when(pl.program_id(0) == 0)
        def _():
            h_ref[...] = jnp.zeros_like(h_ref)
            h_ref[PAD_FRONT:TB, :] = meta_ref[...]

        @pl.when(pl.program_id(0) > 0)
        def _():
            h_ref[...] = x_ref[...]

        hv = h_ref[...]
        r = lax.rsqrt(jnp.mean(hv * hv, axis=-1, keepdims=True) + RMS_EPS)
        xn_ref[...] = (hv * r * g_ref[...]).astype(BF16)

    row = pl.BlockSpec((TB, dm), lambda i: (i, 0))
    return _call(
        body, name=name, grid=(p // TB,),
        in_specs=[pl.BlockSpec((TB, dm), lambda i: (jnp.maximum(i - 1, 0), 0)), pl.BlockSpec((N_META, dm), lambda i: (0, 0)),
                  pl.BlockSpec((1, dm), lambda i: (0, 0))],
        out_specs=[row, row],
        out_shape=[jax.ShapeDtypeStruct((p, dm), F32), jax.ShapeDtypeStruct((p, dm), BF16)],
        args=(x, meta, g), sem=("arbitrary",), carry=carry)


def _rmsnorm_bwd(dxn, h, g, dh_out, *, name, first_layer=False, carry=None):
    p, dm = h.shape
    tm = TB if first_layer else _row_tile(p, 384)

    def body(dxn_ref, h_ref, g_ref, dho_ref, out_ref, aux_ref, dg_ref):
        hv = h_ref[...]
        r = lax.rsqrt(jnp.mean(hv * hv, axis=-1, keepdims=True) + RMS_EPS)
        xh = hv * r
        dy = dxn_ref[...]
        dyn = dy * g_ref[...]
        dh = dho_ref[...] + r * (dyn - xh * jnp.mean(dyn * xh, axis=-1, keepdims=True))
        out_ref[...] = dh

        @pl.when(pl.program_id(0) == 0)
        def _():
            dg_ref[...] = jnp.zeros_like(dg_ref)
            if first_layer:
                aux_ref[...] = dh[PAD_FRONT:TB]

        if not first_layer:
            aux_ref[...] = dh.astype(BF16)
        dg_ref[...] += jnp.sum(dy * xh, axis=0, keepdims=True)

    row = pl.BlockSpec((tm, dm), lambda i: (i, 0))
    vec = pl.BlockSpec((1, dm), lambda i: (0, 0))
    if first_layer:
        out_specs = [pl.BlockSpec((TB, dm), lambda i: (jnp.maximum(i - 1, 0), 0)), pl.BlockSpec((N_META, dm), lambda i: (0, 0)), vec]
        out_shape = [jax.ShapeDtypeStruct((p - TB, dm), F32), jax.ShapeDtypeStruct((N_META, dm), F32)]
    else:
        out_specs = [row, row, vec]
        out_shape = [jax.ShapeDtypeStruct((p, dm), F32), jax.ShapeDtypeStruct((p, dm), BF16)]
    return _call(
        body, name=name, grid=(p // tm,),
        in_specs=[row, row, vec, row], out_specs=out_specs,
        out_shape=out_shape + [jax.ShapeDtypeStruct((1, dm), F32)],
        args=(dxn, h, g, dh_out), sem=("arbitrary",), carry=carry)


def _loss_grad(h, target, *, name):
    p, dm = h.shape

    def body(h_ref, t_ref, dh_ref, dhb_ref, sq_ref):
        n = pl.program_id(0)

        @pl.when(n == 0)
        def _():
            dh_ref[...] = jnp.zeros_like(dh_ref)
            dhb_ref[...] = jnp.zeros_like(dhb_ref)
            sq_ref[...] = jnp.zeros_like(sq_ref)

        @pl.when(n > 0)
        def _():
            err = h_ref[...] - t_ref[...]
            dh = err * (1.0 / dm)
            dh_ref[...] = dh
            dhb_ref[...] = dh.astype(BF16)
            sq_ref[...] += jnp.sum(err * err, axis=0, keepdims=True)

    row = pl.BlockSpec((TB, dm), lambda n: (n, 0))
    return pl.pallas_call(
        body, name=name, grid=(p // TB,),
        in_specs=[row, pl.BlockSpec((TB, dm), lambda n: (jnp.maximum(n - 1, 0), 0))],
        out_specs=[row, row, pl.BlockSpec((1, dm), lambda n: (0, 0))],
        out_shape=[jax.ShapeDtypeStruct((p, dm), F32), jax.ShapeDtypeStruct((p, dm), BF16), jax.ShapeDtypeStruct((1, dm), F32)],
        compiler_params=_params("arbitrary"),
    )(h, target)


def _chunk_masks():
    ri, ci = _iota((TB, TB), 0), _iota((TB, TB), 1)
    same = (ri >> 4) == (ci >> 4)
    causal = same & (ci <= ri)
    lower = jnp.where(causal, 1.0, 0.0).astype(BF16)
    upper = jnp.where(same & (ci >= ri), 1.0, 0.0).astype(BF16)
    ones = jnp.where(same, 1.0, 0.0).astype(BF16)
    return causal, lower, upper, ones


def _hgrn_gates(q, z, lbh, m):
    sig = _sigmoid(z)
    f = lbh + (1.0 - lbh) * sig
    lf = jnp.log(jnp.maximum(f, LOG_FLOOR)) * m
    kk = (1.0 - lbh) * (1.0 - sig) * m
    sq = _sigmoid(q)
    return sig, f, lf, kk, sq, q * sq


def _hgrn_fwd(proj, lb, *, name, carry=None):
    p = proj.shape[0]
    nb = p // TB

    def body(qa_ref, fa_ref, ia_ref, lb_ref, oa_ref, sck_ref, st_ref):
        n = pl.program_id(0)

        @pl.when(n == 0)
        def _():
            st_ref[...] = jnp.zeros_like(st_ref)

        causal, lower, _, ones = _chunk_masks()
        m = ((n * TB + _iota((TB, 1), 0)) >= PAD_FRONT).astype(F32)
        heads = range(HA)
        sls = [slice(hd * DK_A, (hd + 1) * DK_A) for hd in heads]
        rows = [slice(c * CH, (c + 1) * CH) for c in range(NCH)]
        gates = [_hgrn_gates(qa_ref[:, sl], fa_ref[:, sl], lb_ref[:, sl], m) for sl in sls]
        lf = [t[2] for t in gates]
        g = [_xdot(lower, x) for x in lf]
        gl = [_xdot(ones, x) for x in lf]
        qd = [(gates[hd][5] * jnp.exp(g[hd])).astype(BF16) for hd in heads]
        kt = [(gates[hd][3] * jnp.exp(-g[hd])).astype(BF16) for hd in heads]
        kd = [(gates[hd][3] * jnp.exp(gl[hd] - g[hd])).astype(BF16) for hd in heads]
        vb = [ia_ref[:, sl].astype(BF16) for sl in sls]
        a_all = [jnp.exp(x) for x in gl]
        att = [jnp.where(causal, _dot(qd[hd], kt[hd], NT), 0.0).astype(BF16) for hd in heads]
        kv = [[_dot(vb[hd][r], kd[hd][r], TN) for r in rows] for hd in heads]
        o = [_dot(att[hd], vb[hd], NN) for hd in heads]
        before = []
        for hd in heads:
            st = st_ref[hd]
            sck_ref[0, hd] = st
            per_chunk = []
            for c in range(NCH):
                per_chunk.append(st.astype(BF16))
                st = st * a_all[hd][c * CH:c * CH + 1, :] + kv[hd][c]
            st_ref[hd] = st
            before.append(per_chunk)
        inter = [[_dot(qd[hd][rows[c]], before[hd][c], NT) for c in range(NCH)] for hd in heads]
        for hd in heads:
            oa_ref[:, sls[hd]] = o[hd] + jnp.concatenate(inter[hd], axis=0)

    blk = lambda c: pl.BlockSpec((TB, 512), lambda n, c=c: (n, c))
    return _call(
        body, name=name, grid=(nb,),
        in_specs=[blk(C_QA), blk(C_FA), blk(C_IA), pl.BlockSpec((1, 512), lambda n: (0, 0))],
        out_specs=[pl.BlockSpec((TB, 512), lambda n: (n, 0)), pl.BlockSpec((1, HA, TB, TB), lambda n: (n, 0, 0, 0))],
        out_shape=[jax.ShapeDtypeStruct((p, 512), F32), jax.ShapeDtypeStruct((nb, HA, TB, TB), F32)],
        scratch_shapes=[pltpu.VMEM((HA, TB, TB), F32)],
        args=(proj, proj, proj, lb), sem=("arbitrary",), carry=carry)


def _hgrn_bwd(proj, lb, sck, d_oa, *, name, carry=None):
    p = proj.shape[0]
    nb = p // TB

    def body(qa_ref, fa_ref, ia_ref, lb_ref, sck_ref, do_ref, dq_ref, dz_ref, dv_ref, dlb_ref, dst_ref):
        i = pl.program_id(0)
        n = nb - 1 - i

        @pl.when(i == 0)
        def _():
            dst_ref[...] = jnp.zeros_like(dst_ref)
            dlb_ref[...] = jnp.zeros_like(dlb_ref)

        causal, lower, upper, ones = _chunk_masks()
        m = ((n * TB + _iota((TB, 1), 0)) >= PAD_FRONT).astype(F32)
        heads = range(HA)
        sls = [slice(hd * DK_A, (hd + 1) * DK_A) for hd in heads]
        rows = [slice(c * CH, (c + 1) * CH) for c in range(NCH)]
        a_row = lambda a, c: a[c * CH:c * CH + 1, :]
        gates = [_hgrn_gates(qa_ref[:, sl], fa_ref[:, sl], lb_ref[:, sl], m) for sl in sls]
        g = [_xdot(lower, t[2]) for t in gates]
        gl = [_xdot(ones, t[2]) for t in gates]
        e_g = [jnp.exp(x) for x in g]
        e_ng = [jnp.exp(-x) for x in g]
        e_d = [jnp.exp(gl[hd] - g[hd]) for hd in heads]
        a_all = [jnp.exp(x) for x in gl]
        qd_f = [gates[hd][5] * e_g[hd] for hd in heads]
        kt_f = [gates[hd][3] * e_ng[hd] for hd in heads]
        kd_f = [gates[hd][3] * e_d[hd] for hd in heads]
        qd, kt, kd = ([x.astype(BF16) for x in xs] for xs in (qd_f, kt_f, kd_f))
        vb = [ia_ref[:, sl].astype(BF16) for sl in sls]
        dob = [do_ref[:, sl].astype(BF16) for sl in sls]
        att = [jnp.where(causal, _dot(qd[hd], kt[hd], NT), 0.0).astype(BF16) for hd in heads]
        d_att = [jnp.where(causal, _dot(dob[hd], vb[hd], NT), 0.0).astype(BF16) for hd in heads]
        kv = [[_dot(vb[hd][r], kd[hd][r], TN) for r in rows] for hd in heads]
        dqk = [[_dot(dob[hd][r], qd[hd][r], TN) for r in rows] for hd in heads]
        d_v = [_dot(att[hd], dob[hd], TN) for hd in heads]
        d_qd = [_dot(d_att[hd], kt[hd], NN) for hd in heads]
        d_kt = [_dot(d_att[hd], qd[hd], TN) for hd in heads]
        stc, dsc = [], []
        for hd in heads:
            st, before = sck_ref[0, hd], []
            for c in range(NCH):
                before.append(st)
                if c + 1 < NCH:
                    st = st * a_row(a_all[hd], c) + kv[hd][c]
            dst, after = dst_ref[hd], [None] * NCH
            for c in range(NCH - 1, -1, -1):
                after[c] = dst
                dst = dst * a_row(a_all[hd], c) + dqk[hd][c]
            dst_ref[hd] = dst
            stc.append(before)
            dsc.append(after)
        dscb = [[x.astype(BF16) for x in dsc[hd]] for hd in heads]
        dvs = [[_dot(kd[hd][rows[c]], dscb[hd][c], NT) for c in range(NCH)] for hd in heads]
        dkd = [[_dot(vb[hd][rows[c]], dscb[hd][c], NN) for c in range(NCH)] for hd in heads]
        dqd = [[_dot(dob[hd][rows[c]], stc[hd][c].astype(BF16), NN) for c in range(NCH)] for hd in heads]
        dgl = [[jnp.broadcast_to(jnp.sum(dsc[hd][c] * stc[hd][c], axis=0, keepdims=True) * a_row(a_all[hd], c), (CH, TB))
                for c in range(NCH)] for hd in heads]
        d_qd = [d_qd[hd] + jnp.concatenate(dqd[hd], axis=0) for hd in heads]
        d_kd = [jnp.concatenate(dkd[hd], axis=0) for hd in heads]
        kd_term = [d_kd[hd] * kd_f[hd] for hd in heads]
        d_g = [d_qd[hd] * qd_f[hd] - d_kt[hd] * kt_f[hd] - kd_term[hd] for hd in heads]
        d_lf = [_xdot2(upper, d_g[hd]) + _xdot2(ones, kd_term[hd]) + jnp.concatenate(dgl[hd], axis=0) for hd in heads]
        for hd in heads:
            sl = sls[hd]
            sig, f, _, _, sq, _ = gates[hd]
            q, lbh = qa_ref[:, sl], lb_ref[:, sl]
            d_kk = (d_kt[hd] * e_ng[hd] + d_kd[hd] * e_d[hd]) * m
            t1 = d_lf[hd] * m * jnp.where(f > LOG_FLOOR, 1.0 / f, 0.0)
            dq_ref[:, sl] = (d_qd[hd] * e_g[hd] * (sq * (1.0 + q * (1.0 - sq)))).astype(BF16)
            dz_ref[:, sl] = ((t1 - d_kk) * (1.0 - lbh) * sig * (1.0 - sig)).astype(BF16)
            dv_ref[:, sl] = (d_v[hd] + jnp.concatenate(dvs[hd], axis=0)).astype(BF16)
            dlb_ref[:, sl] += jnp.sum((t1 - d_kk) * (1.0 - sig), axis=0, keepdims=True)

    blk = lambda c: pl.BlockSpec((TB, 512), lambda i, c=c: (nb - 1 - i, c))
    out_blk = pl.BlockSpec((TB, 512), lambda i: (nb - 1 - i, 0))
    vec = pl.BlockSpec((1, 512), lambda i: (0, 0))
    return _call(
        body, name=name, grid=(nb,),
        in_specs=[blk(C_QA), blk(C_FA), blk(C_IA), vec,
                  pl.BlockSpec((1, HA, TB, TB), lambda i: (nb - 1 - i, 0, 0, 0)), out_blk],
        out_specs=[out_blk, out_blk, out_blk, vec],
        out_shape=[jax.ShapeDtypeStruct((p, 512), BF16)] * 3 + [jax.ShapeDtypeStruct((1, 512), F32)],
        scratch_shapes=[pltpu.VMEM((HA, TB, TB), F32)],
        args=(proj, proj, proj, lb, sck, d_oa), sem=("arbitrary",), carry=carry)


def _lane():
    return _iota((1, TB), 1)


def _swap_halves(y):
    first = (_lane() & 63) < 32
    return jnp.where(first, pltpu.roll(y, 96, 1), pltpu.roll(y, 32, 1))


def _head_ones():
    ri, ci = _iota((TB, TB), 0), _iota((TB, TB), 1)
    return jnp.where((ri >> 6) == (ci >> 6), 1.0, 0.0).astype(BF16)


def _norm_rope(x, g, cos, sin, bd):
    r = lax.rsqrt(_xdot_r(x * x, bd) * (1.0 / DH) + RMS_EPS)
    y = x * r * g
    return y * cos + _swap_halves(y) * sin


def _norm_rope_bwd(d_out, x, g, cos, sin, bd):
    d = d_out * cos - _swap_halves(d_out) * sin
    r = lax.rsqrt(_xdot_r(x * x, bd) * (1.0 / DH) + RMS_EPS)
    xh = x * r
    dyn = d * g
    dx = r * (dyn - xh * (_xdot_r(dyn * xh, bd) * (1.0 / DH)))
    return dx, jnp.sum(d * xh, axis=0, keepdims=True)


def _dup_heads(k):
    first = _lane() < DH
    r = pltpu.roll(k, DH, 1)
    return jnp.where(first, k, r), jnp.where(first, r, k)


def _qk_prep(proj, gq, gk, cos, sin, *, name):
    p = proj.shape[0]

    def body(qb_ref, kb_ref, vb_ref, gq_ref, gk_ref, cos_ref, sin_ref, qh_ref, k2_ref, v2_ref):
        bd = _head_ones()
        cos_v, sin_v = cos_ref[...], sin_ref[...]
        for j in range(HB // 2):
            sl = slice(j * TB, (j + 1) * TB)
            qh_ref[:, sl] = _norm_rope(qb_ref[:, sl], gq_ref[...], cos_v, sin_v, bd).astype(BF16)
        k0, k1 = _dup_heads(_norm_rope(kb_ref[...], gk_ref[...], cos_v, sin_v, bd))
        k2_ref[:, 0:TB] = k0.astype(BF16)
        k2_ref[:, TB:2 * TB] = k1.astype(BF16)
        v0, v1 = _dup_heads(vb_ref[...])
        v2_ref[:, 0:TB] = v0.astype(BF16)
        v2_ref[:, TB:2 * TB] = v1.astype(BF16)

    vec = pl.BlockSpec((1, TB), lambda n: (0, 0))
    tab = pl.BlockSpec((TB, TB), lambda n: (n, 0))
    return pl.pallas_call(
        body, name=name, grid=(p // TB,),
        in_specs=[pl.BlockSpec((TB, 1024), lambda n: (n, C_QB)), pl.BlockSpec((TB, TB), lambda n: (n, C_KB)),
                  pl.BlockSpec((TB, TB), lambda n: (n, C_VB)), vec, vec, tab, tab],
        out_specs=[pl.BlockSpec((TB, 1024), lambda n: (n, 0)), pl.BlockSpec((TB, 256), lambda n: (n, 0)),
                   pl.BlockSpec((TB, 256), lambda n: (n, 0))],
        out_shape=[jax.ShapeDtypeStruct((p, 1024), BF16), jax.ShapeDtypeStruct((p, 256), BF16),
                   jax.ShapeDtypeStruct((p, 256), BF16)],
        compiler_params=_params("parallel"),
    )(proj, proj, proj, gq, gk, cos, sin)


NKEY = N_META + 2 * TB


def _attn_mask(n):
    r = _iota((TB, NKEY), 0)
    j = _iota((TB, NKEY), 1)
    meta = (j < N_META) & ((n >= 1) | (j + PAD_FRONT <= r))
    prev = (j >= N_META) & (j < N_META + TB) & (n >= 2) & (j - N_META > r)
    cur = (j >= N_META + TB) & (n >= 1) & (j - (N_META + TB) <= r)
    return meta | prev | cur


def _attn_specs():
    cur = lambda w: pl.BlockSpec((TB, w), lambda n: (n, 0))
    prev = pl.BlockSpec((TB, 256), lambda n: (jnp.maximum(n - 1, 0), 0))
    meta = pl.BlockSpec((N_META, 256), lambda n: (PAD_FRONT // N_META, 0))
    sink = pl.BlockSpec(memory_space=pltpu.SMEM)
    return cur, prev, meta, sink


ATTN_GROUP_FWD, ATTN_GROUP_BWD = 8, 4


def _head_queries(q_ref, kv, first):
    out = []
    for jj in range(GRP // 2):
        j = kv * (GRP // 2) + jj
        qj = q_ref[:, j * TB:(j + 1) * TB] * (DH ** -0.5)
        for half in range(2):
            out.append((j, half, jnp.where(first if half == 0 else ~first, qj, jnp.zeros_like(qj))))
    return out


def _attn_fwd(qh, k2, v2, sinks, *, name, carry=None):
    p = qh.shape[0]

    def body(sink_ref, q_ref, kc_ref, kp_ref, km_ref, vc_ref, vp_ref, vm_ref, o_ref, lse_ref):
        n = pl.program_id(0)
        mask = _attn_mask(n)
        lane = _lane()
        first = lane < DH
        lse_tile = jnp.zeros((TB, TB), F32)
        for kv in range(KVH):
            ks = slice(kv * TB, (kv + 1) * TB)
            kall = jnp.concatenate([km_ref[:, ks], kp_ref[:, ks], kc_ref[:, ks]], axis=0)
            vall = jnp.concatenate([vm_ref[:, ks], vp_ref[:, ks], vc_ref[:, ks]], axis=0)
            hq = _head_queries(q_ref, kv, first)
            for g0 in range(0, GRP, ATTN_GROUP_FWD):
                grp = hq[g0:g0 + ATTN_GROUP_FWD]
                idx = range(len(grp))
                s = [jnp.where(mask, _dot(qm, kall, NT), NEG_INF) for _, _, qm in grp]
                sink = [sink_ref[2 * j + half] for j, half, _ in grp]
                mx = [jnp.maximum(jnp.max(s[i], axis=1, keepdims=True), sink[i]) for i in idx]
                pr = [jnp.exp(s[i] - mx[i]) for i in idx]
                den = [jnp.sum(pr[i], axis=1, keepdims=True) + jnp.exp(sink[i] - mx[i]) for i in idx]
                o = [_dot(pr[i].astype(BF16), vall, NN) * (1.0 / den[i]) for i in idx]
                for i, (j, half, _) in enumerate(grp):
                    lse_tile = lse_tile + jnp.where(lane == 2 * j + half, mx[i] + jnp.log(den[i]), 0.0)
                for i in range(0, len(grp), 2):
                    j = grp[i][0]
                    o_ref[:, j * TB:(j + 1) * TB] = jnp.where(first, o[i], o[i + 1])
        lse_ref[...] = lse_tile

    cur, prev, meta, sink = _attn_specs()
    return _call(
        body, name=name, grid=(p // TB,),
        in_specs=[sink, cur(1024), cur(256), prev, meta, cur(256), prev, meta],
        out_specs=[cur(1024), cur(TB)],
        out_shape=[jax.ShapeDtypeStruct((p, 1024), F32), jax.ShapeDtypeStruct((p, TB), F32)],
        args=(sinks, qh, k2, k2, k2, v2, v2, v2), sem=("parallel",), carry=carry)


def _attn_bwd(qh, k2, v2, sinks, o, lse, d_o, *, name, carry=None):
    p = qh.shape[0]

    def body(sink_ref, q_ref, kc_ref, kp_ref, km_ref, vc_ref, vp_ref, vm_ref, o_ref, lse_ref, do_ref,
             dq_ref, dkc_ref, dkp_ref, dvc_ref, dvp_ref, dkm_ref, dvm_ref, dsink_ref):
        n = pl.program_id(0)

        @pl.when(n == 0)
        def _():
            dkm_ref[...] = jnp.zeros_like(dkm_ref)
            dvm_ref[...] = jnp.zeros_like(dvm_ref)
            dsink_ref[...] = jnp.zeros_like(dsink_ref)

        mask = _attn_mask(n)
        lane = _lane()
        first = lane < DH
        lse_tile = lse_ref[...]
        sink_acc = jnp.zeros((TB, TB), F32)
        for kv in range(KVH):
            ks = slice(kv * TB, (kv + 1) * TB)
            kall = jnp.concatenate([km_ref[:, ks], kp_ref[:, ks], kc_ref[:, ks]], axis=0)
            vall = jnp.concatenate([vm_ref[:, ks], vp_ref[:, ks], vc_ref[:, ks]], axis=0)
            d_kall = jnp.zeros((NKEY, TB), F32)
            d_vall = jnp.zeros((NKEY, TB), F32)
            hq = _head_queries(q_ref, kv, first)
            for g0 in range(0, GRP, ATTN_GROUP_BWD):
                grp = hq[g0:g0 + ATTN_GROUP_BWD]
                idx = range(len(grp))
                s = [jnp.where(mask, _dot(qm, kall, NT), NEG_INF) for _, _, qm in grp]
                dom = [jnp.where(first if half == 0 else ~first, do_ref[:, j * TB:(j + 1) * TB], 0.0) for j, half, _ in grp]
                domb = [x.astype(BF16) for x in dom]
                d_w = [_dot(x, vall, NT) for x in domb]
                delta = [jnp.sum(dom[i] * o_ref[:, grp[i][0] * TB:(grp[i][0] + 1) * TB], axis=1, keepdims=True) for i in idx]
                lse_h = [jnp.sum(jnp.where(lane == 2 * j + half, lse_tile, 0.0), axis=1, keepdims=True) for j, half, _ in grp]
                w = [jnp.exp(s[i] - lse_h[i]) for i in idx]
                for i, (j, half, _) in enumerate(grp):
                    w_sink = jnp.exp(sink_ref[2 * j + half] - lse_h[i])
                    sink_acc = sink_acc + jnp.where(lane == 2 * j + half, -(w_sink * delta[i]), 0.0)
                dsb = [(w[i] * (d_w[i] - delta[i])).astype(BF16) for i in idx]
                d_q = [_dot(x, kall, NN) * (DH ** -0.5) for x in dsb]
                d_k = [_dot(dsb[i], grp[i][2], TN) for i in idx]
                d_v = [_dot(w[i].astype(BF16), domb[i], TN) for i in idx]
                for i in idx:
                    d_kall = d_kall + d_k[i]
                    d_vall = d_vall + d_v[i]
                for i in range(0, len(grp), 2):
                    j = grp[i][0]
                    dq_ref[:, j * TB:(j + 1) * TB] = jnp.where(first, d_q[i], d_q[i + 1])
            dkm_ref[:, ks] += d_kall[0:N_META]
            dkp_ref[:, ks] = d_kall[N_META:N_META + TB]
            dkc_ref[:, ks] = d_kall[N_META + TB:NKEY]
            dvm_ref[:, ks] += d_vall[0:N_META]
            dvp_ref[:, ks] = d_vall[N_META:N_META + TB]
            dvc_ref[:, ks] = d_vall[N_META + TB:NKEY]
        dsink_ref[...] += jnp.sum(sink_acc, axis=0, keepdims=True)

    cur, prev, meta, sink = _attn_specs()
    acc = lambda r: pl.BlockSpec((r, 256), lambda n: (0, 0))
    return _call(
        body, name=name, grid=(p // TB,),
        in_specs=[sink, cur(1024), cur(256), prev, meta, cur(256), prev, meta, cur(1024), cur(TB), cur(1024)],
        out_specs=[cur(1024), cur(256), cur(256), cur(256), cur(256), acc(N_META), acc(N_META),
                   pl.BlockSpec((1, TB), lambda n: (0, 0))],
        out_shape=[jax.ShapeDtypeStruct((p, 1024), F32)] + [jax.ShapeDtypeStruct((p, 256), F32)] * 4
        + [jax.ShapeDtypeStruct((N_META, 256), F32)] * 2 + [jax.ShapeDtypeStruct((1, TB), F32)],
        args=(sinks, qh, k2, k2, k2, v2, v2, v2, o, lse, d_o), sem=("arbitrary",), carry=carry)


def _qk_post(proj, gq, gk, cos, sin, dqh, dkc, dkp, dkm, dvc, dvp, dvm, *, name):
    p = proj.shape[0]
    nb = p // TB

    def body(qb_ref, kb_ref, gq_ref, gk_ref, cos_ref, sin_ref, dqh_ref, dkc_ref, dkp_ref, dkm_ref,
             dvc_ref, dvp_ref, dvm_ref, dqb_ref, dkb_ref, dvb_ref, dgq_ref, dgk_ref, tk_ref, tv_ref):
        n = pl.program_id(0)

        @pl.when(n == 0)
        def _():
            dgq_ref[...] = jnp.zeros_like(dgq_ref)
            dgk_ref[...] = jnp.zeros_like(dgk_ref)

        keep = jnp.where(n == nb - 1, 0.0, 1.0)
        tk_ref[...] = dkc_ref[...] + keep * dkp_ref[...]
        tv_ref[...] = dvc_ref[...] + keep * dvp_ref[...]

        @pl.when(n == 0)
        def _():
            tk_ref[PAD_FRONT:TB, :] += dkm_ref[...]
            tv_ref[PAD_FRONT:TB, :] += dvm_ref[...]

        first = _lane() < DH

        def fold(t_ref):
            t0, t1 = t_ref[:, 0:TB], t_ref[:, TB:2 * TB]
            return jnp.where(first, t0 + pltpu.roll(t0, DH, 1), t1 + pltpu.roll(t1, DH, 1))

        bd = _head_ones()
        cos_v, sin_v = cos_ref[...], sin_ref[...]
        dvb_ref[...] = fold(tv_ref).astype(BF16)
        dkb, dgk = _norm_rope_bwd(fold(tk_ref), kb_ref[...], gk_ref[...], cos_v, sin_v, bd)
        dkb_ref[...] = dkb.astype(BF16)
        dgk_ref[...] += dgk
        dgq = jnp.zeros((1, TB), F32)
        for j in range(HB // 2):
            sl = slice(j * TB, (j + 1) * TB)
            dqb, dg = _norm_rope_bwd(dqh_ref[:, sl], qb_ref[:, sl], gq_ref[...], cos_v, sin_v, bd)
            dqb_ref[:, sl] = dqb.astype(BF16)
            dgq = dgq + dg
        dgq_ref[...] += dgq

    vec = pl.BlockSpec((1, TB), lambda n: (0, 0))
    tab = pl.BlockSpec((TB, TB), lambda n: (n, 0))
    cur = lambda w: pl.BlockSpec((TB, w), lambda n: (n, 0))
    nxt = pl.BlockSpec((TB, 256), lambda n: (jnp.minimum(n + 1, nb - 1), 0))
    meta = pl.BlockSpec((N_META, 256), lambda n: (0, 0))
    return pl.pallas_call(
        body, name=name, grid=(nb,),
        in_specs=[pl.BlockSpec((TB, 1024), lambda n: (n, C_QB)), pl.BlockSpec((TB, TB), lambda n: (n, C_KB)),
                  vec, vec, tab, tab, cur(1024), cur(256), nxt, meta, cur(256), nxt, meta],
        out_specs=[cur(1024), cur(TB), cur(TB), vec, vec],
        out_shape=[jax.ShapeDtypeStruct((p, 1024), BF16), jax.ShapeDtypeStruct((p, TB), BF16),
                   jax.ShapeDtypeStruct((p, TB), BF16), jax.ShapeDtypeStruct((1, TB), F32),
                   jax.ShapeDtypeStruct((1, TB), F32)],
        scratch_shapes=[pltpu.VMEM((TB, 256), F32), pltpu.VMEM((TB, 256), F32)],
        compiler_params=_params("arbitrary"),
    )(proj, proj, gq, gk, cos, sin, dqh, dkc, dkp, dkm, dvc, dvp, dvm)


EXT = TB + N_META


def _pool_count_inv(n, w):
    t = n * TB + _iota((TB, 1), 0)
    cnt = jnp.clip(t - (PAD_FRONT - 1), 1, w)
    return 1.0 / cnt.astype(F32)


def _silu_parts(gate):
    s = _sigmoid(gate)
    return gate * s, s * (1.0 + gate * (1.0 - s))


def _mix_fwd(proj, oa, yb, hg, pool_w, pool_scale, *, name):
    p = proj.shape[0]

    def body(ga_ref, gb0, gb1, gb2, gb3, uc0, uc1, up0, up1, gc0, gc1, oa_ref, yb_ref, hg_ref, pw_ref, ps_ref,
             mx_ref, pooled_ref):
        gb_ref, uc_ref, up_ref, gc_ref = _Cols([gb0, gb1, gb2, gb3]), _Cols([uc0, uc1]), _Cols([up0, up1]), _Cols([gc0, gc1])
        n = pl.program_id(0)
        valid = ((n * TB + _iota((TB, 1), 0)) >= PAD_FRONT).astype(F32)
        for hd in range(HA):
            sl = slice(hd * TB, (hd + 1) * TB)
            o = oa_ref[:, sl]
            r = lax.rsqrt(jnp.mean(o * o, axis=-1, keepdims=True) + RMS_EPS)
            act, _ = _silu_parts(ga_ref[:, sl])
            mx_ref[:, sl] = (o * r * hg_ref[...] * act).astype(BF16)
        for j in range(HB // 2):
            sl = slice(j * TB, (j + 1) * TB)
            act, _ = _silu_parts(gb_ref[:, sl])
            mx_ref[:, 512 + j * TB:512 + (j + 1) * TB] = (yb_ref[:, sl] * act).astype(BF16)
        ri, ci = _iota((TB, EXT), 0), _iota((TB, EXT), 1)
        has_prev = jnp.where(n == 0, 0.0, 1.0)
        groups = range(len(POOL_WINDOWS))
        sls = [slice(gi * TB, (gi + 1) * TB) for gi in groups]
        ug = [uc_ref[:, sl] * valid for sl in sls]
        ext = [jnp.concatenate([up_ref[:, sls[gi]] * has_prev, ug[gi]], axis=0) for gi in groups]
        band = [jnp.where((ci <= ri + N_META) & (ci > ri + N_META - w), 1.0, 0.0).astype(BF16) for w in POOL_WINDOWS]
        num = [_xdot(band[gi], ext[gi]) for gi in groups]
        pooled = [(num[gi] * _pool_count_inv(n, w) - ug[gi]) * valid for gi, w in enumerate(POOL_WINDOWS)]
        t = [_dot(pooled[gi].astype(BF16), pw_ref[gi], NN) for gi in groups]
        for gi in groups:
            pooled_ref[:, sls[gi]] = pooled[gi]
            act, _ = _silu_parts(gc_ref[:, sls[gi]])
            mx_ref[:, 1536 + gi * TB:1536 + (gi + 1) * TB] = (t[gi] * ps_ref[:, sls[gi]] * act).astype(BF16)

    cur = lambda w, c=0: pl.BlockSpec((TB, w), lambda n, c=c: (n, c))
    prev16 = lambda c: pl.BlockSpec((N_META, 256), lambda n, c=c: (jnp.maximum(n * (TB // N_META) - 1, 0), c))
    return pl.pallas_call(
        body, name=name, grid=(p // TB,),
        in_specs=[cur(512, C_GA)] + [cur(256, c) for c in C_GB] + [cur(256, c) for c in C_UC]
        + [prev16(c) for c in C_UC] + [cur(256, c) for c in C_GC]
        + [cur(512), cur(1024), pl.BlockSpec((1, TB), lambda n: (0, 0)), pl.BlockSpec((4, TB, TB), lambda n: (0, 0, 0)),
           pl.BlockSpec((1, 512), lambda n: (0, 0))],
        out_specs=[cur(MIX), cur(512)],
        out_shape=[jax.ShapeDtypeStruct((p, MIX), BF16), jax.ShapeDtypeStruct((p, 512), F32)],
        compiler_params=_params("parallel"),
    )(*([proj] * 11), oa, yb, hg, pool_w, pool_scale)


def _mix_bwd(proj, oa, yb, pooled, hg, pool_w, pool_scale, d_mixed, *, name, carry=None):
    p = proj.shape[0]

    def body(ga_ref, gb0, gb1, gb2, gb3, gc0, gc1, oa_ref, yb_ref, pooled_ref, hg_ref, pw_ref, ps_ref, dm_ref,
             doa_ref, dyb_ref, dga_ref, dgb_ref, dgc_ref, dp_ref, dhg_ref, dpw_ref, dps_ref):
        gb_ref, gc_ref = _Cols([gb0, gb1, gb2, gb3]), _Cols([gc0, gc1])
        n = pl.program_id(0)

        @pl.when(n == 0)
        def _():
            dhg_ref[...] = jnp.zeros_like(dhg_ref)
            dpw_ref[...] = jnp.zeros_like(dpw_ref)
            dps_ref[...] = jnp.zeros_like(dps_ref)

        valid = ((n * TB + _iota((TB, 1), 0)) >= PAD_FRONT).astype(F32)
        dhg = jnp.zeros((1, TB), F32)
        for hd in range(HA):
            sl = slice(hd * TB, (hd + 1) * TB)
            o = oa_ref[:, sl]
            r = lax.rsqrt(jnp.mean(o * o, axis=-1, keepdims=True) + RMS_EPS)
            on = o * r
            gate = ga_ref[:, sl]
            act, dact = _silu_parts(gate)
            dmx = dm_ref[:, sl]
            d_ya = dmx * act
            dga_ref[:, sl] = (dmx * on * hg_ref[...] * dact).astype(BF16)
            dyn = d_ya * hg_ref[...]
            doa_ref[:, sl] = r * (dyn - on * jnp.mean(dyn * on, axis=-1, keepdims=True))
            dhg = dhg + jnp.sum(d_ya * on, axis=0, keepdims=True)
        dhg_ref[...] += dhg
        for j in range(HB // 2):
            sl = slice(j * TB, (j + 1) * TB)
            act, dact = _silu_parts(gb_ref[:, sl])
            dmx = dm_ref[:, 512 + j * TB:512 + (j + 1) * TB]
            dyb_ref[:, sl] = dmx * act
            dgb_ref[:, sl] = (dmx * yb_ref[:, sl] * dact).astype(BF16)
        groups = range(len(POOL_WINDOWS))
        sls = [slice(gi * TB, (gi + 1) * TB) for gi in groups]
        pooled_b = [pooled_ref[:, sl].astype(BF16) for sl in sls]
        t = [_dot(pooled_b[gi], pw_ref[gi], NN) for gi in groups]
        d_t = []
        for gi in groups:
            sl = sls[gi]
            act, dact = _silu_parts(gc_ref[:, sl])
            dmx = dm_ref[:, 1536 + gi * TB:1536 + (gi + 1) * TB]
            d_yc = dmx * act
            dgc_ref[:, sl] = (dmx * t[gi] * ps_ref[:, sl] * dact).astype(BF16)
            dps_ref[:, sl] += jnp.sum(d_yc * t[gi], axis=0, keepdims=True)
            d_t.append((d_yc * ps_ref[:, sl]).astype(BF16))
        d_p = [_dot(d_t[gi], pw_ref[gi], NT) for gi in groups]
        d_w = [_dot(pooled_b[gi], d_t[gi], TN) for gi in groups]
        for gi in groups:
            dp_ref[:, sls[gi]] = d_p[gi] * valid
            dpw_ref[gi] += d_w[gi]

    cur = lambda w, c=0: pl.BlockSpec((TB, w), lambda n, c=c: (n, c))
    return _call(
        body, name=name, grid=(p // TB,),
        in_specs=[cur(512, C_GA)] + [cur(256, c) for c in C_GB] + [cur(256, c) for c in C_GC]
        + [cur(512), cur(1024), cur(512), pl.BlockSpec((1, TB), lambda n: (0, 0)),
           pl.BlockSpec((4, TB, TB), lambda n: (0, 0, 0)), pl.BlockSpec((1, 512), lambda n: (0, 0)), cur(MIX)],
        out_specs=[cur(512), cur(1024), cur(512), cur(1024), cur(512), cur(512),
                   pl.BlockSpec((1, TB), lambda n: (0, 0)), pl.BlockSpec((4, TB, TB), lambda n: (0, 0, 0)),
                   pl.BlockSpec((1, 512), lambda n: (0, 0))],
        out_shape=[jax.ShapeDtypeStruct((p, 512), F32), jax.ShapeDtypeStruct((p, 1024), F32),
                   jax.ShapeDtypeStruct((p, 512), BF16), jax.ShapeDtypeStruct((p, 1024), BF16),
                   jax.ShapeDtypeStruct((p, 512), BF16), jax.ShapeDtypeStruct((p, 512), F32),
                   jax.ShapeDtypeStruct((1, TB), F32), jax.ShapeDtypeStruct((4, TB, TB), F32),
                   jax.ShapeDtypeStruct((1, 512), F32)],
        args=(*([proj] * 7), oa, yb, pooled, hg, pool_w, pool_scale, d_mixed), sem=("arbitrary",), carry=carry)


def _pool_bwd(dp, *, name):
    p = dp.shape[0]
    nb = p // TB

    def body(dp_ref, dn_ref, duc_ref):
        n = pl.program_id(0)
        valid = ((n * TB + _iota((TB, 1), 0)) >= PAD_FRONT).astype(F32)
        has_next = jnp.where(n == nb - 1, 0.0, 1.0)
        ri, ci = _iota((TB, EXT), 0), _iota((TB, EXT), 1)
        groups = range(len(POOL_WINDOWS))
        sls = [slice(gi * TB, (gi + 1) * TB) for gi in groups]
        d_p = [dp_ref[:, sl] for sl in sls]
        ext = [jnp.concatenate([d_p[gi] * _pool_count_inv(n, w), dn_ref[:, sls[gi]] * (has_next / w)], axis=0)
               for gi, w in enumerate(POOL_WINDOWS)]
        band = [jnp.where((ci >= ri) & (ci < ri + w), 1.0, 0.0).astype(BF16) for w in POOL_WINDOWS]
        back = [_xdot2(band[gi], ext[gi]) for gi in groups]
        for gi in groups:
            duc_ref[:, sls[gi]] = ((back[gi] - d_p[gi]) * valid).astype(BF16)

    return pl.pallas_call(
        body, name=name, grid=(nb,),
        in_specs=[pl.BlockSpec((TB, 512), lambda n: (n, 0)),
                  pl.BlockSpec((N_META, 512), lambda n: (jnp.minimum(n + 1, nb - 1) * (TB // N_META), 0))],
        out_specs=pl.BlockSpec((TB, 512), lambda n: (n, 0)),
        out_shape=jax.ShapeDtypeStruct((p, 512), BF16),
        compiler_params=_params("parallel"),
    )(dp, dp)


def _carried(carries, key, local, fn, *args, **kw):
    if key not in carries:
        return fn(*args, **kw)
    make_src, gather, done = carries[key]
    *outs, stack = fn(*args, carry=(make_src(local), gather), **kw)
    done(stack)
    return outs


def _layer_fwd(h, w, tag, carries, xn=None):
    if xn is None:
        xn, = _rmsnorm_fwd(h, w["norm_g"], name=f"rmsnorm_fwd{tag}")
    proj, = _carried(carries, "in_proj", None, _mm_nn, xn, w["w_in"], name=f"in_proj{tag}")
    oa, sck = _carried(carries, "hgrn_fwd", None, _hgrn_fwd, proj, w["lb"], name=f"hgrn_fwd{tag}")
    qh, k2, v2 = _qk_prep(proj, w["gq"], w["gk"], w["cos"], w["sin"], name=f"qk_prep{tag}")
    yb, lse = _carried(carries, "attn_fwd", None, _attn_fwd, qh, k2, v2, w["sinks"], name=f"attn_fwd{tag}")
    mixed, pooled = _mix_fwd(proj, oa, yb, w["hg"], w["pool_w"], w["pool_scale"], name=f"mix_fwd{tag}")
    h_next, = _mm_nn(mixed, w["w_out"], h, name=f"out_proj{tag}", tm=640, tn=D_MODEL)
    saved = dict(h=h, xn=xn, proj=proj, oa=oa, sck=sck, qh=qh, k2=k2, v2=v2, yb=yb, lse=lse, mixed=mixed, pooled=pooled)
    return h_next, saved


def _layer_bwd(dh_out, dhb, s, w, tag, carries, first_layer=False):
    g = {}
    d_mixed, = _carried(carries, "d_mixed", g, _mm_nt, dhb, w["w_out"], name=f"d_mixed{tag}", tm=640, tn=MIX)
    g["w_out"], = _mm_tn(s["mixed"], dhb, name=f"dw_out{tag}", tn=1024)
    d_oa, d_yb, d_ga, d_gb, d_gc, d_p, g["hg"], g["pool_w"], g["pool_scale"] = _carried(
        carries, "mix_bwd", g, _mix_bwd,
        s["proj"], s["oa"], s["yb"], s["pooled"], w["hg"], w["pool_w"], w["pool_scale"], d_mixed, name=f"mix_bwd{tag}")
    d_uc = _pool_bwd(d_p, name=f"pool_bwd{tag}")
    d_qh, dkc, dkp, dvc, dvp, dkm, dvm, g["sinks"] = _carried(
        carries, "attn_bwd", g, _attn_bwd, s["qh"], s["k2"], s["v2"], w["sinks"], s["yb"], s["lse"], d_yb,
        name=f"attn_bwd{tag}")
    d_qb, d_kb, d_vb, g["gq"], g["gk"] = _qk_post(s["proj"], w["gq"], w["gk"], w["cos"], w["sin"], d_qh, dkc, dkp, dkm,
                                                  dvc, dvp, dvm, name=f"qk_post{tag}")
    d_qa, d_fa, d_ia, g["lb"] = _carried(carries, "hgrn_bwd", g, _hgrn_bwd, s["proj"], w["lb"], s["sck"], d_oa,
                                         name=f"hgrn_bwd{tag}")
    d_proj = jnp.concatenate([d_qa, d_fa, d_ia, d_ga, d_qb, d_kb, d_vb, d_gb, d_uc, d_gc], axis=1)
    g["w_in_parts"] = []
    for i in range(W_IN_PARTS):
        part, = _carried(carries, f"dw_in_{i}", g, _mm_tn, s["xn"], d_proj, name=f"dw_in_{i}{tag}",
                         m_part=(i, W_IN_PARTS))
        g["w_in_parts"].append(part)
    d_xn, = _carried(carries, "d_xn", g, _mm_nt, d_proj, w["w_in"], name=f"d_xn{tag}", tk=PROJ_COLS)
    out, aux, g["norm_g"] = _rmsnorm_bwd(d_xn, s["h"], w["norm_g"], dh_out, name=f"rmsnorm_bwd{tag}", first_layer=first_layer)
    return out, aux, g


def _peers():
    x, y, c = lax.axis_index("x"), lax.axis_index("y"), lax.axis_index("c")
    out = []
    for k in range(1, N_DEV):
        kx, ky, kc = (k >> 2) & 1, (k >> 1) & 1, k & 1
        px, py, pc = x ^ kx, y ^ ky, c ^ kc
        out.append(((px, py, pc), 4 * px + 2 * py + pc))
    return 4 * x + 2 * y + c, out


def _exchange_copies(src_ref, out_ref, send_sems, recv_sems, local_sem):
    me, peers = _peers()
    mine = pltpu.make_async_copy(src_ref.at[me], out_ref.at[me], local_sem)
    copies = []
    for k, (dev, idx) in enumerate(peers):
        copies.append(pltpu.make_async_remote_copy(
            src_ref=src_ref.at[idx], dst_ref=out_ref.at[me],
            send_sem=send_sems.at[k], recv_sem=recv_sems.at[k],
            device_id=dev, device_id_type=pl.DeviceIdType.MESH))
    return mine, copies


def _gather_copies(src_ref, out_ref, send_sems, recv_sems, local_sem):
    x, y, c = lax.axis_index("x"), lax.axis_index("y"), lax.axis_index("c")
    slot = lambda px, py, pc: out_ref.at[4 * px + 2 * py + pc]
    sibling = (x, y, 1 - c)
    chips = [(1 - x, y), (x, 1 - y), (1 - x, 1 - y)]

    def copy(k, src, block, to):
        return pltpu.make_async_remote_copy(src_ref=src, dst_ref=slot(*block), send_sem=send_sems.at[k],
                                            recv_sem=recv_sems.at[k], device_id=to, device_id_type=pl.DeviceIdType.MESH)

    mine = lambda: pltpu.make_async_copy(src_ref, slot(x, y, c), local_sem)
    own = lambda: ([copy(0, src_ref, (x, y, c), sibling)]
                   + [copy(1 + j, src_ref, (x, y, c), (*chip, c)) for j, chip in enumerate(chips)])
    passing = lambda: [copy(4 + j, slot(*chip, c), (*chip, c), sibling) for j, chip in enumerate(chips)]
    arrivals = lambda: ([copy(0, src_ref, sibling, sibling)]
                        + [copy(1 + j, src_ref, (*chip, c), sibling) for j, chip in enumerate(chips)]
                        + [copy(4 + j, src_ref, (*chip, 1 - c), sibling) for j, chip in enumerate(chips)])
    return mine, own, passing, arrivals


def _exchange_start(*refs, gather):
    if gather:
        mine, own, _, _ = _gather_copies(*refs)
        mine().start()
        for cp in own():
            cp.start()
        return
    mine, copies = _exchange_copies(*refs)
    mine.start()
    for cp in copies:
        cp.start()


def _exchange_pass_on(*refs, gather):
    if gather:
        _, _, passing, arrivals = _gather_copies(*refs)
        arrivals = arrivals()
        for j, cp in enumerate(passing()):
            arrivals[1 + j].wait_recv()
            cp.start()


def _exchange_wait(*refs, gather):
    if gather:
        mine, own, passing, arrivals = _gather_copies(*refs)
        arrivals = arrivals()
        arrivals[0].wait_recv()
        for cp in arrivals[4:]:
            cp.wait_recv()
        for cp in own() + passing():
            cp.wait_send()
        mine().wait()
        return
    mine, copies = _exchange_copies(*refs)
    for cp in copies:
        cp.wait_recv()
    for cp in copies:
        cp.wait_send()
    mine.wait()


def _exchange_scratch():
    return [pltpu.SemaphoreType.DMA((N_DEV - 1,)), pltpu.SemaphoreType.DMA((N_DEV - 1,)), pltpu.SemaphoreType.DMA]


def _exchange(src, *, gather, name):
    rows, cols = src.shape[-2:]

    def body(src_ref, out_ref, send_sems, recv_sems, local_sem):
        _exchange_start(src_ref, out_ref, send_sems, recv_sems, local_sem, gather=gather)
        _exchange_pass_on(src_ref, out_ref, send_sems, recv_sems, local_sem, gather=gather)
        _exchange_wait(src_ref, out_ref, send_sems, recv_sems, local_sem, gather=gather)

    return pl.pallas_call(
        body, name=name,
        in_specs=[pl.BlockSpec(memory_space=pl.ANY)], out_specs=pl.BlockSpec(memory_space=pl.ANY),
        out_shape=jax.ShapeDtypeStruct((N_DEV, rows, cols), src.dtype),
        scratch_shapes=_exchange_scratch(),
    )(src)


def _call(body, *, name, grid, in_specs, out_specs, out_shape, args, sem, scratch_shapes=(), carry=None):
    if carry is None:
        return pl.pallas_call(
            body, name=name, grid=grid, in_specs=list(in_specs), out_specs=list(out_specs), out_shape=list(out_shape),
            scratch_shapes=list(scratch_shapes), compiler_params=_params(*sem))(*args)
    src, gather, late = carry if len(carry) == 3 else (*carry, False)
    n_in, n_out, n_scr = len(in_specs), len(out_specs), len(scratch_shapes)
    rows, cols = src.shape[-2:]

    def carrying(*refs):
        ins, src_ref = refs[:n_in], refs[n_in]
        outs, dst_ref = refs[n_in + 1:n_in + 1 + n_out], refs[n_in + 1 + n_out]
        scr = refs[n_in + 2 + n_out:]
        exch = (src_ref, dst_ref) + tuple(scr[n_scr:])
        step, steps = 0, 1
        for a, size in enumerate(grid):
            step, steps = step * size + pl.program_id(a), steps * size

        @pl.when(step == 0)
        def _():
            _exchange_start(*exch, gather=gather)

        @pl.when(step == (steps - 1 if late else min(steps * 3 // 4, steps - 1)))
        def _():
            _exchange_pass_on(*exch, gather=gather)

        body(*ins, *outs, *scr[:n_scr])

        @pl.when(step == steps - 1)
        def _():
            _exchange_wait(*exch, gather=gather)

    hbm = pl.BlockSpec(memory_space=pl.ANY)
    return pl.pallas_call(
        carrying, name=name, grid=grid, in_specs=list(in_specs) + [hbm], out_specs=list(out_specs) + [hbm],
        out_shape=list(out_shape) + [jax.ShapeDtypeStruct((N_DEV, rows, cols), src.dtype)],
        scratch_shapes=list(scratch_shapes) + _exchange_scratch(),
        compiler_params=_params(*(("arbitrary",) * len(grid))))(*args, src)


def _adamw(stacks, w, m, v, *, name, carry=None):
    nl = len(stacks)
    rows, cols = stacks[0].shape[1:]
    tr = rows
    stack_block_bytes = 8 * 1024 * 1024 // nl
    for cand in (256, 128, 64, 32, 16):
        if rows % cand == 0 and N_DEV * cand * cols * stacks[0].dtype.itemsize <= stack_block_bytes:
            tr = cand
            break
    nt = rows // tr

    def body(*refs):
        s_refs = refs[:nl]
        w_ref, m_ref, v_ref, g_ref, d_ref, nm_ref, nv_ref = refs[nl:]
        for l, s_ref in enumerate(s_refs):
            @pl.when(pl.program_id(0) == l)
            def _(s_ref=s_ref):
                acc = s_ref[0].astype(F32)
                for d in range(1, N_DEV):
                    acc = acc + s_ref[d].astype(F32)
                g_ref[...] = acc

        g = g_ref[...]
        nm = ADAM_B1 * m_ref[...] + (1.0 - ADAM_B1) * g
        nv = ADAM_B2 * v_ref[...] + (1.0 - ADAM_B2) * (g * g)
        m_hat = nm / (1.0 - ADAM_B1 ** ADAM_STEP)
        v_hat = nv / (1.0 - ADAM_B2 ** ADAM_STEP)
        d_ref[...] = -ADAM_LR * (m_hat / (jnp.sqrt(v_hat) + ADAM_EPS) + ADAM_WD * w_ref[...])
        nm_ref[...] = nm
        nv_ref[...] = nv

    blk = pl.BlockSpec((tr, cols), lambda l, i: (l * nt + i, 0))
    return _call(
        body, name=name, grid=(nl, nt),
        in_specs=[pl.BlockSpec((N_DEV, tr, cols), lambda l, i, k=k: (0, jnp.where(l == k, i, 0), 0)) for k in range(nl)]
        + [blk, blk, blk],
        out_specs=[blk] * 4, out_shape=[jax.ShapeDtypeStruct((nl * rows, cols), F32)] * 4,
        args=(*stacks, w, m, v), sem=("arbitrary", "arbitrary"), carry=carry)


def _lb_all(lb_logits):
    sm = jax.nn.softmax(lb_logits.astype(F32), axis=0)
    return jnp.cumsum(sm, axis=0) - sm[0:1]


def _rope_tables(p):
    half = DH // 2
    inv = jnp.power(ROPE_THETA, -jnp.arange(half, dtype=F32) * 2.0 / DH)
    pos = (jnp.arange(p) - PAD_FRONT).astype(F32)
    ang = pos[:, None] * inv[None, :]
    cos, sin = jnp.cos(ang), jnp.sin(ang)
    return jnp.tile(cos, (1, 4)), jnp.tile(jnp.concatenate([-sin, sin], axis=1), (1, 2))


SMALL = (("lb_logits", (DEPTH, 512)), ("q_norm_g", (DEPTH, DH)),
         ("k_norm_g", (DEPTH, DH)), ("attn_sinks", (DEPTH, HB)), ("hgrn_norm_g", (DEPTH, 128)),
         ("pool_w", (DEPTH, 4, 128, 128)), ("pool_scale", (DEPTH, 512)))


def _pack_small(d):
    flat = jnp.concatenate([d[k].astype(F32).reshape(-1) for k, _ in SMALL])
    pad = (-flat.shape[0]) % (8 * 128)
    return jnp.pad(flat, (0, pad)).reshape(-1, 128)


def _unpack_small(a):
    flat = a.reshape(-1)
    out, off = {}, 0
    for k, shp in SMALL:
        n = int(np.prod(shp))
        out[k] = flat[off:off + n].reshape(shp)
        off += n
    return out


def kernel(x, meta_tokens, lb_logits, norm_g, w_in, q_norm_g, k_norm_g, attn_sinks, hgrn_norm_g, pool_w, pool_scale, w_out, loss_target, m_meta_tokens, m_lb_logits, m_norm_g, m_w_in, m_q_norm_g, m_k_norm_g, m_attn_sinks, m_hgrn_norm_g, m_pool_w, m_pool_scale, m_w_out, v_meta_tokens, v_lb_logits, v_norm_g, v_w_in, v_q_norm_g, v_k_norm_g, v_attn_sinks, v_hgrn_norm_g, v_pool_w, v_pool_scale, v_w_out):
    seq = x.shape[1]
    p = seq + TB
    cs = PROJ_COLS // N_DEV
    rs = MIX // N_DEV
    ms = D_MODEL // N_DEV

    full_w_in = lambda st: st.transpose(1, 0, 2).reshape(D_MODEL, PROJ_COLS)
    dw_in_blocks = lambda rows: rows.reshape(rows.shape[0], N_DEV, cs).transpose(1, 0, 2)
    dw_out_blocks = lambda g: g["w_out"].reshape(N_DEV, rs, D_MODEL)
    w_in_bf = w_in.astype(BF16)

    lb_all, lb_vjp = jax.vjp(_lb_all, lb_logits)
    cos, sin = _rope_tables(p)
    layers = []
    for l in range(DEPTH):
        layers.append(dict(
            norm_g=norm_g[l][None], lb=lb_all[l][None],
            gq=jnp.tile(q_norm_g[l], 2)[None], gk=jnp.tile(k_norm_g[l], 2)[None], sinks=attn_sinks[l],
            hg=hgrn_norm_g[l][None], pool_w=pool_w[l].astype(BF16), pool_scale=pool_scale[l][None], cos=cos, sin=sin))
    meta_all = _exchange(meta_tokens, gather=True, name="gather_meta")
    meta_full = meta_all.transpose(1, 0, 2).reshape(N_META, D_MODEL)

    def got_w_out(st):
        st = st.reshape(N_DEV, DEPTH, rs, D_MODEL)
        for l in range(DEPTH):
            layers[l]["w_out"] = st[:, l].reshape(MIX, D_MODEL)

    def got_w_in(l):
        def done(st):
            layers[l]["w_in"] = full_w_in(st)
        return done

    fwd_carries = [
        dict(in_proj=(lambda _: w_in_bf[1], True, got_w_in(1)),
             hgrn_fwd=(lambda _: w_out.reshape(DEPTH * rs, D_MODEL).astype(BF16), True, got_w_out)),
        {}]
    h, xn, w_in_l0 = _embed_rmsnorm_fwd(x[0], meta_full, layers[0]["norm_g"], name="embed_rmsnorm_fwd_l0",
                                        carry=(w_in_bf[0], True, True))
    got_w_in(0)(w_in_l0)
    saved = []
    for l in range(DEPTH):
        h, s = _layer_fwd(h, layers[l], f"_l{l}", fwd_carries[l], xn=xn if l == 0 else None)
        saved.append(s)
    dh, dhb, sq = _loss_grad(h, loss_target[0], name="loss_grad")
    loss = lax.psum(0.5 * jnp.sum(sq) / D_MODEL, ("x", "y", "c"))

    grads = [None] * DEPTH
    win_stacks, wout_stacks, small_stacks = [None] * (W_IN_PARTS * DEPTH), [None] * DEPTH, [None]

    def into(stacks, i):
        def done(st):
            stacks[i] = st
        return done

    def small_grads(g0):
        both = [g0, grads[1]]
        stk = lambda k: jnp.stack([both[l][k][0] for l in range(DEPTH)])
        fold = lambda a: a[:, :DH] + a[:, DH:]
        return _pack_small(dict(
            lb_logits=lb_vjp(stk("lb"))[0], q_norm_g=fold(stk("gq")), k_norm_g=fold(stk("gk")),
            attn_sinks=stk("sinks")[:, :HB], hgrn_norm_g=stk("hg"),
            pool_w=jnp.stack([both[l]["pool_w"] for l in range(DEPTH)]), pool_scale=stk("pool_scale")))

    def part(l, i):
        return (lambda g: dw_in_blocks(g["w_in_parts"][i])), False, into(win_stacks, W_IN_PARTS * l + i)

    bwd_carries = [
        dict(hgrn_bwd=(dw_out_blocks, False, into(wout_stacks, 0)), dw_in_0=(small_grads, True, into(small_stacks, 0)),
             dw_in_1=part(0, 0), d_xn=part(0, 1)),
        dict(attn_bwd=(dw_out_blocks, False, into(wout_stacks, 1)), dw_in_1=part(1, 0), d_xn=part(1, 1))]
    dh, dhb, grads[1] = _layer_bwd(dh, dhb, saved[1], layers[1], "_l1", bwd_carries[1])
    grad_x, d_front, grads[0] = _layer_bwd(dh, dhb, saved[0], layers[0], "_l0", bwd_carries[0], first_layer=True)
    grad_x = grad_x[None]
    dmeta = d_front.reshape(N_META, N_DEV, ms).transpose(1, 0, 2)
    meta_stack = _exchange(dmeta, gather=False, name="scatter_dmeta")
    small_stack = small_stacks[0]
    d_norm_g = jnp.stack([grads[l]["norm_g"][0] for l in range(DEPTH)]).reshape(-1, 128)
    norm_stack = _exchange(d_norm_g, gather=True, name="gather_norm_g_grad")
    adam_norm = [a.reshape(DEPTH, D_MODEL) for a in _adamw(
        [norm_stack], norm_g.reshape(-1, 128), m_norm_g.reshape(-1, 128), v_norm_g.reshape(-1, 128), name="adamw_norm_g")]

    g_wout, d_wout, nm_wout, nv_wout = _adamw(wout_stacks, w_out.reshape(DEPTH * rs, D_MODEL), m_w_out.reshape(DEPTH * rs, D_MODEL),
                                              v_w_out.reshape(DEPTH * rs, D_MODEL), name="adamw_w_out")
    g_win, d_win, nm_win, nv_win = _adamw(win_stacks, w_in.reshape(DEPTH * D_MODEL, cs), m_w_in.reshape(DEPTH * D_MODEL, cs),
                                          v_w_in.reshape(DEPTH * D_MODEL, cs), name="adamw_w_in")
    g_meta, d_meta, nm_meta, nv_meta = _adamw([meta_stack], meta_tokens, m_meta_tokens, v_meta_tokens, name="adamw_meta")
    small_w = dict(lb_logits=lb_logits, norm_g=norm_g, q_norm_g=q_norm_g, k_norm_g=k_norm_g, attn_sinks=attn_sinks,
                   hgrn_norm_g=hgrn_norm_g, pool_w=pool_w, pool_scale=pool_scale)
    small_m = dict(lb_logits=m_lb_logits, norm_g=m_norm_g, q_norm_g=m_q_norm_g, k_norm_g=m_k_norm_g, attn_sinks=m_attn_sinks,
                   hgrn_norm_g=m_hgrn_norm_g, pool_w=m_pool_w, pool_scale=m_pool_scale)
    small_v = dict(lb_logits=v_lb_logits, norm_g=v_norm_g, q_norm_g=v_q_norm_g, k_norm_g=v_k_norm_g, attn_sinks=v_attn_sinks,
                   hgrn_norm_g=v_hgrn_norm_g, pool_w=v_pool_w, pool_scale=v_pool_scale)
    small_out = [_unpack_small(a) for a in _adamw([small_stack], _pack_small(small_w), _pack_small(small_m),
                                                  _pack_small(small_v), name="adamw_small")]

    big = dict(
        meta_tokens=(g_meta, d_meta, nm_meta, nv_meta), norm_g=tuple(adam_norm),
        w_in=tuple(a.reshape(DEPTH, D_MODEL, cs) for a in (g_win, d_win, nm_win, nv_win)),
        w_out=tuple(a.reshape(DEPTH, rs, D_MODEL) for a in (g_wout, d_wout, nm_wout, nv_wout)))
    order = ("meta_tokens", "lb_logits", "norm_g", "w_in", "q_norm_g", "k_norm_g", "attn_sinks", "hgrn_norm_g",
             "pool_w", "pool_scale", "w_out")
    outs = [loss, grad_x]
    for kind in range(4):
        for k in order:
            outs.append(big[k][kind] if k in big else small_out[kind][k])
    return tuple(outs)
```

```python
import functools

import numpy as np
import jax
import jax.numpy as jnp
from jax import lax
from jax.experimental import pallas as pl
from jax.experimental.pallas import tpu as pltpu

F32, BF16 = jnp.float32, jnp.bfloat16

D_MODEL = 2048
DEPTH = 2
N_META = 16
TB = 128
PAD_FRONT = TB - N_META
RMS_EPS = 1e-6
NEG_INF = -1e30
LOG_FLOOR = 1e-30
HA, DK_A = 4, 128
CH = 16
NCH = TB // CH
HB, KVH, DH = 16, 2, 64
GRP = HB // KVH
ROPE_THETA = 10000.0
POOL_WINDOWS = (2, 4, 8, 16)
PROJ_COLS = 5376
MIX = 2048
N_DEV = 8
HGRN_BWD_GROUP = 2
W_IN_PARTS = 2
C_QA, C_FA, C_IA, C_GA = 0, 1, 2, 3
C_QB = 2
C_KB, C_VB = 24, 25
C_GB, C_UC, C_GC = (13, 14, 15, 16), (17, 18), (19, 20)


class _Cols:
    def __init__(self, refs):
        self.refs, self.width = refs, refs[0].shape[1]

    def __getitem__(self, idx):
        rows, sl = idx
        k, off = divmod(sl.start, self.width)
        return self.refs[k][rows, off:off + sl.stop - sl.start]

ADAM_LR, ADAM_B1, ADAM_B2, ADAM_EPS, ADAM_WD, ADAM_STEP = 0.001, 0.9, 0.999, 1e-08, 0.01, 10

VMEM_LIMIT = 48 * 1024 * 1024

NN = ((1,), (0,))
NT = ((1,), (1,))
TN = ((0,), (0,))


def _dot(a, b, dims):
    return lax.dot_general(a, b, (dims, ((), ())), preferred_element_type=F32)


def _split3(x):
    hi = x.astype(BF16)
    r = x - hi.astype(F32)
    mid = r.astype(BF16)
    lo = (r - mid.astype(F32)).astype(BF16)
    return hi, mid, lo


def _xdot(m01, x):
    hi, mid, lo = _split3(x)
    return _dot(m01, hi, NN) + _dot(m01, mid, NN) + _dot(m01, lo, NN)


def _xdot2(m01, x):
    hi = x.astype(BF16)
    lo = (x - hi.astype(F32)).astype(BF16)
    return _dot(m01, hi, NN) + _dot(m01, lo, NN)


def _xdot_r(x, m01):
    hi, mid, lo = _split3(x)
    return _dot(hi, m01, NN) + _dot(mid, m01, NN) + _dot(lo, m01, NN)


def _iota(shape, dim):
    return lax.broadcasted_iota(jnp.int32, shape, dim)


def _params(*sem):
    return pltpu.CompilerParams(dimension_semantics=sem, vmem_limit_bytes=VMEM_LIMIT)


def _row_tile(p, target):
    best = TB
    t = TB
    while t <= target:
        if p % t == 0:
            best = t
        t += TB
    return best


def _col_tile(n, target):
    best = 128
    t = 128
    while t <= target:
        if n % t == 0:
            best = t
        t += 128
    return best


def _sigmoid(x):
    return 1.0 / (1.0 + jnp.exp(-x))


def _mm_nn(a, b, res=None, *, name, tm=1664, tn=768, carry=None):
    m, k = a.shape
    n = b.shape[1]
    tm, tn = _row_tile(m, tm), _col_tile(n, tn)

    def body(*refs):
        if res is None:
            a_ref, b_ref, o_ref = refs
            o_ref[...] = _dot(a_ref[...], b_ref[...], NN)
        else:
            a_ref, b_ref, r_ref, o_ref = refs
            o_ref[...] = r_ref[...] + _dot(a_ref[...], b_ref[...], NN)

    in_specs = [pl.BlockSpec((tm, k), lambda j, i: (i, 0)), pl.BlockSpec((k, tn), lambda j, i: (0, j))]
    args = [a, b]
    if res is not None:
        in_specs.append(pl.BlockSpec((tm, tn), lambda j, i: (i, j)))
        args.append(res)
    return _call(
        body, name=name, grid=(n // tn, m // tm), in_specs=in_specs,
        out_specs=[pl.BlockSpec((tm, tn), lambda j, i: (i, j))],
        out_shape=[jax.ShapeDtypeStruct((m, n), F32)],
        args=args, sem=("parallel", "parallel"), carry=carry)


def _mm_nt(a, b, *, name, tm=640, tn=512, tk=2048, carry=None):
    m, k = a.shape
    n = b.shape[0]
    tm, tn, tk = _row_tile(m, tm), _col_tile(n, tn), _col_tile(k, tk)

    def body(a_ref, b_ref, o_ref):
        if k == tk:
            o_ref[...] = _dot(a_ref[...], b_ref[...], NT)
            return

        @pl.when(pl.program_id(2) == 0)
        def _():
            o_ref[...] = jnp.zeros_like(o_ref)

        o_ref[...] += _dot(a_ref[...], b_ref[...], NT)

    return _call(
        body, name=name, grid=(n // tn, m // tm, k // tk),
        in_specs=[pl.BlockSpec((tm, tk), lambda j, i, kk: (i, kk)), pl.BlockSpec((tn, tk), lambda j, i, kk: (j, kk))],
        out_specs=[pl.BlockSpec((tm, tn), lambda j, i, kk: (i, j))],
        out_shape=[jax.ShapeDtypeStruct((m, n), F32)],
        args=(a, b), sem=("parallel", "parallel", "arbitrary"), carry=carry)


def _mm_tn(a, b, *, name, tm=1024, tn=1344, tk=1664, m_part=None, carry=None):
    k, m = a.shape
    n = b.shape[1]
    first, m = (0, m) if m_part is None else (m_part[0], m // m_part[1])
    tm, tn, tk = _col_tile(m, tm), _col_tile(n, tn), _row_tile(k, tk)
    first *= m // tm
    nk = k // tk

    def body(a_ref, b_ref, o_ref, acc_ref):
        @pl.when(pl.program_id(2) == 0)
        def _():
            acc_ref[...] = jnp.zeros_like(acc_ref)

        acc_ref[...] += _dot(a_ref[...], b_ref[...], TN)

        @pl.when(pl.program_id(2) == nk - 1)
        def _():
            o_ref[...] = acc_ref[...].astype(BF16)

    return _call(
        body, name=name, grid=(m // tm, n // tn, nk),
        in_specs=[pl.BlockSpec((tk, tm), lambda i, j, kk: (kk, first + i)), pl.BlockSpec((tk, tn), lambda i, j, kk: (kk, j))],
        out_specs=[pl.BlockSpec((tm, tn), lambda i, j, kk: (i, j))],
        out_shape=[jax.ShapeDtypeStruct((m, n), BF16)], scratch_shapes=[pltpu.VMEM((tm, tn), F32)],
        args=(a, b), sem=("parallel", "parallel", "arbitrary"), carry=carry)


def _rmsnorm_fwd(h, g, *, name, carry=None):
    p, dm = h.shape
    tm = _row_tile(p, 640)

    def body(h_ref, g_ref, xn_ref):
        hv = h_ref[...]
        r = lax.rsqrt(jnp.mean(hv * hv, axis=-1, keepdims=True) + RMS_EPS)
        xn_ref[...] = (hv * r * g_ref[...]).astype(BF16)

    return _call(
        body, name=name, grid=(p // tm,),
        in_specs=[pl.BlockSpec((tm, dm), lambda i: (i, 0)), pl.BlockSpec((1, dm), lambda i: (0, 0))],
        out_specs=[pl.BlockSpec((tm, dm), lambda i: (i, 0))],
        out_shape=[jax.ShapeDtypeStruct((p, dm), BF16)],
        args=(h, g), sem=("parallel",), carry=carry)


def _embed_rmsnorm_fwd(x, meta, g, *, name, carry=None):
    seq, dm = x.shape
    p = seq + TB

    def body(x_ref, meta_ref, g_ref, h_ref, xn_ref):
        @pl.when(pl.program_id(0) == 0)
        def _():
            h_ref[...] = jnp.zeros_like(h_ref)
            h_ref[PAD_FRONT:TB, :] = meta_ref[...]

        @pl.when(pl.program_id(0) > 0)
        def _():
            h_ref[...] = x_ref[...]

        hv = h_ref[...]
        r = lax.rsqrt(jnp.mean(hv * hv, axis=-1, keepdims=True) + RMS_EPS)
        xn_ref[...] = (hv * r * g_ref[...]).astype(BF16)

    row = pl.BlockSpec((TB, dm), lambda i: (i, 0))
    return _call(
        body, name=name, grid=(p // TB,),
        in_specs=[pl.BlockSpec((TB, dm), lambda i: (jnp.maximum(i - 1, 0), 0)), pl.BlockSpec((N_META, dm), lambda i: (0, 0)),
                  pl.BlockSpec((1, dm), lambda i: (0, 0))],
        out_specs=[row, row],
        out_shape=[jax.ShapeDtypeStruct((p, dm), F32), jax.ShapeDtypeStruct((p, dm), BF16)],
        args=(x, meta, g), sem=("arbitrary",), carry=carry)


def _rmsnorm_bwd(dxn, h, g, dh_out, *, name, first_layer=False, carry=None):
    p, dm = h.shape
    tm = TB if first_layer else _row_tile(p, 384)

    def body(dxn_ref, h_ref, g_ref, dho_ref, out_ref, aux_ref, dg_ref):
        hv = h_ref[...]
        r = lax.rsqrt(jnp.mean(hv * hv, axis=-1, keepdims=True) + RMS_EPS)
        xh = hv * r
        dy = dxn_ref[...]
        dyn = dy * g_ref[...]
        dh = dho_ref[...] + r * (dyn - xh * jnp.mean(dyn * xh, axis=-1, keepdims=True))
        out_ref[...] = dh

        @pl.when(pl.program_id(0) == 0)
        def _():
            dg_ref[...] = jnp.zeros_like(dg_ref)
            if first_layer:
                aux_ref[...] = dh[PAD_FRONT:TB]

        if not first_layer:
            aux_ref[...] = dh.astype(BF16)
        dg_ref[...] += jnp.sum(dy * xh, axis=0, keepdims=True)

    row = pl.BlockSpec((tm, dm), lambda i: (i, 0))
    vec = pl.BlockSpec((1, dm), lambda i: (0, 0))
    if first_layer:
        out_specs = [pl.BlockSpec((TB, dm), lambda i: (jnp.maximum(i - 1, 0), 0)), pl.BlockSpec((N_META, dm), lambda i: (0, 0)), vec]
        out_shape = [jax.ShapeDtypeStruct((p - TB, dm), F32), jax.ShapeDtypeStruct((N_META, dm), F32)]
    else:
        out_specs = [row, row, vec]
        out_shape = [jax.ShapeDtypeStruct((p, dm), F32), jax.ShapeDtypeStruct((p, dm), BF16)]
    return _call(
        body, name=name, grid=(p // tm,),
        in_specs=[row, row, vec, row], out_specs=out_specs,
        out_shape=out_shape + [jax.ShapeDtypeStruct((1, dm), F32)],
        args=(dxn, h, g, dh_out), sem=("arbitrary",), carry=carry)


def _loss_grad(h, target, *, name):
    p, dm = h.shape

    def body(h_ref, t_ref, dh_ref, dhb_ref, sq_ref):
        n = pl.program_id(0)

        @pl.when(n == 0)
        def _():
            dh_ref[...] = jnp.zeros_like(dh_ref)
            dhb_ref[...] = jnp.zeros_like(dhb_ref)
            sq_ref[...] = jnp.zeros_like(sq_ref)

        @pl.when(n > 0)
        def _():
            err = h_ref[...] - t_ref[...]
            dh = err * (1.0 / dm)
            dh_ref[...] = dh
            dhb_ref[...] = dh.astype(BF16)
            sq_ref[...] += jnp.sum(err * err, axis=0, keepdims=True)

    row = pl.BlockSpec((TB, dm), lambda n: (n, 0))
    return pl.pallas_call(
        body, name=name, grid=(p // TB,),
        in_specs=[row, pl.BlockSpec((TB, dm), lambda n: (jnp.maximum(n - 1, 0), 0))],
        out_specs=[row, row, pl.BlockSpec((1, dm), lambda n: (0, 0))],
        out_shape=[jax.ShapeDtypeStruct((p, dm), F32), jax.ShapeDtypeStruct((p, dm), BF16), jax.ShapeDtypeStruct((1, dm), F32)],
        compiler_params=_params("arbitrary"),
    )(h, target)


def _chunk_masks():
    ri, ci = _iota((TB, TB), 0), _iota((TB, TB), 1)
    same = (ri >> 4) == (ci >> 4)
    causal = same & (ci <= ri)
    lower = jnp.where(causal, 1.0, 0.0).astype(BF16)
    upper = jnp.where(same & (ci >= ri), 1.0, 0.0).astype(BF16)
    ones = jnp.where(same, 1.0, 0.0).astype(BF16)
    return causal, lower, upper, ones


def _hgrn_gates(q, z, lbh, m):
    sig = _sigmoid(z)
    f = lbh + (1.0 - lbh) * sig
    lf = jnp.log(jnp.maximum(f, LOG_FLOOR)) * m
    kk = (1.0 - lbh) * (1.0 - sig) * m
    sq = _sigmoid(q)
    return sig, f, lf, kk, sq, q * sq


def _hgrn_fwd(proj, lb, *, name, carry=None):
    p = proj.shape[0]
    nb = p // TB

    def body(qa_ref, fa_ref, ia_ref, lb_ref, oa_ref, sck_ref, st_ref):
        n = pl.program_id(0)

        @pl.when(n == 0)
        def _():
            st_ref[...] = jnp.zeros_like(st_ref)

        causal, lower, _, ones = _chunk_masks()
        m = ((n * TB + _iota((TB, 1), 0)) >= PAD_FRONT).astype(F32)
        heads = range(HA)
        sls = [slice(hd * DK_A, (hd + 1) * DK_A) for hd in heads]
        rows = [slice(c * CH, (c + 1) * CH) for c in range(NCH)]
        gates = [_hgrn_gates(qa_ref[:, sl], fa_ref[:, sl], lb_ref[:, sl], m) for sl in sls]
        lf = [t[2] for t in gates]
        g = [_xdot(lower, x) for x in lf]
        gl = [_xdot(ones, x) for x in lf]
        qd = [(gates[hd][5] * jnp.exp(g[hd])).astype(BF16) for hd in heads]
        kt = [(gates[hd][3] * jnp.exp(-g[hd])).astype(BF16) for hd in heads]
        kd = [(gates[hd][3] * jnp.exp(gl[hd] - g[hd])).astype(BF16) for hd in heads]
        vb = [ia_ref[:, sl].astype(BF16) for sl in sls]
        a_all = [jnp.exp(x) for x in gl]
        att = [jnp.where(causal, _dot(qd[hd], kt[hd], NT), 0.0).astype(BF16) for hd in heads]
        kv = [[_dot(vb[hd][r], kd[hd][r], TN) for r in rows] for hd in heads]
        o = [_dot(att[hd], vb[hd], NN) for hd in heads]
        before = []
        for hd in heads:
            st = st_ref[hd]
            sck_ref[0, hd] = st
            per_chunk = []
            for c in range(NCH):
                per_chunk.append(st.astype(BF16))
                st = st * a_all[hd][c * CH:c * CH + 1, :] + kv[hd][c]
            st_ref[hd] = st
            before.append(per_chunk)
        inter = [[_dot(qd[hd][rows[c]], before[hd][c], NT) for c in range(NCH)] for hd in heads]
        for hd in heads:
            oa_ref[:, sls[hd]] = o[hd] + jnp.concatenate(inter[hd], axis=0)

    blk = lambda c: pl.BlockSpec((TB, 512), lambda n, c=c: (n, c))
    return _call(
        body, name=name, grid=(nb,),
        in_specs=[blk(C_QA), blk(C_FA), blk(C_IA), pl.BlockSpec((1, 512), lambda n: (0, 0))],
        out_specs=[pl.BlockSpec((TB, 512), lambda n: (n, 0)), pl.BlockSpec((1, HA, TB, TB), lambda n: (n, 0, 0, 0))],
        out_shape=[jax.ShapeDtypeStruct((p, 512), F32), jax.ShapeDtypeStruct((nb, HA, TB, TB), F32)],
        scratch_shapes=[pltpu.VMEM((HA, TB, TB), F32)],
        args=(proj, proj, proj, lb), sem=("arbitrary",), carry=carry)


def _hgrn_bwd(proj, lb, sck, d_oa, *, name, carry=None):
    p = proj.shape[0]
    nb = p // TB

    def body(qa_ref, fa_ref, ia_ref, lb_ref, sck_ref, do_ref, dq_ref, dz_ref, dv_ref, dlb_ref, dst_ref):
        i = pl.program_id(0)
        n = nb - 1 - i

        @pl.when(i == 0)
        def _():
            dst_ref[...] = jnp.zeros_like(dst_ref)
            dlb_ref[...] = jnp.zeros_like(dlb_ref)

        causal, lower, upper, ones = _chunk_masks()
        m = ((n * TB + _iota((TB, 1), 0)) >= PAD_FRONT).astype(F32)
        rows = [slice(c * CH, (c + 1) * CH) for c in range(NCH)]
        a_row = lambda a, c: a[c * CH:c * CH + 1, :]
        for h0 in range(0, HA, HGRN_BWD_GROUP):
            heads = range(HGRN_BWD_GROUP)
            sls = [slice((h0 + hd) * DK_A, (h0 + hd + 1) * DK_A) for hd in heads]
            gates = [_hgrn_gates(qa_ref[:, sl], fa_ref[:, sl], lb_ref[:, sl], m) for sl in sls]
            g = [_xdot(lower, t[2]) for t in gates]
            gl = [_xdot(ones, t[2]) for t in gates]
            e_g = [jnp.exp(x) for x in g]
            e_ng = [jnp.exp(-x) for x in g]
            e_d = [jnp.exp(gl[hd] - g[hd]) for hd in heads]
            a_all = [jnp.exp(x) for x in gl]
            qd_f = [gates[hd][5] * e_g[hd] for hd in heads]
            kt_f = [gates[hd][3] * e_ng[hd] for hd in heads]
            kd_f = [gates[hd][3] * e_d[hd] for hd in heads]
            qd, kt, kd = ([x.astype(BF16) for x in xs] for xs in (qd_f, kt_f, kd_f))
            vb = [ia_ref[:, sl].astype(BF16) for sl in sls]
            dob = [do_ref[:, sl].astype(BF16) for sl in sls]
            att = [jnp.where(causal, _dot(qd[hd], kt[hd], NT), 0.0).astype(BF16) for hd in heads]
            d_att = [jnp.where(causal, _dot(dob[hd], vb[hd], NT), 0.0).astype(BF16) for hd in heads]
            kv = [[_dot(vb[hd][r], kd[hd][r], TN) for r in rows] for hd in heads]
            dqk = [[_dot(dob[hd][r], qd[hd][r], TN) for r in rows] for hd in heads]
            d_v = [_dot(att[hd], dob[hd], TN) for hd in heads]
            d_qd = [_dot(d_att[hd], kt[hd], NN) for hd in heads]
            d_kt = [_dot(d_att[hd], qd[hd], TN) for hd in heads]
            stc, dsc = [], []
            for hd in heads:
                st, before = sck_ref[0, h0 + hd], []
                for c in range(NCH):
                    before.append(st)
                    if c + 1 < NCH:
                        st = st * a_row(a_all[hd], c) + kv[hd][c]
                dst, after = dst_ref[h0 + hd], [None] * NCH
                for c in range(NCH - 1, -1, -1):
                    after[c] = dst
                    dst = dst * a_row(a_all[hd], c) + dqk[hd][c]
                dst_ref[h0 + hd] = dst
                stc.append(before)
                dsc.append(after)
            dscb = [[x.astype(BF16) for x in dsc[hd]] for hd in heads]
            dvs = [[_dot(kd[hd][rows[c]], dscb[hd][c], NT) for c in range(NCH)] for hd in heads]
            dkd = [[_dot(vb[hd][rows[c]], dscb[hd][c], NN) for c in range(NCH)] for hd in heads]
            dqd = [[_dot(dob[hd][rows[c]], stc[hd][c].astype(BF16), NN) for c in range(NCH)] for hd in heads]
            dgl = [[jnp.broadcast_to(jnp.sum(dsc[hd][c] * stc[hd][c], axis=0, keepdims=True) * a_row(a_all[hd], c), (CH, TB))
                    for c in range(NCH)] for hd in heads]
            d_qd = [d_qd[hd] + jnp.concatenate(dqd[hd], axis=0) for hd in heads]
            d_kd = [jnp.concatenate(dkd[hd], axis=0) for hd in heads]
            kd_term = [d_kd[hd] * kd_f[hd] for hd in heads]
            d_g = [d_qd[hd] * qd_f[hd] - d_kt[hd] * kt_f[hd] - kd_term[hd] for hd in heads]
            d_lf = [_xdot2(upper, d_g[hd]) + _xdot2(ones, kd_term[hd]) + jnp.concatenate(dgl[hd], axis=0) for hd in heads]
            for hd in heads:
                sl = sls[hd]
                sig, f, _, _, sq, _ = gates[hd]
                q, lbh = qa_ref[:, sl], lb_ref[:, sl]
                d_kk = (d_kt[hd] * e_ng[hd] + d_kd[hd] * e_d[hd]) * m
                t1 = d_lf[hd] * m * jnp.where(f > LOG_FLOOR, 1.0 / f, 0.0)
                dq_ref[:, sl] = (d_qd[hd] * e_g[hd] * (sq * (1.0 + q * (1.0 - sq)))).astype(BF16)
                dz_ref[:, sl] = ((t1 - d_kk) * (1.0 - lbh) * sig * (1.0 - sig)).astype(BF16)
                dv_ref[:, sl] = (d_v[hd] + jnp.concatenate(dvs[hd], axis=0)).astype(BF16)
                dlb_ref[:, sl] += jnp.sum((t1 - d_kk) * (1.0 - sig), axis=0, keepdims=True)

    blk = lambda c: pl.BlockSpec((TB, 512), lambda i, c=c: (nb - 1 - i, c))
    out_blk = pl.BlockSpec((TB, 512), lambda i: (nb - 1 - i, 0))
    vec = pl.BlockSpec((1, 512), lambda i: (0, 0))
    return _call(
        body, name=name, grid=(nb,),
        in_specs=[blk(C_QA), blk(C_FA), blk(C_IA), vec,
                  pl.BlockSpec((1, HA, TB, TB), lambda i: (nb - 1 - i, 0, 0, 0)), out_blk],
        out_specs=[out_blk, out_blk, out_blk, vec],
        out_shape=[jax.ShapeDtypeStruct((p, 512), BF16)] * 3 + [jax.ShapeDtypeStruct((1, 512), F32)],
        scratch_shapes=[pltpu.VMEM((HA, TB, TB), F32)],
        args=(proj, proj, proj, lb, sck, d_oa), sem=("arbitrary",), carry=carry)


def _lane():
    return _iota((1, TB), 1)


def _swap_halves(y):
    first = (_lane() & 63) < 32
    return jnp.where(first, pltpu.roll(y, 96, 1), pltpu.roll(y, 32, 1))


def _head_ones():
    ri, ci = _iota((TB, TB), 0), _iota((TB, TB), 1)
    return jnp.where((ri >> 6) == (ci >> 6), 1.0, 0.0).astype(BF16)


def _norm_rope(x, g, cos, sin, bd):
    r = lax.rsqrt(_xdot_r(x * x, bd) * (1.0 / DH) + RMS_EPS)
    y = x * r * g
    return y * cos + _swap_halves(y) * sin


def _norm_rope_bwd(d_out, x, g, cos, sin, bd):
    d = d_out * cos - _swap_halves(d_out) * sin
    r = lax.rsqrt(_xdot_r(x * x, bd) * (1.0 / DH) + RMS_EPS)
    xh = x * r
    dyn = d * g
    dx = r * (dyn - xh * (_xdot_r(dyn * xh, bd) * (1.0 / DH)))
    return dx, jnp.sum(d * xh, axis=0, keepdims=True)


def _dup_heads(k):
    first = _lane() < DH
    r = pltpu.roll(k, DH, 1)
    return jnp.where(first, k, r), jnp.where(first, r, k)


def _qk_prep(proj, gq, gk, cos, sin, *, name):
    p = proj.shape[0]

    def body(qb_ref, kb_ref, vb_ref, gq_ref, gk_ref, cos_ref, sin_ref, qh_ref, k2_ref, v2_ref):
        bd = _head_ones()
        cos_v, sin_v = cos_ref[...], sin_ref[...]
        for j in range(HB // 2):
            sl = slice(j * TB, (j + 1) * TB)
            qh_ref[:, sl] = _norm_rope(qb_ref[:, sl], gq_ref[...], cos_v, sin_v, bd).astype(BF16)
        k0, k1 = _dup_heads(_norm_rope(kb_ref[...], gk_ref[...], cos_v, sin_v, bd))
        k2_ref[:, 0:TB] = k0.astype(BF16)
        k2_ref[:, TB:2 * TB] = k1.astype(BF16)
        v0, v1 = _dup_heads(vb_ref[...])
        v2_ref[:, 0:TB] = v0.astype(BF16)
        v2_ref[:, TB:2 * TB] = v1.astype(BF16)

    vec = pl.BlockSpec((1, TB), lambda n: (0, 0))
    tab = pl.BlockSpec((TB, TB), lambda n: (n, 0))
    return pl.pallas_call(
        body, name=name, grid=(p // TB,),
        in_specs=[pl.BlockSpec((TB, 1024), lambda n: (n, C_QB)), pl.BlockSpec((TB, TB), lambda n: (n, C_KB)),
                  pl.BlockSpec((TB, TB), lambda n: (n, C_VB)), vec, vec, tab, tab],
        out_specs=[pl.BlockSpec((TB, 1024), lambda n: (n, 0)), pl.BlockSpec((TB, 256), lambda n: (n, 0)),
                   pl.BlockSpec((TB, 256), lambda n: (n, 0))],
        out_shape=[jax.ShapeDtypeStruct((p, 1024), BF16), jax.ShapeDtypeStruct((p, 256), BF16),
                   jax.ShapeDtypeStruct((p, 256), BF16)],
        compiler_params=_params("parallel"),
    )(proj, proj, proj, gq, gk, cos, sin)


NKEY = N_META + 2 * TB


def _attn_mask(n):
    r = _iota((TB, NKEY), 0)
    j = _iota((TB, NKEY), 1)
    meta = (j < N_META) & ((n >= 1) | (j + PAD_FRONT <= r))
    prev = (j >= N_META) & (j < N_META + TB) & (n >= 2) & (j - N_META > r)
    cur = (j >= N_META + TB) & (n >= 1) & (j - (N_META + TB) <= r)
    return meta | prev | cur


def _attn_specs():
    cur = lambda w: pl.BlockSpec((TB, w), lambda n: (n, 0))
    prev = pl.BlockSpec((TB, 256), lambda n: (jnp.maximum(n - 1, 0), 0))
    meta = pl.BlockSpec((N_META, 256), lambda n: (PAD_FRONT // N_META, 0))
    sink = pl.BlockSpec(memory_space=pltpu.SMEM)
    return cur, prev, meta, sink


ATTN_GROUP_FWD, ATTN_GROUP_BWD = 8, 4


def _head_queries(q_ref, kv, first):
    out = []
    for jj in range(GRP // 2):
        j = kv * (GRP // 2) + jj
        qj = q_ref[:, j * TB:(j + 1) * TB] * (DH ** -0.5)
        for half in range(2):
            out.append((j, half, jnp.where(first if half == 0 else ~first, qj, jnp.zeros_like(qj))))
    return out


def _attn_fwd(qh, k2, v2, sinks, *, name, carry=None):
    p = qh.shape[0]

    def body(sink_ref, q_ref, kc_ref, kp_ref, km_ref, vc_ref, vp_ref, vm_ref, o_ref, lse_ref):
        n = pl.program_id(0)
        mask = _attn_mask(n)
        lane = _lane()
        first = lane < DH
        lse_tile = jnp.zeros((TB, TB), F32)
        for kv in range(KVH):
            ks = slice(kv * TB, (kv + 1) * TB)
            kall = jnp.concatenate([km_ref[:, ks], kp_ref[:, ks], kc_ref[:, ks]], axis=0)
            vall = jnp.concatenate([vm_ref[:, ks], vp_ref[:, ks], vc_ref[:, ks]], axis=0)
            hq = _head_queries(q_ref, kv, first)
            for g0 in range(0, GRP, ATTN_GROUP_FWD):
                grp = hq[g0:g0 + ATTN_GROUP_FWD]
                idx = range(len(grp))
                s = [jnp.where(mask, _dot(qm, kall, NT), NEG_INF) for _, _, qm in grp]
                sink = [sink_ref[2 * j + half] for j, half, _ in grp]
                mx = [jnp.maximum(jnp.max(s[i], axis=1, keepdims=True), sink[i]) for i in idx]
                pr = [jnp.exp(s[i] - mx[i]) for i in idx]
                den = [jnp.sum(pr[i], axis=1, keepdims=True) + jnp.exp(sink[i] - mx[i]) for i in idx]
                o = [_dot(pr[i].astype(BF16), vall, NN) * (1.0 / den[i]) for i in idx]
                for i, (j, half, _) in enumerate(grp):
                    lse_tile = lse_tile + jnp.where(lane == 2 * j + half, mx[i] + jnp.log(den[i]), 0.0)
                for i in range(0, len(grp), 2):
                    j = grp[i][0]
                    o_ref[:, j * TB:(j + 1) * TB] = jnp.where(first, o[i], o[i + 1])
        lse_ref[...] = lse_tile

    cur, prev, meta, sink = _attn_specs()
    return _call(
        body, name=name, grid=(p // TB,),
        in_specs=[sink, cur(1024), cur(256), prev, meta, cur(256), prev, meta],
        out_specs=[cur(1024), cur(TB)],
        out_shape=[jax.ShapeDtypeStruct((p, 1024), F32), jax.ShapeDtypeStruct((p, TB), F32)],
        args=(sinks, qh, k2, k2, k2, v2, v2, v2), sem=("parallel",), carry=carry)


def _attn_bwd(qh, k2, v2, sinks, o, lse, d_o, *, name, carry=None):
    p = qh.shape[0]

    def body(sink_ref, q_ref, kc_ref, kp_ref, km_ref, vc_ref, vp_ref, vm_ref, o_ref, lse_ref, do_ref,
             dq_ref, dkc_ref, dkp_ref, dvc_ref, dvp_ref, dkm_ref, dvm_ref, dsink_ref):
        n = pl.program_id(0)

        @pl.when(n == 0)
        def _():
            dkm_ref[...] = jnp.zeros_like(dkm_ref)
            dvm_ref[...] = jnp.zeros_like(dvm_ref)
            dsink_ref[...] = jnp.zeros_like(dsink_ref)

        mask = _attn_mask(n)
        lane = _lane()
        first = lane < DH
        lse_tile = lse_ref[...]
        sink_acc = jnp.zeros((TB, TB), F32)
        for kv in range(KVH):
            ks = slice(kv * TB, (kv + 1) * TB)
            kall = jnp.concatenate([km_ref[:, ks], kp_ref[:, ks], kc_ref[:, ks]], axis=0)
            vall = jnp.concatenate([vm_ref[:, ks], vp_ref[:, ks], vc_ref[:, ks]], axis=0)
            d_kall = jnp.zeros((NKEY, TB), F32)
            d_vall = jnp.zeros((NKEY, TB), F32)
            hq = _head_queries(q_ref, kv, first)
            for g0 in range(0, GRP, ATTN_GROUP_BWD):
                grp = hq[g0:g0 + ATTN_GROUP_BWD]
                idx = range(len(grp))
                s = [jnp.where(mask, _dot(qm, kall, NT), NEG_INF) for _, _, qm in grp]
                dom = [jnp.where(first if half == 0 else ~first, do_ref[:, j * TB:(j + 1) * TB], 0.0) for j, half, _ in grp]
                domb = [x.astype(BF16) for x in dom]
                d_w = [_dot(x, vall, NT) for x in domb]
                delta = [jnp.sum(dom[i] * o_ref[:, grp[i][0] * TB:(grp[i][0] + 1) * TB], axis=1, keepdims=True) for i in idx]
                lse_h = [jnp.sum(jnp.where(lane == 2 * j + half, lse_tile, 0.0), axis=1, keepdims=True) for j, half, _ in grp]
                w = [jnp.exp(s[i] - lse_h[i]) for i in idx]
                for i, (j, half, _) in enumerate(grp):
                    w_sink = jnp.exp(sink_ref[2 * j + half] - lse_h[i])
                    sink_acc = sink_acc + jnp.where(lane == 2 * j + half, -(w_sink * delta[i]), 0.0)
                dsb = [(w[i] * (d_w[i] - delta[i])).astype(BF16) for i in idx]
                d_q = [_dot(x, kall, NN) * (DH ** -0.5) for x in dsb]
                d_k = [_dot(dsb[i], grp[i][2], TN) for i in idx]
                d_v = [_dot(w[i].astype(BF16), domb[i], TN) for i in idx]
                for i in idx:
                    d_kall = d_kall + d_k[i]
                    d_vall = d_vall + d_v[i]
                for i in range(0, len(grp), 2):
                    j = grp[i][0]
                    dq_ref[:, j * TB:(j + 1) * TB] = jnp.where(first, d_q[i], d_q[i + 1])
            dkm_ref[:, ks] += d_kall[0:N_META]
            dkp_ref[:, ks] = d_kall[N_META:N_META + TB]
            dkc_ref[:, ks] = d_kall[N_META + TB:NKEY]
            dvm_ref[:, ks] += d_vall[0:N_META]
            dvp_ref[:, ks] = d_vall[N_META:N_META + TB]
            dvc_ref[:, ks] = d_vall[N_META + TB:NKEY]
        dsink_ref[...] += jnp.sum(sink_acc, axis=0, keepdims=True)

    cur, prev, meta, sink = _attn_specs()
    acc = lambda r: pl.BlockSpec((r, 256), lambda n: (0, 0))
    return _call(
        body, name=name, grid=(p // TB,),
        in_specs=[sink, cur(1024), cur(256), prev, meta, cur(256), prev, meta, cur(1024), cur(TB), cur(1024)],
        out_specs=[cur(1024), cur(256), cur(256), cur(256), cur(256), acc(N_META), acc(N_META),
                   pl.BlockSpec((1, TB), lambda n: (0, 0))],
        out_shape=[jax.ShapeDtypeStruct((p, 1024), F32)] + [jax.ShapeDtypeStruct((p, 256), F32)] * 4
        + [jax.ShapeDtypeStruct((N_META, 256), F32)] * 2 + [jax.ShapeDtypeStruct((1, TB), F32)],
        args=(sinks, qh, k2, k2, k2, v2, v2, v2, o, lse, d_o), sem=("arbitrary",), carry=carry)


def _qk_post(proj, gq, gk, cos, sin, dqh, dkc, dkp, dkm, dvc, dvp, dvm, *, name):
    p = proj.shape[0]
    nb = p // TB

    def body(qb_ref, kb_ref, gq_ref, gk_ref, cos_ref, sin_ref, dqh_ref, dkc_ref, dkp_ref, dkm_ref,
             dvc_ref, dvp_ref, dvm_ref, dqb_ref, dkb_ref, dvb_ref, dgq_ref, dgk_ref, tk_ref, tv_ref):
        n = pl.program_id(0)

        @pl.when(n == 0)
        def _():
            dgq_ref[...] = jnp.zeros_like(dgq_ref)
            dgk_ref[...] = jnp.zeros_like(dgk_ref)

        keep = jnp.where(n == nb - 1, 0.0, 1.0)
        tk_ref[...] = dkc_ref[...] + keep * dkp_ref[...]
        tv_ref[...] = dvc_ref[...] + keep * dvp_ref[...]

        @pl.when(n == 0)
        def _():
            tk_ref[PAD_FRONT:TB, :] += dkm_ref[...]
            tv_ref[PAD_FRONT:TB, :] += dvm_ref[...]

        first = _lane() < DH

        def fold(t_ref):
            t0, t1 = t_ref[:, 0:TB], t_ref[:, TB:2 * TB]
            return jnp.where(first, t0 + pltpu.roll(t0, DH, 1), t1 + pltpu.roll(t1, DH, 1))

        bd = _head_ones()
        cos_v, sin_v = cos_ref[...], sin_ref[...]
        dvb_ref[...] = fold(tv_ref).astype(BF16)
        dkb, dgk = _norm_rope_bwd(fold(tk_ref), kb_ref[...], gk_ref[...], cos_v, sin_v, bd)
        dkb_ref[...] = dkb.astype(BF16)
        dgk_ref[...] += dgk
        dgq = jnp.zeros((1, TB), F32)
        for j in range(HB // 2):
            sl = slice(j * TB, (j + 1) * TB)
            dqb, dg = _norm_rope_bwd(dqh_ref[:, sl], qb_ref[:, sl], gq_ref[...], cos_v, sin_v, bd)
            dqb_ref[:, sl] = dqb.astype(BF16)
            dgq = dgq + dg
        dgq_ref[...] += dgq

    vec = pl.BlockSpec((1, TB), lambda n: (0, 0))
    tab = pl.BlockSpec((TB, TB), lambda n: (n, 0))
    cur = lambda w: pl.BlockSpec((TB, w), lambda n: (n, 0))
    nxt = pl.BlockSpec((TB, 256), lambda n: (jnp.minimum(n + 1, nb - 1), 0))
    meta = pl.BlockSpec((N_META, 256), lambda n: (0, 0))
    return pl.pallas_call(
        body, name=name, grid=(nb,),
        in_specs=[pl.BlockSpec((TB, 1024), lambda n: (n, C_QB)), pl.BlockSpec((TB, TB), lambda n: (n, C_KB)),
                  vec, vec, tab, tab, cur(1024), cur(256), nxt, meta, cur(256), nxt, meta],
        out_specs=[cur(1024), cur(TB), cur(TB), vec, vec],
        out_shape=[jax.ShapeDtypeStruct((p, 1024), BF16), jax.ShapeDtypeStruct((p, TB), BF16),
                   jax.ShapeDtypeStruct((p, TB), BF16), jax.ShapeDtypeStruct((1, TB), F32),
                   jax.ShapeDtypeStruct((1, TB), F32)],
        scratch_shapes=[pltpu.VMEM((TB, 256), F32), pltpu.VMEM((TB, 256), F32)],
        compiler_params=_params("arbitrary"),
    )(proj, proj, gq, gk, cos, sin, dqh, dkc, dkp, dkm, dvc, dvp, dvm)


EXT = TB + N_META


def _pool_count_inv(n, w):
    t = n * TB + _iota((TB, 1), 0)
    cnt = jnp.clip(t - (PAD_FRONT - 1), 1, w)
    return 1.0 / cnt.astype(F32)


def _silu_parts(gate):
    s = _sigmoid(gate)
    return gate * s, s * (1.0 + gate * (1.0 - s))


def _mix_fwd(proj, oa, yb, hg, pool_w, pool_scale, *, name):
    p = proj.shape[0]

    def body(ga_ref, gb0, gb1, gb2, gb3, uc0, uc1, up0, up1, gc0, gc1, oa_ref, yb_ref, hg_ref, pw_ref, ps_ref,
             mx_ref, pooled_ref):
        gb_ref, uc_ref, up_ref, gc_ref = _Cols([gb0, gb1, gb2, gb3]), _Cols([uc0, uc1]), _Cols([up0, up1]), _Cols([gc0, gc1])
        n = pl.program_id(0)
        valid = ((n * TB + _iota((TB, 1), 0)) >= PAD_FRONT).astype(F32)
        for hd in range(HA):
            sl = slice(hd * TB, (hd + 1) * TB)
            o = oa_ref[:, sl]
            r = lax.rsqrt(jnp.mean(o * o, axis=-1, keepdims=True) + RMS_EPS)
            act, _ = _silu_parts(ga_ref[:, sl])
            mx_ref[:, sl] = (o * r * hg_ref[...] * act).astype(BF16)
        for j in range(HB // 2):
            sl = slice(j * TB, (j + 1) * TB)
            act, _ = _silu_parts(gb_ref[:, sl])
            mx_ref[:, 512 + j * TB:512 + (j + 1) * TB] = (yb_ref[:, sl] * act).astype(BF16)
        ri, ci = _iota((TB, EXT), 0), _iota((TB, EXT), 1)
        has_prev = jnp.where(n == 0, 0.0, 1.0)
        groups = range(len(POOL_WINDOWS))
        sls = [slice(gi * TB, (gi + 1) * TB) for gi in groups]
        ug = [uc_ref[:, sl] * valid for sl in sls]
        ext = [jnp.concatenate([up_ref[:, sls[gi]] * has_prev, ug[gi]], axis=0) for gi in groups]
        band = [jnp.where((ci <= ri + N_META) & (ci > ri + N_META - w), 1.0, 0.0).astype(BF16) for w in POOL_WINDOWS]
        num = [_xdot(band[gi], ext[gi]) for gi in groups]
        pooled = [(num[gi] * _pool_count_inv(n, w) - ug[gi]) * valid for gi, w in enumerate(POOL_WINDOWS)]
        t = [_dot(pooled[gi].astype(BF16), pw_ref[gi], NN) for gi in groups]
        for gi in groups:
            pooled_ref[:, sls[gi]] = pooled[gi]
            act, _ = _silu_parts(gc_ref[:, sls[gi]])
            mx_ref[:, 1536 + gi * TB:1536 + (gi + 1) * TB] = (t[gi] * ps_ref[:, sls[gi]] * act).astype(BF16)

    cur = lambda w, c=0: pl.BlockSpec((TB, w), lambda n, c=c: (n, c))
    prev16 = lambda c: pl.BlockSpec((N_META, 256), lambda n, c=c: (jnp.maximum(n * (TB // N_META) - 1, 0), c))
    return pl.pallas_call(
        body, name=name, grid=(p // TB,),
        in_specs=[cur(512, C_GA)] + [cur(256, c) for c in C_GB] + [cur(256, c) for c in C_UC]
        + [prev16(c) for c in C_UC] + [cur(256, c) for c in C_GC]
        + [cur(512), cur(1024), pl.BlockSpec((1, TB), lambda n: (0, 0)), pl.BlockSpec((4, TB, TB), lambda n: (0, 0, 0)),
           pl.BlockSpec((1, 512), lambda n: (0, 0))],
        out_specs=[cur(MIX), cur(512)],
        out_shape=[jax.ShapeDtypeStruct((p, MIX), BF16), jax.ShapeDtypeStruct((p, 512), F32)],
        compiler_params=_params("parallel"),
    )(*([proj] * 11), oa, yb, hg, pool_w, pool_scale)


def _mix_bwd(proj, oa, yb, pooled, hg, pool_w, pool_scale, d_mixed, *, name, carry=None):
    p = proj.shape[0]

    def body(ga_ref, gb0, gb1, gb2, gb3, gc0, gc1, oa_ref, yb_ref, pooled_ref, hg_ref, pw_ref, ps_ref, dm_ref,
             doa_ref, dyb_ref, dga_ref, dgb_ref, dgc_ref, dp_ref, dhg_ref, dpw_ref, dps_ref):
        gb_ref, gc_ref = _Cols([gb0, gb1, gb2, gb3]), _Cols([gc0, gc1])
        n = pl.program_id(0)

        @pl.when(n == 0)
        def _():
            dhg_ref[...] = jnp.zeros_like(dhg_ref)
            dpw_ref[...] = jnp.zeros_like(dpw_ref)
            dps_ref[...] = jnp.zeros_like(dps_ref)

        valid = ((n * TB + _iota((TB, 1), 0)) >= PAD_FRONT).astype(F32)
        dhg = jnp.zeros((1, TB), F32)
        for hd in range(HA):
            sl = slice(hd * TB, (hd + 1) * TB)
            o = oa_ref[:, sl]
            r = lax.rsqrt(jnp.mean(o * o, axis=-1, keepdims=True) + RMS_EPS)
            on = o * r
            gate = ga_ref[:, sl]
            act, dact = _silu_parts(gate)
            dmx = dm_ref[:, sl]
            d_ya = dmx * act
            dga_ref[:, sl] = (dmx * on * hg_ref[...] * dact).astype(BF16)
            dyn = d_ya * hg_ref[...]
            doa_ref[:, sl] = r * (dyn - on * jnp.mean(dyn * on, axis=-1, keepdims=True))
            dhg = dhg + jnp.sum(d_ya * on, axis=0, keepdims=True)
        dhg_ref[...] += dhg
        for j in range(HB // 2):
            sl = slice(j * TB, (j + 1) * TB)
            act, dact = _silu_parts(gb_ref[:, sl])
            dmx = dm_ref[:, 512 + j * TB:512 + (j + 1) * TB]
            dyb_ref[:, sl] = dmx * act
            dgb_ref[:, sl] = (dmx * yb_ref[:, sl] * dact).astype(BF16)
        groups = range(len(POOL_WINDOWS))
        sls = [slice(gi * TB, (gi + 1) * TB) for gi in groups]
        pooled_b = [pooled_ref[:, sl].astype(BF16) for sl in sls]
        t = [_dot(pooled_b[gi], pw_ref[gi], NN) for gi in groups]
        d_t = []
        for gi in groups:
            sl = sls[gi]
            act, dact = _silu_parts(gc_ref[:, sl])
            dmx = dm_ref[:, 1536 + gi * TB:1536 + (gi + 1) * TB]
            d_yc = dmx * act
            dgc_ref[:, sl] = (dmx * t[gi] * ps_ref[:, sl] * dact).astype(BF16)
            dps_ref[:, sl] += jnp.sum(d_yc * t[gi], axis=0, keepdims=True)
            d_t.append((d_yc * ps_ref[:, sl]).astype(BF16))
        d_p = [_dot(d_t[gi], pw_ref[gi], NT) for gi in groups]
        d_w = [_dot(pooled_b[gi], d_t[gi], TN) for gi in groups]
        for gi in groups:
            dp_ref[:, sls[gi]] = d_p[gi] * valid
            dpw_ref[gi] += d_w[gi]

    cur = lambda w, c=0: pl.BlockSpec((TB, w), lambda n, c=c: (n, c))
    return _call(
        body, name=name, grid=(p // TB,),
        in_specs=[cur(512, C_GA)] + [cur(256, c) for c in C_GB] + [cur(256, c) for c in C_GC]
        + [cur(512), cur(1024), cur(512), pl.BlockSpec((1, TB), lambda n: (0, 0)),
           pl.BlockSpec((4, TB, TB), lambda n: (0, 0, 0)), pl.BlockSpec((1, 512), lambda n: (0, 0)), cur(MIX)],
        out_specs=[cur(512), cur(1024), cur(512), cur(1024), cur(512), cur(512),
                   pl.BlockSpec((1, TB), lambda n: (0, 0)), pl.BlockSpec((4, TB, TB), lambda n: (0, 0, 0)),
                   pl.BlockSpec((1, 512), lambda n: (0, 0))],
        out_shape=[jax.ShapeDtypeStruct((p, 512), F32), jax.ShapeDtypeStruct((p, 1024), F32),
                   jax.ShapeDtypeStruct((p, 512), BF16), jax.ShapeDtypeStruct((p, 1024), BF16),
                   jax.ShapeDtypeStruct((p, 512), BF16), jax.ShapeDtypeStruct((p, 512), F32),
                   jax.ShapeDtypeStruct((1, TB), F32), jax.ShapeDtypeStruct((4, TB, TB), F32),
                   jax.ShapeDtypeStruct((1, 512), F32)],
        args=(*([proj] * 7), oa, yb, pooled, hg, pool_w, pool_scale, d_mixed), sem=("arbitrary",), carry=carry)


def _pool_bwd(dp, *, name):
    p = dp.shape[0]
    nb = p // TB

    def body(dp_ref, dn_ref, duc_ref):
        n = pl.program_id(0)
        valid = ((n * TB + _iota((TB, 1), 0)) >= PAD_FRONT).astype(F32)
        has_next = jnp.where(n == nb - 1, 0.0, 1.0)
        ri, ci = _iota((TB, EXT), 0), _iota((TB, EXT), 1)
        groups = range(len(POOL_WINDOWS))
        sls = [slice(gi * TB, (gi + 1) * TB) for gi in groups]
        d_p = [dp_ref[:, sl] for sl in sls]
        ext = [jnp.concatenate([d_p[gi] * _pool_count_inv(n, w), dn_ref[:, sls[gi]] * (has_next / w)], axis=0)
               for gi, w in enumerate(POOL_WINDOWS)]
        band = [jnp.where((ci >= ri) & (ci < ri + w), 1.0, 0.0).astype(BF16) for w in POOL_WINDOWS]
        back = [_xdot2(band[gi], ext[gi]) for gi in groups]
        for gi in groups:
            duc_ref[:, sls[gi]] = ((back[gi] - d_p[gi]) * valid).astype(BF16)

    return pl.pallas_call(
        body, name=name, grid=(nb,),
        in_specs=[pl.BlockSpec((TB, 512), lambda n: (n, 0)),
                  pl.BlockSpec((N_META, 512), lambda n: (jnp.minimum(n + 1, nb - 1) * (TB // N_META), 0))],
        out_specs=pl.BlockSpec((TB, 512), lambda n: (n, 0)),
        out_shape=jax.ShapeDtypeStruct((p, 512), BF16),
        compiler_params=_params("parallel"),
    )(dp, dp)


def _carried(carries, key, local, fn, *args, **kw):
    if key not in carries:
        return fn(*args, **kw)
    make_src, gather, done = carries[key]
    *outs, stack = fn(*args, carry=(make_src(local), gather), **kw)
    done(stack)
    return outs


def _layer_fwd(h, w, tag, carries, xn=None):
    if xn is None:
        xn, = _rmsnorm_fwd(h, w["norm_g"], name=f"rmsnorm_fwd{tag}")
    proj, = _carried(carries, "in_proj", None, _mm_nn, xn, w["w_in"], name=f"in_proj{tag}")
    oa, sck = _carried(carries, "hgrn_fwd", None, _hgrn_fwd, proj, w["lb"], name=f"hgrn_fwd{tag}")
    qh, k2, v2 = _qk_prep(proj, w["gq"], w["gk"], w["cos"], w["sin"], name=f"qk_prep{tag}")
    yb, lse = _carried(carries, "attn_fwd", None, _attn_fwd, qh, k2, v2, w["sinks"], name=f"attn_fwd{tag}")
    mixed, pooled = _mix_fwd(proj, oa, yb, w["hg"], w["pool_w"], w["pool_scale"], name=f"mix_fwd{tag}")
    h_next, = _mm_nn(mixed, w["w_out"], h, name=f"out_proj{tag}", tm=640, tn=D_MODEL)
    saved = dict(h=h, xn=xn, proj=proj, oa=oa, sck=sck, qh=qh, k2=k2, v2=v2, yb=yb, lse=lse, mixed=mixed, pooled=pooled)
    return h_next, saved


def _layer_bwd(dh_out, dhb, s, w, tag, carries, first_layer=False):
    g = {}
    d_mixed, = _carried(carries, "d_mixed", g, _mm_nt, dhb, w["w_out"], name=f"d_mixed{tag}", tm=640, tn=MIX)
    g["w_out"], = _mm_tn(s["mixed"], dhb, name=f"dw_out{tag}", tn=1024)
    d_oa, d_yb, d_ga, d_gb, d_gc, d_p, g["hg"], g["pool_w"], g["pool_scale"] = _carried(
        carries, "mix_bwd", g, _mix_bwd,
        s["proj"], s["oa"], s["yb"], s["pooled"], w["hg"], w["pool_w"], w["pool_scale"], d_mixed, name=f"mix_bwd{tag}")
    d_uc = _pool_bwd(d_p, name=f"pool_bwd{tag}")
    d_qh, dkc, dkp, dvc, dvp, dkm, dvm, g["sinks"] = _carried(
        carries, "attn_bwd", g, _attn_bwd, s["qh"], s["k2"], s["v2"], w["sinks"], s["yb"], s["lse"], d_yb,
        name=f"attn_bwd{tag}")
    d_qb, d_kb, d_vb, g["gq"], g["gk"] = _qk_post(s["proj"], w["gq"], w["gk"], w["cos"], w["sin"], d_qh, dkc, dkp, dkm,
                                                  dvc, dvp, dvm, name=f"qk_post{tag}")
    d_qa, d_fa, d_ia, g["lb"] = _carried(carries, "hgrn_bwd", g, _hgrn_bwd, s["proj"], w["lb"], s["sck"], d_oa,
                                         name=f"hgrn_bwd{tag}")
    d_proj = jnp.concatenate([d_qa, d_fa, d_ia, d_ga, d_qb, d_kb, d_vb, d_gb, d_uc, d_gc], axis=1)
    g["w_in_parts"] = []
    for i in range(W_IN_PARTS):
        part, = _carried(carries, f"dw_in_{i}", g, _mm_tn, s["xn"], d_proj, name=f"dw_in_{i}{tag}",
                         m_part=(i, W_IN_PARTS))
        g["w_in_parts"].append(part)
    d_xn, = _carried(carries, "d_xn", g, _mm_nt, d_proj, w["w_in"], name=f"d_xn{tag}", tk=PROJ_COLS)
    out, aux, g["norm_g"] = _rmsnorm_bwd(d_xn, s["h"], w["norm_g"], dh_out, name=f"rmsnorm_bwd{tag}", first_layer=first_layer)
    return out, aux, g


def _peers():
    x, y, c = lax.axis_index("x"), lax.axis_index("y"), lax.axis_index("c")
    out = []
    for k in range(1, N_DEV):
        kx, ky, kc = (k >> 2) & 1, (k >> 1) & 1, k & 1
        px, py, pc = x ^ kx, y ^ ky, c ^ kc
        out.append(((px, py, pc), 4 * px + 2 * py + pc))
    return 4 * x + 2 * y + c, out


def _exchange_copies(src_ref, out_ref, send_sems, recv_sems, local_sem):
    me, peers = _peers()
    mine = pltpu.make_async_copy(src_ref.at[me], out_ref.at[me], local_sem)
    copies = []
    for k, (dev, idx) in enumerate(peers):
        copies.append(pltpu.make_async_remote_copy(
            src_ref=src_ref.at[idx], dst_ref=out_ref.at[me],
            send_sem=send_sems.at[k], recv_sem=recv_sems.at[k],
            device_id=dev, device_id_type=pl.DeviceIdType.MESH))
    return mine, copies


def _gather_copies(src_ref, out_ref, send_sems, recv_sems, local_sem):
    x, y, c = lax.axis_index("x"), lax.axis_index("y"), lax.axis_index("c")
    slot = lambda px, py, pc: out_ref.at[4 * px + 2 * py + pc]
    sibling = (x, y, 1 - c)
    chips = [(1 - x, y), (x, 1 - y), (1 - x, 1 - y)]

    def copy(k, src, block, to):
        return pltpu.make_async_remote_copy(src_ref=src, dst_ref=slot(*block), send_sem=send_sems.at[k],
                                            recv_sem=recv_sems.at[k], device_id=to, device_id_type=pl.DeviceIdType.MESH)

    mine = lambda: pltpu.make_async_copy(src_ref, slot(x, y, c), local_sem)
    own = lambda: ([copy(0, src_ref, (x, y, c), sibling)]
                   + [copy(1 + j, src_ref, (x, y, c), (*chip, c)) for j, chip in enumerate(chips)])
    passing = lambda: [copy(4 + j, slot(*chip, c), (*chip, c), sibling) for j, chip in enumerate(chips)]
    arrivals = lambda: ([copy(0, src_ref, sibling, sibling)]
                        + [copy(1 + j, src_ref, (*chip, c), sibling) for j, chip in enumerate(chips)]
                        + [copy(4 + j, src_ref, (*chip, 1 - c), sibling) for j, chip in enumerate(chips)])
    return mine, own, passing, arrivals


def _exchange_start(*refs, gather):
    if gather:
        mine, own, _, _ = _gather_copies(*refs)
        mine().start()
        for cp in own():
            cp.start()
        return
    mine, copies = _exchange_copies(*refs)
    mine.start()
    for cp in copies:
        cp.start()


def _exchange_pass_on(*refs, gather):
    if gather:
        _, _, passing, arrivals = _gather_copies(*refs)
        arrivals = arrivals()
        for j, cp in enumerate(passing()):
            arrivals[1 + j].wait_recv()
            cp.start()


def _exchange_wait(*refs, gather):
    if gather:
        mine, own, passing, arrivals = _gather_copies(*refs)
        arrivals = arrivals()
        arrivals[0].wait_recv()
        for cp in arrivals[4:]:
            cp.wait_recv()
        for cp in own() + passing():
            cp.wait_send()
        mine().wait()
        return
    mine, copies = _exchange_copies(*refs)
    for cp in copies:
        cp.wait_recv()
    for cp in copies:
        cp.wait_send()
    mine.wait()


def _exchange_scratch():
    return [pltpu.SemaphoreType.DMA((N_DEV - 1,)), pltpu.SemaphoreType.DMA((N_DEV - 1,)), pltpu.SemaphoreType.DMA]


def _exchange(src, *, gather, name):
    rows, cols = src.shape[-2:]

    def body(src_ref, out_ref, send_sems, recv_sems, local_sem):
        _exchange_start(src_ref, out_ref, send_sems, recv_sems, local_sem, gather=gather)
        _exchange_pass_on(src_ref, out_ref, send_sems, recv_sems, local_sem, gather=gather)
        _exchange_wait(src_ref, out_ref, send_sems, recv_sems, local_sem, gather=gather)

    return pl.pallas_call(
        body, name=name,
        in_specs=[pl.BlockSpec(memory_space=pl.ANY)], out_specs=pl.BlockSpec(memory_space=pl.ANY),
        out_shape=jax.ShapeDtypeStruct((N_DEV, rows, cols), src.dtype),
        scratch_shapes=_exchange_scratch(),
    )(src)


def _call(body, *, name, grid, in_specs, out_specs, out_shape, args, sem, scratch_shapes=(), carry=None):
    if carry is None:
        return pl.pallas_call(
            body, name=name, grid=grid, in_specs=list(in_specs), out_specs=list(out_specs), out_shape=list(out_shape),
            scratch_shapes=list(scratch_shapes), compiler_params=_params(*sem))(*args)
    src, gather, late = carry if len(carry) == 3 else (*carry, False)
    n_in, n_out, n_scr = len(in_specs), len(out_specs), len(scratch_shapes)
    rows, cols = src.shape[-2:]

    def carrying(*refs):
        ins, src_ref = refs[:n_in], refs[n_in]
        outs, dst_ref = refs[n_in + 1:n_in + 1 + n_out], refs[n_in + 1 + n_out]
        scr = refs[n_in + 2 + n_out:]
        exch = (src_ref, dst_ref) + tuple(scr[n_scr:])
        step, steps = 0, 1
        for a, size in enumerate(grid):
            step, steps = step * size + pl.program_id(a), steps * size

        @pl.when(step == 0)
        def _():
            _exchange_start(*exch, gather=gather)

        @pl.when(step == (steps - 1 if late else min(steps * 3 // 4, steps - 1)))
        def _():
            _exchange_pass_on(*exch, gather=gather)

        body(*ins, *outs, *scr[:n_scr])

        @pl.when(step == steps - 1)
        def _():
            _exchange_wait(*exch, gather=gather)

    hbm = pl.BlockSpec(memory_space=pl.ANY)
    return pl.pallas_call(
        carrying, name=name, grid=grid, in_specs=list(in_specs) + [hbm], out_specs=list(out_specs) + [hbm],
        out_shape=list(out_shape) + [jax.ShapeDtypeStruct((N_DEV, rows, cols), src.dtype)],
        scratch_shapes=list(scratch_shapes) + _exchange_scratch(),
        compiler_params=_params(*(("arbitrary",) * len(grid))))(*args, src)


def _adamw(stacks, w, m, v, *, name, carry=None):
    nl = len(stacks)
    rows, cols = stacks[0].shape[1:]
    tr = rows
    stack_block_bytes = 8 * 1024 * 1024 // nl
    for cand in (256, 128, 64, 32, 16):
        if rows % cand == 0 and N_DEV * cand * cols * stacks[0].dtype.itemsize <= stack_block_bytes:
            tr = cand
            break
    nt = rows // tr

    def body(*refs):
        s_refs = refs[:nl]
        w_ref, m_ref, v_ref, g_ref, d_ref, nm_ref, nv_ref = refs[nl:]
        for l, s_ref in enumerate(s_refs):
            @pl.when(pl.program_id(0) == l)
            def _(s_ref=s_ref):
                acc = s_ref[0].astype(F32)
                for d in range(1, N_DEV):
                    acc = acc + s_ref[d].astype(F32)
                g_ref[...] = acc

        g = g_ref[...]
        nm = ADAM_B1 * m_ref[...] + (1.0 - ADAM_B1) * g
        nv = ADAM_B2 * v_ref[...] + (1.0 - ADAM_B2) * (g * g)
        m_hat = nm / (1.0 - ADAM_B1 ** ADAM_STEP)
        v_hat = nv / (1.0 - ADAM_B2 ** ADAM_STEP)
        d_ref[...] = -ADAM_LR * (m_hat / (jnp.sqrt(v_hat) + ADAM_EPS) + ADAM_WD * w_ref[...])
        nm_ref[...] = nm
        nv_ref[...] = nv

    blk = pl.BlockSpec((tr, cols), lambda l, i: (l * nt + i, 0))
    return _call(
        body, name=name, grid=(nl, nt),
        in_specs=[pl.BlockSpec((N_DEV, tr, cols), lambda l, i, k=k: (0, jnp.where(l == k, i, 0), 0)) for k in range(nl)]
        + [blk, blk, blk],
        out_specs=[blk] * 4, out_shape=[jax.ShapeDtypeStruct((nl * rows, cols), F32)] * 4,
        args=(*stacks, w, m, v), sem=("arbitrary", "arbitrary"), carry=carry)


def _lb_all(lb_logits):
    sm = jax.nn.softmax(lb_logits.astype(F32), axis=0)
    return jnp.cumsum(sm, axis=0) - sm[0:1]


def _rope_tables(p):
    half = DH // 2
    inv = jnp.power(ROPE_THETA, -jnp.arange(half, dtype=F32) * 2.0 / DH)
    pos = (jnp.arange(p) - PAD_FRONT).astype(F32)
    ang = pos[:, None] * inv[None, :]
    cos, sin = jnp.cos(ang), jnp.sin(ang)
    return jnp.tile(cos, (1, 4)), jnp.tile(jnp.concatenate([-sin, sin], axis=1), (1, 2))


SMALL = (("lb_logits", (DEPTH, 512)), ("q_norm_g", (DEPTH, DH)),
         ("k_norm_g", (DEPTH, DH)), ("attn_sinks", (DEPTH, HB)), ("hgrn_norm_g", (DEPTH, 128)),
         ("pool_w", (DEPTH, 4, 128, 128)), ("pool_scale", (DEPTH, 512)))


def _pack_small(d):
    flat = jnp.concatenate([d[k].astype(F32).reshape(-1) for k, _ in SMALL])
    pad = (-flat.shape[0]) % (8 * 128)
    return jnp.pad(flat, (0, pad)).reshape(-1, 128)


def _unpack_small(a):
    flat = a.reshape(-1)
    out, off = {}, 0
    for k, shp in SMALL:
        n = int(np.prod(shp))
        out[k] = flat[off:off + n].reshape(shp)
        off += n
    return out


def kernel(x, meta_tokens, lb_logits, norm_g, w_in, q_norm_g, k_norm_g, attn_sinks, hgrn_norm_g, pool_w, pool_scale, w_out, loss_target, m_meta_tokens, m_lb_logits, m_norm_g, m_w_in, m_q_norm_g, m_k_norm_g, m_attn_sinks, m_hgrn_norm_g, m_pool_w, m_pool_scale, m_w_out, v_meta_tokens, v_lb_logits, v_norm_g, v_w_in, v_q_norm_g, v_k_norm_g, v_attn_sinks, v_hgrn_norm_g, v_pool_w, v_pool_scale, v_w_out):
    seq = x.shape[1]
    p = seq + TB
    cs = PROJ_COLS // N_DEV
    rs = MIX // N_DEV
    ms = D_MODEL // N_DEV

    full_w_in = lambda st: st.transpose(1, 0, 2).reshape(D_MODEL, PROJ_COLS)
    dw_in_blocks = lambda rows: rows.reshape(rows.shape[0], N_DEV, cs).transpose(1, 0, 2)
    dw_out_blocks = lambda g: g["w_out"].reshape(N_DEV, rs, D_MODEL)
    w_in_bf = w_in.astype(BF16)

    lb_all, lb_vjp = jax.vjp(_lb_all, lb_logits)
    cos, sin = _rope_tables(p)
    layers = []
    for l in range(DEPTH):
        layers.append(dict(
            norm_g=norm_g[l][None], lb=lb_all[l][None],
            gq=jnp.tile(q_norm_g[l], 2)[None], gk=jnp.tile(k_norm_g[l], 2)[None], sinks=attn_sinks[l],
            hg=hgrn_norm_g[l][None], pool_w=pool_w[l].astype(BF16), pool_scale=pool_scale[l][None], cos=cos, sin=sin))
    meta_all = _exchange(meta_tokens, gather=True, name="gather_meta")
    meta_full = meta_all.transpose(1, 0, 2).reshape(N_META, D_MODEL)

    def got_w_out(st):
        st = st.reshape(N_DEV, DEPTH, rs, D_MODEL)
        for l in range(DEPTH):
            layers[l]["w_out"] = st[:, l].reshape(MIX, D_MODEL)

    def got_w_in(l):
        def done(st):
            layers[l]["w_in"] = full_w_in(st)
        return done

    fwd_carries = [
        dict(in_proj=(lambda _: w_in_bf[1], True, got_w_in(1)),
             hgrn_fwd=(lambda _: w_out.reshape(DEPTH * rs, D_MODEL).astype(BF16), True, got_w_out)),
        {}]
    h, xn, w_in_l0 = _embed_rmsnorm_fwd(x[0], meta_full, layers[0]["norm_g"], name="embed_rmsnorm_fwd_l0",
                                        carry=(w_in_bf[0], True, True))
    got_w_in(0)(w_in_l0)
    saved = []
    for l in range(DEPTH):
        h, s = _layer_fwd(h, layers[l], f"_l{l}", fwd_carries[l], xn=xn if l == 0 else None)
        saved.append(s)
    dh, dhb, sq = _loss_grad(h, loss_target[0], name="loss_grad")
    loss = lax.psum(0.5 * jnp.sum(sq) / D_MODEL, ("x", "y", "c"))

    grads = [None] * DEPTH
    win_stacks, wout_stacks, small_stacks = [None] * (W_IN_PARTS * DEPTH), [None] * DEPTH, [None]

    def into(stacks, i):
        def done(st):
            stacks[i] = st
        return done

    def small_grads(g0):
        both = [g0, grads[1]]
        stk = lambda k: jnp.stack([both[l][k][0] for l in range(DEPTH)])
        fold = lambda a: a[:, :DH] + a[:, DH:]
        return _pack_small(dict(
            lb_logits=lb_vjp(stk("lb"))[0], q_norm_g=fold(stk("gq")), k_norm_g=fold(stk("gk")),
            attn_sinks=stk("sinks")[:, :HB], hgrn_norm_g=stk("hg"),
            pool_w=jnp.stack([both[l]["pool_w"] for l in range(DEPTH)]), pool_scale=stk("pool_scale")))

    def part(l, i):
        return (lambda g: dw_in_blocks(g["w_in_parts"][i])), False, into(win_stacks, W_IN_PARTS * l + i)

    bwd_carries = [
        dict(hgrn_bwd=(dw_out_blocks, False, into(wout_stacks, 0)), dw_in_0=(small_grads, True, into(small_stacks, 0)),
             dw_in_1=part(0, 0), d_xn=part(0, 1)),
        dict(attn_bwd=(dw_out_blocks, False, into(wout_stacks, 1)), dw_in_1=part(1, 0), d_xn=part(1, 1))]
    dh, dhb, grads[1] = _layer_bwd(dh, dhb, saved[1], layers[1], "_l1", bwd_carries[1])
    grad_x, d_front, grads[0] = _layer_bwd(dh, dhb, saved[0], layers[0], "_l0", bwd_carries[0], first_layer=True)
    grad_x = grad_x[None]
    dmeta = d_front.reshape(N_META, N_DEV, ms).transpose(1, 0, 2)
    meta_stack = _exchange(dmeta, gather=False, name="scatter_dmeta")
    small_stack = small_stacks[0]
    d_norm_g = jnp.stack([grads[l]["norm_g"][0] for l in range(DEPTH)]).reshape(-1, 128)
    norm_stack = _exchange(d_norm_g, gather=True, name="gather_norm_g_grad")
    adam_norm = [a.reshape(DEPTH, D_MODEL) for a in _adamw(
        [norm_stack], norm_g.reshape(-1, 128), m_norm_g.reshape(-1, 128), v_norm_g.reshape(-1, 128), name="adamw_norm_g")]

    g_wout, d_wout, nm_wout, nv_wout = _adamw(wout_stacks, w_out.reshape(DEPTH * rs, D_MODEL), m_w_out.reshape(DEPTH * rs, D_MODEL),
                                              v_w_out.reshape(DEPTH * rs, D_MODEL), name="adamw_w_out")
    g_win, d_win, nm_win, nv_win = _adamw(win_stacks, w_in.reshape(DEPTH * D_MODEL, cs), m_w_in.reshape(DEPTH * D_MODEL, cs),
                                          v_w_in.reshape(DEPTH * D_MODEL, cs), name="adamw_w_in")
    g_meta, d_meta, nm_meta, nv_meta = _adamw([meta_stack], meta_tokens, m_meta_tokens, v_meta_tokens, name="adamw_meta")
    small_w = dict(lb_logits=lb_logits, norm_g=norm_g, q_norm_g=q_norm_g, k_norm_g=k_norm_g, attn_sinks=attn_sinks,
                   hgrn_norm_g=hgrn_norm_g, pool_w=pool_w, pool_scale=pool_scale)
    small_m = dict(lb_logits=m_lb_logits, norm_g=m_norm_g, q_norm_g=m_q_norm_g, k_norm_g=m_k_norm_g, attn_sinks=m_attn_sinks,
                   hgrn_norm_g=m_hgrn_norm_g, pool_w=m_pool_w, pool_scale=m_pool_scale)
    small_v = dict(lb_logits=v_lb_logits, norm_g=v_norm_g, q_norm_g=v_q_norm_g, k_norm_g=v_k_norm_g, attn_sinks=v_attn_sinks,
                   hgrn_norm_g=v_hgrn_norm_g, pool_w=v_pool_w, pool_scale=v_pool_scale)
    small_out = [_unpack_small(a) for a in _adamw([small_stack], _pack_small(small_w), _pack_small(small_m),
                                                  _pack_small(small_v), name="adamw_small")]

    big = dict(
        meta_tokens=(g_meta, d_meta, nm_meta, nv_meta), norm_g=tuple(adam_norm),
        w_in=tuple(a.reshape(DEPTH, D_MODEL, cs) for a in (g_win, d_win, nm_win, nv_win)),
        w_out=tuple(a.reshape(DEPTH, rs, D_MODEL) for a in (g_wout, d_wout, nm_wout, nv_wout)))
    order = ("meta_tokens", "lb_logits", "norm_g", "w_in", "q_norm_g", "k_norm_g", "attn_sinks", "hgrn_norm_g",
             "pool_w", "pool_scale", "w_out")
    outs = [loss, grad_x]
    for kind in range(4):
        for k in order:
            outs.append(big[k][kind] if k in big else small_out[kind][k])
    return tuple(outs)
```
